```python
import math
import functools
import jax
import jax.numpy as jnp
from jax import lax

D_MODEL = 2048
BATCH = 1
SEQ = 8192
DEPTH = 2
DEC_BATCH = 32
DEC_SEQ = 8
PAST_LEN = 8192
PAGE_SIZE = 128

HEAD_DIM = 128
NSA_HEADS = D_MODEL // (2 * HEAD_DIM)
NSA_KV_GROUPS = 2
NSA_REP = NSA_HEADS // NSA_KV_GROUPS
NSA_WIDTH = NSA_HEADS * HEAD_DIM
NSA_KV_KINDS = 6
NSA_PAGED_KINDS = 4
CMP_BLOCK = 64
SEL_BLOCK = 64
N_SEL = 16
WINDOW = 512
PHI_HIDDEN = 2 * HEAD_DIM
Q_BLOCK = 64
FORCED_SCORE = 1e9
GMLP_GROUPS = 4
GMLP_GROUP_DIM = D_MODEL // 16
GMLP_WIDTH = GMLP_GROUPS * GMLP_GROUP_DIM
CHUNK = 128
POOL_WINDOWS = (2, 4, 8, 16)
POOL_GROUP_DIM = D_MODEL // 16
POOL_WIDTH = len(POOL_WINDOWS) * POOL_GROUP_DIM
POOL_BUF = max(POOL_WINDOWS) - 1
N_BUCKETS = 32
MAX_DISTANCE = 128
N_EXPERTS = 16
N_EXPERT_GROUPS = 4
EXPERTS_PER_GROUP = N_EXPERTS // N_EXPERT_GROUPS
TOP_K = 2
D_EXPERT = 512
DN_ALPHA = (2 * DEPTH) ** 0.25
DN_BETA = (8 * DEPTH) ** -0.25
LN_EPS = 1e-5
NEG_INF = -1e30
IN_SIZES = (NSA_WIDTH, NSA_KV_KINDS * NSA_KV_GROUPS * HEAD_DIM, 3 * NSA_HEADS, GMLP_WIDTH, GMLP_WIDTH, POOL_WIDTH, 3 * D_MODEL)
N_IN = sum(IN_SIZES)

kernel_name = 'hybrid_nsa_gmlp_pool_moe_decoder_step'


def layer_norm(x, gain=None, bias=None):
    xf = x.astype(jnp.float32)
    mu = jnp.mean(xf, axis=-1, keepdims=True)
    var = jnp.mean(jnp.square(xf - mu), axis=-1, keepdims=True)
    y = (xf - mu) * lax.rsqrt(var + LN_EPS)
    if gain is not None:
        y = y * gain.astype(jnp.float32) + bias.astype(jnp.float32)
    return y.astype(x.dtype)


def masked_softmax(logits, mask, axis=-1):
    logits = jnp.where(mask, logits.astype(jnp.float32), NEG_INF)
    return jax.nn.softmax(logits, axis=axis) * mask


def rel_bucket(dist):
    n = jnp.maximum(dist, 0)
    max_exact = N_BUCKETS // 2
    nf = jnp.maximum(n, max_exact).astype(jnp.float32)
    large = max_exact + (jnp.log(nf / max_exact) / math.log(MAX_DISTANCE / max_exact) * (N_BUCKETS - max_exact)).astype(jnp.int32)
    return jnp.where(n < max_exact, n, jnp.minimum(large, N_BUCKETS - 1))


def head_bias(rel_bias, dist):
    b = rel_bias[rel_bucket(dist)].reshape(dist.shape + (NSA_KV_GROUPS, NSA_REP))
    return jnp.transpose(b, (2, 3, 0, 1)).astype(jnp.float32)


def split_columns(z):
    parts, start = [], 0
    for size in IN_SIZES:
        parts.append(z[..., start:start + size])
        start += size
    return parts


def compress_blocks(rows, pe, w1, w2):
    B, L, G, dh = rows.shape
    nb = L // CMP_BLOCK
    blk = rows.reshape(B, nb, CMP_BLOCK, G, dh) + pe[:, None, :]
    blk = jnp.transpose(blk, (0, 1, 3, 2, 4)).reshape(B, nb, G, CMP_BLOCK * dh)
    return jax.nn.gelu(blk @ w1) @ w2


def to_blocks(rows):
    B, L, G, dh = rows.shape
    return jnp.transpose(rows.reshape(B, L // SEL_BLOCK, SEL_BLOCK, G, dh), (0, 3, 1, 2, 4))


def nsa_compressed(q, q_pos, kc, vc, rel_bias):
    nb = kc.shape[1]
    end_pos = (jnp.arange(nb) + 1) * CMP_BLOCK - 1
    dist = q_pos[:, None] - end_pos[None, :]
    logits = jnp.einsum('bqgrd,bngd->bgrqn', q, kc) * (HEAD_DIM ** -0.5) + head_bias(rel_bias, dist)
    p = masked_softmax(logits, dist >= 0)
    o = jnp.einsum('bgrqn,bngd->bqgrd', p.astype(vc.dtype), vc)
    return o, jnp.sum(p, axis=2)


def select_blocks(importance, q_pos):
    nb = importance.shape[-1]
    blk = jnp.arange(nb)[None, :]
    cur = (q_pos // SEL_BLOCK)[:, None]
    forced = (blk == 0) | (blk == cur) | (blk == cur - 1)
    score = jnp.where(forced, FORCED_SCORE, importance)
    score = jnp.where(blk <= cur, score, -1.0)
    top, idx = lax.top_k(score, min(N_SEL, nb))
    return idx, top >= 0.0


def nsa_selected(q, q_pos, idx, valid, k_blocks, v_blocks, rel_bias):
    B, G = k_blocks.shape[:2]
    nq, n = idx.shape[2], idx.shape[3]
    b_ix = jnp.arange(B)[:, None, None, None]
    g_ix = jnp.arange(G)[None, :, None, None]
    kg = k_blocks[b_ix, g_ix, idx].reshape(B, G, nq, n * SEL_BLOCK, HEAD_DIM)
    vg = v_blocks[b_ix, g_ix, idx].reshape(B, G, nq, n * SEL_BLOCK, HEAD_DIM)
    k_pos = (idx[..., None] * SEL_BLOCK + jnp.arange(SEL_BLOCK)).reshape(B, G, nq, n * SEL_BLOCK)
    dist = q_pos[None, None, :, None] - k_pos
    mask = jnp.repeat(valid, SEL_BLOCK, axis=-1) & (dist >= 0)
    bias = rel_bias.reshape(N_BUCKETS, NSA_KV_GROUPS, NSA_REP)[rel_bucket(dist), g_ix].astype(jnp.float32)
    logits = jnp.einsum('bqgrd,bgqkd->bgqkr', q, kg) * (HEAD_DIM ** -0.5) + bias
    p = masked_softmax(logits, mask[..., None], axis=-2)
    return jnp.einsum('bgqkr,bgqkd->bqgrd', p.astype(vg.dtype), vg)


def nsa_window(q, q_pos, kw, vw, k_pos, rel_bias):
    dist = q_pos[:, None] - k_pos[None, :]
    mask = (dist >= 0) & (dist < WINDOW) & (k_pos >= 0)[None, :]
    logits = jnp.einsum('bqgrd,bkgd->bgrqk', q, kw) * (HEAD_DIM ** -0.5) + head_bias(rel_bias, dist)
    p = masked_softmax(logits, mask)
    return jnp.einsum('bgrqk,bkgd->bqgrd', p.astype(vw.dtype), vw)


def combine_nsa(gates, o_c, o_s, o_w):
    B, L = o_c.shape[:2]
    o = gates[:, :, 0, :, :, None] * o_c + gates[:, :, 1, :, :, None] * o_s + gates[:, :, 2, :, :, None] * o_w
    return o.reshape(B, L, NSA_WIDTH)


def nsa_prompt(q, kv, gates, rel_bias, phi):
    pe, w1, w2 = phi
    B, S = q.shape[:2]
    pos = jnp.arange(S)
    kc = compress_blocks(kv[:, :, 0], pe[0], w1[0], w2[0])
    vc = compress_blocks(kv[:, :, 1], pe[1], w1[1], w2[1])
    o_c, importance = nsa_compressed(q, pos, kc, vc, rel_bias)
    idx, valid = select_blocks(importance, pos)
    k_blocks, v_blocks = to_blocks(kv[:, :, 2]), to_blocks(kv[:, :, 3])
    front = ((0, 0), (WINDOW, 0), (0, 0), (0, 0))
    kw, vw = jnp.pad(kv[:, :, 4], front), jnp.pad(kv[:, :, 5], front)

    def query_block(i):
        q0 = i * Q_BLOCK
        qi = lax.dynamic_slice_in_dim(q, q0, Q_BLOCK, axis=1)
        pi = q0 + jnp.arange(Q_BLOCK)
        o_s = nsa_selected(qi, pi, lax.dynamic_slice_in_dim(idx, q0, Q_BLOCK, axis=2),
                           lax.dynamic_slice_in_dim(valid, q0, Q_BLOCK, axis=2), k_blocks, v_blocks, rel_bias)
        k_pos = q0 - WINDOW + jnp.arange(WINDOW + Q_BLOCK)
        o_w = nsa_window(qi, pi, lax.dynamic_slice_in_dim(kw, q0, WINDOW + Q_BLOCK, axis=1),
                         lax.dynamic_slice_in_dim(vw, q0, WINDOW + Q_BLOCK, axis=1), k_pos, rel_bias)
        return o_s, o_w

    o_s, o_w = lax.map(query_block, jnp.arange(S // Q_BLOCK))
    o_s = jnp.moveaxis(o_s, 0, 1).reshape(q.shape)
    o_w = jnp.moveaxis(o_w, 0, 1).reshape(q.shape)
    o = combine_nsa(gates, o_c, o_s, o_w)
    return o, kv[:, :, :NSA_PAGED_KINDS], kv[:, S - min(WINDOW, S):, NSA_PAGED_KINDS:]


def nsa_sample(q, kv, gates, past, win_buf, rel_bias, phi):
    pe, w1, w2 = phi
    B, T = q.shape[:2]
    P = past.shape[1]
    pos = P + jnp.arange(T)
    L = P + T
    lp = -(-L // CMP_BLOCK) * CMP_BLOCK
    full = jnp.concatenate([past, kv[:, :, :NSA_PAGED_KINDS]], axis=1)
    full = jnp.pad(full, ((0, 0), (0, lp - L), (0, 0), (0, 0), (0, 0)))
    kc = compress_blocks(full[:, :, 0], pe[0], w1[0], w2[0])
    vc = compress_blocks(full[:, :, 1], pe[1], w1[1], w2[1])
    o_c, importance = nsa_compressed(q, pos, kc, vc, rel_bias)
    idx, valid = select_blocks(importance, pos)
    o_s = nsa_selected(q, pos, idx, valid, to_blocks(full[:, :, 2]), to_blocks(full[:, :, 3]), rel_bias)
    win = jnp.concatenate([win_buf, kv[:, :, NSA_PAGED_KINDS:]], axis=1)
    wb = win_buf.shape[1]
    k_pos = P - wb + jnp.arange(wb + T)
    o_w = nsa_window(q, pos, win[:, :, 0], win[:, :, 1], k_pos, rel_bias)
    o = combine_nsa(gates, o_c, o_s, o_w)
    return o, kv[:, :, :NSA_PAGED_KINDS], win[:, win.shape[1] - min(WINDOW, L):]


def gmlp_mix(u, v, ws, bs):
    B, L, _ = v.shape
    lp = -(-L // CHUNK) * CHUNK
    vc = jnp.pad(v, ((0, 0), (0, lp - L), (0, 0))).reshape(B, lp // CHUNK, CHUNK, GMLP_GROUPS, GMLP_GROUP_DIM)
    causal = jnp.tril(jnp.ones((CHUNK, CHUNK), ws.dtype))
    mixed = jnp.einsum('gts,bnsgc->bntgc', ws * causal, vc) + bs.T[:, :, None]
    return u * mixed.reshape(B, lp, GMLP_WIDTH)[:, :L]


def pool_mix(xc, buf, pos, pool_w, pool_scale):
    B, L, _ = xc.shape
    ext = jnp.concatenate([buf, xc], axis=1)
    cs = jnp.pad(jnp.cumsum(ext.astype(jnp.float32), axis=1), ((0, 0), (1, 0), (0, 0)))
    hi = cs[:, POOL_BUF + 1:]
    diffs = []
    for gi, w in enumerate(POOL_WINDOWS):
        c0, c1 = gi * POOL_GROUP_DIM, (gi + 1) * POOL_GROUP_DIM
        lo = cs[:, POOL_BUF + 1 - w:POOL_BUF + 1 - w + L, c0:c1]
        count = jnp.minimum(pos + 1, w).astype(jnp.float32)[None, :, None]
        diffs.append((hi[..., c0:c1] - lo) / count - xc[..., c0:c1].astype(jnp.float32))
    d = jnp.stack(diffs, axis=2).astype(xc.dtype)
    y = jnp.einsum('blgc,gce->blge', d, pool_w).reshape(B, L, POOL_WIDTH) * pool_scale
    return y, ext[:, ext.shape[1] - POOL_BUF:]


def moe(h, router_w, router_b, w_gate, w_up, w_down):
    B, L, D = h.shape
    x = h.reshape(B * L, D)
    affinity = jax.nn.sigmoid((x @ router_w).astype(jnp.float32))
    biased = (affinity + router_b.astype(jnp.float32)).reshape(-1, N_EXPERT_GROUPS, EXPERTS_PER_GROUP)
    group_score = jnp.sum(lax.top_k(biased, TOP_K)[0], axis=-1)
    group = jnp.argmax(group_score, axis=-1)
    in_group = jnp.take_along_axis(biased, group[:, None, None], axis=1)[:, 0]
    _, local = lax.top_k(in_group, TOP_K)
    expert = group[:, None] * EXPERTS_PER_GROUP + local
    sel = jnp.take_along_axis(affinity, expert, axis=1)
    weight = sel / jnp.sum(sel, axis=-1, keepdims=True)
    combine = jnp.sum(jax.nn.one_hot(expert, N_EXPERTS, dtype=jnp.float32) * weight[..., None], axis=1).astype(x.dtype)
    act = jax.nn.silu(jnp.einsum('td,edf->tef', x, w_gate)) * jnp.einsum('td,edf->tef', x, w_up)
    y = jnp.einsum('tef,efd->td', act * combine[..., None], w_down)
    return y.reshape(B, L, D)


def token_mixer(h, pos, pool_buf, nsa_fn, w_in, gmlp_norm_g, gmlp_norm_b, gmlp_ws, gmlp_bs,
                pool_w, pool_scale, proj_a, proj_b, proj_c, w_o):
    B, L, _ = h.shape
    q, kv, g_nsa, u, v, xc, g_merge = split_columns(h @ w_in)
    q = q.reshape(B, L, NSA_KV_GROUPS, NSA_REP, HEAD_DIM)
    kv = kv.reshape(B, L, NSA_KV_KINDS, NSA_KV_GROUPS, HEAD_DIM)
    g_nsa = jax.nn.sigmoid(g_nsa).reshape(B, L, 3, NSA_KV_GROUPS, NSA_REP)
    g_merge = jax.nn.sigmoid(g_merge).reshape(B, L, 3, D_MODEL)
    o_b, kv_rows, win_rows = nsa_fn(q, kv, g_nsa)
    v = layer_norm(v, gmlp_norm_g, gmlp_norm_b)
    o_a = gmlp_mix(u, v, gmlp_ws, gmlp_bs)
    o_c, pool_rows = pool_mix(xc, pool_buf, pos, pool_w, pool_scale)
    merged = g_merge[:, :, 0] * (o_a @ proj_a) + g_merge[:, :, 1] * (o_b @ proj_b) + g_merge[:, :, 2] * (o_c @ proj_c)
    return merged @ w_o, kv_rows, win_rows, pool_rows, v


def setup_inputs(seed: int = 0) -> dict:
    key = jax.random.key(seed)
    keys = jax.random.split(key, 48)
    counter = [0]

    def next_key():
        k = keys[counter[0]]
        counter[0] += 1
        return k

    def normal(shape, scale):
        return jax.random.normal(next_key(), shape, jnp.float32) * scale

    n_pages = PAST_LEN // PAGE_SIZE
    n_pool = (DEC_BATCH * n_pages * 5) // 4
    win_buf = min(WINDOW, PAST_LEN)
    kvh = (NSA_KV_GROUPS, HEAD_DIM)
    x_prompt = normal((BATCH, SEQ, D_MODEL), 1.0)
    x_sample = normal((DEC_BATCH, DEC_SEQ, D_MODEL), 1.0)
    cache_nsa_kv = normal((DEPTH, n_pool, PAGE_SIZE, NSA_PAGED_KINDS) + kvh, 1.0)
    state_win_kv = normal((DEPTH, DEC_BATCH, win_buf, NSA_KV_KINDS - NSA_PAGED_KINDS) + kvh, 1.0)
    state_pool = normal((DEPTH, DEC_BATCH, POOL_BUF, POOL_WIDTH), 1.0)
    page_table = jax.random.permutation(next_key(), n_pool)[:DEC_BATCH * n_pages].reshape(DEC_BATCH, n_pages).astype(jnp.int32)
    return {
        'x_prompt': x_prompt,
        'x_sample': x_sample,
        'cache_nsa_kv': cache_nsa_kv,
        'state_win_kv': state_win_kv,
        'state_pool': state_pool,
        'page_table': page_table,
        'c_prompt': normal((BATCH, D_MODEL), 1.0),
        'c_sample': normal((DEC_BATCH, D_MODEL), 1.0),
        'rel_bias': normal((N_BUCKETS, NSA_HEADS), 0.5),
        'router_w': normal((D_MODEL, N_EXPERTS), D_MODEL ** -0.5),
        'router_b': normal((N_EXPERTS,), 0.01),
        'w_in': normal((DEPTH, D_MODEL, N_IN), D_MODEL ** -0.5),
        'nsa_phi_pe': normal((DEPTH, 2, CMP_BLOCK, HEAD_DIM), 0.1),
        'nsa_phi_w1': normal((DEPTH, 2, CMP_BLOCK * HEAD_DIM, PHI_HIDDEN), (CMP_BLOCK * HEAD_DIM) ** -0.5),
        'nsa_phi_w2': normal((DEPTH, 2, PHI_HIDDEN, HEAD_DIM), PHI_HIDDEN ** -0.5),
        'gmlp_norm_g': 1.0 + normal((DEPTH, GMLP_WIDTH), 0.1),
        'gmlp_norm_b': normal((DEPTH, GMLP_WIDTH), 0.02),
        'gmlp_ws': normal((DEPTH, GMLP_GROUPS, CHUNK, CHUNK), CHUNK ** -0.5),
        'gmlp_bs': 1.0 + normal((DEPTH, GMLP_GROUPS, CHUNK), 0.1),
        'pool_w': normal((DEPTH, len(POOL_WINDOWS), POOL_GROUP_DIM, POOL_GROUP_DIM), POOL_GROUP_DIM ** -0.5),
        'pool_scale': 1.0 + normal((DEPTH, POOL_WIDTH), 0.1),
        'proj_a': normal((DEPTH, GMLP_WIDTH, D_MODEL), DN_BETA * GMLP_WIDTH ** -0.5),
        'proj_b': normal((DEPTH, NSA_WIDTH, D_MODEL), DN_BETA * NSA_WIDTH ** -0.5),
        'proj_c': normal((DEPTH, POOL_WIDTH, D_MODEL), DN_BETA * POOL_WIDTH ** -0.5),
        'w_o': normal((DEPTH, D_MODEL, D_MODEL), DN_BETA * D_MODEL ** -0.5),
        'ada_w': normal((DEPTH, D_MODEL, 6 * D_MODEL), 0.5 * D_MODEL ** -0.5),
        'ada_b': normal((DEPTH, 6 * D_MODEL), 0.02),
        'ln1_g': 1.0 + normal((DEPTH, D_MODEL), 0.1),
        'ln1_b': normal((DEPTH, D_MODEL), 0.02),
        'ln2_g': 1.0 + normal((DEPTH, D_MODEL), 0.1),
        'ln2_b': normal((DEPTH, D_MODEL), 0.02),
        'exp_w_gate': normal((DEPTH, N_EXPERTS, D_MODEL, D_EXPERT), D_MODEL ** -0.5),
        'exp_w_up': normal((DEPTH, N_EXPERTS, D_MODEL, D_EXPERT), D_MODEL ** -0.5),
        'exp_w_down': normal((DEPTH, N_EXPERTS, D_EXPERT, D_MODEL), DN_BETA * D_EXPERT ** -0.5),
    }


def reference(x_prompt, x_sample, cache_nsa_kv, state_win_kv, state_pool, page_table, c_prompt, c_sample,
              rel_bias, router_w, router_b, w_in, nsa_phi_pe, nsa_phi_w1, nsa_phi_w2, gmlp_norm_g, gmlp_norm_b,
              gmlp_ws, gmlp_bs, pool_w, pool_scale, proj_a, proj_b, proj_c, w_o, ada_w, ada_b,
              ln1_g, ln1_b, ln2_g, ln2_b, exp_w_gate, exp_w_up, exp_w_down):
    n_seq, n_pages = page_table.shape
    past_len = n_pages * PAGE_SIZE
    pos_p = jnp.arange(x_prompt.shape[1])
    pos_s = past_len + jnp.arange(x_sample.shape[1])

    def layer(l, x, c, pos, pool_buf, nsa_fn):
        sh1, sc1, gt1, sh2, sc2, gt2 = jnp.split((jax.nn.silu(c) @ ada_w[l] + ada_b[l])[:, None, :], 6, axis=-1)
        h = layer_norm(x) * (1 + sc1) + sh1
        mix, kv_rows, win_rows, pool_rows, v_rows = token_mixer(
            h, pos, pool_buf, nsa_fn, w_in[l], gmlp_norm_g[l], gmlp_norm_b[l], gmlp_ws[l], gmlp_bs[l],
            pool_w[l], pool_scale[l], proj_a[l], proj_b[l], proj_c[l], w_o[l])
        x = layer_norm(DN_ALPHA * x + (1 + gt1) * mix, ln1_g[l], ln1_b[l])
        h = layer_norm(x) * (1 + sc2) + sh2
        ffn = moe(h, router_w, router_b, exp_w_gate[l], exp_w_up[l], exp_w_down[l])
        x = layer_norm(DN_ALPHA * x + (1 + gt2) * ffn, ln2_g[l], ln2_b[l])
        return x, kv_rows, win_rows, pool_rows, v_rows

    xp, xs = x_prompt, x_sample
    nsa_p, nsa_s, win_p, win_s, pool_p, pool_s, v_s = [], [], [], [], [], [], []
    for l in range(DEPTH):
        phi = (nsa_phi_pe[l], nsa_phi_w1[l], nsa_phi_w2[l])
        zero_buf = jnp.zeros((xp.shape[0], POOL_BUF, POOL_WIDTH), xp.dtype)
        xp, kv_r, win_r, pool_r, _ = layer(l, xp, c_prompt, pos_p, zero_buf,
                                           functools.partial(nsa_prompt, rel_bias=rel_bias, phi=phi))
        nsa_p.append(kv_r)
        win_p.append(win_r)
        pool_p.append(pool_r)
        past = cache_nsa_kv[l][page_table].reshape(n_seq, past_len, NSA_PAGED_KINDS, NSA_KV_GROUPS, HEAD_DIM)
        xs, kv_r, win_r, pool_r, v_r = layer(l, xs, c_sample, pos_s, state_pool[l],
                                             functools.partial(nsa_sample, past=past, win_buf=state_win_kv[l], rel_bias=rel_bias, phi=phi))
        nsa_s.append(kv_r)
        win_s.append(win_r)
        pool_s.append(pool_r)
        v_s.append(v_r)
    return (xp, xs, jnp.stack(nsa_p), jnp.stack(nsa_s), jnp.stack(win_p), jnp.stack(win_s),
            jnp.stack(pool_p), jnp.stack(pool_s), jnp.stack(v_s))
```

```python
import functools
import math

import jax
import jax.numpy as jnp
from jax import lax
from jax.experimental import pallas as pl
from jax.experimental.pallas import tpu as pltpu

F32 = jnp.float32
BF16 = jnp.bfloat16

D_MODEL = 2048
HEAD_DIM = 128
N_GROUPS = 2
N_REP = 4
N_HEADS = N_GROUPS * N_REP
NSA_WIDTH = N_HEADS * HEAD_DIM
KV_KINDS = 6
PAGED_KINDS = 4
BLOCK = 64
N_SEL = 16
WINDOW = 512
WIN_BLOCKS = WINDOW // BLOCK
PHI_HIDDEN = 2 * HEAD_DIM
PAGE = 128
GMLP_GROUPS = 4
GMLP_WIDTH = 512
CHUNK = 128
POOL_WINDOWS = (2, 4, 8, 16)
POOL_GROUP_DIM = 128
POOL_WIDTH = 512
POOL_BUF = 15
N_BUCKETS = 32
MAX_DISTANCE = 128
N_EXPERTS = 16
N_EXPERT_GROUPS = 4
EXPERTS_PER_GROUP = 4
D_EXPERT = 512
LN_EPS = 1e-5
NEG = -1e30
FORCED_SCORE = 1e9

COL_Q = 0
COL_KV = 1024
COL_U = 2560
COL_V = 3072
COL_XC = 3584
COL_GM = 4096
COL_GN = 10240
N_IN_PAD = 10752
IN_TN = 512

LANES = 128
VMEM_LIMIT = 56 * 1024 * 1024
CHUNK_BLOCKS = 8
CHUNK_KEYS = CHUNK_BLOCKS * BLOCK


def _cparams(sem):
    return pltpu.CompilerParams(dimension_semantics=sem, vmem_limit_bytes=VMEM_LIMIT)


def _ln(x):
    mu = jnp.mean(x, axis=-1, keepdims=True)
    xc = x - mu
    var = jnp.mean(xc * xc, axis=-1, keepdims=True)
    return xc * lax.rsqrt(var + LN_EPS)


def _sigmoid(x):
    return 1.0 / (1.0 + jnp.exp(-x))


def _silu(x):
    return x * _sigmoid(x)


def _gelu_tanh(x):
    return 0.5 * x * (1.0 + jnp.tanh(math.sqrt(2.0 / math.pi) * (x + 0.044715 * (x * x * x))))


def _dot(a, b):
    return jnp.dot(a, b, preferred_element_type=F32)


def _ada_kernel(c_ref, w_ref, b_ref, o_ref):
    c = c_ref[...]
    o_ref[...] = jnp.dot(_silu(c), w_ref[...], preferred_element_type=F32,
                         precision=lax.Precision.HIGHEST) + b_ref[...]


def _ada_mods(c_all, ada_w, ada_b):
    depth = ada_w.shape[0]
    rows = c_all.shape[0]
    n = ada_w.shape[2]
    tn = 1024
    return pl.pallas_call(
        _ada_kernel,
        grid=(depth, n // tn),
        in_specs=[pl.BlockSpec((rows, D_MODEL), lambda l, j: (0, 0)),
                  pl.BlockSpec((None, D_MODEL, tn), lambda l, j: (l, 0, j)),
                  pl.BlockSpec((None, 1, tn), lambda l, j: (l, 0, j))],
        out_specs=pl.BlockSpec((None, rows, tn), lambda l, j: (l, 0, j)),
        out_shape=jax.ShapeDtypeStruct((depth, rows, n), F32),
        compiler_params=_cparams(("parallel", "parallel")),
        name="ada_mods",
    )(c_all, ada_w, ada_b.reshape(depth, 1, n))


def _mod_rows(ref, tm):
    return ref[...] if ref.shape[0] == tm else ref[0:1, :]


def _inproj_kernel(x_ref, sh_ref, sc_ref, w_ref, z_ref, h_scr):
    tm = x_ref.shape[0]

    @pl.when(pl.program_id(1) == 0)
    def _():
        h = _ln(x_ref[...]) * (1.0 + _mod_rows(sc_ref, tm)) + _mod_rows(sh_ref, tm)
        h_scr[...] = h.astype(BF16)

    z_ref[...] = _dot(h_scr[...], w_ref[...])


def _mod_spec(mods, tm, col):
    rows = mods.shape[0]
    if rows == 8:
        return pl.BlockSpec((8, D_MODEL), lambda i, *_: (0, col))
    return pl.BlockSpec((tm, D_MODEL), lambda i, *_: (i, col))


def _in_proj(x, mods, w_pad, tm):
    t = x.shape[0]
    return pl.pallas_call(
        _inproj_kernel,
        grid=(t // tm, N_IN_PAD // IN_TN),
        in_specs=[pl.BlockSpec((tm, D_MODEL), lambda i, j: (i, 0)),
                  _mod_spec(mods, tm, 0), _mod_spec(mods, tm, 1),
                  pl.BlockSpec((D_MODEL, IN_TN), lambda i, j: (0, j))],
        out_specs=pl.BlockSpec((tm, IN_TN), lambda i, j: (i, j)),
        out_shape=jax.ShapeDtypeStruct((t, N_IN_PAD), F32),
        scratch_shapes=[pltpu.VMEM((tm, D_MODEL), BF16)],
        compiler_params=_cparams(("parallel", "arbitrary")),
        name="in_proj",
    )(x, mods, mods, w_pad)


def _compress_pair(x0_ref, x1_ref, nb, pe_ref, w1_ref, w2_ref):
    def step(p, acc):
        pe = pe_ref[pl.ds(p, 1), :]
        a0 = x0_ref[pl.ds(p, nb, stride=BLOCK), :] + pe
        a1 = x1_ref[pl.ds(p, nb, stride=BLOCK), :] + pe
        a = jnp.concatenate([a0, a1], axis=0).astype(BF16)
        w = w1_ref[pl.ds(pl.multiple_of(p * HEAD_DIM, HEAD_DIM), HEAD_DIM), :]
        return acc + _dot(a, w)

    acc = lax.fori_loop(0, BLOCK, step, jnp.zeros((2 * nb, PHI_HIDDEN), F32))
    return _dot(_gelu_tanh(acc).astype(BF16), w2_ref[...])


def _compress_prompt_kernel(x0_ref, x1_ref, pe_ref, w1_ref, w2_ref, o_ref, *, nb):
    o_ref[...] = _compress_pair(x0_ref, x1_ref, nb, pe_ref, w1_ref, w2_ref)


def _compress_prompt(z, pe, w1b, w2b):
    s = z.shape[0]
    nb = s // BLOCK
    cb = COL_KV // HEAD_DIM
    return pl.pallas_call(
        functools.partial(_compress_prompt_kernel, nb=nb),
        grid=(2,),
        in_specs=[pl.BlockSpec((s, HEAD_DIM), lambda k: (0, cb + 2 * k)),
                  pl.BlockSpec((s, HEAD_DIM), lambda k: (0, cb + 2 * k + 1)),
                  pl.BlockSpec((None, BLOCK, HEAD_DIM), lambda k: (k, 0, 0)),
                  pl.BlockSpec((None, BLOCK * HEAD_DIM, PHI_HIDDEN), lambda k: (k, 0, 0)),
                  pl.BlockSpec((None, PHI_HIDDEN, HEAD_DIM), lambda k: (k, 0, 0))],
        out_specs=pl.BlockSpec((None, 2 * nb, HEAD_DIM), lambda k: (k, 0, 0)),
        out_shape=jax.ShapeDtypeStruct((2, 2 * nb, HEAD_DIM), F32),
        compiler_params=_cparams(("parallel",)),
        name="compress_prompt",
    )(z, z, pe, w1b, w2b)


def _compress_sample_kernel(pt_ref, cache_ref, new_ref, pe_ref, w1_ref, w2_ref, o_ref, xbuf, sem,
                            *, n_pages, nbp, t_new):
    b = pl.program_id(0)
    past = n_pages * PAGE

    def page_copy(pg, kg):
        return pltpu.make_async_copy(
            cache_ref.at[pt_ref[b, pg], :, pl.ds(kg * HEAD_DIM, HEAD_DIM)],
            xbuf.at[kg, pl.ds(pl.multiple_of(pg * PAGE, PAGE), PAGE), :],
            sem)

    def start(pg, c):
        for kg in range(4):
            page_copy(pg, kg).start()
        return c

    lax.fori_loop(0, n_pages, start, 0)
    tail = nbp * BLOCK - past
    for kg in range(4):
        xbuf[kg, pl.ds(past, tail), :] = jnp.zeros((tail, HEAD_DIM), F32)
        xbuf[kg, pl.ds(past, t_new), :] = new_ref[:, kg * HEAD_DIM:(kg + 1) * HEAD_DIM]

    def wait(pg, c):
        for kg in range(4):
            page_copy(pg, kg).wait()
        return c

    lax.fori_loop(0, n_pages, wait, 0)
    for kind in range(2):
        o_ref[kind] = _compress_pair(xbuf.at[2 * kind], xbuf.at[2 * kind + 1], nbp,
                                     pe_ref.at[kind], w1_ref.at[kind], w2_ref.at[kind])


def _compress_sample(pt, cache2, new_cmp, pe, w1b, w2b, nbp):
    bsz, n_pages = pt.shape
    t_new = new_cmp.shape[1]
    grid_spec = pltpu.PrefetchScalarGridSpec(
        num_scalar_prefetch=1,
        grid=(bsz,),
        in_specs=[pl.BlockSpec(memory_space=pl.ANY),
                  pl.BlockSpec((None, t_new, 4 * HEAD_DIM), lambda b, pt: (b, 0, 0)),
                  pl.BlockSpec((2, BLOCK, HEAD_DIM), lambda b, pt: (0, 0, 0)),
                  pl.BlockSpec((2, BLOCK * HEAD_DIM, PHI_HIDDEN), lambda b, pt: (0, 0, 0)),
                  pl.BlockSpec((2, PHI_HIDDEN, HEAD_DIM), lambda b, pt: (0, 0, 0))],
        out_specs=pl.BlockSpec((None, 2, 2 * nbp, HEAD_DIM), lambda b, pt: (b, 0, 0, 0)),
        scratch_shapes=[pltpu.VMEM((4, nbp * BLOCK, HEAD_DIM), F32),
                        pltpu.SemaphoreType.DMA(())],
    )
    return pl.pallas_call(
        functools.partial(_compress_sample_kernel, n_pages=n_pages, nbp=nbp, t_new=t_new),
        grid_spec=grid_spec,
        out_shape=jax.ShapeDtypeStruct((bsz, 2, 2 * nbp, HEAD_DIM), F32),
        compiler_params=_cparams(("arbitrary",)),
        name="compress_sample",
    )(pt, cache2, new_cmp, pe, w1b, w2b)


def _select_blocks(score, n_io):
    def body(_, carry):
        work, sel = carry
        mval = jnp.max(work, axis=0, keepdims=True)
        cand = jnp.where(work == mval, n_io, jnp.int32(1 << 30))
        idx = jnp.min(cand, axis=0, keepdims=True)
        pick = n_io == idx
        return jnp.where(pick, -jnp.inf, work), jnp.where(pick, 1.0, sel)

    _, sel = lax.fori_loop(0, N_SEL, body, (score, jnp.zeros(score.shape, F32)))
    return sel


def _nsa_prompt_kernel(qT_ref, kc_ref, vcT_ref, ksel_ref, vselT_ref, kwin_ref, vwinT_ref,
                       st_ref, wt_ref, ct_ref, gate_ref, o_ref, selneg_scr, *, nb):
    i = pl.program_id(1)
    qT = qT_ref[...]
    nl = N_REP * BLOCK

    n_io = lax.broadcasted_iota(jnp.int32, (nb, nl), 0)
    qq = lax.broadcasted_iota(jnp.int32, (nb, nl), 1) % BLOCK
    m = i - n_io
    ct = ct_ref[...]
    bias = jnp.where(m == 0, ct[0:1], jnp.where(m == 1, ct[1:2], jnp.where(m == 2, ct[2:3], ct[3:4])))
    valid = (m >= 1) | ((m == 0) & (qq == BLOCK - 1))
    s = jnp.where(valid, _dot(kc_ref[...], qT) + bias, NEG)
    mx = jnp.max(s, axis=0, keepdims=True)
    p = jnp.where(valid, jnp.exp(s - mx), 0.0)
    l = jnp.sum(p, axis=0, keepdims=True)
    pn = p / jnp.where(l > 0.0, l, 1.0)
    o_cmp = _dot(vcT_ref[...], pn.astype(BF16))

    half = pn[:, :LANES] + pn[:, LANES:]
    imp = half + pltpu.roll(half, BLOCK, 1)
    n_io1 = n_io[:, :LANES]
    forced = (n_io1 == 0) | (n_io1 == i) | (n_io1 == i - 1)
    score = jnp.where(forced, FORCED_SCORE, imp)
    score = jnp.where(n_io1 <= i, score, -1.0)
    sel = _select_blocks(score, n_io1)
    neg1 = jnp.where((sel > 0.5) & (score >= 0.0), 0.0, NEG)
    selneg_scr[...] = jnp.concatenate([neg1, neg1], axis=1)

    c_i = i // CHUNK_BLOCKS

    def branch(k_ref, vT_ref, tile_ref, n_tiles, use_sel, n_chunks):
        def body(cc, carry):
            m_run, l_run, acc = carry
            c = c_i - cc
            sfull = _dot(k_ref[c], qT)
            parts = []
            for b in range(CHUNK_BLOCKS):
                n = c * CHUNK_BLOCKS + b
                mm = i - n
                if use_sel:
                    tidx = jnp.where(mm < 0, n_tiles - 1, jnp.minimum(mm, n_tiles - 2))
                else:
                    tidx = jnp.where((mm < 0) | (mm > n_tiles - 2), n_tiles - 1, mm)
                sb = sfull[b * BLOCK:(b + 1) * BLOCK] + tile_ref[tidx]
                if use_sel:
                    sb = sb + selneg_scr[pl.ds(n, 1), :]
                parts.append(sb)
            s2 = jnp.concatenate(parts, axis=0)
            m_new = jnp.maximum(m_run, jnp.max(s2, axis=0, keepdims=True))
            alpha = jnp.exp(m_run - m_new)
            pp = jnp.exp(s2 - m_new)
            l_new = alpha * l_run + jnp.sum(pp, axis=0, keepdims=True)
            acc = alpha * acc + _dot(vT_ref[c], pp.astype(BF16))
            return m_new, l_new, acc

        init = (jnp.full((1, nl), NEG, F32), jnp.zeros((1, nl), F32), jnp.zeros((HEAD_DIM, nl), F32))
        _, l_fin, acc = lax.fori_loop(0, n_chunks, body, init)
        return acc / l_fin

    o_sel = branch(ksel_ref, vselT_ref, st_ref, st_ref.shape[0], True, c_i + 1)
    o_win = branch(kwin_ref, vwinT_ref, wt_ref, wt_ref.shape[0], False, jnp.minimum(c_i + 1, 2))

    gates = _sigmoid(gate_ref[...])
    oT = gates[0:1] * o_cmp + gates[1:2] * o_sel + gates[2:3] * o_win
    o = oT.T
    for r in range(N_REP):
        o_ref[:, r * HEAD_DIM:(r + 1) * HEAD_DIM] = o[r * BLOCK:(r + 1) * BLOCK].astype(o_ref.dtype)


def _nsa_prompt(qT, kc, vcT, ksel, vselT, kwin, vwinT, st, wt, ct, gates, s):
    nb = s // BLOCK
    nq = s // BLOCK
    nl = N_REP * BLOCK
    nch = s // CHUNK_KEYS

    def whole(arr):
        shp = arr.shape[1:]
        return pl.BlockSpec((None,) + shp, lambda g, i: (g,) + (0,) * len(shp))

    return pl.pallas_call(
        functools.partial(_nsa_prompt_kernel, nb=nb),
        grid=(N_GROUPS, nq),
        in_specs=[pl.BlockSpec((None, HEAD_DIM, nl), lambda g, i: (g, 0, i)),
                  whole(kc), whole(vcT), whole(ksel), whole(vselT), whole(kwin), whole(vwinT),
                  whole(st), whole(wt), whole(ct),
                  pl.BlockSpec((None, None, 8, nl), lambda g, i: (g, i, 0, 0))],
        out_specs=pl.BlockSpec((BLOCK, N_REP * HEAD_DIM), lambda g, i: (i, g)),
        out_shape=jax.ShapeDtypeStruct((s, NSA_WIDTH), BF16),
        scratch_shapes=[pltpu.VMEM((nb, nl), F32)],
        compiler_params=_cparams(("parallel", "arbitrary")),
        name="nsa_prompt",
    )(qT, kc, vcT, ksel, vselT, kwin, vwinT, st, wt, ct, gates)


def _nsa_sample_kernel(pt_ref, cache_ref, qT_ref, kc_ref, vcT_ref, knew_ref, vnewT_ref, kwin_ref, vwinT_ref,
                       cb_ref, nearb_ref, farb_ref, newb_ref, winb_ref, gate_ref, o_ref,
                       kbuf, vbuf, s_scr, selneg_scr, sem_k, sem_v, *, n_pages, t_new):
    b = pl.program_id(0)
    qT = qT_ref[...]
    nbp = kc_ref.shape[0]
    cur = n_pages * (PAGE // BLOCK)

    def k_copy(pg):
        return pltpu.make_async_copy(
            cache_ref.at[pt_ref[b, pg], :, pl.ds(2 * N_GROUPS * HEAD_DIM, N_GROUPS * HEAD_DIM)],
            kbuf.at[pl.ds(pl.multiple_of(pg * PAGE, PAGE), PAGE), :], sem_k)

    def v_copy(pg):
        return pltpu.make_async_copy(
            cache_ref.at[pt_ref[b, pg], :, pl.ds(3 * N_GROUPS * HEAD_DIM, N_GROUPS * HEAD_DIM)],
            vbuf.at[pl.ds(pl.multiple_of(pg * PAGE, PAGE), PAGE), :], sem_v)

    def start(pg, c):
        k_copy(pg).start()
        v_copy(pg).start()
        return c

    lax.fori_loop(0, n_pages, start, 0)

    def softmax_rows(s):
        mx = jnp.max(s, axis=0, keepdims=True)
        p = jnp.where(s > 0.5 * NEG, jnp.exp(s - mx), 0.0)
        l = jnp.sum(p, axis=0, keepdims=True)
        return p / jnp.where(l > 0.0, l, 1.0)

    pn = softmax_rows(_dot(kc_ref[...], qT) + cb_ref[...])
    o_cmp = _dot(vcT_ref[...], pn.astype(BF16))

    tot = pn + pltpu.roll(pn, t_new, 1) + pltpu.roll(pn, 2 * t_new, 1) + pltpu.roll(pn, 3 * t_new, 1)
    lane = lax.broadcasted_iota(jnp.int32, pn.shape, 1)
    top = jnp.where((lane // t_new) % N_REP == N_REP - 1, tot, 0.0)
    imp = top + pltpu.roll(top, LANES - t_new, 1) + pltpu.roll(top, LANES - 2 * t_new, 1) \
        + pltpu.roll(top, LANES - 3 * t_new, 1)
    n_io = lax.broadcasted_iota(jnp.int32, pn.shape, 0)
    forced = (n_io == 0) | (n_io == cur) | (n_io == cur - 1)
    score = jnp.where(forced, FORCED_SCORE, imp)
    score = jnp.where(n_io <= cur, score, -jnp.inf)
    sel = _select_blocks(score, n_io)
    selneg_scr[...] = jnp.where(sel > 0.5, 0.0, NEG)

    pw = softmax_rows(_dot(kwin_ref[...], qT) + winb_ref[...])
    o_win = _dot(vwinT_ref[...], pw.astype(BF16))

    s_new = _dot(knew_ref[...], qT) + newb_ref[...] + selneg_scr[pl.ds(cur, 1), :]

    def wait_k(pg, c):
        k_copy(pg).wait()
        return c

    lax.fori_loop(0, n_pages, wait_k, 0)
    far = farb_ref[0:1, :]

    def logits(pg, mx):
        r0 = pl.multiple_of(pg * PAGE, PAGE)
        s = _dot(kbuf[pl.ds(r0, PAGE), :].astype(BF16), qT)
        s = s + jnp.where(pg == n_pages - 1, nearb_ref[...], far)
        blk = PAGE // BLOCK
        neg = jnp.concatenate(
            [jnp.broadcast_to(selneg_scr[pl.ds(pg * blk + j, 1), :], (BLOCK, LANES)) for j in range(blk)], axis=0)
        s = s + neg
        s_scr[pl.ds(r0, PAGE), :] = s
        return jnp.maximum(mx, jnp.max(s, axis=0, keepdims=True))

    mx = lax.fori_loop(0, n_pages, logits, jnp.max(s_new, axis=0, keepdims=True))

    def wait_v(pg, c):
        v_copy(pg).wait()
        return c

    lax.fori_loop(0, n_pages, wait_v, 0)
    p_new = jnp.exp(s_new - mx)
    acc0 = _dot(vnewT_ref[...], p_new.astype(BF16))
    l0 = jnp.sum(p_new, axis=0, keepdims=True)

    def pv(pg, carry):
        l_run, acc = carry
        r0 = pl.multiple_of(pg * PAGE, PAGE)
        pp = jnp.exp(s_scr[pl.ds(r0, PAGE), :] - mx)
        v = vbuf[pl.ds(r0, PAGE), :].astype(BF16)
        acc = acc + lax.dot_general(v, pp.astype(BF16), (((0,), (0,)), ((), ())), preferred_element_type=F32)
        return l_run + jnp.sum(pp, axis=0, keepdims=True), acc

    l_fin, acc = lax.fori_loop(0, n_pages, pv, (l0, acc0))
    o_sel = acc / l_fin

    gates = _sigmoid(gate_ref[...])
    o_ref[...] = gates[0:1] * o_cmp + gates[1:2] * o_sel + gates[2:3] * o_win


def _nsa_sample(pt, cache2, qT, kc, vcT, knew, vnewT, kwin, vwinT, cb, nearb, farb, newb, winb, gates, t_new):
    bsz, n_pages = pt.shape
    gh = N_GROUPS * HEAD_DIM

    def per_b(arr):
        shp = arr.shape[1:]
        return pl.BlockSpec((None,) + shp, lambda b, pt: (b,) + (0,) * len(shp))

    def const(arr):
        return pl.BlockSpec(arr.shape, lambda b, pt: (0,) * arr.ndim)

    grid_spec = pltpu.PrefetchScalarGridSpec(
        num_scalar_prefetch=1,
        grid=(bsz,),
        in_specs=[pl.BlockSpec(memory_space=pl.ANY),
                  per_b(qT), per_b(kc), per_b(vcT), per_b(knew), per_b(vnewT), per_b(kwin), per_b(vwinT),
                  const(cb), const(nearb), const(farb), const(newb), const(winb), per_b(gates)],
        out_specs=pl.BlockSpec((None, gh, LANES), lambda b, pt: (b, 0, 0)),
        scratch_shapes=[pltpu.VMEM((n_pages * PAGE, gh), F32),
                        pltpu.VMEM((n_pages * PAGE, gh), F32),
                        pltpu.VMEM((n_pages * PAGE, LANES), F32),
                        pltpu.VMEM((kc.shape[1], LANES), F32),
                        pltpu.SemaphoreType.DMA(()),
                        pltpu.SemaphoreType.DMA(())],
    )
    return pl.pallas_call(
        functools.partial(_nsa_sample_kernel, n_pages=n_pages, t_new=t_new),
        grid_spec=grid_spec,
        out_shape=jax.ShapeDtypeStruct((bsz, gh, LANES), F32),
        compiler_params=_cparams(("arbitrary",)),
        name="nsa_sample",
    )(pt, cache2, qT, kc, vcT, knew, vnewT, kwin, vwinT, cb, nearb, farb, newb, winb, gates)


def _gmlp_kernel(u_ref, v_ref, m_ref, bcol_ref, g_ref, b_ref, oa_ref, vn_ref):
    vn = _ln(v_ref[...]) * g_ref[...] + b_ref[...]
    vn_ref[...] = vn
    vb = vn.astype(BF16)
    u = u_ref[...]
    gd = GMLP_WIDTH // GMLP_GROUPS
    for g in range(GMLP_GROUPS):
        mixed = _dot(m_ref[g], vb[:, g * gd:(g + 1) * gd]) + bcol_ref[:, g:g + 1]
        oa_ref[:, g * gd:(g + 1) * gd] = (u[:, g * gd:(g + 1) * gd] * mixed).astype(oa_ref.dtype)


def _gmlp(z, mix_m, bcol, gn_g, gn_b, tc):
    t = z.shape[0]
    return pl.pallas_call(
        _gmlp_kernel,
        grid=(t // tc,),
        in_specs=[pl.BlockSpec((tc, GMLP_WIDTH), lambda i: (i, COL_U // GMLP_WIDTH)),
                  pl.BlockSpec((tc, GMLP_WIDTH), lambda i: (i, COL_V // GMLP_WIDTH)),
                  pl.BlockSpec((GMLP_GROUPS, tc, tc), lambda i: (0, 0, 0)),
                  pl.BlockSpec((tc, LANES), lambda i: (0, 0)),
                  pl.BlockSpec((1, GMLP_WIDTH), lambda i: (0, 0)),
                  pl.BlockSpec((1, GMLP_WIDTH), lambda i: (0, 0))],
        out_specs=[pl.BlockSpec((tc, GMLP_WIDTH), lambda i: (i, 0)),
                   pl.BlockSpec((tc, GMLP_WIDTH), lambda i: (i, 0))],
        out_shape=[jax.ShapeDtypeStruct((t, GMLP_WIDTH), BF16),
                   jax.ShapeDtypeStruct((t, GMLP_WIDTH), F32)],
        compiler_params=_cparams(("parallel",)),
        name="gmlp",
    )(z, z, mix_m, bcol, gn_g, gn_b)


def _pool_kernel(x_ref, halo_ref, pw_ref, ps_ref, o_ref, ext_scr, *, tm, zero_first, pos0, pos_step):
    i = pl.program_id(0)
    halo = halo_ref[...]
    if zero_first:
        halo = jnp.where(i == 0, 0.0, halo)
    ext_scr[0:16, :] = halo
    x = x_ref[...]
    ext_scr[16:16 + tm, :] = x
    pos = pos0 + i * pos_step + lax.broadcasted_iota(jnp.int32, (tm, POOL_GROUP_DIM), 0)
    for gi, w in enumerate(POOL_WINDOWS):
        c0 = gi * POOL_GROUP_DIM
        acc = ext_scr[16:16 + tm, c0:c0 + POOL_GROUP_DIM]
        for k in range(1, w):
            acc = acc + ext_scr[16 - k:16 - k + tm, c0:c0 + POOL_GROUP_DIM]
        count = jnp.minimum(pos + 1, w).astype(F32)
        d = acc / count - x[:, c0:c0 + POOL_GROUP_DIM]
        y = _dot(d.astype(BF16), pw_ref[gi]) * ps_ref[:, c0:c0 + POOL_GROUP_DIM]
        o_ref[:, c0:c0 + POOL_GROUP_DIM] = y.astype(o_ref.dtype)


def _pool(z, halo_src, halo_map, pwb, ps, tm, zero_first, pos0, pos_step):
    t = z.shape[0]
    return pl.pallas_call(
        functools.partial(_pool_kernel, tm=tm, zero_first=zero_first, pos0=pos0, pos_step=pos_step),
        grid=(t // tm,),
        in_specs=[pl.BlockSpec((tm, POOL_WIDTH), lambda i: (i, COL_XC // POOL_WIDTH)),
                  pl.BlockSpec((16, POOL_WIDTH), halo_map),
                  pl.BlockSpec((len(POOL_WINDOWS), POOL_GROUP_DIM, POOL_GROUP_DIM), lambda i: (0, 0, 0)),
                  pl.BlockSpec((1, POOL_WIDTH), lambda i: (0, 0))],
        out_specs=pl.BlockSpec((tm, POOL_WIDTH), lambda i: (i, 0)),
        out_shape=jax.ShapeDtypeStruct((t, POOL_WIDTH), BF16),
        scratch_shapes=[pltpu.VMEM((16 + tm, POOL_WIDTH), F32)],
        compiler_params=_cparams(("parallel",)),
        name="pool",
    )(z, halo_src, pwb, ps)


def _route(logits_t, rb_col):
    aff = _sigmoid(logits_t)
    biased = aff + rb_col
    rows = [biased[e:e + 1] for e in range(N_EXPERTS)]
    top2 = []
    gscore = []
    for g in range(N_EXPERT_GROUPS):
        grp = rows[g * EXPERTS_PER_GROUP:(g + 1) * EXPERTS_PER_GROUP]
        gs = None
        for a in range(EXPERTS_PER_GROUP):
            rank = None
            for c in range(EXPERTS_PER_GROUP):
                if c == a:
                    continue
                ahead = (grp[c] >= grp[a]) if c < a else (grp[c] > grp[a])
                rank = ahead.astype(F32) if rank is None else rank + ahead.astype(F32)
            in2 = rank < 1.5
            top2.append(in2)
            contrib = jnp.where(in2, grp[a], 0.0)
            gs = contrib if gs is None else gs + contrib
        gscore.append(gs)
    out = []
    for g in range(N_EXPERT_GROUPS):
        win = None
        for c in range(N_EXPERT_GROUPS):
            if c == g:
                continue
            ok = (gscore[g] > gscore[c]) if c < g else (gscore[g] >= gscore[c])
            win = ok if win is None else (win & ok)
        for a in range(EXPERTS_PER_GROUP):
            e = g * EXPERTS_PER_GROUP + a
            out.append(jnp.where(win & top2[e], aff[e:e + 1], 0.0))
    selw = jnp.concatenate(out, axis=0)
    return selw / jnp.sum(selw, axis=0, keepdims=True)


def _merge_kernel(x_ref, oa_ref, ob_ref, oc_ref, g0_ref, g1_ref, g2_ref, gt1_ref, sh2_ref, sc2_ref,
                  ln1g_ref, ln1b_ref, pa_ref, pb_ref, pc_ref, wo_ref, rw_ref, rb_ref,
                  x1_ref, h2_ref, comb_ref, *, alpha):
    tm = x_ref.shape[0]

    merged = _sigmoid(g0_ref[...]) * _dot(oa_ref[...], pa_ref[...])
    merged = merged + _sigmoid(g1_ref[...]) * _dot(ob_ref[...], pb_ref[...])
    merged = merged + _sigmoid(g2_ref[...]) * _dot(oc_ref[...], pc_ref[...])
    mix = _dot(merged.astype(BF16), wo_ref[...])
    x1 = _ln(alpha * x_ref[...] + (1.0 + _mod_rows(gt1_ref, tm)) * mix) * ln1g_ref[...] + ln1b_ref[...]
    x1_ref[...] = x1
    h2 = _ln(x1) * (1.0 + _mod_rows(sc2_ref, tm)) + _mod_rows(sh2_ref, tm)
    h2_ref[...] = h2.astype(BF16)
    logits = jnp.dot(h2, rw_ref[...], preferred_element_type=F32, precision=lax.Precision.HIGHEST)
    comb_t = _route(logits.T[:N_EXPERTS], rb_ref[...][:N_EXPERTS, 0:1])
    comb_full = jnp.concatenate([comb_t, jnp.zeros((LANES - N_EXPERTS, tm), F32)], axis=0)
    comb_ref[...] = comb_full.T


def _merge(x, z, oa, ob, oc, mods, ln1g, ln1b, pa, pb, pc, wo, rw, rb, tm, alpha):
    t = x.shape[0]

    def const(arr):
        return pl.BlockSpec(arr.shape, lambda i: (0,) * arr.ndim, pipeline_mode=pl.Buffered(1))

    gcol = COL_GM // D_MODEL
    return pl.pallas_call(
        functools.partial(_merge_kernel, alpha=alpha),
        grid=(t // tm,),
        in_specs=[pl.BlockSpec((tm, D_MODEL), lambda i: (i, 0)),
                  pl.BlockSpec((tm, GMLP_WIDTH), lambda i: (i, 0)),
                  pl.BlockSpec((tm, NSA_WIDTH), lambda i: (i, 0)),
                  pl.BlockSpec((tm, POOL_WIDTH), lambda i: (i, 0)),
                  pl.BlockSpec((tm, D_MODEL), lambda i: (i, gcol)),
                  pl.BlockSpec((tm, D_MODEL), lambda i: (i, gcol + 1)),
                  pl.BlockSpec((tm, D_MODEL), lambda i: (i, gcol + 2)),
                  _mod_spec(mods, tm, 2), _mod_spec(mods, tm, 3), _mod_spec(mods, tm, 4),
                  const(ln1g), const(ln1b), const(pa), const(pb), const(pc), const(wo), const(rw), const(rb)],
        out_specs=[pl.BlockSpec((tm, D_MODEL), lambda i: (i, 0)),
                   pl.BlockSpec((tm, D_MODEL), lambda i: (i, 0)),
                   pl.BlockSpec((tm, LANES), lambda i: (i, 0))],
        out_shape=[jax.ShapeDtypeStruct((t, D_MODEL), F32),
                   jax.ShapeDtypeStruct((t, D_MODEL), BF16),
                   jax.ShapeDtypeStruct((t, LANES), F32)],
        compiler_params=_cparams(("parallel",)),
        name="merge",
    )(x, oa, ob, oc, z, z, z, mods, mods, mods, ln1g, ln1b, pa, pb, pc, wo, rw, rb)


def _moe_kernel(h_ref, comb_ref, wg_ref, wu_ref, wd_ref, x1_ref, gt2_ref, g_ref, b_ref, o_ref, acc_scr, *, alpha):
    e = pl.program_id(1)
    tm = h_ref.shape[0]

    @pl.when(e == 0)
    def _():
        acc_scr[...] = jnp.zeros_like(acc_scr)

    h = h_ref[...]
    comb = comb_ref[...]
    lane = lax.broadcasted_iota(jnp.int32, comb.shape, 1)
    w_col = jnp.sum(jnp.where(lane == e, comb, 0.0), axis=1, keepdims=True)
    act = _silu(_dot(h, wg_ref[...])) * _dot(h, wu_ref[...])
    acc_scr[...] += _dot((act * w_col).astype(BF16), wd_ref[...])

    @pl.when(e == N_EXPERTS - 1)
    def _():
        o_ref[...] = _ln(alpha * x1_ref[...] + (1.0 + _mod_rows(gt2_ref, tm)) * acc_scr[...]) * g_ref[...] + b_ref[...]


def _moe(h2, comb, wg, wu, wd, x1, mods, ln2g, ln2b, tm, alpha):
    t = h2.shape[0]
    return pl.pallas_call(
        functools.partial(_moe_kernel, alpha=alpha),
        grid=(t // tm, N_EXPERTS),
        in_specs=[pl.BlockSpec((tm, D_MODEL), lambda i, e: (i, 0)),
                  pl.BlockSpec((tm, LANES), lambda i, e: (i, 0)),
                  pl.BlockSpec((None, D_MODEL, D_EXPERT), lambda i, e: (e, 0, 0)),
                  pl.BlockSpec((None, D_MODEL, D_EXPERT), lambda i, e: (e, 0, 0)),
                  pl.BlockSpec((None, D_EXPERT, D_MODEL), lambda i, e: (e, 0, 0)),
                  pl.BlockSpec((tm, D_MODEL), lambda i, e: (i, 0)),
                  _mod_spec(mods, tm, 5),
                  pl.BlockSpec((1, D_MODEL), lambda i, e: (0, 0)),
                  pl.BlockSpec((1, D_MODEL), lambda i, e: (0, 0))],
        out_specs=pl.BlockSpec((tm, D_MODEL), lambda i, e: (i, 0)),
        out_shape=jax.ShapeDtypeStruct((t, D_MODEL), F32),
        scratch_shapes=[pltpu.VMEM((tm, D_MODEL), F32)],
        compiler_params=_cparams(("parallel", "arbitrary")),
        name="moe",
    )(h2, comb, wg, wu, wd, x1, mods, ln2g, ln2b)


def _rel_bucket(dist):
    n = jnp.maximum(dist, 0)
    max_exact = N_BUCKETS // 2
    nf = jnp.maximum(n, max_exact).astype(F32)
    large = max_exact + (jnp.log(nf / max_exact) / math.log(MAX_DISTANCE / max_exact)
                         * (N_BUCKETS - max_exact)).astype(jnp.int32)
    return jnp.where(n < max_exact, n, jnp.minimum(large, N_BUCKETS - 1))


def _bias_of(rel_bias, dist):
    return rel_bias[_rel_bucket(dist)].astype(F32)


def _prompt_tiles(rel_bias):
    kk = jnp.arange(BLOCK)[:, None]
    qq = jnp.arange(BLOCK)[None, :]

    def lanes(tile8, g):
        return jnp.concatenate([tile8[:, :, g * N_REP + r] for r in range(N_REP)], axis=1)

    def tile(m, lo, hi):
        d = m * BLOCK + qq - kk
        ok = (d >= lo) & (d < hi)
        return jnp.where(ok[:, :, None], _bias_of(rel_bias, d), NEG)

    big = 1 << 20
    neg_tile = jnp.full((BLOCK, BLOCK, N_HEADS), NEG, F32)
    sel_tiles = [tile(m, 0, big) for m in range(4)] + [neg_tile]
    win_tiles = [tile(m, 0, WINDOW) for m in range(WIN_BLOCKS + 1)] + [neg_tile]
    st = jnp.stack([jnp.stack([lanes(t, g) for t in sel_tiles]) for g in range(N_GROUPS)])
    wt = jnp.stack([jnp.stack([lanes(t, g) for t in win_tiles]) for g in range(N_GROUPS)])
    rows = []
    for m in range(4):
        d = m * BLOCK + jnp.arange(BLOCK) - (BLOCK - 1)
        rows.append(_bias_of(rel_bias, d))
    ct = jnp.stack([jnp.stack([jnp.concatenate([rw[:, g * N_REP + r] for r in range(N_REP)]) for rw in rows]
                              + [jnp.zeros((N_REP * BLOCK,), F32)] * 4) for g in range(N_GROUPS)])
    return st, wt, ct


def _sample_lanes(tile8, t_new):
    rows = tile8.shape[0]
    x = jnp.transpose(tile8, (0, 2, 1)).reshape(rows, N_HEADS * t_new)
    return jnp.pad(x, ((0, 0), (0, LANES - N_HEADS * t_new)))


def _sample_tiles(rel_bias, past, t_new, nbp, win_rows, win_pad):
    tq = past + jnp.arange(t_new)[None, :]

    def tile(k_pos, ok_extra=None, lo=0, hi=1 << 20):
        d = tq - k_pos[:, None]
        ok = (d >= lo) & (d < hi)
        if ok_extra is not None:
            ok = ok & ok_extra[:, None]
        return _sample_lanes(jnp.where(ok[:, :, None], _bias_of(rel_bias, d), NEG), t_new)

    n = jnp.arange(nbp)
    cb = tile(n * BLOCK + BLOCK - 1, n < past // BLOCK + 1)
    nearb = tile(past - PAGE + jnp.arange(PAGE))
    farb = jnp.broadcast_to(_sample_lanes(jnp.broadcast_to(
        rel_bias[N_BUCKETS - 1].astype(F32)[None, None, :], (1, t_new, N_HEADS)), t_new), (8, LANES))
    kn = jnp.arange(PAGE)
    newb = tile(past + kn, kn < t_new)
    wb = min(WINDOW, past)
    j = jnp.arange(win_pad)
    k_pos = past - wb + j
    winb = tile(k_pos, (j < win_rows) & (k_pos >= 0), 0, WINDOW)
    return cb, nearb, farb, newb, winb


def _mix_mats(ws, bs, tc, rep):
    n = tc // rep
    causal = jnp.tril(jnp.ones((CHUNK, CHUNK), F32))
    wc = (ws * causal)[:, :n, :n]
    eye = jnp.eye(rep, dtype=F32)
    m = jnp.einsum('ab,gts->gatbs', eye, wc).reshape(GMLP_GROUPS, tc, tc)
    bcol = jnp.tile(bs[:, :n].T, (rep, 1))
    return m.astype(BF16), jnp.pad(bcol, ((0, 0), (0, LANES - GMLP_GROUPS)))


def kernel(x_prompt, x_sample, cache_nsa_kv, state_win_kv, state_pool, page_table, c_prompt, c_sample, rel_bias, router_w, router_b, w_in, nsa_phi_pe, nsa_phi_w1, nsa_phi_w2, gmlp_norm_g, gmlp_norm_b, gmlp_ws, gmlp_bs, pool_w, pool_scale, proj_a, proj_b, proj_c, w_o, ada_w, ada_b, ln1_g, ln1_b, ln2_g, ln2_b, exp_w_gate, exp_w_up, exp_w_down):
    depth = w_in.shape[0]
    alpha = (2 * depth) ** 0.25
    s = x_prompt.shape[1]
    bsz, t_new = x_sample.shape[0], x_sample.shape[1]
    n_pool = cache_nsa_kv.shape[1]
    n_pages = page_table.shape[1]
    past = n_pages * PAGE
    ts = bsz * t_new
    assert x_prompt.shape[0] == 1 and s % CHUNK_KEYS == 0 and past % PAGE == 0
    assert N_HEADS * t_new <= LANES and t_new <= BLOCK and (past // BLOCK) >= N_SEL
    nb_s = past // BLOCK + 1
    nbp = -(-nb_s // 8) * 8
    scale = HEAD_DIM ** -0.5

    n_c = 1 + bsz
    c_all = jnp.pad(jnp.concatenate([c_prompt, c_sample], axis=0), ((0, -n_c % 8), (0, 0)))
    mods_all = _ada_mods(c_all, ada_w, ada_b)

    cache2 = cache_nsa_kv.reshape(depth * n_pool, PAGE, PAGED_KINDS * N_GROUPS * HEAD_DIM)
    st, wt, ct = _prompt_tiles(rel_bias)
    win_rows = min(WINDOW, past) + t_new
    win_pad = -(-win_rows // LANES) * LANES
    cb_s, nearb_s, farb_s, newb_s, winb_s = _sample_tiles(rel_bias, past, t_new, 256, win_rows, win_pad)
    rw = jnp.pad(router_w, ((0, 0), (0, LANES - N_EXPERTS)))
    rb = jnp.pad(router_b.reshape(N_EXPERTS, 1), ((0, LANES - N_EXPERTS), (0, LANES - 1)))

    xp = x_prompt.reshape(s, D_MODEL)
    xs = x_sample.reshape(ts, D_MODEL)
    outs = {k: [] for k in ("nsa_p", "nsa_s", "win_p", "win_s", "pool_p", "pool_s", "v_s")}

    for l in range(depth):
        w = w_in[l]
        w_pad = jnp.concatenate(
            [w[:, 0:2560], w[:, 2584:10264], w[:, 2560:2584],
             jnp.zeros((D_MODEL, N_IN_PAD - 10264), F32)], axis=1).astype(BF16)
        pe = nsa_phi_pe[l]
        w1b = nsa_phi_w1[l].astype(BF16)
        w2b = nsa_phi_w2[l].astype(BF16)
        pab, pbb, pcb, wob = (a[l].astype(BF16) for a in (proj_a, proj_b, proj_c, w_o))
        wg, wu, wd = (a[l].astype(BF16) for a in (exp_w_gate, exp_w_up, exp_w_down))
        pwb = pool_w[l].astype(BF16)
        ps = pool_scale[l].reshape(1, POOL_WIDTH)
        gng = gmlp_norm_g[l].reshape(1, GMLP_WIDTH)
        gnb = gmlp_norm_b[l].reshape(1, GMLP_WIDTH)
        ln1g, ln1b, ln2g, ln2b = (a[l].reshape(1, D_MODEL) for a in (ln1_g, ln1_b, ln2_g, ln2_b))
        mods_p = jnp.broadcast_to(mods_all[l, 0:1], (8, 6 * D_MODEL))
        mods_s = jnp.repeat(mods_all[l, 1:1 + bsz], t_new, axis=0)

        z = _in_proj(xp, mods_p, w_pad, 1024)
        kv = z[:, COL_KV:COL_KV + 1536].reshape(s, KV_KINDS, N_GROUPS, HEAD_DIM)
        outs["nsa_p"].append(kv[:, :PAGED_KINDS].reshape(1, s, PAGED_KINDS, N_GROUPS, HEAD_DIM))
        outs["win_p"].append(kv[s - min(WINDOW, s):, PAGED_KINDS:].reshape(1, min(WINDOW, s), 2, N_GROUPS, HEAD_DIM))
        outs["pool_p"].append(z[s - POOL_BUF:, COL_XC:COL_XC + POOL_WIDTH].reshape(1, POOL_BUF, POOL_WIDTH))

        nb = s // BLOCK
        cmp = _compress_prompt(z, pe, w1b, w2b).reshape(2, N_GROUPS, nb, HEAD_DIM)
        kc = cmp[0].astype(BF16)
        vcT = jnp.transpose(cmp[1], (0, 2, 1)).astype(BF16)
        nch = s // CHUNK_KEYS

        def k_chunks(kind):
            return jnp.transpose(kv[:, kind], (1, 0, 2)).reshape(N_GROUPS, nch, CHUNK_KEYS, HEAD_DIM).astype(BF16)

        def vT_chunks(kind):
            a = jnp.transpose(kv[:, kind], (1, 0, 2)).reshape(N_GROUPS, nch, CHUNK_KEYS, HEAD_DIM)
            return jnp.transpose(a, (0, 1, 3, 2)).astype(BF16)

        q = z[:, COL_Q:COL_Q + NSA_WIDTH].reshape(nb, BLOCK, N_GROUPS, N_REP, HEAD_DIM) * scale
        qT = jnp.transpose(q, (2, 4, 0, 3, 1)).reshape(N_GROUPS, HEAD_DIM, nb * N_REP * BLOCK).astype(BF16)
        gn = z[:, COL_GN:COL_GN + 3 * N_HEADS].reshape(nb, BLOCK, 3, N_GROUPS, N_REP)
        gates = jnp.transpose(gn, (3, 0, 2, 4, 1)).reshape(N_GROUPS, nb, 3, N_REP * BLOCK)
        gates = jnp.pad(gates, ((0, 0), (0, 0), (0, 5), (0, 0)))
        ob = _nsa_prompt(qT, kc, vcT, k_chunks(2), vT_chunks(3), k_chunks(4), vT_chunks(5), st, wt, ct, gates, s)

        mm, bcol = _mix_mats(gmlp_ws[l], gmlp_bs[l], CHUNK, 1)
        oa, _ = _gmlp(z, mm, bcol, gng, gnb, CHUNK)
        tp = 512
        cxb = COL_XC // POOL_WIDTH
        oc = _pool(z, z, lambda i: (jnp.maximum(i * (tp // 16) - 1, 0), cxb), pwb, ps, tp, True, 0, tp)
        x1, h2, comb = _merge(xp, z, oa, ob, oc, mods_p, ln1g, ln1b, pab, pbb, pcb, wob, rw, rb, 256, alpha)
        xp = _moe(h2, comb, wg, wu, wd, x1, mods_p, ln2g, ln2b, 512, alpha)

        zs = _in_proj(xs, mods_s, w_pad, ts)
        kvs = zs[:, COL_KV:COL_KV + 1536].reshape(bsz, t_new, KV_KINDS, N_GROUPS, HEAD_DIM)
        outs["nsa_s"].append(kvs[:, :, :PAGED_KINDS])
        win_full = jnp.concatenate([state_win_kv[l], kvs[:, :, PAGED_KINDS:]], axis=1)
        outs["win_s"].append(win_full[:, win_full.shape[1] - min(WINDOW, past + t_new):])
        xcs = zs[:, COL_XC:COL_XC + POOL_WIDTH].reshape(bsz, t_new, POOL_WIDTH)
        ext = jnp.concatenate([state_pool[l], xcs], axis=1)
        outs["pool_s"].append(ext[:, ext.shape[1] - POOL_BUF:])

        pt = page_table + l * n_pool
        new_cmp = kvs[:, :, 0:2].reshape(bsz, t_new, 2 * N_GROUPS * HEAD_DIM)
        cmp_s = _compress_sample(pt, cache2, new_cmp, pe, w1b, w2b, nbp)
        cmp_s = cmp_s.reshape(bsz, 2, N_GROUPS, nbp, HEAD_DIM)
        cmp_s = jnp.pad(cmp_s, ((0, 0), (0, 0), (0, 0), (0, 256 - nbp), (0, 0)))
        gh = N_GROUPS * HEAD_DIM
        kc_s = jnp.transpose(cmp_s[:, 0], (0, 2, 1, 3)).reshape(bsz, 256, gh).astype(BF16)
        vcT_s = jnp.transpose(cmp_s[:, 1], (0, 1, 3, 2)).reshape(bsz, gh, 256).astype(BF16)
        knew = jnp.pad(kvs[:, :, 2].reshape(bsz, t_new, gh), ((0, 0), (0, PAGE - t_new), (0, 0))).astype(BF16)
        vnewT = jnp.transpose(jnp.pad(kvs[:, :, 3].reshape(bsz, t_new, gh), ((0, 0), (0, PAGE - t_new), (0, 0))),
                              (0, 2, 1)).astype(BF16)
        wf = jnp.pad(win_full.reshape(bsz, win_rows, 2, gh), ((0, 0), (0, win_pad - win_rows), (0, 0), (0, 0)))
        kwin_s = wf[:, :, 0].astype(BF16)
        vwinT_s = jnp.transpose(wf[:, :, 1], (0, 2, 1)).astype(BF16)
        qs = zs[:, COL_Q:COL_Q + NSA_WIDTH].reshape(bsz, t_new, N_GROUPS, N_REP, HEAD_DIM) * scale
        qs = jnp.transpose(qs, (0, 2, 4, 3, 1)).reshape(bsz, N_GROUPS, HEAD_DIM, N_REP * t_new)
        qT_s = jnp.zeros((bsz, gh, LANES), F32)
        for g in range(N_GROUPS):
            qT_s = qT_s.at[:, g * HEAD_DIM:(g + 1) * HEAD_DIM,
                           g * N_REP * t_new:(g + 1) * N_REP * t_new].set(qs[:, g])
        qT_s = qT_s.astype(BF16)
        gns = zs[:, COL_GN:COL_GN + 3 * N_HEADS].reshape(bsz, t_new, 3, N_HEADS)
        gates_s = jnp.transpose(gns, (0, 2, 3, 1)).reshape(bsz, 3, N_HEADS * t_new)
        gates_s = jnp.pad(gates_s, ((0, 0), (0, 5), (0, LANES - N_HEADS * t_new)))
        oT = _nsa_sample(pt, cache2, qT_s, kc_s, vcT_s, knew, vnewT, kwin_s, vwinT_s,
                         cb_s, nearb_s, farb_s, newb_s, winb_s, gates_s, t_new)
        o4 = oT[:, :, :N_HEADS * t_new].reshape(bsz, N_GROUPS, HEAD_DIM, N_GROUPS, N_REP, t_new)
        o4 = jnp.stack([o4[:, g, :, g] for g in range(N_GROUPS)], axis=1)
        ob_s = jnp.transpose(o4, (0, 4, 1, 3, 2)).reshape(ts, NSA_WIDTH).astype(BF16)

        mm_s, bcol_s = _mix_mats(gmlp_ws[l], gmlp_bs[l], ts, bsz)
        oa_s, vn_s = _gmlp(zs, mm_s, bcol_s, gng, gnb, ts)
        outs["v_s"].append(vn_s.reshape(bsz, t_new, GMLP_WIDTH))
        halo = jnp.pad(state_pool[l], ((0, 0), (16 - POOL_BUF, 0), (0, 0))).reshape(bsz * 16, POOL_WIDTH)
        oc_s = _pool(zs, halo, lambda i: (i, 0), pwb, ps, t_new, False, past, 0)
        x1s, h2s, comb_s = _merge(xs, zs, oa_s, ob_s, oc_s, mods_s, ln1g, ln1b, pab, pbb, pcb, wob, rw, rb, ts, alpha)
        xs = _moe(h2s, comb_s, wg, wu, wd, x1s, mods_s, ln2g, ln2b, ts, alpha)

    return (xp.reshape(1, s, D_MODEL), xs.reshape(bsz, t_new, D_MODEL),
            jnp.stack(outs["nsa_p"]), jnp.stack(outs["nsa_s"]), jnp.stack(outs["win_p"]), jnp.stack(outs["win_s"]),
            jnp.stack(outs["pool_p"]), jnp.stack(outs["pool_s"]), jnp.stack(outs["v_s"]))
```

```python
import functools
import math

import jax
import jax.numpy as jnp
from jax import lax
from jax.experimental import pallas as pl
from jax.experimental.pallas import tpu as pltpu

F32 = jnp.float32
BF16 = jnp.bfloat16

D_MODEL = 2048
HEAD_DIM = 128
N_GROUPS = 2
N_REP = 4
N_HEADS = N_GROUPS * N_REP
NSA_WIDTH = N_HEADS * HEAD_DIM
KV_KINDS = 6
PAGED_KINDS = 4
BLOCK = 64
N_SEL = 16
WINDOW = 512
WIN_BLOCKS = WINDOW // BLOCK
PHI_HIDDEN = 2 * HEAD_DIM
PAGE = 128
GMLP_GROUPS = 4
GMLP_WIDTH = 512
CHUNK = 128
POOL_WINDOWS = (2, 4, 8, 16)
POOL_GROUP_DIM = 128
POOL_WIDTH = 512
POOL_BUF = 15
N_BUCKETS = 32
MAX_DISTANCE = 128
N_EXPERTS = 16
N_EXPERT_GROUPS = 4
EXPERTS_PER_GROUP = 4
D_EXPERT = 512
LN_EPS = 1e-5
NEG = -1e30
FORCED_SCORE = 1e9

COL_Q = 0
COL_KV = 1024
COL_U = 2560
COL_V = 3072
COL_XC = 3584
COL_GM = 4096
COL_GN = 10240
N_IN_PAD = 10752
IN_TN = 512

LANES = 128
VMEM_LIMIT = 56 * 1024 * 1024
CHUNK_BLOCKS = 8
CHUNK_KEYS = CHUNK_BLOCKS * BLOCK
XC_PITCH = BLOCK + 8
SAMPLE_RING_PAGES = 16
SAMPLE_BLOCKS_PAD = 256


def _cparams(sem):
    return pltpu.CompilerParams(dimension_semantics=sem, vmem_limit_bytes=VMEM_LIMIT)


def _ln(x):
    mu = jnp.mean(x, axis=-1, keepdims=True)
    xc = x - mu
    var = jnp.mean(xc * xc, axis=-1, keepdims=True)
    return xc * lax.rsqrt(var + LN_EPS)


def _sigmoid(x):
    return 1.0 / (1.0 + jnp.exp(-x))


def _silu(x):
    return x * _sigmoid(x)


def _gelu_tanh(x):
    return 0.5 * x * (1.0 + jnp.tanh(math.sqrt(2.0 / math.pi) * (x + 0.044715 * (x * x * x))))


def _dot(a, b):
    return jnp.dot(a, b, preferred_element_type=F32)


QK_SCALE = HEAD_DIM ** -0.5


def _qk(k, qT):
    return _dot(k, qT) * QK_SCALE


def _ada_kernel(c_ref, w_ref, b_ref, o_ref):
    c = c_ref[...]
    o_ref[...] = _dot(_silu(c).astype(BF16), w_ref[...].astype(BF16)) + b_ref[...]


def _ada_mods(c_all, ada_w, ada_b):
    depth = ada_w.shape[0]
    rows = c_all.shape[0]
    n = ada_w.shape[2]
    tn = 1024
    return pl.pallas_call(
        _ada_kernel,
        grid=(depth, n // tn),
        in_specs=[pl.BlockSpec((rows, D_MODEL), lambda l, j: (0, 0)),
                  pl.BlockSpec((None, D_MODEL, tn), lambda l, j: (l, 0, j)),
                  pl.BlockSpec((None, 1, tn), lambda l, j: (l, 0, j))],
        out_specs=pl.BlockSpec((None, rows, tn), lambda l, j: (l, 0, j)),
        out_shape=jax.ShapeDtypeStruct((depth, rows, n), F32),
        compiler_params=_cparams(("parallel", "parallel")),
        name="ada_mods",
    )(c_all, ada_w, ada_b.reshape(depth, 1, n))


def _mod_rows(ref, tm):
    return ref[...] if ref.shape[0] == tm else ref[0:1, :]


def _inproj_kernel(x_ref, sh_ref, sc_ref, w_ref, z_ref, h_scr):
    tm = x_ref.shape[0]

    @pl.when(pl.program_id(1) == 0)
    def _():
        h = _ln(x_ref[...]) * (1.0 + _mod_rows(sc_ref, tm)) + _mod_rows(sh_ref, tm)
        h_scr[...] = h.astype(BF16)

    z_ref[...] = _dot(h_scr[...], w_ref[...])


def _mod_spec(mods, tm, col):
    rows = mods.shape[0]
    if rows == 8:
        return pl.BlockSpec((8, D_MODEL), lambda i, *_: (0, col))
    return pl.BlockSpec((tm, D_MODEL), lambda i, *_: (i, col))


def _in_proj(x, mods, w_pad, tm):
    t = x.shape[0]
    return pl.pallas_call(
        _inproj_kernel,
        grid=(t // tm, N_IN_PAD // IN_TN),
        in_specs=[pl.BlockSpec((tm, D_MODEL), lambda i, j: (i, 0)),
                  _mod_spec(mods, tm, 0), _mod_spec(mods, tm, 1),
                  pl.BlockSpec((D_MODEL, IN_TN), lambda i, j: (0, j))],
        out_specs=pl.BlockSpec((tm, IN_TN), lambda i, j: (i, j)),
        out_shape=jax.ShapeDtypeStruct((t, N_IN_PAD), F32),
        scratch_shapes=[pltpu.VMEM((tm, D_MODEL), BF16)],
        compiler_params=_cparams(("parallel", "arbitrary")),
        name="in_proj",
    )(x, mods, mods, w_pad)


def _compress_pair(x0_ref, x1_ref, nb, pe_ref, w1_ref, w2_ref):
    def step(p, acc):
        pe = pe_ref[pl.ds(p, 1), :]
        a0 = x0_ref[pl.ds(p, nb, stride=BLOCK), :] + pe
        a1 = x1_ref[pl.ds(p, nb, stride=BLOCK), :] + pe
        a = jnp.concatenate([a0, a1], axis=0).astype(BF16)
        w = w1_ref[pl.ds(pl.multiple_of(p * HEAD_DIM, HEAD_DIM), HEAD_DIM), :]
        return acc + _dot(a, w)

    acc = lax.fori_loop(0, BLOCK, step, jnp.zeros((2 * nb, PHI_HIDDEN), F32))
    return _dot(_gelu_tanh(acc).astype(BF16), w2_ref[...])


def _compress_prompt_kernel(x0_ref, x1_ref, pe_ref, w1_ref, w2_ref, o_ref, *, nb):
    o_ref[...] = _compress_pair(x0_ref, x1_ref, nb, pe_ref, w1_ref, w2_ref)


def _compress_prompt(z, pe, w1b, w2b):
    s = z.shape[0]
    nb = s // BLOCK
    cb = COL_KV // HEAD_DIM
    return pl.pallas_call(
        functools.partial(_compress_prompt_kernel, nb=nb),
        grid=(2,),
        in_specs=[pl.BlockSpec((s, HEAD_DIM), lambda k: (0, cb + 2 * k)),
                  pl.BlockSpec((s, HEAD_DIM), lambda k: (0, cb + 2 * k + 1)),
                  pl.BlockSpec((None, BLOCK, HEAD_DIM), lambda k: (k, 0, 0)),
                  pl.BlockSpec((None, BLOCK * HEAD_DIM, PHI_HIDDEN), lambda k: (k, 0, 0)),
                  pl.BlockSpec((None, PHI_HIDDEN, HEAD_DIM), lambda k: (k, 0, 0))],
        out_specs=pl.BlockSpec((None, 2 * nb, HEAD_DIM), lambda k: (k, 0, 0)),
        out_shape=jax.ShapeDtypeStruct((2, 2 * nb, HEAD_DIM), F32),
        compiler_params=_cparams(("parallel",)),
        name="compress_prompt",
    )(z, z, pe, w1b, w2b)


def _select_blocks(score, n_io):
    def body(_, carry):
        work, sel = carry
        mval = jnp.max(work, axis=0, keepdims=True)
        cand = jnp.where(work == mval, n_io, jnp.int32(1 << 30))
        idx = jnp.min(cand, axis=0, keepdims=True)
        pick = n_io == idx
        return jnp.where(pick, -jnp.inf, work), jnp.where(pick, 1.0, sel)

    _, sel = lax.fori_loop(0, N_SEL, body, (score, jnp.zeros(score.shape, F32)))
    return sel


def _nsa_prompt_kernel(qT_ref, kc_ref, vcT_ref, ksel_ref, vselT_ref, kwin_ref, vwinT_ref,
                       st_ref, wt_ref, ct_ref, gate_ref, o_ref, selneg_scr, *, nb):
    i = pl.program_id(1)
    qT = qT_ref[...]
    nl = N_REP * BLOCK

    n_io = lax.broadcasted_iota(jnp.int32, (nb, nl), 0)
    qq = lax.broadcasted_iota(jnp.int32, (nb, nl), 1) % BLOCK
    m = i - n_io
    ct = ct_ref[...]
    bias = jnp.where(m == 0, ct[0:1], jnp.where(m == 1, ct[1:2], jnp.where(m == 2, ct[2:3], ct[3:4])))
    valid = (m >= 1) | ((m == 0) & (qq == BLOCK - 1))
    s = jnp.where(valid, _qk(kc_ref[...], qT) + bias, NEG)
    mx = jnp.max(s, axis=0, keepdims=True)
    p = jnp.where(valid, jnp.exp(s - mx), 0.0)
    l = jnp.sum(p, axis=0, keepdims=True)
    pn = p / jnp.where(l > 0.0, l, 1.0)
    o_cmp = _dot(vcT_ref[...], pn.astype(BF16))

    half = pn[:, :LANES] + pn[:, LANES:]
    imp = half + pltpu.roll(half, BLOCK, 1)
    n_io1 = n_io[:, :LANES]
    forced = (n_io1 == 0) | (n_io1 == i) | (n_io1 == i - 1)
    score = jnp.where(forced, FORCED_SCORE, imp)
    score = jnp.where(n_io1 <= i, score, -1.0)
    sel = _select_blocks(score, n_io1)
    neg1 = jnp.where((sel > 0.5) & (score >= 0.0), 0.0, NEG)
    selneg_scr[...] = jnp.concatenate([neg1, neg1], axis=1)

    c_i = i // CHUNK_BLOCKS

    def branch(k_ref, vT_ref, tile_ref, n_tiles, use_sel, n_chunks):
        def body(cc, carry):
            m_run, l_run, acc = carry
            c = c_i - cc
            sfull = _qk(k_ref[c], qT)
            parts = []
            for b in range(CHUNK_BLOCKS):
                n = c * CHUNK_BLOCKS + b
                mm = i - n
                if use_sel:
                    tidx = jnp.where(mm < 0, n_tiles - 1, jnp.minimum(mm, n_tiles - 2))
                else:
                    tidx = jnp.where((mm < 0) | (mm > n_tiles - 2), n_tiles - 1, mm)
                sb = sfull[b * BLOCK:(b + 1) * BLOCK] + tile_ref[tidx]
                if use_sel:
                    sb = sb + selneg_scr[pl.ds(n, 1), :]
                parts.append(sb)
            s2 = jnp.concatenate(parts, axis=0)
            m_new = jnp.maximum(m_run, jnp.max(s2, axis=0, keepdims=True))
            alpha = jnp.exp(m_run - m_new)
            pp = jnp.exp(s2 - m_new)
            l_new = alpha * l_run + jnp.sum(pp, axis=0, keepdims=True)
            acc = alpha * acc + _dot(vT_ref[c], pp.astype(BF16))
            return m_new, l_new, acc

        init = (jnp.full((1, nl), NEG, F32), jnp.zeros((1, nl), F32), jnp.zeros((HEAD_DIM, nl), F32))
        _, l_fin, acc = lax.fori_loop(0, n_chunks, body, init)
        return acc / l_fin

    o_sel = branch(ksel_ref, vselT_ref, st_ref, st_ref.shape[0], True, c_i + 1)
    o_win = branch(kwin_ref, vwinT_ref, wt_ref, wt_ref.shape[0], False, jnp.minimum(c_i + 1, 2))

    gates = _sigmoid(gate_ref[...])
    oT = gates[0:1] * o_cmp + gates[1:2] * o_sel + gates[2:3] * o_win
    o = oT.T
    for r in range(N_REP):
        o_ref[:, r * HEAD_DIM:(r + 1) * HEAD_DIM] = o[r * BLOCK:(r + 1) * BLOCK].astype(o_ref.dtype)


def _nsa_prompt(qT, kc, vcT, ksel, vselT, kwin, vwinT, st, wt, ct, gates, s):
    nb = s // BLOCK
    nq = s // BLOCK
    nl = N_REP * BLOCK
    nch = s // CHUNK_KEYS

    def whole(arr):
        shp = arr.shape[1:]
        return pl.BlockSpec((None,) + shp, lambda g, i: (g,) + (0,) * len(shp))

    return pl.pallas_call(
        functools.partial(_nsa_prompt_kernel, nb=nb),
        grid=(N_GROUPS, nq),
        in_specs=[pl.BlockSpec((None, HEAD_DIM, nl), lambda g, i: (g, 0, i)),
                  whole(kc), whole(vcT), whole(ksel), whole(vselT), whole(kwin), whole(vwinT),
                  whole(st), whole(wt), whole(ct),
                  pl.BlockSpec((None, None, 8, nl), lambda g, i: (g, i, 0, 0))],
        out_specs=pl.BlockSpec((BLOCK, N_REP * HEAD_DIM), lambda g, i: (i, g)),
        out_shape=jax.ShapeDtypeStruct((s, NSA_WIDTH), BF16),
        scratch_shapes=[pltpu.VMEM((nb, nl), F32)],
        compiler_params=_cparams(("parallel", "arbitrary")),
        name="nsa_prompt",
    )(qT, kc, vcT, ksel, vselT, kwin, vwinT, st, wt, ct, gates)


def _sample_nsa_kernel(pt_ref, cache_ref, newc_ref, qT_ref, knew_ref, vnew_ref, kwin_ref, vwin_ref,
                       pe_ref, w1_ref, w2_ref, cb_ref, nearb_ref, farb_ref, newb_ref, winb_ref, gate_ref, o_ref,
                       ring, xc, kcs, vsel, s_scr, selneg_scr, sems, *, n_pages, t_new, nbp, n_ring):
    b = pl.program_id(0)
    total = pl.num_programs(0) * n_pages
    past = n_pages * PAGE
    cur = past // BLOCK
    kinds_per_row = PAGED_KINDS * N_GROUPS
    qT = qT_ref[...]

    def page_dma(t, slot):
        return pltpu.make_async_copy(cache_ref.at[pt_ref[t // n_pages, t % n_pages]], ring.at[slot], sems.at[slot])

    @pl.when(b == 0)
    def _():
        for t in range(n_ring):
            page_dma(t, t).start()

    far = farb_ref[0:1, :]

    def page_step(j, c):
        t = b * n_pages + j
        slot = t % n_ring
        page_dma(t, slot).wait()
        buf = ring.at[slot]

        def rows(kg):
            return buf[pl.ds(kg, PAGE, stride=kinds_per_row), :]

        r0 = pl.multiple_of(j * PAGE, PAGE)
        base = pl.multiple_of(j * (PAGE // BLOCK) * XC_PITCH, 8)
        for kg in range(2 * N_GROUPS):
            x = rows(kg)
            for h in range(PAGE // BLOCK):
                xc[kg, pl.ds(base + h * XC_PITCH, BLOCK), :] = x[h * BLOCK:(h + 1) * BLOCK]
        k = jnp.concatenate([rows(4), rows(5)], axis=1).astype(BF16)
        s_scr[pl.ds(r0, PAGE), :] = _qk(k, qT) + jnp.where(j == n_pages - 1, nearb_ref[...], far)
        vsel[pl.ds(r0, PAGE), :] = jnp.concatenate([rows(6), rows(7)], axis=1).astype(BF16)

        @pl.when(t + n_ring < total)
        def _():
            page_dma(t + n_ring, slot).start()

        return c

    lax.fori_loop(0, n_pages, page_step, 0)

    tail = (nbp - cur) * XC_PITCH
    for kg in range(2 * N_GROUPS):
        xc[kg, pl.ds(cur * XC_PITCH, tail), :] = jnp.zeros((tail, HEAD_DIM), F32)
        xc[kg, pl.ds(cur * XC_PITCH, t_new), :] = newc_ref[:, kg * HEAD_DIM:(kg + 1) * HEAD_DIM]
    s_scr[pl.ds(past, PAGE), :] = _qk(knew_ref[...], qT) + newb_ref[...]
    vsel[pl.ds(past, PAGE), :] = vnew_ref[...]

    for kind in range(2):
        def step(pp, acc):
            p = 2 * pp
            pe0 = pe_ref[kind, pl.ds(p, 1), :]
            pe1 = pe_ref[kind, pl.ds(p + 1, 1), :]
            parts = []
            for g in range(N_GROUPS):
                x0 = xc[2 * kind + g, pl.ds(p, nbp, stride=XC_PITCH), :] + pe0
                x1 = xc[2 * kind + g, pl.ds(p + 1, nbp, stride=XC_PITCH), :] + pe1
                parts.append(jnp.concatenate([x0, x1], axis=1))
            a = jnp.concatenate(parts, axis=0).astype(BF16)
            w = w1_ref[kind, pl.ds(pl.multiple_of(p * HEAD_DIM, 2 * HEAD_DIM), 2 * HEAD_DIM), :]
            return acc + _dot(a, w)

        acc = lax.fori_loop(0, BLOCK // 2, step, jnp.zeros((N_GROUPS * nbp, PHI_HIDDEN), F32), unroll=2)
        out = _dot(_gelu_tanh(acc).astype(BF16), w2_ref[kind])
        kcs[kind, nbp:, :] = jnp.zeros((kcs.shape[1] - nbp, N_GROUPS * HEAD_DIM), F32)
        for g in range(N_GROUPS):
            kcs[kind, 0:nbp, g * HEAD_DIM:(g + 1) * HEAD_DIM] = out[g * nbp:(g + 1) * nbp]

    def softmax_rows(s):
        mx = jnp.max(s, axis=0, keepdims=True)
        p = jnp.where(s > 0.5 * NEG, jnp.exp(s - mx), 0.0)
        l = jnp.sum(p, axis=0, keepdims=True)
        return p / jnp.where(l > 0.0, l, 1.0)

    pn = softmax_rows(_qk(kcs[0].astype(BF16), qT) + cb_ref[...])
    o_cmp = _dot(pn.T.astype(BF16), kcs[1].astype(BF16))

    tot = pn + pltpu.roll(pn, t_new, 1) + pltpu.roll(pn, 2 * t_new, 1) + pltpu.roll(pn, 3 * t_new, 1)
    lane = lax.broadcasted_iota(jnp.int32, pn.shape, 1)
    top = jnp.where((lane // t_new) % N_REP == N_REP - 1, tot, 0.0)
    imp = top + pltpu.roll(top, LANES - t_new, 1) + pltpu.roll(top, LANES - 2 * t_new, 1) \
        + pltpu.roll(top, LANES - 3 * t_new, 1)
    n_io = lax.broadcasted_iota(jnp.int32, pn.shape, 0)
    forced = (n_io == 0) | (n_io == cur) | (n_io == cur - 1)
    score = jnp.where(forced, FORCED_SCORE, imp)
    score = jnp.where(n_io <= cur, score, -jnp.inf)
    sel = _select_blocks(score, n_io)
    selneg_scr[...] = jnp.where(sel > 0.5, 0.0, NEG)

    pw = softmax_rows(_qk(kwin_ref[...], qT) + winb_ref[...])
    o_win = _dot(pw.T.astype(BF16), vwin_ref[...])

    blk = PAGE // BLOCK

    def mask_step(c, mx):
        r0 = pl.multiple_of(c * PAGE, PAGE)
        neg = jnp.concatenate(
            [jnp.broadcast_to(selneg_scr[pl.ds(c * blk + h, 1), :], (BLOCK, LANES)) for h in range(blk)], axis=0)
        s = s_scr[pl.ds(r0, PAGE), :] + neg
        s_scr[pl.ds(r0, PAGE), :] = s
        return jnp.maximum(mx, jnp.max(s, axis=0, keepdims=True))

    mx = lax.fori_loop(0, n_pages + 1, mask_step, jnp.full((1, LANES), NEG, F32))

    def pv_rows(r0, n_rows, carry):
        l_run, acc = carry
        pp = jnp.exp(s_scr[pl.ds(r0, n_rows), :] - mx)
        acc = acc + _dot(pp.T.astype(BF16), vsel[pl.ds(r0, n_rows), :])
        return l_run + jnp.sum(pp, axis=0, keepdims=True), acc

    pv_pages = 4 if n_pages % 4 == 0 else 1
    pv_keys = pv_pages * PAGE
    carry = lax.fori_loop(0, n_pages // pv_pages,
                          lambda c, carry: pv_rows(pl.multiple_of(c * pv_keys, pv_keys), pv_keys, carry),
                          (jnp.zeros((1, LANES), F32), jnp.zeros((LANES, N_GROUPS * HEAD_DIM), F32)))
    l_fin, acc = pv_rows(past, PAGE, carry)
    l_col = jnp.broadcast_to(l_fin, (LANES, LANES)).T[:, 0:1]
    o_sel = acc / l_col

    gates = _sigmoid(gate_ref[...])
    o_ref[...] = gates[:, 0:1] * o_cmp + gates[:, 1:2] * o_sel + gates[:, 2:3] * o_win


def _sample_nsa(pt, cache_pages, new_cmp, qT, knew, vnew, kwin, vwin, pe, w1b, w2b,
                cb, nearb, farb, newb, winb, gates, nbp):
    bsz, n_pages = pt.shape
    t_new = new_cmp.shape[1]
    gh = N_GROUPS * HEAD_DIM
    n_keys = n_pages * PAGE + PAGE
    n_ring = min(SAMPLE_RING_PAGES, n_pages)

    def per_b(arr):
        shp = arr.shape[1:]
        return pl.BlockSpec((None,) + shp, lambda b, pt: (b,) + (0,) * len(shp))

    def const(arr, **kw):
        return pl.BlockSpec(arr.shape, lambda b, pt: (0,) * arr.ndim, **kw)

    grid_spec = pltpu.PrefetchScalarGridSpec(
        num_scalar_prefetch=1,
        grid=(bsz,),
        in_specs=[pl.BlockSpec(memory_space=pl.ANY),
                  per_b(new_cmp), per_b(qT), per_b(knew), per_b(vnew), per_b(kwin), per_b(vwin),
                  const(pe), const(w1b, pipeline_mode=pl.Buffered(1)), const(w2b),
                  const(cb), const(nearb), const(farb), const(newb), const(winb), per_b(gates)],
        out_specs=pl.BlockSpec((None, LANES, gh), lambda b, pt: (b, 0, 0)),
        scratch_shapes=[pltpu.VMEM((n_ring,) + cache_pages.shape[1:], F32),
                        pltpu.VMEM((2 * N_GROUPS, nbp * XC_PITCH, HEAD_DIM), F32),
                        pltpu.VMEM((2, cb.shape[0], gh), F32),
                        pltpu.VMEM((n_keys, gh), BF16),
                        pltpu.VMEM((n_keys, LANES), F32),
                        pltpu.VMEM((cb.shape[0], LANES), F32),
                        pltpu.SemaphoreType.DMA((n_ring,))],
    )
    return pl.pallas_call(
        functools.partial(_sample_nsa_kernel, n_pages=n_pages, t_new=t_new, nbp=nbp, n_ring=n_ring),
        grid_spec=grid_spec,
        out_shape=jax.ShapeDtypeStruct((bsz, LANES, gh), F32),
        compiler_params=_cparams(("arbitrary",)),
        name="sample_nsa",
    )(pt, cache_pages, new_cmp, qT, knew, vnew, kwin, vwin, pe, w1b, w2b, cb, nearb, farb, newb, winb, gates)


def _gmlp_kernel(u_ref, v_ref, m_ref, bcol_ref, g_ref, b_ref, oa_ref, vn_ref):
    vn = _ln(v_ref[...]) * g_ref[...] + b_ref[...]
    vn_ref[...] = vn
    vb = vn.astype(BF16)
    u = u_ref[...]
    gd = GMLP_WIDTH // GMLP_GROUPS
    for g in range(GMLP_GROUPS):
        mixed = _dot(m_ref[g], vb[:, g * gd:(g + 1) * gd]) + bcol_ref[:, g:g + 1]
        oa_ref[:, g * gd:(g + 1) * gd] = (u[:, g * gd:(g + 1) * gd] * mixed).astype(oa_ref.dtype)


def _gmlp(z, mix_m, bcol, gn_g, gn_b, tc):
    t = z.shape[0]
    return pl.pallas_call(
        _gmlp_kernel,
        grid=(t // tc,),
        in_specs=[pl.BlockSpec((tc, GMLP_WIDTH), lambda i: (i, COL_U // GMLP_WIDTH)),
                  pl.BlockSpec((tc, GMLP_WIDTH), lambda i: (i, COL_V // GMLP_WIDTH)),
                  pl.BlockSpec((GMLP_GROUPS, tc, tc), lambda i: (0, 0, 0)),
                  pl.BlockSpec((tc, LANES), lambda i: (0, 0)),
                  pl.BlockSpec((1, GMLP_WIDTH), lambda i: (0, 0)),
                  pl.BlockSpec((1, GMLP_WIDTH), lambda i: (0, 0))],
        out_specs=[pl.BlockSpec((tc, GMLP_WIDTH), lambda i: (i, 0)),
                   pl.BlockSpec((tc, GMLP_WIDTH), lambda i: (i, 0))],
        out_shape=[jax.ShapeDtypeStruct((t, GMLP_WIDTH), BF16),
                   jax.ShapeDtypeStruct((t, GMLP_WIDTH), F32)],
        compiler_params=_cparams(("parallel",)),
        name="gmlp",
    )(z, z, mix_m, bcol, gn_g, gn_b)


def _pool_kernel(x_ref, halo_ref, pw_ref, ps_ref, o_ref, ext_scr, *, tm, zero_first, pos0, pos_step):
    i = pl.program_id(0)
    halo = halo_ref[...]
    if zero_first:
        halo = jnp.where(i == 0, 0.0, halo)
    ext_scr[0:16, :] = halo
    x = x_ref[...]
    ext_scr[16:16 + tm, :] = x
    pos = pos0 + i * pos_step + lax.broadcasted_iota(jnp.int32, (tm, POOL_GROUP_DIM), 0)
    for gi, w in enumerate(POOL_WINDOWS):
        c0 = gi * POOL_GROUP_DIM
        acc = ext_scr[16:16 + tm, c0:c0 + POOL_GROUP_DIM]
        for k in range(1, w):
            acc = acc + ext_scr[16 - k:16 - k + tm, c0:c0 + POOL_GROUP_DIM]
        count = jnp.minimum(pos + 1, w).astype(F32)
        d = acc / count - x[:, c0:c0 + POOL_GROUP_DIM]
        y = _dot(d.astype(BF16), pw_ref[gi]) * ps_ref[:, c0:c0 + POOL_GROUP_DIM]
        o_ref[:, c0:c0 + POOL_GROUP_DIM] = y.astype(o_ref.dtype)


def _pool(z, halo_src, halo_map, pwb, ps, tm, zero_first, pos0, pos_step):
    t = z.shape[0]
    return pl.pallas_call(
        functools.partial(_pool_kernel, tm=tm, zero_first=zero_first, pos0=pos0, pos_step=pos_step),
        grid=(t // tm,),
        in_specs=[pl.BlockSpec((tm, POOL_WIDTH), lambda i: (i, COL_XC // POOL_WIDTH)),
                  pl.BlockSpec((16, POOL_WIDTH), halo_map),
                  pl.BlockSpec((len(POOL_WINDOWS), POOL_GROUP_DIM, POOL_GROUP_DIM), lambda i: (0, 0, 0)),
                  pl.BlockSpec((1, POOL_WIDTH), lambda i: (0, 0))],
        out_specs=pl.BlockSpec((tm, POOL_WIDTH), lambda i: (i, 0)),
        out_shape=jax.ShapeDtypeStruct((t, POOL_WIDTH), BF16),
        scratch_shapes=[pltpu.VMEM((16 + tm, POOL_WIDTH), F32)],
        compiler_params=_cparams(("parallel",)),
        name="pool",
    )(z, halo_src, pwb, ps)


def _route(logits_t, rb_col):
    aff = _sigmoid(logits_t)
    biased = aff + rb_col
    rows = [biased[e:e + 1] for e in range(N_EXPERTS)]
    top2 = []
    gscore = []
    for g in range(N_EXPERT_GROUPS):
        grp = rows[g * EXPERTS_PER_GROUP:(g + 1) * EXPERTS_PER_GROUP]
        gs = None
        for a in range(EXPERTS_PER_GROUP):
            rank = None
            for c in range(EXPERTS_PER_GROUP):
                if c == a:
                    continue
                ahead = (grp[c] >= grp[a]) if c < a else (grp[c] > grp[a])
                rank = ahead.astype(F32) if rank is None else rank + ahead.astype(F32)
            in2 = rank < 1.5
            top2.append(in2)
            contrib = jnp.where(in2, grp[a], 0.0)
            gs = contrib if gs is None else gs + contrib
        gscore.append(gs)
    out = []
    for g in range(N_EXPERT_GROUPS):
        win = None
        for c in range(N_EXPERT_GROUPS):
            if c == g:
                continue
            ok = (gscore[g] > gscore[c]) if c < g else (gscore[g] >= gscore[c])
            win = ok if win is None else (win & ok)
        for a in range(EXPERTS_PER_GROUP):
            e = g * EXPERTS_PER_GROUP + a
            out.append(jnp.where(win & top2[e], aff[e:e + 1], 0.0))
    selw = jnp.concatenate(out, axis=0)
    return selw / jnp.sum(selw, axis=0, keepdims=True)


def _merge_kernel(x_ref, oa_ref, ob_ref, oc_ref, g0_ref, g1_ref, g2_ref, gt1_ref, sh2_ref, sc2_ref,
                  ln1g_ref, ln1b_ref, pa_ref, pb_ref, pc_ref, wo_ref, rw_ref, rb_ref,
                  x1_ref, h2_ref, comb_ref, *, alpha):
    tm = x_ref.shape[0]

    merged = _sigmoid(g0_ref[...]) * _dot(oa_ref[...], pa_ref[...])
    merged = merged + _sigmoid(g1_ref[...]) * _dot(ob_ref[...], pb_ref[...])
    merged = merged + _sigmoid(g2_ref[...]) * _dot(oc_ref[...], pc_ref[...])
    mix = _dot(merged.astype(BF16), wo_ref[...])
    x1 = _ln(alpha * x_ref[...] + (1.0 + _mod_rows(gt1_ref, tm)) * mix) * ln1g_ref[...] + ln1b_ref[...]
    x1_ref[...] = x1
    h2 = _ln(x1) * (1.0 + _mod_rows(sc2_ref, tm)) + _mod_rows(sh2_ref, tm)
    h2_ref[...] = h2.astype(BF16)
    logits = _dot(h2.astype(BF16), rw_ref[...])
    comb_t = _route(logits.T[:N_EXPERTS], rb_ref[...][:N_EXPERTS, 0:1])
    comb_full = jnp.concatenate([comb_t, jnp.zeros((LANES - N_EXPERTS, tm), F32)], axis=0)
    comb_ref[...] = comb_full.T


def _merge(x, z, oa, ob, oc, mods, ln1g, ln1b, pa, pb, pc, wo, rw, rb, tm, alpha):
    t = x.shape[0]

    def const(arr):
        return pl.BlockSpec(arr.shape, lambda i: (0,) * arr.ndim, pipeline_mode=pl.Buffered(1))

    gcol = COL_GM // D_MODEL
    return pl.pallas_call(
        functools.partial(_merge_kernel, alpha=alpha),
        grid=(t // tm,),
        in_specs=[pl.BlockSpec((tm, D_MODEL), lambda i: (i, 0)),
                  pl.BlockSpec((tm, GMLP_WIDTH), lambda i: (i, 0)),
                  pl.BlockSpec((tm, NSA_WIDTH), lambda i: (i, 0)),
                  pl.BlockSpec((tm, POOL_WIDTH), lambda i: (i, 0)),
                  pl.BlockSpec((tm, D_MODEL), lambda i: (i, gcol)),
                  pl.BlockSpec((tm, D_MODEL), lambda i: (i, gcol + 1)),
                  pl.BlockSpec((tm, D_MODEL), lambda i: (i, gcol + 2)),
                  _mod_spec(mods, tm, 2), _mod_spec(mods, tm, 3), _mod_spec(mods, tm, 4),
                  const(ln1g), const(ln1b), const(pa), const(pb), const(pc), const(wo), const(rw), const(rb)],
        out_specs=[pl.BlockSpec((tm, D_MODEL), lambda i: (i, 0)),
                   pl.BlockSpec((tm, D_MODEL), lambda i: (i, 0)),
                   pl.BlockSpec((tm, LANES), lambda i: (i, 0))],
        out_shape=[jax.ShapeDtypeStruct((t, D_MODEL), F32),
                   jax.ShapeDtypeStruct((t, D_MODEL), BF16),
                   jax.ShapeDtypeStruct((t, LANES), F32)],
        compiler_params=_cparams(("parallel",)),
        name="merge",
    )(x, oa, ob, oc, z, z, z, mods, mods, mods, ln1g, ln1b, pa, pb, pc, wo, rw, rb)


def _moe_kernel(h_ref, comb_ref, wg_ref, wu_ref, wd_ref, x1_ref, gt2_ref, g_ref, b_ref, o_ref, acc_scr, *, alpha):
    e = pl.program_id(1)
    tm = h_ref.shape[0]

    @pl.when(e == 0)
    def _():
        acc_scr[...] = jnp.zeros_like(acc_scr)

    h = h_ref[...]
    comb = comb_ref[...]
    lane = lax.broadcasted_iota(jnp.int32, comb.shape, 1)
    w_col = jnp.sum(jnp.where(lane == e, comb, 0.0), axis=1, keepdims=True)
    act = _silu(_dot(h, wg_ref[...])) * _dot(h, wu_ref[...])
    acc_scr[...] += _dot((act * w_col).astype(BF16), wd_ref[...])

    @pl.when(e == N_EXPERTS - 1)
    def _():
        o_ref[...] = _ln(alpha * x1_ref[...] + (1.0 + _mod_rows(gt2_ref, tm)) * acc_scr[...]) * g_ref[...] + b_ref[...]


def _moe(h2, comb, wg, wu, wd, x1, mods, ln2g, ln2b, tm, alpha):
    t = h2.shape[0]
    return pl.pallas_call(
        functools.partial(_moe_kernel, alpha=alpha),
        grid=(t // tm, N_EXPERTS),
        in_specs=[pl.BlockSpec((tm, D_MODEL), lambda i, e: (i, 0)),
                  pl.BlockSpec((tm, LANES), lambda i, e: (i, 0)),
                  pl.BlockSpec((None, D_MODEL, D_EXPERT), lambda i, e: (e, 0, 0)),
                  pl.BlockSpec((None, D_MODEL, D_EXPERT), lambda i, e: (e, 0, 0)),
                  pl.BlockSpec((None, D_EXPERT, D_MODEL), lambda i, e: (e, 0, 0)),
                  pl.BlockSpec((tm, D_MODEL), lambda i, e: (i, 0)),
                  _mod_spec(mods, tm, 5),
                  pl.BlockSpec((1, D_MODEL), lambda i, e: (0, 0)),
                  pl.BlockSpec((1, D_MODEL), lambda i, e: (0, 0))],
        out_specs=pl.BlockSpec((tm, D_MODEL), lambda i, e: (i, 0)),
        out_shape=jax.ShapeDtypeStruct((t, D_MODEL), F32),
        scratch_shapes=[pltpu.VMEM((tm, D_MODEL), F32)],
        compiler_params=_cparams(("parallel", "arbitrary")),
        name="moe",
    )(h2, comb, wg, wu, wd, x1, mods, ln2g, ln2b)


def _rel_bucket(dist):
    n = jnp.maximum(dist, 0)
    max_exact = N_BUCKETS // 2
    nf = jnp.maximum(n, max_exact).astype(F32)
    large = max_exact + (jnp.log(nf / max_exact) / math.log(MAX_DISTANCE / max_exact)
                         * (N_BUCKETS - max_exact)).astype(jnp.int32)
    return jnp.where(n < max_exact, n, jnp.minimum(large, N_BUCKETS - 1))


def _bias_of(rel_bias, dist):
    return rel_bias[_rel_bucket(dist)].astype(F32)


def _prompt_tiles(rel_bias):
    kk = jnp.arange(BLOCK)[:, None]
    qq = jnp.arange(BLOCK)[None, :]

    def lanes(tile8, g):
        return jnp.concatenate([tile8[:, :, g * N_REP + r] for r in range(N_REP)], axis=1)

    def tile(m, lo, hi):
        d = m * BLOCK + qq - kk
        ok = (d >= lo) & (d < hi)
        return jnp.where(ok[:, :, None], _bias_of(rel_bias, d), NEG)

    big = 1 << 20
    neg_tile = jnp.full((BLOCK, BLOCK, N_HEADS), NEG, F32)
    sel_tiles = [tile(m, 0, big) for m in range(4)] + [neg_tile]
    win_tiles = [tile(m, 0, WINDOW) for m in range(WIN_BLOCKS + 1)] + [neg_tile]
    st = jnp.stack([jnp.stack([lanes(t, g) for t in sel_tiles]) for g in range(N_GROUPS)])
    wt = jnp.stack([jnp.stack([lanes(t, g) for t in win_tiles]) for g in range(N_GROUPS)])
    rows = []
    for m in range(4):
        d = m * BLOCK + jnp.arange(BLOCK) - (BLOCK - 1)
        rows.append(_bias_of(rel_bias, d))
    ct = jnp.stack([jnp.stack([jnp.concatenate([rw[:, g * N_REP + r] for r in range(N_REP)]) for rw in rows]
                              + [jnp.zeros((N_REP * BLOCK,), F32)] * 4) for g in range(N_GROUPS)])
    return st, wt, ct


def _sample_lanes(tile8, t_new):
    rows = tile8.shape[0]
    x = jnp.transpose(tile8, (0, 2, 1)).reshape(rows, N_HEADS * t_new)
    return jnp.pad(x, ((0, 0), (0, LANES - N_HEADS * t_new)))


def _sample_tiles(rel_bias, past, t_new, nbp, win_rows, win_pad):
    tq = past + jnp.arange(t_new)[None, :]

    def tile(k_pos, ok_extra=None, lo=0, hi=1 << 20):
        d = tq - k_pos[:, None]
        ok = (d >= lo) & (d < hi)
        if ok_extra is not None:
            ok = ok & ok_extra[:, None]
        return _sample_lanes(jnp.where(ok[:, :, None], _bias_of(rel_bias, d), NEG), t_new)

    n = jnp.arange(nbp)
    cb = tile(n * BLOCK + BLOCK - 1, n < past // BLOCK + 1)
    nearb = tile(past - PAGE + jnp.arange(PAGE))
    farb = jnp.broadcast_to(_sample_lanes(jnp.broadcast_to(
        rel_bias[N_BUCKETS - 1].astype(F32)[None, None, :], (1, t_new, N_HEADS)), t_new), (8, LANES))
    kn = jnp.arange(PAGE)
    newb = tile(past + kn, kn < t_new)
    wb = min(WINDOW, past)
    j = jnp.arange(win_pad)
    k_pos = past - wb + j
    winb = tile(k_pos, (j < win_rows) & (k_pos >= 0), 0, WINDOW)
    return cb, nearb, farb, newb, winb


def _mix_mats(ws, bs, tc, rep):
    n = tc // rep
    causal = jnp.tril(jnp.ones((CHUNK, CHUNK), F32))
    wc = (ws * causal)[:, :n, :n]
    eye = jnp.eye(rep, dtype=F32)
    m = jnp.einsum('ab,gts->gatbs', eye, wc).reshape(GMLP_GROUPS, tc, tc)
    bcol = jnp.tile(bs[:, :n].T, (rep, 1))
    return m.astype(BF16), jnp.pad(bcol, ((0, 0), (0, LANES - GMLP_GROUPS)))


def kernel(x_prompt, x_sample, cache_nsa_kv, state_win_kv, state_pool, page_table, c_prompt, c_sample, rel_bias, router_w, router_b, w_in, nsa_phi_pe, nsa_phi_w1, nsa_phi_w2, gmlp_norm_g, gmlp_norm_b, gmlp_ws, gmlp_bs, pool_w, pool_scale, proj_a, proj_b, proj_c, w_o, ada_w, ada_b, ln1_g, ln1_b, ln2_g, ln2_b, exp_w_gate, exp_w_up, exp_w_down):
    depth = w_in.shape[0]
    alpha = (2 * depth) ** 0.25
    s = x_prompt.shape[1]
    bsz, t_new = x_sample.shape[0], x_sample.shape[1]
    n_pool = cache_nsa_kv.shape[1]
    n_pages = page_table.shape[1]
    past = n_pages * PAGE
    ts = bsz * t_new
    assert x_prompt.shape[0] == 1 and s % CHUNK_KEYS == 0 and past % PAGE == 0
    assert N_HEADS * t_new <= LANES and t_new <= BLOCK and (past // BLOCK) >= N_SEL
    nb_s = past // BLOCK + 1
    nbp = -(-nb_s // 8) * 8

    n_c = 1 + bsz
    c_all = jnp.pad(jnp.concatenate([c_prompt, c_sample], axis=0), ((0, -n_c % 8), (0, 0)))
    mods_all = _ada_mods(c_all, ada_w, ada_b)

    cache_pages = cache_nsa_kv.reshape(depth * n_pool, PAGE * PAGED_KINDS * N_GROUPS, HEAD_DIM)
    st, wt, ct = _prompt_tiles(rel_bias)
    win_rows = min(WINDOW, past) + t_new
    win_pad = -(-win_rows // LANES) * LANES
    assert nbp <= SAMPLE_BLOCKS_PAD
    cb_s, nearb_s, farb_s, newb_s, winb_s = _sample_tiles(rel_bias, past, t_new, SAMPLE_BLOCKS_PAD, win_rows, win_pad)
    rw = jnp.pad(router_w, ((0, 0), (0, LANES - N_EXPERTS))).astype(BF16)
    rb = jnp.pad(router_b.reshape(N_EXPERTS, 1), ((0, LANES - N_EXPERTS), (0, LANES - 1)))

    xp = x_prompt.reshape(s, D_MODEL)
    xs = x_sample.reshape(ts, D_MODEL)
    outs = {k: [] for k in ("nsa_p", "nsa_s", "win_p", "win_s", "pool_p", "pool_s", "v_s")}

    for l in range(depth):
        w = w_in[l]
        w_pad = jnp.concatenate(
            [w[:, 0:2560], w[:, 2584:10264], w[:, 2560:2584],
             jnp.zeros((D_MODEL, N_IN_PAD - 10264), F32)], axis=1).astype(BF16)
        pe = nsa_phi_pe[l]
        w1b = nsa_phi_w1[l].astype(BF16)
        w2b = nsa_phi_w2[l].astype(BF16)
        pab, pbb, pcb, wob = (a[l].astype(BF16) for a in (proj_a, proj_b, proj_c, w_o))
        wg, wu, wd = (a[l].astype(BF16) for a in (exp_w_gate, exp_w_up, exp_w_down))
        pwb = pool_w[l].astype(BF16)
        ps = pool_scale[l].reshape(1, POOL_WIDTH)
        gng = gmlp_norm_g[l].reshape(1, GMLP_WIDTH)
        gnb = gmlp_norm_b[l].reshape(1, GMLP_WIDTH)
        ln1g, ln1b, ln2g, ln2b = (a[l].reshape(1, D_MODEL) for a in (ln1_g, ln1_b, ln2_g, ln2_b))
        mods_p = jnp.broadcast_to(mods_all[l, 0:1], (8, 6 * D_MODEL))
        mods_s = jnp.repeat(mods_all[l, 1:1 + bsz], t_new, axis=0)

        z = _in_proj(xp, mods_p, w_pad, 1024)
        kv = z[:, COL_KV:COL_KV + 1536].reshape(s, KV_KINDS, N_GROUPS, HEAD_DIM)
        outs["nsa_p"].append(kv[:, :PAGED_KINDS].reshape(1, s, PAGED_KINDS, N_GROUPS, HEAD_DIM))
        outs["win_p"].append(kv[s - min(WINDOW, s):, PAGED_KINDS:].reshape(1, min(WINDOW, s), 2, N_GROUPS, HEAD_DIM))
        outs["pool_p"].append(z[s - POOL_BUF:, COL_XC:COL_XC + POOL_WIDTH].reshape(1, POOL_BUF, POOL_WIDTH))

        nb = s // BLOCK
        cmp = _compress_prompt(z, pe, w1b, w2b).reshape(2, N_GROUPS, nb, HEAD_DIM)
        kc = cmp[0].astype(BF16)
        vcT = jnp.transpose(cmp[1], (0, 2, 1)).astype(BF16)
        nch = s // CHUNK_KEYS

        def k_chunks(kind):
            return jnp.transpose(kv[:, kind], (1, 0, 2)).reshape(N_GROUPS, nch, CHUNK_KEYS, HEAD_DIM).astype(BF16)

        def vT_chunks(kind):
            a = jnp.transpose(kv[:, kind], (1, 0, 2)).reshape(N_GROUPS, nch, CHUNK_KEYS, HEAD_DIM)
            return jnp.transpose(a, (0, 1, 3, 2)).astype(BF16)

        q = z[:, COL_Q:COL_Q + NSA_WIDTH].reshape(nb, BLOCK, N_GROUPS, N_REP, HEAD_DIM)
        qT = jnp.transpose(q, (2, 4, 0, 3, 1)).reshape(N_GROUPS, HEAD_DIM, nb * N_REP * BLOCK).astype(BF16)
        gn = z[:, COL_GN:COL_GN + 3 * N_HEADS].reshape(nb, BLOCK, 3, N_GROUPS, N_REP)
        gates = jnp.transpose(gn, (3, 0, 2, 4, 1)).reshape(N_GROUPS, nb, 3, N_REP * BLOCK)
        gates = jnp.pad(gates, ((0, 0), (0, 0), (0, 5), (0, 0)))
        ob = _nsa_prompt(qT, kc, vcT, k_chunks(2), vT_chunks(3), k_chunks(4), vT_chunks(5), st, wt, ct, gates, s)

        mm, bcol = _mix_mats(gmlp_ws[l], gmlp_bs[l], CHUNK, 1)
        oa, _ = _gmlp(z, mm, bcol, gng, gnb, CHUNK)
        tp = 512
        cxb = COL_XC // POOL_WIDTH
        oc = _pool(z, z, lambda i: (jnp.maximum(i * (tp // 16) - 1, 0), cxb), pwb, ps, tp, True, 0, tp)
        x1, h2, comb = _merge(xp, z, oa, ob, oc, mods_p, ln1g, ln1b, pab, pbb, pcb, wob, rw, rb, 256, alpha)
        xp = _moe(h2, comb, wg, wu, wd, x1, mods_p, ln2g, ln2b, 512, alpha)

        zs = _in_proj(xs, mods_s, w_pad, ts)
        kvs = zs[:, COL_KV:COL_KV + 1536].reshape(bsz, t_new, KV_KINDS, N_GROUPS, HEAD_DIM)
        outs["nsa_s"].append(kvs[:, :, :PAGED_KINDS])
        win_full = jnp.concatenate([state_win_kv[l], kvs[:, :, PAGED_KINDS:]], axis=1)
        outs["win_s"].append(win_full[:, win_full.shape[1] - min(WINDOW, past + t_new):])
        xcs = zs[:, COL_XC:COL_XC + POOL_WIDTH].reshape(bsz, t_new, POOL_WIDTH)
        ext = jnp.concatenate([state_pool[l], xcs], axis=1)
        outs["pool_s"].append(ext[:, ext.shape[1] - POOL_BUF:])

        pt = page_table + l * n_pool
        new_cmp = kvs[:, :, 0:2].reshape(bsz, t_new, 2 * N_GROUPS * HEAD_DIM)
        gh = N_GROUPS * HEAD_DIM
        knew = jnp.pad(kvs[:, :, 2].reshape(bsz, t_new, gh), ((0, 0), (0, PAGE - t_new), (0, 0))).astype(BF16)
        vnew = jnp.pad(kvs[:, :, 3].reshape(bsz, t_new, gh), ((0, 0), (0, PAGE - t_new), (0, 0))).astype(BF16)
        wf = jnp.pad(win_full.reshape(bsz, win_rows, 2, gh), ((0, 0), (0, win_pad - win_rows), (0, 0), (0, 0)))
        kwin_s = wf[:, :, 0].astype(BF16)
        vwin_s = wf[:, :, 1].astype(BF16)
        qs = zs[:, COL_Q:COL_Q + NSA_WIDTH].reshape(bsz, t_new, N_GROUPS, N_REP, HEAD_DIM)
        qs = jnp.transpose(qs, (0, 2, 4, 3, 1)).reshape(bsz, N_GROUPS, HEAD_DIM, N_REP * t_new)
        qT_s = jnp.zeros((bsz, gh, LANES), F32)
        for g in range(N_GROUPS):
            qT_s = qT_s.at[:, g * HEAD_DIM:(g + 1) * HEAD_DIM,
                           g * N_REP * t_new:(g + 1) * N_REP * t_new].set(qs[:, g])
        qT_s = qT_s.astype(BF16)
        gns = zs[:, COL_GN:COL_GN + 3 * N_HEADS].reshape(bsz, t_new, 3, N_HEADS)
        gates_s = jnp.transpose(gns, (0, 3, 1, 2)).reshape(bsz, N_HEADS * t_new, 3)
        gates_s = jnp.pad(gates_s, ((0, 0), (0, LANES - N_HEADS * t_new), (0, LANES - 3)))
        o_s = _sample_nsa(pt, cache_pages, new_cmp, qT_s, knew, vnew, kwin_s, vwin_s, pe, w1b, w2b,
                          cb_s, nearb_s, farb_s, newb_s, winb_s, gates_s, nbp)
        o4 = o_s[:, :N_HEADS * t_new].reshape(bsz, N_GROUPS, N_REP, t_new, N_GROUPS, HEAD_DIM)
        o4 = jnp.stack([o4[:, g, :, :, g] for g in range(N_GROUPS)], axis=1)
        ob_s = jnp.transpose(o4, (0, 3, 1, 2, 4)).reshape(ts, NSA_WIDTH).astype(BF16)

        mm_s, bcol_s = _mix_mats(gmlp_ws[l], gmlp_bs[l], ts, bsz)
        oa_s, vn_s = _gmlp(zs, mm_s, bcol_s, gng, gnb, ts)
        outs["v_s"].append(vn_s.reshape(bsz, t_new, GMLP_WIDTH))
        halo = jnp.pad(state_pool[l], ((0, 0), (16 - POOL_BUF, 0), (0, 0))).reshape(bsz * 16, POOL_WIDTH)
        oc_s = _pool(zs, halo, lambda i: (i, 0), pwb, ps, t_new, False, past, 0)
        x1s, h2s, comb_s = _merge(xs, zs, oa_s, ob_s, oc_s, mods_s, ln1g, ln1b, pab, pbb, pcb, wob, rw, rb, ts, alpha)
        xs = _moe(h2s, comb_s, wg, wu, wd, x1s, mods_s, ln2g, ln2b, ts, alpha)

    return (xp.reshape(1, s, D_MODEL), xs.reshape(bsz, t_new, D_MODEL),
            jnp.stack(outs["nsa_p"]), jnp.stack(outs["nsa_s"]), jnp.stack(outs["win_p"]), jnp.stack(outs["win_s"]),
            jnp.stack(outs["pool_p"]), jnp.stack(outs["pool_s"]), jnp.stack(outs["v_s"]))
```

```python
import functools
import math

import jax
import jax.numpy as jnp
from jax import lax
from jax.experimental import pallas as pl
from jax.experimental.pallas import tpu as pltpu

F32 = jnp.float32
BF16 = jnp.bfloat16

D_MODEL = 2048
HEAD_DIM = 128
N_GROUPS = 2
N_REP = 4
N_HEADS = N_GROUPS * N_REP
NSA_WIDTH = N_HEADS * HEAD_DIM
KV_KINDS = 6
PAGED_KINDS = 4
BLOCK = 64
N_SEL = 16
WINDOW = 512
WIN_BLOCKS = WINDOW // BLOCK
PHI_HIDDEN = 2 * HEAD_DIM
PAGE = 128
GMLP_GROUPS = 4
GMLP_WIDTH = 512
CHUNK = 128
POOL_WINDOWS = (2, 4, 8, 16)
POOL_GROUP_DIM = 128
POOL_WIDTH = 512
POOL_BUF = 15
N_BUCKETS = 32
MAX_DISTANCE = 128
N_EXPERTS = 16
N_EXPERT_GROUPS = 4
EXPERTS_PER_GROUP = 4
D_EXPERT = 512
LN_EPS = 1e-5
NEG = -1e30
FORCED_SCORE = 1e9

COL_Q = 0
COL_KV = 1024
COL_U = 2560
COL_V = 3072
COL_XC = 3584
COL_GM = 4096
COL_GN = 10240
N_IN_PAD = 10752
IN_TN = 512

LANES = 128
VMEM_LIMIT = 56 * 1024 * 1024
CHUNK_BLOCKS = 8
CHUNK_KEYS = CHUNK_BLOCKS * BLOCK
XC_PITCH = BLOCK + 8
SAMPLE_RING_PAGES = 16
SAMPLE_BLOCKS_PAD = 256


def _cparams(sem):
    return pltpu.CompilerParams(dimension_semantics=sem, vmem_limit_bytes=VMEM_LIMIT)


def _ln(x):
    mu = jnp.mean(x, axis=-1, keepdims=True)
    xc = x - mu
    var = jnp.mean(xc * xc, axis=-1, keepdims=True)
    return xc * lax.rsqrt(var + LN_EPS)


def _sigmoid(x):
    return 1.0 / (1.0 + jnp.exp(-x))


def _silu(x):
    return x * _sigmoid(x)


def _gelu_tanh(x):
    return 0.5 * x * (1.0 + jnp.tanh(math.sqrt(2.0 / math.pi) * (x + 0.044715 * (x * x * x))))


def _dot(a, b):
    return jnp.dot(a, b, preferred_element_type=F32)


QK_SCALE = HEAD_DIM ** -0.5
EXP2_SCALE = QK_SCALE * math.log2(math.e)


def _qk(k, qT):
    return _dot(k, qT) * QK_SCALE


def _ada_kernel(c_ref, w_ref, b_ref, o_ref):
    c = c_ref[...]
    o_ref[...] = _dot(_silu(c).astype(BF16), w_ref[...].astype(BF16)) + b_ref[...]


def _ada_mods(c_all, ada_w, ada_b):
    depth = ada_w.shape[0]
    rows = c_all.shape[0]
    n = ada_w.shape[2]
    tn = 1024
    return pl.pallas_call(
        _ada_kernel,
        grid=(depth, n // tn),
        in_specs=[pl.BlockSpec((rows, D_MODEL), lambda l, j: (0, 0)),
                  pl.BlockSpec((None, D_MODEL, tn), lambda l, j: (l, 0, j)),
                  pl.BlockSpec((None, 1, tn), lambda l, j: (l, 0, j))],
        out_specs=pl.BlockSpec((None, rows, tn), lambda l, j: (l, 0, j)),
        out_shape=jax.ShapeDtypeStruct((depth, rows, n), F32),
        compiler_params=_cparams(("parallel", "parallel")),
        name="ada_mods",
    )(c_all, ada_w, ada_b.reshape(depth, 1, n))


def _mod_rows(ref, tm):
    return ref[...] if ref.shape[0] == tm else ref[0:1, :]


def _inproj_kernel(x_ref, sh_ref, sc_ref, w_ref, z_ref, h_scr):
    tm = x_ref.shape[0]

    @pl.when(pl.program_id(1) == 0)
    def _():
        h = _ln(x_ref[...]) * (1.0 + _mod_rows(sc_ref, tm)) + _mod_rows(sh_ref, tm)
        h_scr[...] = h.astype(BF16)

    z_ref[...] = _dot(h_scr[...], w_ref[...])


def _mod_spec(mods, tm, col):
    rows = mods.shape[0]
    if rows == 8:
        return pl.BlockSpec((8, D_MODEL), lambda i, *_: (0, col))
    return pl.BlockSpec((tm, D_MODEL), lambda i, *_: (i, col))


def _in_proj(x, mods, w_pad, tm):
    t = x.shape[0]
    return pl.pallas_call(
        _inproj_kernel,
        grid=(t // tm, N_IN_PAD // IN_TN),
        in_specs=[pl.BlockSpec((tm, D_MODEL), lambda i, j: (i, 0)),
                  _mod_spec(mods, tm, 0), _mod_spec(mods, tm, 1),
                  pl.BlockSpec((D_MODEL, IN_TN), lambda i, j: (0, j))],
        out_specs=pl.BlockSpec((tm, IN_TN), lambda i, j: (i, j)),
        out_shape=jax.ShapeDtypeStruct((t, N_IN_PAD), F32),
        scratch_shapes=[pltpu.VMEM((tm, D_MODEL), BF16)],
        compiler_params=_cparams(("parallel", "arbitrary")),
        name="in_proj",
    )(x, mods, mods, w_pad)


def _compress_pair(x0_ref, x1_ref, nb, pe_ref, w1_ref, w2_ref):
    def step(p, acc):
        pe = pe_ref[pl.ds(p, 1), :]
        a0 = x0_ref[pl.ds(p, nb, stride=BLOCK), :] + pe
        a1 = x1_ref[pl.ds(p, nb, stride=BLOCK), :] + pe
        a = jnp.concatenate([a0, a1], axis=0).astype(BF16)
        w = w1_ref[pl.ds(pl.multiple_of(p * HEAD_DIM, HEAD_DIM), HEAD_DIM), :]
        return acc + _dot(a, w)

    acc = lax.fori_loop(0, BLOCK, step, jnp.zeros((2 * nb, PHI_HIDDEN), F32))
    return _dot(_gelu_tanh(acc).astype(BF16), w2_ref[...])


def _compress_prompt_kernel(x0_ref, x1_ref, pe_ref, w1_ref, w2_ref, o_ref, *, nb):
    o_ref[...] = _compress_pair(x0_ref, x1_ref, nb, pe_ref, w1_ref, w2_ref)


def _compress_prompt(z, pe, w1b, w2b):
    s = z.shape[0]
    nb = s // BLOCK
    cb = COL_KV // HEAD_DIM
    return pl.pallas_call(
        functools.partial(_compress_prompt_kernel, nb=nb),
        grid=(2,),
        in_specs=[pl.BlockSpec((s, HEAD_DIM), lambda k: (0, cb + 2 * k)),
                  pl.BlockSpec((s, HEAD_DIM), lambda k: (0, cb + 2 * k + 1)),
                  pl.BlockSpec((None, BLOCK, HEAD_DIM), lambda k: (k, 0, 0)),
                  pl.BlockSpec((None, BLOCK * HEAD_DIM, PHI_HIDDEN), lambda k: (k, 0, 0)),
                  pl.BlockSpec((None, PHI_HIDDEN, HEAD_DIM), lambda k: (k, 0, 0))],
        out_specs=pl.BlockSpec((None, 2 * nb, HEAD_DIM), lambda k: (k, 0, 0)),
        out_shape=jax.ShapeDtypeStruct((2, 2 * nb, HEAD_DIM), F32),
        compiler_params=_cparams(("parallel",)),
        name="compress_prompt",
    )(z, z, pe, w1b, w2b)


def _select_blocks(score, n_io):
    def body(_, carry):
        work, sel = carry
        mval = jnp.max(work, axis=0, keepdims=True)
        cand = jnp.where(work == mval, n_io, jnp.int32(1 << 30))
        idx = jnp.min(cand, axis=0, keepdims=True)
        pick = n_io == idx
        return jnp.where(pick, -jnp.inf, work), jnp.where(pick, 1.0, sel)

    _, sel = lax.fori_loop(0, N_SEL, body, (score, jnp.zeros(score.shape, F32)))
    return sel


def _nsa_prompt_kernel(qT_ref, kc_ref, vcT_ref, ksel_ref, vselT_ref, kwin_ref, vwinT_ref,
                       st_ref, wt_ref, ct_ref, gate_ref, o_ref, selneg_scr, selfar_scr, s_a, s_b, p_a, p_b, *, nb):
    i = pl.program_id(1)
    qT = qT_ref[...]
    nl = N_REP * BLOCK

    n_io = lax.broadcasted_iota(jnp.int32, (nb, nl), 0)
    qq = lax.broadcasted_iota(jnp.int32, (nb, nl), 1) % BLOCK
    m = i - n_io
    ct = ct_ref[...]
    bias = jnp.where(m == 0, ct[0:1], jnp.where(m == 1, ct[1:2], jnp.where(m == 2, ct[2:3], ct[3:4])))
    valid = (m >= 1) | ((m == 0) & (qq == BLOCK - 1))
    s = jnp.where(valid, _qk(kc_ref[...], qT) + bias, NEG)
    mx = jnp.max(s, axis=0, keepdims=True)
    p = jnp.where(valid, jnp.exp(s - mx), 0.0)
    l = jnp.sum(p, axis=0, keepdims=True)
    pn = p / jnp.where(l > 0.0, l, 1.0)
    o_cmp = _dot(vcT_ref[...], pn.astype(BF16))

    half = pn[:, :LANES] + pn[:, LANES:]
    imp = half + pltpu.roll(half, BLOCK, 1)
    n_io1 = n_io[:, :LANES]
    forced = (n_io1 == 0) | (n_io1 == i) | (n_io1 == i - 1)
    score = jnp.where(forced, FORCED_SCORE, imp)
    score = jnp.where(n_io1 <= i, score, -1.0)
    sel = _select_blocks(score, n_io1)
    neg1 = jnp.where((sel > 0.5) & (score >= 0.0), 0.0, NEG)
    neg2 = jnp.concatenate([neg1, neg1], axis=1)
    selneg_scr[...] = neg2
    far_row = st_ref[st_ref.shape[0] - 2][0:1, :]
    selfar_scr[...] = jnp.where(neg2 == 0.0, far_row, NEG)

    c_i = i // CHUNK_BLOCKS

    def branch(k_ref, vT_ref, tile_ref, n_tiles, use_sel, n_chunks):
        def logits(c, dst):
            dst[...] = _dot(k_ref[jnp.maximum(c, 0)], qT)

        def pv(c, p_ref):
            return _dot(vT_ref[jnp.clip(c, 0, vT_ref.shape[0] - 1)], p_ref[...])

        def consume(c, src, m_run, near):
            live = c >= 0
            cz = jnp.maximum(c, 0)
            parts = []
            for b in range(CHUNK_BLOCKS):
                n = cz * CHUNK_BLOCKS + b
                sb = src[b * BLOCK:(b + 1) * BLOCK, :]
                if near:
                    mm = i - n
                    if use_sel:
                        tidx = jnp.where(mm < 0, n_tiles - 1, jnp.minimum(mm, n_tiles - 2))
                    else:
                        tidx = jnp.where((mm < 0) | (mm > n_tiles - 2), n_tiles - 1, mm)
                    sb = sb + tile_ref[jnp.where(live, tidx, n_tiles - 1)]
                    if use_sel:
                        sb = sb + selneg_scr[pl.ds(n, 1), :]
                else:
                    sb = sb + jnp.where(live, selfar_scr[pl.ds(n, 1), :], NEG)
                parts.append(sb)
            s2 = jnp.concatenate(parts, axis=0)
            m_new = jnp.maximum(m_run, jnp.max(s2, axis=0, keepdims=True))
            alpha = jnp.exp2((m_run - m_new) * EXP2_SCALE)
            pp = jnp.exp2((s2 - m_new) * EXP2_SCALE)
            return m_new, alpha, pp.astype(BF16)

        def pair(t, carry, near):
            m_run, acc = carry
            c0 = c_i - 2 * t
            logits(c0 - 1, s_b)
            r_prev = pv(c0 + 1, p_b)
            m_run, alpha, pp = consume(c0, s_a, m_run, near)
            p_a[...] = pp
            acc = alpha * (acc + r_prev)
            logits(c0 - 2, s_a)
            r_cur = pv(c0, p_a)
            m_run, alpha, pp = consume(c0 - 1, s_b, m_run, near)
            p_b[...] = pp
            return m_run, alpha * (acc + r_cur)

        logits(c_i, s_a)
        p_b[...] = jnp.zeros(p_b.shape, BF16)
        carry = (jnp.full((1, nl), NEG, F32), jnp.zeros((vT_ref.shape[1], nl), F32))
        carry = pair(0, carry, True)
        n_pairs = (n_chunks + 1) // 2 if use_sel else 1
        if use_sel:
            carry = lax.fori_loop(1, n_pairs, lambda t, cr: pair(t, cr, False), carry)
        acc = carry[1] + pv(c_i - 2 * n_pairs + 1, p_b)
        return acc[:HEAD_DIM] / acc[HEAD_DIM:HEAD_DIM + 1]

    o_sel = branch(ksel_ref, vselT_ref, st_ref, st_ref.shape[0], True, c_i + 1)
    o_win = branch(kwin_ref, vwinT_ref, wt_ref, wt_ref.shape[0], False, jnp.minimum(c_i + 1, 2))

    gates = _sigmoid(gate_ref[...])
    oT = gates[0:1] * o_cmp + gates[1:2] * o_sel + gates[2:3] * o_win
    o = oT.T
    for r in range(N_REP):
        o_ref[:, r * HEAD_DIM:(r + 1) * HEAD_DIM] = o[r * BLOCK:(r + 1) * BLOCK].astype(o_ref.dtype)


def _nsa_prompt(qT, kc, vcT, ksel, vselT, kwin, vwinT, st, wt, ct, gates, s):
    nb = s // BLOCK
    nq = s // BLOCK
    nl = N_REP * BLOCK
    nch = s // CHUNK_KEYS

    def whole(arr):
        shp = arr.shape[1:]
        return pl.BlockSpec((None,) + shp, lambda g, i: (g,) + (0,) * len(shp))

    return pl.pallas_call(
        functools.partial(_nsa_prompt_kernel, nb=nb),
        grid=(N_GROUPS, nq),
        in_specs=[pl.BlockSpec((None, HEAD_DIM, nl), lambda g, i: (g, 0, i)),
                  whole(kc), whole(vcT), whole(ksel), whole(vselT), whole(kwin), whole(vwinT),
                  whole(st), whole(wt), whole(ct),
                  pl.BlockSpec((None, None, 8, nl), lambda g, i: (g, i, 0, 0))],
        out_specs=pl.BlockSpec((BLOCK, N_REP * HEAD_DIM), lambda g, i: (i, g)),
        out_shape=jax.ShapeDtypeStruct((s, NSA_WIDTH), BF16),
        scratch_shapes=[pltpu.VMEM((nb, nl), F32),
                        pltpu.VMEM((nb, nl), F32),
                        pltpu.VMEM((CHUNK_KEYS, nl), F32),
                        pltpu.VMEM((CHUNK_KEYS, nl), F32),
                        pltpu.VMEM((CHUNK_KEYS, nl), BF16),
                        pltpu.VMEM((CHUNK_KEYS, nl), BF16)],
        compiler_params=_cparams(("parallel", "arbitrary")),
        name="nsa_prompt",
    )(qT, kc, vcT, ksel, vselT, kwin, vwinT, st, wt, ct, gates)


def _sample_nsa_kernel(pt_ref, cache_ref, newc_ref, qT_ref, knew_ref, vnew_ref, kwin_ref, vwin_ref,
                       pe_ref, w1_ref, w2_ref, cb_ref, nearb_ref, farb_ref, newb_ref, winb_ref, gate_ref, o_ref,
                       ring, xc, kcs, vsel, s_scr, selneg_scr, sems, *, n_pages, t_new, nbp, n_ring):
    b = pl.program_id(0)
    total = pl.num_programs(0) * n_pages
    past = n_pages * PAGE
    cur = past // BLOCK
    kinds_per_row = PAGED_KINDS * N_GROUPS
    qT = qT_ref[...]

    def page_dma(t, slot):
        return pltpu.make_async_copy(cache_ref.at[pt_ref[t // n_pages, t % n_pages]], ring.at[slot], sems.at[slot])

    @pl.when(b == 0)
    def _():
        for t in range(n_ring):
            page_dma(t, t).start()

    far = farb_ref[0:1, :]

    def page_step(j, c):
        t = b * n_pages + j
        slot = t % n_ring
        page_dma(t, slot).wait()
        buf = ring.at[slot]

        def rows(kg):
            return buf[pl.ds(kg, PAGE, stride=kinds_per_row), :]

        r0 = pl.multiple_of(j * PAGE, PAGE)
        base = pl.multiple_of(j * (PAGE // BLOCK) * XC_PITCH, 8)
        for kg in range(2 * N_GROUPS):
            x = rows(kg)
            for h in range(PAGE // BLOCK):
                xc[kg, pl.ds(base + h * XC_PITCH, BLOCK), :] = x[h * BLOCK:(h + 1) * BLOCK]
        k = jnp.concatenate([rows(4), rows(5)], axis=1).astype(BF16)
        s_scr[pl.ds(r0, PAGE), :] = _qk(k, qT) + jnp.where(j == n_pages - 1, nearb_ref[...], far)
        vsel[pl.ds(r0, PAGE), :] = jnp.concatenate([rows(6), rows(7)], axis=1).astype(BF16)

        @pl.when(t + n_ring < total)
        def _():
            page_dma(t + n_ring, slot).start()

        return c

    lax.fori_loop(0, n_pages, page_step, 0)

    tail = (nbp - cur) * XC_PITCH
    for kg in range(2 * N_GROUPS):
        xc[kg, pl.ds(cur * XC_PITCH, tail), :] = jnp.zeros((tail, HEAD_DIM), F32)
        xc[kg, pl.ds(cur * XC_PITCH, t_new), :] = newc_ref[:, kg * HEAD_DIM:(kg + 1) * HEAD_DIM]
    s_scr[pl.ds(past, PAGE), :] = _qk(knew_ref[...], qT) + newb_ref[...]
    vsel[pl.ds(past, PAGE), :] = vnew_ref[...]

    for kind in range(2):
        def step(pp, acc):
            p = 2 * pp
            pe0 = pe_ref[kind, pl.ds(p, 1), :]
            pe1 = pe_ref[kind, pl.ds(p + 1, 1), :]
            parts = []
            for g in range(N_GROUPS):
                x0 = xc[2 * kind + g, pl.ds(p, nbp, stride=XC_PITCH), :] + pe0
                x1 = xc[2 * kind + g, pl.ds(p + 1, nbp, stride=XC_PITCH), :] + pe1
                parts.append(jnp.concatenate([x0, x1], axis=1))
            a = jnp.concatenate(parts, axis=0).astype(BF16)
            w = w1_ref[kind, pl.ds(pl.multiple_of(p * HEAD_DIM, 2 * HEAD_DIM), 2 * HEAD_DIM), :]
            return acc + _dot(a, w)

        acc = lax.fori_loop(0, BLOCK // 2, step, jnp.zeros((N_GROUPS * nbp, PHI_HIDDEN), F32), unroll=2)
        out = _dot(_gelu_tanh(acc).astype(BF16), w2_ref[kind])
        kcs[kind, nbp:, :] = jnp.zeros((kcs.shape[1] - nbp, N_GROUPS * HEAD_DIM), F32)
        for g in range(N_GROUPS):
            kcs[kind, 0:nbp, g * HEAD_DIM:(g + 1) * HEAD_DIM] = out[g * nbp:(g + 1) * nbp]

    def softmax_rows(s):
        mx = jnp.max(s, axis=0, keepdims=True)
        p = jnp.where(s > 0.5 * NEG, jnp.exp(s - mx), 0.0)
        l = jnp.sum(p, axis=0, keepdims=True)
        return p / jnp.where(l > 0.0, l, 1.0)

    pn = softmax_rows(_qk(kcs[0].astype(BF16), qT) + cb_ref[...])
    o_cmp = _dot(pn.T.astype(BF16), kcs[1].astype(BF16))

    tot = pn + pltpu.roll(pn, t_new, 1) + pltpu.roll(pn, 2 * t_new, 1) + pltpu.roll(pn, 3 * t_new, 1)
    lane = lax.broadcasted_iota(jnp.int32, pn.shape, 1)
    top = jnp.where((lane // t_new) % N_REP == N_REP - 1, tot, 0.0)
    imp = top + pltpu.roll(top, LANES - t_new, 1) + pltpu.roll(top, LANES - 2 * t_new, 1) \
        + pltpu.roll(top, LANES - 3 * t_new, 1)
    n_io = lax.broadcasted_iota(jnp.int32, pn.shape, 0)
    forced = (n_io == 0) | (n_io == cur) | (n_io == cur - 1)
    score = jnp.where(forced, FORCED_SCORE, imp)
    score = jnp.where(n_io <= cur, score, -jnp.inf)
    sel = _select_blocks(score, n_io)
    selneg_scr[...] = jnp.where(sel > 0.5, 0.0, NEG)

    pw = softmax_rows(_qk(kwin_ref[...], qT) + winb_ref[...])
    o_win = _dot(pw.T.astype(BF16), vwin_ref[...])

    blk = PAGE // BLOCK

    def mask_step(c, mx):
        r0 = pl.multiple_of(c * PAGE, PAGE)
        neg = jnp.concatenate(
            [jnp.broadcast_to(selneg_scr[pl.ds(c * blk + h, 1), :], (BLOCK, LANES)) for h in range(blk)], axis=0)
        s = s_scr[pl.ds(r0, PAGE), :] + neg
        s_scr[pl.ds(r0, PAGE), :] = s
        return jnp.maximum(mx, jnp.max(s, axis=0, keepdims=True))

    mx = lax.fori_loop(0, n_pages + 1, mask_step, jnp.full((1, LANES), NEG, F32))

    def pv_rows(r0, n_rows, carry):
        l_run, acc = carry
        pp = jnp.exp(s_scr[pl.ds(r0, n_rows), :] - mx)
        acc = acc + _dot(pp.T.astype(BF16), vsel[pl.ds(r0, n_rows), :])
        return l_run + jnp.sum(pp, axis=0, keepdims=True), acc

    pv_pages = 4 if n_pages % 4 == 0 else 1
    pv_keys = pv_pages * PAGE
    carry = lax.fori_loop(0, n_pages // pv_pages,
                          lambda c, carry: pv_rows(pl.multiple_of(c * pv_keys, pv_keys), pv_keys, carry),
                          (jnp.zeros((1, LANES), F32), jnp.zeros((LANES, N_GROUPS * HEAD_DIM), F32)))
    l_fin, acc = pv_rows(past, PAGE, carry)
    l_col = jnp.broadcast_to(l_fin, (LANES, LANES)).T[:, 0:1]
    o_sel = acc / l_col

    gates = _sigmoid(gate_ref[...])
    o_ref[...] = gates[:, 0:1] * o_cmp + gates[:, 1:2] * o_sel + gates[:, 2:3] * o_win


def _sample_nsa(pt, cache_pages, new_cmp, qT, knew, vnew, kwin, vwin, pe, w1b, w2b,
                cb, nearb, farb, newb, winb, gates, nbp):
    bsz, n_pages = pt.shape
    t_new = new_cmp.shape[1]
    gh = N_GROUPS * HEAD_DIM
    n_keys = n_pages * PAGE + PAGE
    n_ring = min(SAMPLE_RING_PAGES, n_pages)

    def per_b(arr):
        shp = arr.shape[1:]
        return pl.BlockSpec((None,) + shp, lambda b, pt: (b,) + (0,) * len(shp))

    def const(arr, **kw):
        return pl.BlockSpec(arr.shape, lambda b, pt: (0,) * arr.ndim, **kw)

    grid_spec = pltpu.PrefetchScalarGridSpec(
        num_scalar_prefetch=1,
        grid=(bsz,),
        in_specs=[pl.BlockSpec(memory_space=pl.ANY),
                  per_b(new_cmp), per_b(qT), per_b(knew), per_b(vnew), per_b(kwin), per_b(vwin),
                  const(pe), const(w1b, pipeline_mode=pl.Buffered(1)), const(w2b),
                  const(cb), const(nearb), const(farb), const(newb), const(winb), per_b(gates)],
        out_specs=pl.BlockSpec((None, LANES, gh), lambda b, pt: (b, 0, 0)),
        scratch_shapes=[pltpu.VMEM((n_ring,) + cache_pages.shape[1:], F32),
                        pltpu.VMEM((2 * N_GROUPS, nbp * XC_PITCH, HEAD_DIM), F32),
                        pltpu.VMEM((2, cb.shape[0], gh), F32),
                        pltpu.VMEM((n_keys, gh), BF16),
                        pltpu.VMEM((n_keys, LANES), F32),
                        pltpu.VMEM((cb.shape[0], LANES), F32),
                        pltpu.SemaphoreType.DMA((n_ring,))],
    )
    return pl.pallas_call(
        functools.partial(_sample_nsa_kernel, n_pages=n_pages, t_new=t_new, nbp=nbp, n_ring=n_ring),
        grid_spec=grid_spec,
        out_shape=jax.ShapeDtypeStruct((bsz, LANES, gh), F32),
        compiler_params=_cparams(("arbitrary",)),
        name="sample_nsa",
    )(pt, cache_pages, new_cmp, qT, knew, vnew, kwin, vwin, pe, w1b, w2b, cb, nearb, farb, newb, winb, gates)


def _gmlp_kernel(u_ref, v_ref, m_ref, bcol_ref, g_ref, b_ref, oa_ref, vn_ref):
    vn = _ln(v_ref[...]) * g_ref[...] + b_ref[...]
    vn_ref[...] = vn
    vb = vn.astype(BF16)
    u = u_ref[...]
    gd = GMLP_WIDTH // GMLP_GROUPS
    for g in range(GMLP_GROUPS):
        mixed = _dot(m_ref[g], vb[:, g * gd:(g + 1) * gd]) + bcol_ref[:, g:g + 1]
        oa_ref[:, g * gd:(g + 1) * gd] = (u[:, g * gd:(g + 1) * gd] * mixed).astype(oa_ref.dtype)


def _gmlp(z, mix_m, bcol, gn_g, gn_b, tc):
    t = z.shape[0]
    return pl.pallas_call(
        _gmlp_kernel,
        grid=(t // tc,),
        in_specs=[pl.BlockSpec((tc, GMLP_WIDTH), lambda i: (i, COL_U // GMLP_WIDTH)),
                  pl.BlockSpec((tc, GMLP_WIDTH), lambda i: (i, COL_V // GMLP_WIDTH)),
                  pl.BlockSpec((GMLP_GROUPS, tc, tc), lambda i: (0, 0, 0)),
                  pl.BlockSpec((tc, LANES), lambda i: (0, 0)),
                  pl.BlockSpec((1, GMLP_WIDTH), lambda i: (0, 0)),
                  pl.BlockSpec((1, GMLP_WIDTH), lambda i: (0, 0))],
        out_specs=[pl.BlockSpec((tc, GMLP_WIDTH), lambda i: (i, 0)),
                   pl.BlockSpec((tc, GMLP_WIDTH), lambda i: (i, 0))],
        out_shape=[jax.ShapeDtypeStruct((t, GMLP_WIDTH), BF16),
                   jax.ShapeDtypeStruct((t, GMLP_WIDTH), F32)],
        compiler_params=_cparams(("parallel",)),
        name="gmlp",
    )(z, z, mix_m, bcol, gn_g, gn_b)


def _pool_kernel(x_ref, halo_ref, pw_ref, ps_ref, o_ref, ext_scr, *, tm, zero_first, pos0, pos_step):
    i = pl.program_id(0)
    halo = halo_ref[...]
    if zero_first:
        halo = jnp.where(i == 0, 0.0, halo)
    ext_scr[0:16, :] = halo
    x = x_ref[...]
    ext_scr[16:16 + tm, :] = x
    pos = pos0 + i * pos_step + lax.broadcasted_iota(jnp.int32, (tm, POOL_GROUP_DIM), 0)
    for gi, w in enumerate(POOL_WINDOWS):
        c0 = gi * POOL_GROUP_DIM
        acc = ext_scr[16:16 + tm, c0:c0 + POOL_GROUP_DIM]
        for k in range(1, w):
            acc = acc + ext_scr[16 - k:16 - k + tm, c0:c0 + POOL_GROUP_DIM]
        count = jnp.minimum(pos + 1, w).astype(F32)
        d = acc / count - x[:, c0:c0 + POOL_GROUP_DIM]
        y = _dot(d.astype(BF16), pw_ref[gi]) * ps_ref[:, c0:c0 + POOL_GROUP_DIM]
        o_ref[:, c0:c0 + POOL_GROUP_DIM] = y.astype(o_ref.dtype)


def _pool(z, halo_src, halo_map, pwb, ps, tm, zero_first, pos0, pos_step):
    t = z.shape[0]
    return pl.pallas_call(
        functools.partial(_pool_kernel, tm=tm, zero_first=zero_first, pos0=pos0, pos_step=pos_step),
        grid=(t // tm,),
        in_specs=[pl.BlockSpec((tm, POOL_WIDTH), lambda i: (i, COL_XC // POOL_WIDTH)),
                  pl.BlockSpec((16, POOL_WIDTH), halo_map),
                  pl.BlockSpec((len(POOL_WINDOWS), POOL_GROUP_DIM, POOL_GROUP_DIM), lambda i: (0, 0, 0)),
                  pl.BlockSpec((1, POOL_WIDTH), lambda i: (0, 0))],
        out_specs=pl.BlockSpec((tm, POOL_WIDTH), lambda i: (i, 0)),
        out_shape=jax.ShapeDtypeStruct((t, POOL_WIDTH), BF16),
        scratch_shapes=[pltpu.VMEM((16 + tm, POOL_WIDTH), F32)],
        compiler_params=_cparams(("parallel",)),
        name="pool",
    )(z, halo_src, pwb, ps)


def _route(logits_t, rb_col):
    aff = _sigmoid(logits_t)
    biased = aff + rb_col
    rows = [biased[e:e + 1] for e in range(N_EXPERTS)]
    top2 = []
    gscore = []
    for g in range(N_EXPERT_GROUPS):
        grp = rows[g * EXPERTS_PER_GROUP:(g + 1) * EXPERTS_PER_GROUP]
        gs = None
        for a in range(EXPERTS_PER_GROUP):
            rank = None
            for c in range(EXPERTS_PER_GROUP):
                if c == a:
                    continue
                ahead = (grp[c] >= grp[a]) if c < a else (grp[c] > grp[a])
                rank = ahead.astype(F32) if rank is None else rank + ahead.astype(F32)
            in2 = rank < 1.5
            top2.append(in2)
            contrib = jnp.where(in2, grp[a], 0.0)
            gs = contrib if gs is None else gs + contrib
        gscore.append(gs)
    out = []
    for g in range(N_EXPERT_GROUPS):
        win = None
        for c in range(N_EXPERT_GROUPS):
            if c == g:
                continue
            ok = (gscore[g] > gscore[c]) if c < g else (gscore[g] >= gscore[c])
            win = ok if win is None else (win & ok)
        for a in range(EXPERTS_PER_GROUP):
            e = g * EXPERTS_PER_GROUP + a
            out.append(jnp.where(win & top2[e], aff[e:e + 1], 0.0))
    selw = jnp.concatenate(out, axis=0)
    return selw / jnp.sum(selw, axis=0, keepdims=True)


def _merge_kernel(x_ref, oa_ref, ob_ref, oc_ref, g0_ref, g1_ref, g2_ref, gt1_ref, sh2_ref, sc2_ref,
                  ln1g_ref, ln1b_ref, pa_ref, pb_ref, pc_ref, wo_ref, rw_ref, rb_ref,
                  x1_ref, h2_ref, comb_ref, *, alpha):
    tm = x_ref.shape[0]

    merged = _sigmoid(g0_ref[...]) * _dot(oa_ref[...], pa_ref[...])
    merged = merged + _sigmoid(g1_ref[...]) * _dot(ob_ref[...], pb_ref[...])
    merged = merged + _sigmoid(g2_ref[...]) * _dot(oc_ref[...], pc_ref[...])
    mix = _dot(merged.astype(BF16), wo_ref[...])
    x1 = _ln(alpha * x_ref[...] + (1.0 + _mod_rows(gt1_ref, tm)) * mix) * ln1g_ref[...] + ln1b_ref[...]
    x1_ref[...] = x1
    h2 = _ln(x1) * (1.0 + _mod_rows(sc2_ref, tm)) + _mod_rows(sh2_ref, tm)
    h2_ref[...] = h2.astype(BF16)
    logits = _dot(h2.astype(BF16), rw_ref[...])
    comb_t = _route(logits.T[:N_EXPERTS], rb_ref[...][:N_EXPERTS, 0:1])
    comb_full = jnp.concatenate([comb_t, jnp.zeros((LANES - N_EXPERTS, tm), F32)], axis=0)
    comb_ref[...] = comb_full.T


def _merge(x, z, oa, ob, oc, mods, ln1g, ln1b, pa, pb, pc, wo, rw, rb, tm, alpha):
    t = x.shape[0]

    def const(arr):
        return pl.BlockSpec(arr.shape, lambda i: (0,) * arr.ndim, pipeline_mode=pl.Buffered(1))

    gcol = COL_GM // D_MODEL
    return pl.pallas_call(
        functools.partial(_merge_kernel, alpha=alpha),
        grid=(t // tm,),
        in_specs=[pl.BlockSpec((tm, D_MODEL), lambda i: (i, 0)),
                  pl.BlockSpec((tm, GMLP_WIDTH), lambda i: (i, 0)),
                  pl.BlockSpec((tm, NSA_WIDTH), lambda i: (i, 0)),
                  pl.BlockSpec((tm, POOL_WIDTH), lambda i: (i, 0)),
                  pl.BlockSpec((tm, D_MODEL), lambda i: (i, gcol)),
                  pl.BlockSpec((tm, D_MODEL), lambda i: (i, gcol + 1)),
                  pl.BlockSpec((tm, D_MODEL), lambda i: (i, gcol + 2)),
                  _mod_spec(mods, tm, 2), _mod_spec(mods, tm, 3), _mod_spec(mods, tm, 4),
                  const(ln1g), const(ln1b), const(pa), const(pb), const(pc), const(wo), const(rw), const(rb)],
        out_specs=[pl.BlockSpec((tm, D_MODEL), lambda i: (i, 0)),
                   pl.BlockSpec((tm, D_MODEL), lambda i: (i, 0)),
                   pl.BlockSpec((tm, LANES), lambda i: (i, 0))],
        out_shape=[jax.ShapeDtypeStruct((t, D_MODEL), F32),
                   jax.ShapeDtypeStruct((t, D_MODEL), BF16),
                   jax.ShapeDtypeStruct((t, LANES), F32)],
        compiler_params=_cparams(("parallel",)),
        name="merge",
    )(x, oa, ob, oc, z, z, z, mods, mods, mods, ln1g, ln1b, pa, pb, pc, wo, rw, rb)


def _moe_kernel(h_ref, comb_ref, wg_ref, wu_ref, wd_ref, x1_ref, gt2_ref, g_ref, b_ref, o_ref, acc_scr, *, alpha):
    e = pl.program_id(1)
    tm = h_ref.shape[0]

    @pl.when(e == 0)
    def _():
        acc_scr[...] = jnp.zeros_like(acc_scr)

    h = h_ref[...]
    comb = comb_ref[...]
    lane = lax.broadcasted_iota(jnp.int32, comb.shape, 1)
    w_col = jnp.sum(jnp.where(lane == e, comb, 0.0), axis=1, keepdims=True)
    act = _silu(_dot(h, wg_ref[...])) * _dot(h, wu_ref[...])
    acc_scr[...] += _dot((act * w_col).astype(BF16), wd_ref[...])

    @pl.when(e == N_EXPERTS - 1)
    def _():
        o_ref[...] = _ln(alpha * x1_ref[...] + (1.0 + _mod_rows(gt2_ref, tm)) * acc_scr[...]) * g_ref[...] + b_ref[...]


def _moe(h2, comb, wg, wu, wd, x1, mods, ln2g, ln2b, tm, alpha):
    t = h2.shape[0]
    return pl.pallas_call(
        functools.partial(_moe_kernel, alpha=alpha),
        grid=(t // tm, N_EXPERTS),
        in_specs=[pl.BlockSpec((tm, D_MODEL), lambda i, e: (i, 0)),
                  pl.BlockSpec((tm, LANES), lambda i, e: (i, 0)),
                  pl.BlockSpec((None, D_MODEL, D_EXPERT), lambda i, e: (e, 0, 0)),
                  pl.BlockSpec((None, D_MODEL, D_EXPERT), lambda i, e: (e, 0, 0)),
                  pl.BlockSpec((None, D_EXPERT, D_MODEL), lambda i, e: (e, 0, 0)),
                  pl.BlockSpec((tm, D_MODEL), lambda i, e: (i, 0)),
                  _mod_spec(mods, tm, 5),
                  pl.BlockSpec((1, D_MODEL), lambda i, e: (0, 0)),
                  pl.BlockSpec((1, D_MODEL), lambda i, e: (0, 0))],
        out_specs=pl.BlockSpec((tm, D_MODEL), lambda i, e: (i, 0)),
        out_shape=jax.ShapeDtypeStruct((t, D_MODEL), F32),
        scratch_shapes=[pltpu.VMEM((tm, D_MODEL), F32)],
        compiler_params=_cparams(("parallel", "arbitrary")),
        name="moe",
    )(h2, comb, wg, wu, wd, x1, mods, ln2g, ln2b)


def _rel_bucket(dist):
    n = jnp.maximum(dist, 0)
    max_exact = N_BUCKETS // 2
    nf = jnp.maximum(n, max_exact).astype(F32)
    large = max_exact + (jnp.log(nf / max_exact) / math.log(MAX_DISTANCE / max_exact)
                         * (N_BUCKETS - max_exact)).astype(jnp.int32)
    return jnp.where(n < max_exact, n, jnp.minimum(large, N_BUCKETS - 1))


def _bias_of(rel_bias, dist):
    return rel_bias[_rel_bucket(dist)].astype(F32)


def _prompt_tiles(rel_bias):
    kk = jnp.arange(BLOCK)[:, None]
    qq = jnp.arange(BLOCK)[None, :]

    def lanes(tile8, g):
        return jnp.concatenate([tile8[:, :, g * N_REP + r] for r in range(N_REP)], axis=1)

    def tile(m, lo, hi):
        d = m * BLOCK + qq - kk
        ok = (d >= lo) & (d < hi)
        return jnp.where(ok[:, :, None], _bias_of(rel_bias, d), NEG)

    big = 1 << 20
    neg_tile = jnp.full((BLOCK, BLOCK, N_HEADS), NEG, F32)
    sel_tiles = [tile(m, 0, big) for m in range(4)] + [neg_tile]
    win_tiles = [tile(m, 0, WINDOW) for m in range(WIN_BLOCKS + 1)] + [neg_tile]
    st = jnp.stack([jnp.stack([lanes(t, g) for t in sel_tiles]) for g in range(N_GROUPS)]) / QK_SCALE
    wt = jnp.stack([jnp.stack([lanes(t, g) for t in win_tiles]) for g in range(N_GROUPS)]) / QK_SCALE
    rows = []
    for m in range(4):
        d = m * BLOCK + jnp.arange(BLOCK) - (BLOCK - 1)
        rows.append(_bias_of(rel_bias, d))
    ct = jnp.stack([jnp.stack([jnp.concatenate([rw[:, g * N_REP + r] for r in range(N_REP)]) for rw in rows]
                              + [jnp.zeros((N_REP * BLOCK,), F32)] * 4) for g in range(N_GROUPS)])
    return st, wt, ct


def _sample_lanes(tile8, t_new):
    rows = tile8.shape[0]
    x = jnp.transpose(tile8, (0, 2, 1)).reshape(rows, N_HEADS * t_new)
    return jnp.pad(x, ((0, 0), (0, LANES - N_HEADS * t_new)))


def _sample_tiles(rel_bias, past, t_new, nbp, win_rows, win_pad):
    tq = past + jnp.arange(t_new)[None, :]

    def tile(k_pos, ok_extra=None, lo=0, hi=1 << 20):
        d = tq - k_pos[:, None]
        ok = (d >= lo) & (d < hi)
        if ok_extra is not None:
            ok = ok & ok_extra[:, None]
        return _sample_lanes(jnp.where(ok[:, :, None], _bias_of(rel_bias, d), NEG), t_new)

    n = jnp.arange(nbp)
    cb = tile(n * BLOCK + BLOCK - 1, n < past // BLOCK + 1)
    nearb = tile(past - PAGE + jnp.arange(PAGE))
    farb = jnp.broadcast_to(_sample_lanes(jnp.broadcast_to(
        rel_bias[N_BUCKETS - 1].astype(F32)[None, None, :], (1, t_new, N_HEADS)), t_new), (8, LANES))
    kn = jnp.arange(PAGE)
    newb = tile(past + kn, kn < t_new)
    wb = min(WINDOW, past)
    j = jnp.arange(win_pad)
    k_pos = past - wb + j
    winb = tile(k_pos, (j < win_rows) & (k_pos >= 0), 0, WINDOW)
    return cb, nearb, farb, newb, winb


def _mix_mats(ws, bs, tc, rep):
    n = tc // rep
    causal = jnp.tril(jnp.ones((CHUNK, CHUNK), F32))
    wc = (ws * causal)[:, :n, :n]
    eye = jnp.eye(rep, dtype=F32)
    m = jnp.einsum('ab,gts->gatbs', eye, wc).reshape(GMLP_GROUPS, tc, tc)
    bcol = jnp.tile(bs[:, :n].T, (rep, 1))
    return m.astype(BF16), jnp.pad(bcol, ((0, 0), (0, LANES - GMLP_GROUPS)))


def kernel(x_prompt, x_sample, cache_nsa_kv, state_win_kv, state_pool, page_table, c_prompt, c_sample, rel_bias, router_w, router_b, w_in, nsa_phi_pe, nsa_phi_w1, nsa_phi_w2, gmlp_norm_g, gmlp_norm_b, gmlp_ws, gmlp_bs, pool_w, pool_scale, proj_a, proj_b, proj_c, w_o, ada_w, ada_b, ln1_g, ln1_b, ln2_g, ln2_b, exp_w_gate, exp_w_up, exp_w_down):
    depth = w_in.shape[0]
    alpha = (2 * depth) ** 0.25
    s = x_prompt.shape[1]
    bsz, t_new = x_sample.shape[0], x_sample.shape[1]
    n_pool = cache_nsa_kv.shape[1]
    n_pages = page_table.shape[1]
    past = n_pages * PAGE
    ts = bsz * t_new
    assert x_prompt.shape[0] == 1 and s % CHUNK_KEYS == 0 and past % PAGE == 0
    assert N_HEADS * t_new <= LANES and t_new <= BLOCK and (past // BLOCK) >= N_SEL
    nb_s = past // BLOCK + 1
    nbp = -(-nb_s // 8) * 8

    n_c = 1 + bsz
    c_all = jnp.pad(jnp.concatenate([c_prompt, c_sample], axis=0), ((0, -n_c % 8), (0, 0)))
    mods_all = _ada_mods(c_all, ada_w, ada_b)

    cache_pages = cache_nsa_kv.reshape(depth * n_pool, PAGE * PAGED_KINDS * N_GROUPS, HEAD_DIM)
    st, wt, ct = _prompt_tiles(rel_bias)
    win_rows = min(WINDOW, past) + t_new
    win_pad = -(-win_rows // LANES) * LANES
    assert nbp <= SAMPLE_BLOCKS_PAD
    cb_s, nearb_s, farb_s, newb_s, winb_s = _sample_tiles(rel_bias, past, t_new, SAMPLE_BLOCKS_PAD, win_rows, win_pad)
    rw = jnp.pad(router_w, ((0, 0), (0, LANES - N_EXPERTS))).astype(BF16)
    rb = jnp.pad(router_b.reshape(N_EXPERTS, 1), ((0, LANES - N_EXPERTS), (0, LANES - 1)))

    xp = x_prompt.reshape(s, D_MODEL)
    xs = x_sample.reshape(ts, D_MODEL)
    outs = {k: [] for k in ("nsa_p", "nsa_s", "win_p", "win_s", "pool_p", "pool_s", "v_s")}

    for l in range(depth):
        w = w_in[l]
        w_pad = jnp.concatenate(
            [w[:, 0:2560], w[:, 2584:10264], w[:, 2560:2584],
             jnp.zeros((D_MODEL, N_IN_PAD - 10264), F32)], axis=1).astype(BF16)
        pe = nsa_phi_pe[l]
        w1b = nsa_phi_w1[l].astype(BF16)
        w2b = nsa_phi_w2[l].astype(BF16)
        pab, pbb, pcb, wob = (a[l].astype(BF16) for a in (proj_a, proj_b, proj_c, w_o))
        wg, wu, wd = (a[l].astype(BF16) for a in (exp_w_gate, exp_w_up, exp_w_down))
        pwb = pool_w[l].astype(BF16)
        ps = pool_scale[l].reshape(1, POOL_WIDTH)
        gng = gmlp_norm_g[l].reshape(1, GMLP_WIDTH)
        gnb = gmlp_norm_b[l].reshape(1, GMLP_WIDTH)
        ln1g, ln1b, ln2g, ln2b = (a[l].reshape(1, D_MODEL) for a in (ln1_g, ln1_b, ln2_g, ln2_b))
        mods_p = jnp.broadcast_to(mods_all[l, 0:1], (8, 6 * D_MODEL))
        mods_s = jnp.repeat(mods_all[l, 1:1 + bsz], t_new, axis=0)

        z = _in_proj(xp, mods_p, w_pad, 1024)
        gh = N_GROUPS * HEAD_DIM
        outs["nsa_p"].append(z[:, COL_KV:COL_KV + PAGED_KINDS * gh].reshape(1, s, PAGED_KINDS, N_GROUPS, HEAD_DIM))
        outs["win_p"].append(z[s - min(WINDOW, s):, COL_KV + PAGED_KINDS * gh:COL_KV + KV_KINDS * gh]
                             .reshape(1, min(WINDOW, s), 2, N_GROUPS, HEAD_DIM))
        outs["pool_p"].append(z[s - POOL_BUF:, COL_XC:COL_XC + POOL_WIDTH].reshape(1, POOL_BUF, POOL_WIDTH))

        nb = s // BLOCK
        cmp = _compress_prompt(z, pe, w1b, w2b).reshape(2, N_GROUPS, nb, HEAD_DIM)
        kc = cmp[0].astype(BF16)
        vcT = jnp.transpose(cmp[1], (0, 2, 1)).astype(BF16)
        nch = s // CHUNK_KEYS

        def kv_cols(kind, g):
            c0 = COL_KV + (kind * N_GROUPS + g) * HEAD_DIM
            return z[:, c0:c0 + HEAD_DIM].reshape(nch, CHUNK_KEYS, HEAD_DIM)

        def k_chunks(kind):
            return jnp.stack([kv_cols(kind, g) for g in range(N_GROUPS)]).astype(BF16)

        def vT_chunks(kind):
            vt = jnp.stack([jnp.transpose(kv_cols(kind, g), (0, 2, 1)) for g in range(N_GROUPS)])
            ones = jnp.zeros((N_GROUPS, nch, 8, CHUNK_KEYS), F32).at[:, :, 0].set(1.0)
            return jnp.concatenate([vt, ones], axis=2).astype(BF16)

        q = z[:, COL_Q:COL_Q + NSA_WIDTH].reshape(nb, BLOCK, N_GROUPS, N_REP, HEAD_DIM)
        qT = jnp.transpose(q, (2, 4, 0, 3, 1)).reshape(N_GROUPS, HEAD_DIM, nb * N_REP * BLOCK).astype(BF16)
        gn = z[:, COL_GN:COL_GN + 3 * N_HEADS].reshape(nb, BLOCK, 3, N_GROUPS, N_REP)
        gates = jnp.transpose(gn, (3, 0, 2, 4, 1)).reshape(N_GROUPS, nb, 3, N_REP * BLOCK)
        gates = jnp.pad(gates, ((0, 0), (0, 0), (0, 5), (0, 0)))
        ob = _nsa_prompt(qT, kc, vcT, k_chunks(2), vT_chunks(3), k_chunks(4), vT_chunks(5), st, wt, ct, gates, s)

        mm, bcol = _mix_mats(gmlp_ws[l], gmlp_bs[l], CHUNK, 1)
        oa, _ = _gmlp(z, mm, bcol, gng, gnb, CHUNK)
        tp = 512
        cxb = COL_XC // POOL_WIDTH
        oc = _pool(z, z, lambda i: (jnp.maximum(i * (tp // 16) - 1, 0), cxb), pwb, ps, tp, True, 0, tp)
        x1, h2, comb = _merge(xp, z, oa, ob, oc, mods_p, ln1g, ln1b, pab, pbb, pcb, wob, rw, rb, 256, alpha)
        xp = _moe(h2, comb, wg, wu, wd, x1, mods_p, ln2g, ln2b, 512, alpha)

        zs = _in_proj(xs, mods_s, w_pad, ts)
        kvs = zs[:, COL_KV:COL_KV + 1536].reshape(bsz, t_new, KV_KINDS, N_GROUPS, HEAD_DIM)
        outs["nsa_s"].append(kvs[:, :, :PAGED_KINDS])
        win_full = jnp.concatenate([state_win_kv[l], kvs[:, :, PAGED_KINDS:]], axis=1)
        outs["win_s"].append(win_full[:, win_full.shape[1] - min(WINDOW, past + t_new):])
        xcs = zs[:, COL_XC:COL_XC + POOL_WIDTH].reshape(bsz, t_new, POOL_WIDTH)
        ext = jnp.concatenate([state_pool[l], xcs], axis=1)
        outs["pool_s"].append(ext[:, ext.shape[1] - POOL_BUF:])

        pt = page_table + l * n_pool
        new_cmp = kvs[:, :, 0:2].reshape(bsz, t_new, 2 * N_GROUPS * HEAD_DIM)
        gh = N_GROUPS * HEAD_DIM
        knew = jnp.pad(kvs[:, :, 2].reshape(bsz, t_new, gh), ((0, 0), (0, PAGE - t_new), (0, 0))).astype(BF16)
        vnew = jnp.pad(kvs[:, :, 3].reshape(bsz, t_new, gh), ((0, 0), (0, PAGE - t_new), (0, 0))).astype(BF16)
        wf = jnp.pad(win_full.reshape(bsz, win_rows, 2, gh), ((0, 0), (0, win_pad - win_rows), (0, 0), (0, 0)))
        kwin_s = wf[:, :, 0].astype(BF16)
        vwin_s = wf[:, :, 1].astype(BF16)
        qs = zs[:, COL_Q:COL_Q + NSA_WIDTH].reshape(bsz, t_new, N_GROUPS, N_REP, HEAD_DIM)
        qs = jnp.transpose(qs, (0, 2, 4, 3, 1)).reshape(bsz, N_GROUPS, HEAD_DIM, N_REP * t_new)
        qT_s = jnp.zeros((bsz, gh, LANES), F32)
        for g in range(N_GROUPS):
            qT_s = qT_s.at[:, g * HEAD_DIM:(g + 1) * HEAD_DIM,
                           g * N_REP * t_new:(g + 1) * N_REP * t_new].set(qs[:, g])
        qT_s = qT_s.astype(BF16)
        gns = zs[:, COL_GN:COL_GN + 3 * N_HEADS].reshape(bsz, t_new, 3, N_HEADS)
        gates_s = jnp.transpose(gns, (0, 3, 1, 2)).reshape(bsz, N_HEADS * t_new, 3)
        gates_s = jnp.pad(gates_s, ((0, 0), (0, LANES - N_HEADS * t_new), (0, LANES - 3)))
        o_s = _sample_nsa(pt, cache_pages, new_cmp, qT_s, knew, vnew, kwin_s, vwin_s, pe, w1b, w2b,
                          cb_s, nearb_s, farb_s, newb_s, winb_s, gates_s, nbp)
        o4 = o_s[:, :N_HEADS * t_new].reshape(bsz, N_GROUPS, N_REP, t_new, N_GROUPS, HEAD_DIM)
        o4 = jnp.stack([o4[:, g, :, :, g] for g in range(N_GROUPS)], axis=1)
        ob_s = jnp.transpose(o4, (0, 3, 1, 2, 4)).reshape(ts, NSA_WIDTH).astype(BF16)

        mm_s, bcol_s = _mix_mats(gmlp_ws[l], gmlp_bs[l], ts, bsz)
        oa_s, vn_s = _gmlp(zs, mm_s, bcol_s, gng, gnb, ts)
        outs["v_s"].append(vn_s.reshape(bsz, t_new, GMLP_WIDTH))
        halo = jnp.pad(state_pool[l], ((0, 0), (16 - POOL_BUF, 0), (0, 0))).reshape(bsz * 16, POOL_WIDTH)
        oc_s = _pool(zs, halo, lambda i: (i, 0), pwb, ps, t_new, False, past, 0)
        x1s, h2s, comb_s = _merge(xs, zs, oa_s, ob_s, oc_s, mods_s, ln1g, ln1b, pab, pbb, pcb, wob, rw, rb, ts, alpha)
        xs = _moe(h2s, comb_s, wg, wu, wd, x1s, mods_s, ln2g, ln2b, ts, alpha)

    return (xp.reshape(1, s, D_MODEL), xs.reshape(bsz, t_new, D_MODEL),
            jnp.stack(outs["nsa_p"]), jnp.stack(outs["nsa_s"]), jnp.stack(outs["win_p"]), jnp.stack(outs["win_s"]),
            jnp.stack(outs["pool_p"]), jnp.stack(outs["pool_s"]), jnp.stack(outs["v_s"]))
```

```python
import functools
import math

import jax
import jax.numpy as jnp
from jax import lax
from jax.experimental import pallas as pl
from jax.experimental.pallas import tpu as pltpu

F32 = jnp.float32
BF16 = jnp.bfloat16

D_MODEL = 2048
HEAD_DIM = 128
N_GROUPS = 2
N_REP = 4
N_HEADS = N_GROUPS * N_REP
NSA_WIDTH = N_HEADS * HEAD_DIM
KV_KINDS = 6
PAGED_KINDS = 4
BLOCK = 64
N_SEL = 16
WINDOW = 512
WIN_BLOCKS = WINDOW // BLOCK
PHI_HIDDEN = 2 * HEAD_DIM
PAGE = 128
GMLP_GROUPS = 4
GMLP_WIDTH = 512
CHUNK = 128
POOL_WINDOWS = (2, 4, 8, 16)
POOL_GROUP_DIM = 128
POOL_WIDTH = 512
POOL_BUF = 15
N_BUCKETS = 32
MAX_DISTANCE = 128
N_EXPERTS = 16
N_EXPERT_GROUPS = 4
EXPERTS_PER_GROUP = 4
D_EXPERT = 512
LN_EPS = 1e-5
NEG = -1e30
FORCED_SCORE = 1e9

COL_Q = 0
COL_KV = 1024
COL_U = 2560
COL_V = 3072
COL_XC = 3584
COL_GM = 4096
COL_GN = 10240
N_IN_PAD = 10752
IN_TN = 512

LANES = 128
VMEM_LIMIT = 56 * 1024 * 1024
CHUNK_BLOCKS = 8
CHUNK_KEYS = CHUNK_BLOCKS * BLOCK
XC_PITCH = BLOCK + 8
SAMPLE_RING_PAGES = 16
SAMPLE_BLOCKS_PAD = 256


def _cparams(sem):
    return pltpu.CompilerParams(dimension_semantics=sem, vmem_limit_bytes=VMEM_LIMIT)


def _ln(x):
    mu = jnp.mean(x, axis=-1, keepdims=True)
    xc = x - mu
    var = jnp.mean(xc * xc, axis=-1, keepdims=True)
    return xc * lax.rsqrt(var + LN_EPS)


def _sigmoid(x):
    return 1.0 / (1.0 + jnp.exp(-x))


def _silu(x):
    return x * _sigmoid(x)


def _gelu_tanh(x):
    return 0.5 * x * (1.0 + jnp.tanh(math.sqrt(2.0 / math.pi) * (x + 0.044715 * (x * x * x))))


def _dot(a, b):
    return jnp.dot(a, b, preferred_element_type=F32)


QK_SCALE = HEAD_DIM ** -0.5
EXP2_SCALE = QK_SCALE * math.log2(math.e)


def _qk(k, qT):
    return _dot(k, qT) * QK_SCALE


def _ada_kernel(c_ref, w_ref, b_ref, o_ref):
    c = c_ref[...]
    o_ref[...] = _dot(_silu(c).astype(BF16), w_ref[...].astype(BF16)) + b_ref[...]


def _ada_mods(c_all, ada_w, ada_b):
    depth = ada_w.shape[0]
    rows = c_all.shape[0]
    n = ada_w.shape[2]
    tn = 1024
    return pl.pallas_call(
        _ada_kernel,
        grid=(depth, n // tn),
        in_specs=[pl.BlockSpec((rows, D_MODEL), lambda l, j: (0, 0)),
                  pl.BlockSpec((None, D_MODEL, tn), lambda l, j: (l, 0, j)),
                  pl.BlockSpec((None, 1, tn), lambda l, j: (l, 0, j))],
        out_specs=pl.BlockSpec((None, rows, tn), lambda l, j: (l, 0, j)),
        out_shape=jax.ShapeDtypeStruct((depth, rows, n), F32),
        compiler_params=_cparams(("parallel", "parallel")),
        name="ada_mods",
    )(c_all, ada_w, ada_b.reshape(depth, 1, n))


def _mod_rows(ref, tm):
    return ref[...] if ref.shape[0] == tm else ref[0:1, :]


N_QKV_TILES = COL_U // IN_TN
N_MAIN_TILES = COL_GN // IN_TN


def _inproj_kernel(x_ref, sh_ref, sc_ref, wa_ref, wb_ref, wc_ref, z_ref, zb_ref, h_scr):
    tm = x_ref.shape[0]
    j = pl.program_id(1)

    @pl.when(j == 0)
    def _():
        h = _ln(x_ref[...]) * (1.0 + _mod_rows(sc_ref, tm)) + _mod_rows(sh_ref, tm)
        h_scr[...] = h.astype(BF16)

    @pl.when(j < N_QKV_TILES)
    def _():
        z = _dot(h_scr[...], wa_ref[...])
        z_ref[...] = z
        zb_ref[...] = z.astype(BF16)

    @pl.when((j >= N_QKV_TILES) & (j < N_MAIN_TILES))
    def _():
        z_ref[...] = _dot(h_scr[...], wb_ref[...])

    @pl.when(j == N_MAIN_TILES)
    def _():
        z_ref[...] = _dot(h_scr[...], wc_ref[...])


def _mod_spec(mods, tm, col):
    rows = mods.shape[0]
    if rows == 8:
        return pl.BlockSpec((8, D_MODEL), lambda i, *_: (0, col))
    return pl.BlockSpec((tm, D_MODEL), lambda i, *_: (i, col))


def _in_proj_weights(w):
    n_gate = 3 * N_HEADS
    wa = w[:, :COL_U].astype(BF16)
    wb = w[:, COL_U + n_gate:].astype(BF16)
    wc = jnp.pad(w[:, COL_U:COL_U + n_gate], ((0, 0), (0, IN_TN - n_gate))).astype(BF16)
    return wa, wb, wc


def _in_proj(x, mods, weights, tm):
    t = x.shape[0]
    wa, wb, wc = weights
    n_b = N_MAIN_TILES - N_QKV_TILES
    return pl.pallas_call(
        _inproj_kernel,
        grid=(t // tm, N_IN_PAD // IN_TN),
        in_specs=[pl.BlockSpec((tm, D_MODEL), lambda i, j: (i, 0)),
                  _mod_spec(mods, tm, 0), _mod_spec(mods, tm, 1),
                  pl.BlockSpec((D_MODEL, IN_TN), lambda i, j: (0, jnp.minimum(j, N_QKV_TILES - 1))),
                  pl.BlockSpec((D_MODEL, IN_TN), lambda i, j: (0, jnp.clip(j - N_QKV_TILES, 0, n_b - 1))),
                  pl.BlockSpec((D_MODEL, IN_TN), lambda i, j: (0, 0))],
        out_specs=[pl.BlockSpec((tm, IN_TN), lambda i, j: (i, j)),
                   pl.BlockSpec((tm, IN_TN), lambda i, j: (i, jnp.minimum(j, N_QKV_TILES - 1)))],
        out_shape=[jax.ShapeDtypeStruct((t, N_IN_PAD), F32),
                   jax.ShapeDtypeStruct((t, COL_U), BF16)],
        scratch_shapes=[pltpu.VMEM((tm, D_MODEL), BF16)],
        compiler_params=_cparams(("parallel", "arbitrary")),
        name="in_proj",
    )(x, mods, mods, wa, wb, wc)


def _compress_pair(x0_ref, x1_ref, nb, pe_ref, w1_ref, w2_ref):
    def step(p, acc):
        pe = pe_ref[pl.ds(p, 1), :]
        a0 = x0_ref[pl.ds(p, nb, stride=BLOCK), :] + pe
        a1 = x1_ref[pl.ds(p, nb, stride=BLOCK), :] + pe
        a = jnp.concatenate([a0, a1], axis=0).astype(BF16)
        w = w1_ref[pl.ds(pl.multiple_of(p * HEAD_DIM, HEAD_DIM), HEAD_DIM), :]
        return acc + _dot(a, w)

    acc = lax.fori_loop(0, BLOCK, step, jnp.zeros((2 * nb, PHI_HIDDEN), F32))
    return _dot(_gelu_tanh(acc).astype(BF16), w2_ref[...])


def _compress_prompt_kernel(x0_ref, x1_ref, pe_ref, w1_ref, w2_ref, o_ref, *, nb):
    o_ref[...] = _compress_pair(x0_ref, x1_ref, nb, pe_ref, w1_ref, w2_ref)


def _compress_prompt(z, pe, w1b, w2b):
    s = z.shape[0]
    nb = s // BLOCK
    cb = COL_KV // HEAD_DIM
    return pl.pallas_call(
        functools.partial(_compress_prompt_kernel, nb=nb),
        grid=(2,),
        in_specs=[pl.BlockSpec((s, HEAD_DIM), lambda k: (0, cb + 2 * k)),
                  pl.BlockSpec((s, HEAD_DIM), lambda k: (0, cb + 2 * k + 1)),
                  pl.BlockSpec((None, BLOCK, HEAD_DIM), lambda k: (k, 0, 0)),
                  pl.BlockSpec((None, BLOCK * HEAD_DIM, PHI_HIDDEN), lambda k: (k, 0, 0)),
                  pl.BlockSpec((None, PHI_HIDDEN, HEAD_DIM), lambda k: (k, 0, 0))],
        out_specs=pl.BlockSpec((None, 2 * nb, HEAD_DIM), lambda k: (k, 0, 0)),
        out_shape=jax.ShapeDtypeStruct((2, 2 * nb, HEAD_DIM), F32),
        compiler_params=_cparams(("parallel",)),
        name="compress_prompt",
    )(z, z, pe, w1b, w2b)


def _select_blocks(score, n_io):
    def body(_, carry):
        work, sel = carry
        mval = jnp.max(work, axis=0, keepdims=True)
        cand = jnp.where(work == mval, n_io, jnp.int32(1 << 30))
        idx = jnp.min(cand, axis=0, keepdims=True)
        pick = n_io == idx
        return jnp.where(pick, -jnp.inf, work), jnp.where(pick, 1.0, sel)

    _, sel = lax.fori_loop(0, N_SEL, body, (score, jnp.zeros(score.shape, F32)))
    return sel


def _nsa_prompt_kernel(qT_ref, kc_ref, vcT_ref, ksel_ref, vselT_ref, kwin_ref, vwinT_ref,
                       st_ref, wt_ref, ct_ref, gate_ref, o_ref, selneg_scr, selfar_scr, s_a, s_b, p_a, p_b, *, nb):
    i = pl.program_id(1)
    qT = qT_ref[...]
    nl = N_REP * BLOCK

    n_io = lax.broadcasted_iota(jnp.int32, (nb, nl), 0)
    qq = lax.broadcasted_iota(jnp.int32, (nb, nl), 1) % BLOCK
    m = i - n_io
    ct = ct_ref[...]
    bias = jnp.where(m == 0, ct[0:1], jnp.where(m == 1, ct[1:2], jnp.where(m == 2, ct[2:3], ct[3:4])))
    valid = (m >= 1) | ((m == 0) & (qq == BLOCK - 1))
    s = jnp.where(valid, _qk(kc_ref[...], qT) + bias, NEG)
    mx = jnp.max(s, axis=0, keepdims=True)
    p = jnp.where(valid, jnp.exp(s - mx), 0.0)
    l = jnp.sum(p, axis=0, keepdims=True)
    pn = p / jnp.where(l > 0.0, l, 1.0)
    o_cmp = _dot(vcT_ref[...], pn.astype(BF16))

    half = pn[:, :LANES] + pn[:, LANES:]
    imp = half + pltpu.roll(half, BLOCK, 1)
    n_io1 = n_io[:, :LANES]
    forced = (n_io1 == 0) | (n_io1 == i) | (n_io1 == i - 1)
    score = jnp.where(forced, FORCED_SCORE, imp)
    score = jnp.where(n_io1 <= i, score, -1.0)
    sel = _select_blocks(score, n_io1)
    neg1 = jnp.where((sel > 0.5) & (score >= 0.0), 0.0, NEG)
    neg2 = jnp.concatenate([neg1, neg1], axis=1)
    selneg_scr[...] = neg2
    far_row = st_ref[st_ref.shape[0] - 2][0:1, :]
    selfar_scr[...] = jnp.where(neg2 == 0.0, far_row, NEG)

    c_i = i // CHUNK_BLOCKS

    def branch(k_ref, vT_ref, tile_ref, n_tiles, use_sel, n_chunks):
        def logits(c, dst):
            dst[...] = _dot(k_ref[jnp.maximum(c, 0)], qT)

        def pv(c, p_ref):
            return _dot(vT_ref[jnp.clip(c, 0, vT_ref.shape[0] - 1)], p_ref[...])

        def consume(c, src, m_run, near):
            live = c >= 0
            cz = jnp.maximum(c, 0)
            parts = []
            for b in range(CHUNK_BLOCKS):
                n = cz * CHUNK_BLOCKS + b
                sb = src[b * BLOCK:(b + 1) * BLOCK, :]
                if near:
                    mm = i - n
                    if use_sel:
                        tidx = jnp.where(mm < 0, n_tiles - 1, jnp.minimum(mm, n_tiles - 2))
                    else:
                        tidx = jnp.where((mm < 0) | (mm > n_tiles - 2), n_tiles - 1, mm)
                    sb = sb + tile_ref[jnp.where(live, tidx, n_tiles - 1)]
                    if use_sel:
                        sb = sb + selneg_scr[pl.ds(n, 1), :]
                else:
                    sb = sb + jnp.where(live, selfar_scr[pl.ds(n, 1), :], NEG)
                parts.append(sb)
            s2 = jnp.concatenate(parts, axis=0)
            m_new = jnp.maximum(m_run, jnp.max(s2, axis=0, keepdims=True))
            alpha = jnp.exp2((m_run - m_new) * EXP2_SCALE)
            pp = jnp.exp2((s2 - m_new) * EXP2_SCALE)
            return m_new, alpha, pp.astype(BF16)

        def pair(t, carry, near):
            m_run, acc = carry
            c0 = c_i - 2 * t
            logits(c0 - 1, s_b)
            r_prev = pv(c0 + 1, p_b)
            m_run, alpha, pp = consume(c0, s_a, m_run, near)
            p_a[...] = pp
            acc = alpha * (acc + r_prev)
            logits(c0 - 2, s_a)
            r_cur = pv(c0, p_a)
            m_run, alpha, pp = consume(c0 - 1, s_b, m_run, near)
            p_b[...] = pp
            return m_run, alpha * (acc + r_cur)

        logits(c_i, s_a)
        p_b[...] = jnp.zeros(p_b.shape, BF16)
        carry = (jnp.full((1, nl), NEG, F32), jnp.zeros((vT_ref.shape[1], nl), F32))
        carry = pair(0, carry, True)
        n_pairs = (n_chunks + 1) // 2 if use_sel else 1
        if use_sel:
            carry = lax.fori_loop(1, n_pairs, lambda t, cr: pair(t, cr, False), carry)
        acc = carry[1] + pv(c_i - 2 * n_pairs + 1, p_b)
        return acc[:HEAD_DIM] / acc[HEAD_DIM:HEAD_DIM + 1]

    o_sel = branch(ksel_ref, vselT_ref, st_ref, st_ref.shape[0], True, c_i + 1)
    o_win = branch(kwin_ref, vwinT_ref, wt_ref, wt_ref.shape[0], False, jnp.minimum(c_i + 1, 2))

    gates = _sigmoid(gate_ref[...])
    oT = gates[0:1] * o_cmp + gates[1:2] * o_sel + gates[2:3] * o_win
    o = oT.T
    for r in range(N_REP):
        o_ref[:, r * HEAD_DIM:(r + 1) * HEAD_DIM] = o[r * BLOCK:(r + 1) * BLOCK].astype(o_ref.dtype)


def _nsa_prompt(qT, kc, vcT, ksel, vselT, kwin, vwinT, st, wt, ct, gates, s):
    nb = s // BLOCK
    nq = s // BLOCK
    nl = N_REP * BLOCK
    nch = s // CHUNK_KEYS

    def whole(arr):
        shp = arr.shape[1:]
        return pl.BlockSpec((None,) + shp, lambda g, i: (g,) + (0,) * len(shp))

    return pl.pallas_call(
        functools.partial(_nsa_prompt_kernel, nb=nb),
        grid=(N_GROUPS, nq),
        in_specs=[pl.BlockSpec((None, HEAD_DIM, nl), lambda g, i: (g, 0, i)),
                  whole(kc), whole(vcT), whole(ksel), whole(vselT), whole(kwin), whole(vwinT),
                  whole(st), whole(wt), whole(ct),
                  pl.BlockSpec((None, None, 8, nl), lambda g, i: (g, i, 0, 0))],
        out_specs=pl.BlockSpec((BLOCK, N_REP * HEAD_DIM), lambda g, i: (i, g)),
        out_shape=jax.ShapeDtypeStruct((s, NSA_WIDTH), BF16),
        scratch_shapes=[pltpu.VMEM((nb, nl), F32),
                        pltpu.VMEM((nb, nl), F32),
                        pltpu.VMEM((CHUNK_KEYS, nl), F32),
                        pltpu.VMEM((CHUNK_KEYS, nl), F32),
                        pltpu.VMEM((CHUNK_KEYS, nl), BF16),
                        pltpu.VMEM((CHUNK_KEYS, nl), BF16)],
        compiler_params=_cparams(("parallel", "arbitrary")),
        name="nsa_prompt",
    )(qT, kc, vcT, ksel, vselT, kwin, vwinT, st, wt, ct, gates)


def _sample_nsa_kernel(pt_ref, cache_ref, newc_ref, qT_ref, knew_ref, vnew_ref, kwin_ref, vwin_ref,
                       pe_ref, w1_ref, w2_ref, cb_ref, nearb_ref, farb_ref, newb_ref, winb_ref, gate_ref, o_ref,
                       ring, xc, kcs, vsel, s_scr, selneg_scr, sems, *, n_pages, t_new, nbp, n_ring):
    b = pl.program_id(0)
    total = pl.num_programs(0) * n_pages
    past = n_pages * PAGE
    cur = past // BLOCK
    kinds_per_row = PAGED_KINDS * N_GROUPS
    qT = qT_ref[...]

    def page_dma(t, slot):
        return pltpu.make_async_copy(cache_ref.at[pt_ref[t // n_pages, t % n_pages]], ring.at[slot], sems.at[slot])

    @pl.when(b == 0)
    def _():
        for t in range(n_ring):
            page_dma(t, t).start()

    far = farb_ref[0:1, :]

    def page_step(j, c):
        t = b * n_pages + j
        slot = t % n_ring
        page_dma(t, slot).wait()
        buf = ring.at[slot]

        def rows(kg):
            return buf[pl.ds(kg, PAGE, stride=kinds_per_row), :]

        r0 = pl.multiple_of(j * PAGE, PAGE)
        base = pl.multiple_of(j * (PAGE // BLOCK) * XC_PITCH, 8)
        for kg in range(2 * N_GROUPS):
            x = rows(kg)
            for h in range(PAGE // BLOCK):
                xc[kg, pl.ds(base + h * XC_PITCH, BLOCK), :] = x[h * BLOCK:(h + 1) * BLOCK]
        k = jnp.concatenate([rows(4), rows(5)], axis=1).astype(BF16)
        s_scr[pl.ds(r0, PAGE), :] = _qk(k, qT) + jnp.where(j == n_pages - 1, nearb_ref[...], far)
        vsel[pl.ds(r0, PAGE), :] = jnp.concatenate([rows(6), rows(7)], axis=1).astype(BF16)

        @pl.when(t + n_ring < total)
        def _():
            page_dma(t + n_ring, slot).start()

        return c

    lax.fori_loop(0, n_pages, page_step, 0)

    tail = (nbp - cur) * XC_PITCH
    for kg in range(2 * N_GROUPS):
        xc[kg, pl.ds(cur * XC_PITCH, tail), :] = jnp.zeros((tail, HEAD_DIM), F32)
        xc[kg, pl.ds(cur * XC_PITCH, t_new), :] = newc_ref[:, kg * HEAD_DIM:(kg + 1) * HEAD_DIM]
    s_scr[pl.ds(past, PAGE), :] = _qk(knew_ref[...], qT) + newb_ref[...]
    vsel[pl.ds(past, PAGE), :] = vnew_ref[...]

    for kind in range(2):
        def step(pp, acc):
            p = 2 * pp
            pe0 = pe_ref[kind, pl.ds(p, 1), :]
            pe1 = pe_ref[kind, pl.ds(p + 1, 1), :]
            parts = []
            for g in range(N_GROUPS):
                x0 = xc[2 * kind + g, pl.ds(p, nbp, stride=XC_PITCH), :] + pe0
                x1 = xc[2 * kind + g, pl.ds(p + 1, nbp, stride=XC_PITCH), :] + pe1
                parts.append(jnp.concatenate([x0, x1], axis=1))
            a = jnp.concatenate(parts, axis=0).astype(BF16)
            w = w1_ref[kind, pl.ds(pl.multiple_of(p * HEAD_DIM, 2 * HEAD_DIM), 2 * HEAD_DIM), :]
            return acc + _dot(a, w)

        acc = lax.fori_loop(0, BLOCK // 2, step, jnp.zeros((N_GROUPS * nbp, PHI_HIDDEN), F32), unroll=2)
        out = _dot(_gelu_tanh(acc).astype(BF16), w2_ref[kind])
        kcs[kind, nbp:, :] = jnp.zeros((kcs.shape[1] - nbp, N_GROUPS * HEAD_DIM), F32)
        for g in range(N_GROUPS):
            kcs[kind, 0:nbp, g * HEAD_DIM:(g + 1) * HEAD_DIM] = out[g * nbp:(g + 1) * nbp]

    def softmax_rows(s):
        mx = jnp.max(s, axis=0, keepdims=True)
        p = jnp.where(s > 0.5 * NEG, jnp.exp(s - mx), 0.0)
        l = jnp.sum(p, axis=0, keepdims=True)
        return p / jnp.where(l > 0.0, l, 1.0)

    pn = softmax_rows(_qk(kcs[0].astype(BF16), qT) + cb_ref[...])
    o_cmp = _dot(pn.T.astype(BF16), kcs[1].astype(BF16))

    tot = pn + pltpu.roll(pn, t_new, 1) + pltpu.roll(pn, 2 * t_new, 1) + pltpu.roll(pn, 3 * t_new, 1)
    lane = lax.broadcasted_iota(jnp.int32, pn.shape, 1)
    top = jnp.where((lane // t_new) % N_REP == N_REP - 1, tot, 0.0)
    imp = top + pltpu.roll(top, LANES - t_new, 1) + pltpu.roll(top, LANES - 2 * t_new, 1) \
        + pltpu.roll(top, LANES - 3 * t_new, 1)
    n_io = lax.broadcasted_iota(jnp.int32, pn.shape, 0)
    forced = (n_io == 0) | (n_io == cur) | (n_io == cur - 1)
    score = jnp.where(forced, FORCED_SCORE, imp)
    score = jnp.where(n_io <= cur, score, -jnp.inf)
    sel = _select_blocks(score, n_io)
    selneg_scr[...] = jnp.where(sel > 0.5, 0.0, NEG)

    pw = softmax_rows(_qk(kwin_ref[...], qT) + winb_ref[...])
    o_win = _dot(pw.T.astype(BF16), vwin_ref[...])

    blk = PAGE // BLOCK

    def mask_step(c, mx):
        r0 = pl.multiple_of(c * PAGE, PAGE)
        neg = jnp.concatenate(
            [jnp.broadcast_to(selneg_scr[pl.ds(c * blk + h, 1), :], (BLOCK, LANES)) for h in range(blk)], axis=0)
        s = s_scr[pl.ds(r0, PAGE), :] + neg
        s_scr[pl.ds(r0, PAGE), :] = s
        return jnp.maximum(mx, jnp.max(s, axis=0, keepdims=True))

    mx = lax.fori_loop(0, n_pages + 1, mask_step, jnp.full((1, LANES), NEG, F32))

    def pv_rows(r0, n_rows, carry):
        l_run, acc = carry
        pp = jnp.exp(s_scr[pl.ds(r0, n_rows), :] - mx)
        acc = acc + _dot(pp.T.astype(BF16), vsel[pl.ds(r0, n_rows), :])
        return l_run + jnp.sum(pp, axis=0, keepdims=True), acc

    pv_pages = 4 if n_pages % 4 == 0 else 1
    pv_keys = pv_pages * PAGE
    carry = lax.fori_loop(0, n_pages // pv_pages,
                          lambda c, carry: pv_rows(pl.multiple_of(c * pv_keys, pv_keys), pv_keys, carry),
                          (jnp.zeros((1, LANES), F32), jnp.zeros((LANES, N_GROUPS * HEAD_DIM), F32)))
    l_fin, acc = pv_rows(past, PAGE, carry)
    l_col = jnp.broadcast_to(l_fin, (LANES, LANES)).T[:, 0:1]
    o_sel = acc / l_col

    gates = _sigmoid(gate_ref[...])
    o_ref[...] = gates[:, 0:1] * o_cmp + gates[:, 1:2] * o_sel + gates[:, 2:3] * o_win


def _sample_nsa(pt, cache_pages, new_cmp, qT, knew, vnew, kwin, vwin, pe, w1b, w2b,
                cb, nearb, farb, newb, winb, gates, nbp):
    bsz, n_pages = pt.shape
    t_new = new_cmp.shape[1]
    gh = N_GROUPS * HEAD_DIM
    n_keys = n_pages * PAGE + PAGE
    n_ring = min(SAMPLE_RING_PAGES, n_pages)

    def per_b(arr):
        shp = arr.shape[1:]
        return pl.BlockSpec((None,) + shp, lambda b, pt: (b,) + (0,) * len(shp))

    def const(arr, **kw):
        return pl.BlockSpec(arr.shape, lambda b, pt: (0,) * arr.ndim, **kw)

    grid_spec = pltpu.PrefetchScalarGridSpec(
        num_scalar_prefetch=1,
        grid=(bsz,),
        in_specs=[pl.BlockSpec(memory_space=pl.ANY),
                  per_b(new_cmp), per_b(qT), per_b(knew), per_b(vnew), per_b(kwin), per_b(vwin),
                  const(pe), const(w1b, pipeline_mode=pl.Buffered(1)), const(w2b),
                  const(cb), const(nearb), const(farb), const(newb), const(winb), per_b(gates)],
        out_specs=pl.BlockSpec((None, LANES, gh), lambda b, pt: (b, 0, 0)),
        scratch_shapes=[pltpu.VMEM((n_ring,) + cache_pages.shape[1:], F32),
                        pltpu.VMEM((2 * N_GROUPS, nbp * XC_PITCH, HEAD_DIM), F32),
                        pltpu.VMEM((2, cb.shape[0], gh), F32),
                        pltpu.VMEM((n_keys, gh), BF16),
                        pltpu.VMEM((n_keys, LANES), F32),
                        pltpu.VMEM((cb.shape[0], LANES), F32),
                        pltpu.SemaphoreType.DMA((n_ring,))],
    )
    return pl.pallas_call(
        functools.partial(_sample_nsa_kernel, n_pages=n_pages, t_new=t_new, nbp=nbp, n_ring=n_ring),
        grid_spec=grid_spec,
        out_shape=jax.ShapeDtypeStruct((bsz, LANES, gh), F32),
        compiler_params=_cparams(("arbitrary",)),
        name="sample_nsa",
    )(pt, cache_pages, new_cmp, qT, knew, vnew, kwin, vwin, pe, w1b, w2b, cb, nearb, farb, newb, winb, gates)


def _gmlp_kernel(u_ref, v_ref, m_ref, bcol_ref, g_ref, b_ref, oa_ref, vn_ref):
    vn = _ln(v_ref[...]) * g_ref[...] + b_ref[...]
    vn_ref[...] = vn
    vb = vn.astype(BF16)
    u = u_ref[...]
    gd = GMLP_WIDTH // GMLP_GROUPS
    for g in range(GMLP_GROUPS):
        mixed = _dot(m_ref[g], vb[:, g * gd:(g + 1) * gd]) + bcol_ref[:, g:g + 1]
        oa_ref[:, g * gd:(g + 1) * gd] = (u[:, g * gd:(g + 1) * gd] * mixed).astype(oa_ref.dtype)


def _gmlp(z, mix_m, bcol, gn_g, gn_b, tc):
    t = z.shape[0]
    return pl.pallas_call(
        _gmlp_kernel,
        grid=(t // tc,),
        in_specs=[pl.BlockSpec((tc, GMLP_WIDTH), lambda i: (i, COL_U // GMLP_WIDTH)),
                  pl.BlockSpec((tc, GMLP_WIDTH), lambda i: (i, COL_V // GMLP_WIDTH)),
                  pl.BlockSpec((GMLP_GROUPS, tc, tc), lambda i: (0, 0, 0)),
                  pl.BlockSpec((tc, LANES), lambda i: (0, 0)),
                  pl.BlockSpec((1, GMLP_WIDTH), lambda i: (0, 0)),
                  pl.BlockSpec((1, GMLP_WIDTH), lambda i: (0, 0))],
        out_specs=[pl.BlockSpec((tc, GMLP_WIDTH), lambda i: (i, 0)),
                   pl.BlockSpec((tc, GMLP_WIDTH), lambda i: (i, 0))],
        out_shape=[jax.ShapeDtypeStruct((t, GMLP_WIDTH), BF16),
                   jax.ShapeDtypeStruct((t, GMLP_WIDTH), F32)],
        compiler_params=_cparams(("parallel",)),
        name="gmlp",
    )(z, z, mix_m, bcol, gn_g, gn_b)


def _pool_kernel(x_ref, halo_ref, pw_ref, ps_ref, o_ref, ext_scr, *, tm, zero_first, pos0, pos_step):
    i = pl.program_id(0)
    halo = halo_ref[...]
    if zero_first:
        halo = jnp.where(i == 0, 0.0, halo)
    ext_scr[0:16, :] = halo
    x = x_ref[...]
    ext_scr[16:16 + tm, :] = x
    pos = pos0 + i * pos_step + lax.broadcasted_iota(jnp.int32, (tm, POOL_GROUP_DIM), 0)
    for gi, w in enumerate(POOL_WINDOWS):
        c0 = gi * POOL_GROUP_DIM
        acc = ext_scr[16:16 + tm, c0:c0 + POOL_GROUP_DIM]
        for k in range(1, w):
            acc = acc + ext_scr[16 - k:16 - k + tm, c0:c0 + POOL_GROUP_DIM]
        count = jnp.minimum(pos + 1, w).astype(F32)
        d = acc / count - x[:, c0:c0 + POOL_GROUP_DIM]
        y = _dot(d.astype(BF16), pw_ref[gi]) * ps_ref[:, c0:c0 + POOL_GROUP_DIM]
        o_ref[:, c0:c0 + POOL_GROUP_DIM] = y.astype(o_ref.dtype)


def _pool(z, halo_src, halo_map, pwb, ps, tm, zero_first, pos0, pos_step):
    t = z.shape[0]
    return pl.pallas_call(
        functools.partial(_pool_kernel, tm=tm, zero_first=zero_first, pos0=pos0, pos_step=pos_step),
        grid=(t // tm,),
        in_specs=[pl.BlockSpec((tm, POOL_WIDTH), lambda i: (i, COL_XC // POOL_WIDTH)),
                  pl.BlockSpec((16, POOL_WIDTH), halo_map),
                  pl.BlockSpec((len(POOL_WINDOWS), POOL_GROUP_DIM, POOL_GROUP_DIM), lambda i: (0, 0, 0)),
                  pl.BlockSpec((1, POOL_WIDTH), lambda i: (0, 0))],
        out_specs=pl.BlockSpec((tm, POOL_WIDTH), lambda i: (i, 0)),
        out_shape=jax.ShapeDtypeStruct((t, POOL_WIDTH), BF16),
        scratch_shapes=[pltpu.VMEM((16 + tm, POOL_WIDTH), F32)],
        compiler_params=_cparams(("parallel",)),
        name="pool",
    )(z, halo_src, pwb, ps)


def _route(logits_t, rb_col):
    aff = _sigmoid(logits_t)
    biased = aff + rb_col
    rows = [biased[e:e + 1] for e in range(N_EXPERTS)]
    top2 = []
    gscore = []
    for g in range(N_EXPERT_GROUPS):
        grp = rows[g * EXPERTS_PER_GROUP:(g + 1) * EXPERTS_PER_GROUP]
        gs = None
        for a in range(EXPERTS_PER_GROUP):
            rank = None
            for c in range(EXPERTS_PER_GROUP):
                if c == a:
                    continue
                ahead = (grp[c] >= grp[a]) if c < a else (grp[c] > grp[a])
                rank = ahead.astype(F32) if rank is None else rank + ahead.astype(F32)
            in2 = rank < 1.5
            top2.append(in2)
            contrib = jnp.where(in2, grp[a], 0.0)
            gs = contrib if gs is None else gs + contrib
        gscore.append(gs)
    out = []
    for g in range(N_EXPERT_GROUPS):
        win = None
        for c in range(N_EXPERT_GROUPS):
            if c == g:
                continue
            ok = (gscore[g] > gscore[c]) if c < g else (gscore[g] >= gscore[c])
            win = ok if win is None else (win & ok)
        for a in range(EXPERTS_PER_GROUP):
            e = g * EXPERTS_PER_GROUP + a
            out.append(jnp.where(win & top2[e], aff[e:e + 1], 0.0))
    selw = jnp.concatenate(out, axis=0)
    return selw / jnp.sum(selw, axis=0, keepdims=True)


def _merge_kernel(x_ref, oa_ref, ob_ref, oc_ref, g0_ref, g1_ref, g2_ref, gt1_ref, sh2_ref, sc2_ref,
                  ln1g_ref, ln1b_ref, pa_ref, pb_ref, pc_ref, wo_ref, rw_ref, rb_ref,
                  x1_ref, h2_ref, comb_ref, *, alpha):
    tm = x_ref.shape[0]

    merged = _sigmoid(g0_ref[...]) * _dot(oa_ref[...], pa_ref[...])
    merged = merged + _sigmoid(g1_ref[...]) * _dot(ob_ref[...], pb_ref[...])
    merged = merged + _sigmoid(g2_ref[...]) * _dot(oc_ref[...], pc_ref[...])
    mix = _dot(merged.astype(BF16), wo_ref[...])
    x1 = _ln(alpha * x_ref[...] + (1.0 + _mod_rows(gt1_ref, tm)) * mix) * ln1g_ref[...] + ln1b_ref[...]
    x1_ref[...] = x1
    h2 = _ln(x1) * (1.0 + _mod_rows(sc2_ref, tm)) + _mod_rows(sh2_ref, tm)
    h2_ref[...] = h2.astype(BF16)
    logits = _dot(h2.astype(BF16), rw_ref[...])
    comb_t = _route(logits.T[:N_EXPERTS], rb_ref[...][:N_EXPERTS, 0:1])
    comb_full = jnp.concatenate([comb_t, jnp.zeros((LANES - N_EXPERTS, tm), F32)], axis=0)
    comb_ref[...] = comb_full.T


def _merge(x, z, oa, ob, oc, mods, ln1g, ln1b, pa, pb, pc, wo, rw, rb, tm, alpha):
    t = x.shape[0]

    def const(arr):
        return pl.BlockSpec(arr.shape, lambda i: (0,) * arr.ndim, pipeline_mode=pl.Buffered(1))

    gcol = COL_GM // D_MODEL
    return pl.pallas_call(
        functools.partial(_merge_kernel, alpha=alpha),
        grid=(t // tm,),
        in_specs=[pl.BlockSpec((tm, D_MODEL), lambda i: (i, 0)),
                  pl.BlockSpec((tm, GMLP_WIDTH), lambda i: (i, 0)),
                  pl.BlockSpec((tm, NSA_WIDTH), lambda i: (i, 0)),
                  pl.BlockSpec((tm, POOL_WIDTH), lambda i: (i, 0)),
                  pl.BlockSpec((tm, D_MODEL), lambda i: (i, gcol)),
                  pl.BlockSpec((tm, D_MODEL), lambda i: (i, gcol + 1)),
                  pl.BlockSpec((tm, D_MODEL), lambda i: (i, gcol + 2)),
                  _mod_spec(mods, tm, 2), _mod_spec(mods, tm, 3), _mod_spec(mods, tm, 4),
                  const(ln1g), const(ln1b), const(pa), const(pb), const(pc), const(wo), const(rw), const(rb)],
        out_specs=[pl.BlockSpec((tm, D_MODEL), lambda i: (i, 0)),
                   pl.BlockSpec((tm, D_MODEL), lambda i: (i, 0)),
                   pl.BlockSpec((tm, LANES), lambda i: (i, 0))],
        out_shape=[jax.ShapeDtypeStruct((t, D_MODEL), F32),
                   jax.ShapeDtypeStruct((t, D_MODEL), BF16),
                   jax.ShapeDtypeStruct((t, LANES), F32)],
        compiler_params=_cparams(("parallel",)),
        name="merge",
    )(x, oa, ob, oc, z, z, z, mods, mods, mods, ln1g, ln1b, pa, pb, pc, wo, rw, rb)


def _moe_kernel(h_ref, comb_ref, wg_ref, wu_ref, wd_ref, x1_ref, gt2_ref, g_ref, b_ref, o_ref, acc_scr, *, alpha):
    e = pl.program_id(1)
    tm = h_ref.shape[0]

    @pl.when(e == 0)
    def _():
        acc_scr[...] = jnp.zeros_like(acc_scr)

    h = h_ref[...]
    comb = comb_ref[...]
    lane = lax.broadcasted_iota(jnp.int32, comb.shape, 1)
    w_col = jnp.sum(jnp.where(lane == e, comb, 0.0), axis=1, keepdims=True)
    act = _silu(_dot(h, wg_ref[...])) * _dot(h, wu_ref[...])
    acc_scr[...] += _dot((act * w_col).astype(BF16), wd_ref[...])

    @pl.when(e == N_EXPERTS - 1)
    def _():
        o_ref[...] = _ln(alpha * x1_ref[...] + (1.0 + _mod_rows(gt2_ref, tm)) * acc_scr[...]) * g_ref[...] + b_ref[...]


def _moe(h2, comb, wg, wu, wd, x1, mods, ln2g, ln2b, tm, alpha):
    t = h2.shape[0]
    return pl.pallas_call(
        functools.partial(_moe_kernel, alpha=alpha),
        grid=(t // tm, N_EXPERTS),
        in_specs=[pl.BlockSpec((tm, D_MODEL), lambda i, e: (i, 0)),
                  pl.BlockSpec((tm, LANES), lambda i, e: (i, 0)),
                  pl.BlockSpec((None, D_MODEL, D_EXPERT), lambda i, e: (e, 0, 0)),
                  pl.BlockSpec((None, D_MODEL, D_EXPERT), lambda i, e: (e, 0, 0)),
                  pl.BlockSpec((None, D_EXPERT, D_MODEL), lambda i, e: (e, 0, 0)),
                  pl.BlockSpec((tm, D_MODEL), lambda i, e: (i, 0)),
                  _mod_spec(mods, tm, 5),
                  pl.BlockSpec((1, D_MODEL), lambda i, e: (0, 0)),
                  pl.BlockSpec((1, D_MODEL), lambda i, e: (0, 0))],
        out_specs=pl.BlockSpec((tm, D_MODEL), lambda i, e: (i, 0)),
        out_shape=jax.ShapeDtypeStruct((t, D_MODEL), F32),
        scratch_shapes=[pltpu.VMEM((tm, D_MODEL), F32)],
        compiler_params=_cparams(("parallel", "arbitrary")),
        name="moe",
    )(h2, comb, wg, wu, wd, x1, mods, ln2g, ln2b)


def _rel_bucket(dist):
    n = jnp.maximum(dist, 0)
    max_exact = N_BUCKETS // 2
    nf = jnp.maximum(n, max_exact).astype(F32)
    large = max_exact + (jnp.log(nf / max_exact) / math.log(MAX_DISTANCE / max_exact)
                         * (N_BUCKETS - max_exact)).astype(jnp.int32)
    return jnp.where(n < max_exact, n, jnp.minimum(large, N_BUCKETS - 1))


def _bias_of(rel_bias, dist):
    return rel_bias[_rel_bucket(dist)].astype(F32)


def _prompt_tiles(rel_bias):
    kk = jnp.arange(BLOCK)[:, None]
    qq = jnp.arange(BLOCK)[None, :]

    def lanes(tile8, g):
        return jnp.concatenate([tile8[:, :, g * N_REP + r] for r in range(N_REP)], axis=1)

    def tile(m, lo, hi):
        d = m * BLOCK + qq - kk
        ok = (d >= lo) & (d < hi)
        return jnp.where(ok[:, :, None], _bias_of(rel_bias, d), NEG)

    big = 1 << 20
    neg_tile = jnp.full((BLOCK, BLOCK, N_HEADS), NEG, F32)
    sel_tiles = [tile(m, 0, big) for m in range(4)] + [neg_tile]
    win_tiles = [tile(m, 0, WINDOW) for m in range(WIN_BLOCKS + 1)] + [neg_tile]
    st = jnp.stack([jnp.stack([lanes(t, g) for t in sel_tiles]) for g in range(N_GROUPS)]) / QK_SCALE
    wt = jnp.stack([jnp.stack([lanes(t, g) for t in win_tiles]) for g in range(N_GROUPS)]) / QK_SCALE
    rows = []
    for m in range(4):
        d = m * BLOCK + jnp.arange(BLOCK) - (BLOCK - 1)
        rows.append(_bias_of(rel_bias, d))
    ct = jnp.stack([jnp.stack([jnp.concatenate([rw[:, g * N_REP + r] for r in range(N_REP)]) for rw in rows]
                              + [jnp.zeros((N_REP * BLOCK,), F32)] * 4) for g in range(N_GROUPS)])
    return st, wt, ct


def _sample_lanes(tile8, t_new):
    rows = tile8.shape[0]
    x = jnp.transpose(tile8, (0, 2, 1)).reshape(rows, N_HEADS * t_new)
    return jnp.pad(x, ((0, 0), (0, LANES - N_HEADS * t_new)))


def _sample_tiles(rel_bias, past, t_new, nbp, win_rows, win_pad):
    tq = past + jnp.arange(t_new)[None, :]

    def tile(k_pos, ok_extra=None, lo=0, hi=1 << 20):
        d = tq - k_pos[:, None]
        ok = (d >= lo) & (d < hi)
        if ok_extra is not None:
            ok = ok & ok_extra[:, None]
        return _sample_lanes(jnp.where(ok[:, :, None], _bias_of(rel_bias, d), NEG), t_new)

    n = jnp.arange(nbp)
    cb = tile(n * BLOCK + BLOCK - 1, n < past // BLOCK + 1)
    nearb = tile(past - PAGE + jnp.arange(PAGE))
    farb = jnp.broadcast_to(_sample_lanes(jnp.broadcast_to(
        rel_bias[N_BUCKETS - 1].astype(F32)[None, None, :], (1, t_new, N_HEADS)), t_new), (8, LANES))
    kn = jnp.arange(PAGE)
    newb = tile(past + kn, kn < t_new)
    wb = min(WINDOW, past)
    j = jnp.arange(win_pad)
    k_pos = past - wb + j
    winb = tile(k_pos, (j < win_rows) & (k_pos >= 0), 0, WINDOW)
    return cb, nearb, farb, newb, winb


def _mix_mats(ws, bs, tc, rep):
    n = tc // rep
    causal = jnp.tril(jnp.ones((CHUNK, CHUNK), F32))
    wc = (ws * causal)[:, :n, :n]
    eye = jnp.eye(rep, dtype=F32)
    m = jnp.einsum('ab,gts->gatbs', eye, wc).reshape(GMLP_GROUPS, tc, tc)
    bcol = jnp.tile(bs[:, :n].T, (rep, 1))
    return m.astype(BF16), jnp.pad(bcol, ((0, 0), (0, LANES - GMLP_GROUPS)))


def kernel(x_prompt, x_sample, cache_nsa_kv, state_win_kv, state_pool, page_table, c_prompt, c_sample, rel_bias, router_w, router_b, w_in, nsa_phi_pe, nsa_phi_w1, nsa_phi_w2, gmlp_norm_g, gmlp_norm_b, gmlp_ws, gmlp_bs, pool_w, pool_scale, proj_a, proj_b, proj_c, w_o, ada_w, ada_b, ln1_g, ln1_b, ln2_g, ln2_b, exp_w_gate, exp_w_up, exp_w_down):
    depth = w_in.shape[0]
    alpha = (2 * depth) ** 0.25
    s = x_prompt.shape[1]
    bsz, t_new = x_sample.shape[0], x_sample.shape[1]
    n_pool = cache_nsa_kv.shape[1]
    n_pages = page_table.shape[1]
    past = n_pages * PAGE
    ts = bsz * t_new
    assert x_prompt.shape[0] == 1 and s % CHUNK_KEYS == 0 and past % PAGE == 0
    assert N_HEADS * t_new <= LANES and t_new <= BLOCK and (past // BLOCK) >= N_SEL
    nb_s = past // BLOCK + 1
    nbp = -(-nb_s // 8) * 8

    n_c = 1 + bsz
    c_all = jnp.pad(jnp.concatenate([c_prompt, c_sample], axis=0), ((0, -n_c % 8), (0, 0)))
    mods_all = _ada_mods(c_all, ada_w, ada_b)

    cache_pages = cache_nsa_kv.reshape(depth * n_pool, PAGE * PAGED_KINDS * N_GROUPS, HEAD_DIM)
    st, wt, ct = _prompt_tiles(rel_bias)
    win_rows = min(WINDOW, past) + t_new
    win_pad = -(-win_rows // LANES) * LANES
    assert nbp <= SAMPLE_BLOCKS_PAD
    cb_s, nearb_s, farb_s, newb_s, winb_s = _sample_tiles(rel_bias, past, t_new, SAMPLE_BLOCKS_PAD, win_rows, win_pad)
    rw = jnp.pad(router_w, ((0, 0), (0, LANES - N_EXPERTS))).astype(BF16)
    rb = jnp.pad(router_b.reshape(N_EXPERTS, 1), ((0, LANES - N_EXPERTS), (0, LANES - 1)))

    xp = x_prompt.reshape(s, D_MODEL)
    xs = x_sample.reshape(ts, D_MODEL)
    outs = {k: [] for k in ("nsa_p", "nsa_s", "win_p", "win_s", "pool_p", "pool_s", "v_s")}

    for l in range(depth):
        w_parts = _in_proj_weights(w_in[l])
        pe = nsa_phi_pe[l]
        w1b = nsa_phi_w1[l].astype(BF16)
        w2b = nsa_phi_w2[l].astype(BF16)
        pab, pbb, pcb, wob = (a[l].astype(BF16) for a in (proj_a, proj_b, proj_c, w_o))
        wg, wu, wd = (a[l].astype(BF16) for a in (exp_w_gate, exp_w_up, exp_w_down))
        pwb = pool_w[l].astype(BF16)
        ps = pool_scale[l].reshape(1, POOL_WIDTH)
        gng = gmlp_norm_g[l].reshape(1, GMLP_WIDTH)
        gnb = gmlp_norm_b[l].reshape(1, GMLP_WIDTH)
        ln1g, ln1b, ln2g, ln2b = (a[l].reshape(1, D_MODEL) for a in (ln1_g, ln1_b, ln2_g, ln2_b))
        mods_p = jnp.broadcast_to(mods_all[l, 0:1], (8, 6 * D_MODEL))
        mods_s = jnp.repeat(mods_all[l, 1:1 + bsz], t_new, axis=0)

        z, zb = _in_proj(xp, mods_p, w_parts, 1024)
        gh = N_GROUPS * HEAD_DIM
        outs["nsa_p"].append(z[:, COL_KV:COL_KV + PAGED_KINDS * gh].reshape(1, s, PAGED_KINDS, N_GROUPS, HEAD_DIM))
        outs["win_p"].append(z[s - min(WINDOW, s):, COL_KV + PAGED_KINDS * gh:COL_KV + KV_KINDS * gh]
                             .reshape(1, min(WINDOW, s), 2, N_GROUPS, HEAD_DIM))
        outs["pool_p"].append(z[s - POOL_BUF:, COL_XC:COL_XC + POOL_WIDTH].reshape(1, POOL_BUF, POOL_WIDTH))

        nb = s // BLOCK
        cmp = _compress_prompt(z, pe, w1b, w2b).reshape(2, N_GROUPS, nb, HEAD_DIM)
        kc = cmp[0].astype(BF16)
        vcT = jnp.transpose(cmp[1], (0, 2, 1)).astype(BF16)
        nch = s // CHUNK_KEYS

        def kv_cols(kind, g):
            c0 = COL_KV + (kind * N_GROUPS + g) * HEAD_DIM
            return zb[:, c0:c0 + HEAD_DIM].reshape(nch, CHUNK_KEYS, HEAD_DIM)

        def k_chunks(kind):
            return jnp.stack([kv_cols(kind, g) for g in range(N_GROUPS)])

        def vT_chunks(kind):
            vt = jnp.stack([jnp.transpose(kv_cols(kind, g), (0, 2, 1)) for g in range(N_GROUPS)])
            ones = jnp.zeros((N_GROUPS, nch, 8, CHUNK_KEYS), BF16).at[:, :, 0].set(1.0)
            return jnp.concatenate([vt, ones], axis=2)

        q = zb[:, COL_Q:COL_Q + NSA_WIDTH].reshape(nb, BLOCK, N_GROUPS, N_REP, HEAD_DIM)
        qT = jnp.transpose(q, (2, 4, 0, 3, 1)).reshape(N_GROUPS, HEAD_DIM, nb * N_REP * BLOCK)
        gn = z[:, COL_GN:COL_GN + 3 * N_HEADS].reshape(nb, BLOCK, 3, N_GROUPS, N_REP)
        gates = jnp.transpose(gn, (3, 0, 2, 4, 1)).reshape(N_GROUPS, nb, 3, N_REP * BLOCK)
        gates = jnp.pad(gates, ((0, 0), (0, 0), (0, 5), (0, 0)))
        ob = _nsa_prompt(qT, kc, vcT, k_chunks(2), vT_chunks(3), k_chunks(4), vT_chunks(5), st, wt, ct, gates, s)

        mm, bcol = _mix_mats(gmlp_ws[l], gmlp_bs[l], CHUNK, 1)
        oa, _ = _gmlp(z, mm, bcol, gng, gnb, CHUNK)
        tp = 512
        cxb = COL_XC // POOL_WIDTH
        oc = _pool(z, z, lambda i: (jnp.maximum(i * (tp // 16) - 1, 0), cxb), pwb, ps, tp, True, 0, tp)
        x1, h2, comb = _merge(xp, z, oa, ob, oc, mods_p, ln1g, ln1b, pab, pbb, pcb, wob, rw, rb, 256, alpha)
        xp = _moe(h2, comb, wg, wu, wd, x1, mods_p, ln2g, ln2b, 512, alpha)

        zs, zbs = _in_proj(xs, mods_s, w_parts, ts)
        kvs = zs[:, COL_KV:COL_KV + 1536].reshape(bsz, t_new, KV_KINDS, N_GROUPS, HEAD_DIM)
        kvbs = zbs[:, COL_KV:COL_KV + 1536].reshape(bsz, t_new, KV_KINDS, gh)
        outs["nsa_s"].append(kvs[:, :, :PAGED_KINDS])
        win_full = jnp.concatenate([state_win_kv[l], kvs[:, :, PAGED_KINDS:]], axis=1)
        outs["win_s"].append(win_full[:, win_full.shape[1] - min(WINDOW, past + t_new):])
        xcs = zs[:, COL_XC:COL_XC + POOL_WIDTH].reshape(bsz, t_new, POOL_WIDTH)
        ext = jnp.concatenate([state_pool[l], xcs], axis=1)
        outs["pool_s"].append(ext[:, ext.shape[1] - POOL_BUF:])

        pt = page_table + l * n_pool
        new_cmp = kvs[:, :, 0:2].reshape(bsz, t_new, 2 * N_GROUPS * HEAD_DIM)
        knew = jnp.pad(kvbs[:, :, 2], ((0, 0), (0, PAGE - t_new), (0, 0)))
        vnew = jnp.pad(kvbs[:, :, 3], ((0, 0), (0, PAGE - t_new), (0, 0)))
        wf = jnp.pad(win_full.reshape(bsz, win_rows, 2, gh), ((0, 0), (0, win_pad - win_rows), (0, 0), (0, 0)))
        kwin_s = wf[:, :, 0].astype(BF16)
        vwin_s = wf[:, :, 1].astype(BF16)
        qs = zbs[:, COL_Q:COL_Q + NSA_WIDTH].reshape(bsz, t_new, N_GROUPS, N_REP, HEAD_DIM)
        qs = jnp.transpose(qs, (0, 2, 4, 3, 1)).reshape(bsz, N_GROUPS, HEAD_DIM, N_REP * t_new)
        n_q = N_REP * t_new
        qT_s = jnp.concatenate(
            [jnp.pad(qs[:, g], ((0, 0), (0, 0), (g * n_q, LANES - (g + 1) * n_q))) for g in range(N_GROUPS)], axis=1)
        gns = zs[:, COL_GN:COL_GN + 3 * N_HEADS].reshape(bsz, t_new, 3, N_HEADS)
        gates_s = jnp.transpose(gns, (0, 3, 1, 2)).reshape(bsz, N_HEADS * t_new, 3)
        gates_s = jnp.pad(gates_s, ((0, 0), (0, LANES - N_HEADS * t_new), (0, LANES - 3)))
        o_s = _sample_nsa(pt, cache_pages, new_cmp, qT_s, knew, vnew, kwin_s, vwin_s, pe, w1b, w2b,
                          cb_s, nearb_s, farb_s, newb_s, winb_s, gates_s, nbp)
        o4 = o_s[:, :N_HEADS * t_new].reshape(bsz, N_GROUPS, N_REP, t_new, N_GROUPS, HEAD_DIM)
        o4 = jnp.stack([o4[:, g, :, :, g] for g in range(N_GROUPS)], axis=1)
        ob_s = jnp.transpose(o4, (0, 3, 1, 2, 4)).reshape(ts, NSA_WIDTH).astype(BF16)

        mm_s, bcol_s = _mix_mats(gmlp_ws[l], gmlp_bs[l], ts, bsz)
        oa_s, vn_s = _gmlp(zs, mm_s, bcol_s, gng, gnb, ts)
        outs["v_s"].append(vn_s.reshape(bsz, t_new, GMLP_WIDTH))
        halo = jnp.pad(state_pool[l], ((0, 0), (16 - POOL_BUF, 0), (0, 0))).reshape(bsz * 16, POOL_WIDTH)
        oc_s = _pool(zs, halo, lambda i: (i, 0), pwb, ps, t_new, False, past, 0)
        x1s, h2s, comb_s = _merge(xs, zs, oa_s, ob_s, oc_s, mods_s, ln1g, ln1b, pab, pbb, pcb, wob, rw, rb, ts, alpha)
        xs = _moe(h2s, comb_s, wg, wu, wd, x1s, mods_s, ln2g, ln2b, ts, alpha)

    return (xp.reshape(1, s, D_MODEL), xs.reshape(bsz, t_new, D_MODEL),
            jnp.stack(outs["nsa_p"]), jnp.stack(outs["nsa_s"]), jnp.stack(outs["win_p"]), jnp.stack(outs["win_s"]),
            jnp.stack(outs["pool_p"]), jnp.stack(outs["pool_s"]), jnp.stack(outs["v_s"]))
```

```python
import functools
import math

import jax
import jax.numpy as jnp
from jax import lax
from jax.experimental import pallas as pl
from jax.experimental.pallas import tpu as pltpu

F32 = jnp.float32
BF16 = jnp.bfloat16

D_MODEL = 2048
HEAD_DIM = 128
N_GROUPS = 2
N_REP = 4
N_HEADS = N_GROUPS * N_REP
NSA_WIDTH = N_HEADS * HEAD_DIM
KV_KINDS = 6
PAGED_KINDS = 4
BLOCK = 64
N_SEL = 16
WINDOW = 512
WIN_BLOCKS = WINDOW // BLOCK
PHI_HIDDEN = 2 * HEAD_DIM
PAGE = 128
GMLP_GROUPS = 4
GMLP_WIDTH = 512
CHUNK = 128
POOL_WINDOWS = (2, 4, 8, 16)
POOL_GROUP_DIM = 128
POOL_WIDTH = 512
POOL_BUF = 15
N_BUCKETS = 32
MAX_DISTANCE = 128
N_EXPERTS = 16
N_EXPERT_GROUPS = 4
EXPERTS_PER_GROUP = 4
D_EXPERT = 512
LN_EPS = 1e-5
NEG = -1e30
FORCED_SCORE = 1e9

COL_Q = 0
COL_KV = 1024
COL_U = 2560
COL_V = 3072
COL_XC = 3584
COL_GM = 4096
COL_GN = 10240
N_IN_PAD = 10752
IN_TN = 512

LANES = 128
VMEM_LIMIT = 56 * 1024 * 1024
CHUNK_BLOCKS = 8
CHUNK_KEYS = CHUNK_BLOCKS * BLOCK
XC_PITCH = BLOCK + 8
SAMPLE_RING_PAGES = 16
SAMPLE_BLOCKS_PAD = 256
MOE_GROUP_TILE = 256
MOE_FINISH_TILE = 256


def _cparams(sem):
    return pltpu.CompilerParams(dimension_semantics=sem, vmem_limit_bytes=VMEM_LIMIT)


def _ln(x):
    mu = jnp.mean(x, axis=-1, keepdims=True)
    xc = x - mu
    var = jnp.mean(xc * xc, axis=-1, keepdims=True)
    return xc * lax.rsqrt(var + LN_EPS)


def _sigmoid(x):
    return 1.0 / (1.0 + jnp.exp(-x))


def _silu(x):
    return x * _sigmoid(x)


def _gelu_tanh(x):
    return 0.5 * x * (1.0 + jnp.tanh(math.sqrt(2.0 / math.pi) * (x + 0.044715 * (x * x * x))))


def _dot(a, b):
    return jnp.dot(a, b, preferred_element_type=F32)


QK_SCALE = HEAD_DIM ** -0.5
EXP2_SCALE = QK_SCALE * math.log2(math.e)


def _qk(k, qT):
    return _dot(k, qT) * QK_SCALE


def _ada_kernel(c_ref, w_ref, b_ref, o_ref):
    c = c_ref[...]
    o_ref[...] = _dot(_silu(c).astype(BF16), w_ref[...].astype(BF16)) + b_ref[...]


def _ada_mods(c_all, ada_w, ada_b):
    depth = ada_w.shape[0]
    rows = c_all.shape[0]
    n = ada_w.shape[2]
    tn = 1024
    return pl.pallas_call(
        _ada_kernel,
        grid=(depth, n // tn),
        in_specs=[pl.BlockSpec((rows, D_MODEL), lambda l, j: (0, 0)),
                  pl.BlockSpec((None, D_MODEL, tn), lambda l, j: (l, 0, j)),
                  pl.BlockSpec((None, 1, tn), lambda l, j: (l, 0, j))],
        out_specs=pl.BlockSpec((None, rows, tn), lambda l, j: (l, 0, j)),
        out_shape=jax.ShapeDtypeStruct((depth, rows, n), F32),
        compiler_params=_cparams(("parallel", "parallel")),
        name="ada_mods",
    )(c_all, ada_w, ada_b.reshape(depth, 1, n))


def _mod_rows(ref, tm):
    return ref[...] if ref.shape[0] == tm else ref[0:1, :]


N_QKV_TILES = COL_U // IN_TN
N_MAIN_TILES = COL_GN // IN_TN


def _inproj_kernel(x_ref, sh_ref, sc_ref, wa_ref, wb_ref, wc_ref, z_ref, zb_ref, h_scr):
    tm = x_ref.shape[0]
    j = pl.program_id(1)

    @pl.when(j == 0)
    def _():
        h = _ln(x_ref[...]) * (1.0 + _mod_rows(sc_ref, tm)) + _mod_rows(sh_ref, tm)
        h_scr[...] = h.astype(BF16)

    @pl.when(j < N_QKV_TILES)
    def _():
        z = _dot(h_scr[...], wa_ref[...])
        z_ref[...] = z
        zb_ref[...] = z.astype(BF16)

    @pl.when((j >= N_QKV_TILES) & (j < N_MAIN_TILES))
    def _():
        z_ref[...] = _dot(h_scr[...], wb_ref[...])

    @pl.when(j == N_MAIN_TILES)
    def _():
        z_ref[...] = _dot(h_scr[...], wc_ref[...])


def _mod_spec(mods, tm, col):
    rows = mods.shape[0]
    if rows == 8:
        return pl.BlockSpec((8, D_MODEL), lambda i, *_: (0, col))
    return pl.BlockSpec((tm, D_MODEL), lambda i, *_: (i, col))


def _in_proj_weights(w):
    n_gate = 3 * N_HEADS
    wa = w[:, :COL_U].astype(BF16)
    wb = w[:, COL_U + n_gate:].astype(BF16)
    wc = jnp.pad(w[:, COL_U:COL_U + n_gate], ((0, 0), (0, IN_TN - n_gate))).astype(BF16)
    return wa, wb, wc


def _in_proj(x, mods, weights, tm):
    t = x.shape[0]
    wa, wb, wc = weights
    n_b = N_MAIN_TILES - N_QKV_TILES
    return pl.pallas_call(
        _inproj_kernel,
        grid=(t // tm, N_IN_PAD // IN_TN),
        in_specs=[pl.BlockSpec((tm, D_MODEL), lambda i, j: (i, 0)),
                  _mod_spec(mods, tm, 0), _mod_spec(mods, tm, 1),
                  pl.BlockSpec((D_MODEL, IN_TN), lambda i, j: (0, jnp.minimum(j, N_QKV_TILES - 1))),
                  pl.BlockSpec((D_MODEL, IN_TN), lambda i, j: (0, jnp.clip(j - N_QKV_TILES, 0, n_b - 1))),
                  pl.BlockSpec((D_MODEL, IN_TN), lambda i, j: (0, 0))],
        out_specs=[pl.BlockSpec((tm, IN_TN), lambda i, j: (i, j)),
                   pl.BlockSpec((tm, IN_TN), lambda i, j: (i, jnp.minimum(j, N_QKV_TILES - 1)))],
        out_shape=[jax.ShapeDtypeStruct((t, N_IN_PAD), F32),
                   jax.ShapeDtypeStruct((t, COL_U), BF16)],
        scratch_shapes=[pltpu.VMEM((tm, D_MODEL), BF16)],
        compiler_params=_cparams(("parallel", "arbitrary")),
        name="in_proj",
    )(x, mods, mods, wa, wb, wc)


def _compress_pair(x0_ref, x1_ref, nb, pe_ref, w1_ref, w2_ref):
    def step(p, acc):
        pe = pe_ref[pl.ds(p, 1), :]
        a0 = x0_ref[pl.ds(p, nb, stride=BLOCK), :] + pe
        a1 = x1_ref[pl.ds(p, nb, stride=BLOCK), :] + pe
        a = jnp.concatenate([a0, a1], axis=0).astype(BF16)
        w = w1_ref[pl.ds(pl.multiple_of(p * HEAD_DIM, HEAD_DIM), HEAD_DIM), :]
        return acc + _dot(a, w)

    acc = lax.fori_loop(0, BLOCK, step, jnp.zeros((2 * nb, PHI_HIDDEN), F32))
    return _dot(_gelu_tanh(acc).astype(BF16), w2_ref[...])


def _compress_prompt_kernel(x0_ref, x1_ref, pe_ref, w1_ref, w2_ref, o_ref, *, nb):
    o_ref[...] = _compress_pair(x0_ref, x1_ref, nb, pe_ref, w1_ref, w2_ref)


def _compress_prompt(z, pe, w1b, w2b):
    s = z.shape[0]
    nb = s // BLOCK
    cb = COL_KV // HEAD_DIM
    return pl.pallas_call(
        functools.partial(_compress_prompt_kernel, nb=nb),
        grid=(2,),
        in_specs=[pl.BlockSpec((s, HEAD_DIM), lambda k: (0, cb + 2 * k)),
                  pl.BlockSpec((s, HEAD_DIM), lambda k: (0, cb + 2 * k + 1)),
                  pl.BlockSpec((None, BLOCK, HEAD_DIM), lambda k: (k, 0, 0)),
                  pl.BlockSpec((None, BLOCK * HEAD_DIM, PHI_HIDDEN), lambda k: (k, 0, 0)),
                  pl.BlockSpec((None, PHI_HIDDEN, HEAD_DIM), lambda k: (k, 0, 0))],
        out_specs=pl.BlockSpec((None, 2 * nb, HEAD_DIM), lambda k: (k, 0, 0)),
        out_shape=jax.ShapeDtypeStruct((2, 2 * nb, HEAD_DIM), F32),
        compiler_params=_cparams(("parallel",)),
        name="compress_prompt",
    )(z, z, pe, w1b, w2b)


def _select_blocks(score, n_io):
    def body(_, carry):
        work, sel = carry
        mval = jnp.max(work, axis=0, keepdims=True)
        cand = jnp.where(work == mval, n_io, jnp.int32(1 << 30))
        idx = jnp.min(cand, axis=0, keepdims=True)
        pick = n_io == idx
        return jnp.where(pick, -jnp.inf, work), jnp.where(pick, 1.0, sel)

    _, sel = lax.fori_loop(0, N_SEL, body, (score, jnp.zeros(score.shape, F32)))
    return sel


def _nsa_prompt_kernel(qT_ref, kc_ref, vcT_ref, ksel_ref, vselT_ref, kwin_ref, vwinT_ref,
                       st_ref, wt_ref, ct_ref, gate_ref, o_ref, selneg_scr, selfar_scr, s_a, s_b, p_a, p_b, *, nb):
    i = pl.program_id(1)
    qT = qT_ref[...]
    nl = N_REP * BLOCK

    n_io = lax.broadcasted_iota(jnp.int32, (nb, nl), 0)
    qq = lax.broadcasted_iota(jnp.int32, (nb, nl), 1) % BLOCK
    m = i - n_io
    ct = ct_ref[...]
    bias = jnp.where(m == 0, ct[0:1], jnp.where(m == 1, ct[1:2], jnp.where(m == 2, ct[2:3], ct[3:4])))
    valid = (m >= 1) | ((m == 0) & (qq == BLOCK - 1))
    s = jnp.where(valid, _qk(kc_ref[...], qT) + bias, NEG)
    mx = jnp.max(s, axis=0, keepdims=True)
    p = jnp.where(valid, jnp.exp(s - mx), 0.0)
    l = jnp.sum(p, axis=0, keepdims=True)
    pn = p / jnp.where(l > 0.0, l, 1.0)
    o_cmp = _dot(vcT_ref[...], pn.astype(BF16))

    half = pn[:, :LANES] + pn[:, LANES:]
    imp = half + pltpu.roll(half, BLOCK, 1)
    n_io1 = n_io[:, :LANES]
    forced = (n_io1 == 0) | (n_io1 == i) | (n_io1 == i - 1)
    score = jnp.where(forced, FORCED_SCORE, imp)
    score = jnp.where(n_io1 <= i, score, -1.0)
    sel = _select_blocks(score, n_io1)
    neg1 = jnp.where((sel > 0.5) & (score >= 0.0), 0.0, NEG)
    neg2 = jnp.concatenate([neg1, neg1], axis=1)
    selneg_scr[...] = neg2
    far_row = st_ref[st_ref.shape[0] - 2][0:1, :]
    selfar_scr[...] = jnp.where(neg2 == 0.0, far_row, NEG)

    c_i = i // CHUNK_BLOCKS

    def branch(k_ref, vT_ref, tile_ref, n_tiles, use_sel, n_chunks):
        def logits(c, dst):
            dst[...] = _dot(k_ref[jnp.maximum(c, 0)], qT)

        def pv(c, p_ref):
            return _dot(vT_ref[jnp.clip(c, 0, vT_ref.shape[0] - 1)], p_ref[...])

        def consume(c, src, m_run, near):
            live = c >= 0
            cz = jnp.maximum(c, 0)
            parts = []
            for b in range(CHUNK_BLOCKS):
                n = cz * CHUNK_BLOCKS + b
                sb = src[b * BLOCK:(b + 1) * BLOCK, :]
                if near:
                    mm = i - n
                    if use_sel:
                        tidx = jnp.where(mm < 0, n_tiles - 1, jnp.minimum(mm, n_tiles - 2))
                    else:
                        tidx = jnp.where((mm < 0) | (mm > n_tiles - 2), n_tiles - 1, mm)
                    sb = sb + tile_ref[jnp.where(live, tidx, n_tiles - 1)]
                    if use_sel:
                        sb = sb + selneg_scr[pl.ds(n, 1), :]
                else:
                    sb = sb + jnp.where(live, selfar_scr[pl.ds(n, 1), :], NEG)
                parts.append(sb)
            s2 = jnp.concatenate(parts, axis=0)
            m_new = jnp.maximum(m_run, jnp.max(s2, axis=0, keepdims=True))
            alpha = jnp.exp2((m_run - m_new) * EXP2_SCALE)
            pp = jnp.exp2((s2 - m_new) * EXP2_SCALE)
            return m_new, alpha, pp.astype(BF16)

        def pair(t, carry, near):
            m_run, acc = carry
            c0 = c_i - 2 * t
            logits(c0 - 1, s_b)
            r_prev = pv(c0 + 1, p_b)
            m_run, alpha, pp = consume(c0, s_a, m_run, near)
            p_a[...] = pp
            acc = alpha * (acc + r_prev)
            logits(c0 - 2, s_a)
            r_cur = pv(c0, p_a)
            m_run, alpha, pp = consume(c0 - 1, s_b, m_run, near)
            p_b[...] = pp
            return m_run, alpha * (acc + r_cur)

        logits(c_i, s_a)
        p_b[...] = jnp.zeros(p_b.shape, BF16)
        carry = (jnp.full((1, nl), NEG, F32), jnp.zeros((vT_ref.shape[1], nl), F32))
        carry = pair(0, carry, True)
        n_pairs = (n_chunks + 1) // 2 if use_sel else 1
        if use_sel:
            carry = lax.fori_loop(1, n_pairs, lambda t, cr: pair(t, cr, False), carry)
        acc = carry[1] + pv(c_i - 2 * n_pairs + 1, p_b)
        return acc[:HEAD_DIM] / acc[HEAD_DIM:HEAD_DIM + 1]

    o_sel = branch(ksel_ref, vselT_ref, st_ref, st_ref.shape[0], True, c_i + 1)
    o_win = branch(kwin_ref, vwinT_ref, wt_ref, wt_ref.shape[0], False, jnp.minimum(c_i + 1, 2))

    gates = _sigmoid(gate_ref[...])
    oT = gates[0:1] * o_cmp + gates[1:2] * o_sel + gates[2:3] * o_win
    o = oT.T
    for r in range(N_REP):
        o_ref[:, r * HEAD_DIM:(r + 1) * HEAD_DIM] = o[r * BLOCK:(r + 1) * BLOCK].astype(o_ref.dtype)


def _nsa_prompt(qT, kc, vcT, ksel, vselT, kwin, vwinT, st, wt, ct, gates, s):
    nb = s // BLOCK
    nq = s // BLOCK
    nl = N_REP * BLOCK
    nch = s // CHUNK_KEYS

    def whole(arr):
        shp = arr.shape[1:]
        return pl.BlockSpec((None,) + shp, lambda g, i: (g,) + (0,) * len(shp))

    return pl.pallas_call(
        functools.partial(_nsa_prompt_kernel, nb=nb),
        grid=(N_GROUPS, nq),
        in_specs=[pl.BlockSpec((None, HEAD_DIM, nl), lambda g, i: (g, 0, i)),
                  whole(kc), whole(vcT), whole(ksel), whole(vselT), whole(kwin), whole(vwinT),
                  whole(st), whole(wt), whole(ct),
                  pl.BlockSpec((None, None, 8, nl), lambda g, i: (g, i, 0, 0))],
        out_specs=pl.BlockSpec((BLOCK, N_REP * HEAD_DIM), lambda g, i: (i, g)),
        out_shape=jax.ShapeDtypeStruct((s, NSA_WIDTH), BF16),
        scratch_shapes=[pltpu.VMEM((nb, nl), F32),
                        pltpu.VMEM((nb, nl), F32),
                        pltpu.VMEM((CHUNK_KEYS, nl), F32),
                        pltpu.VMEM((CHUNK_KEYS, nl), F32),
                        pltpu.VMEM((CHUNK_KEYS, nl), BF16),
                        pltpu.VMEM((CHUNK_KEYS, nl), BF16)],
        compiler_params=_cparams(("parallel", "arbitrary")),
        name="nsa_prompt",
    )(qT, kc, vcT, ksel, vselT, kwin, vwinT, st, wt, ct, gates)


def _sample_nsa_kernel(pt_ref, cache_ref, newc_ref, qT_ref, knew_ref, vnew_ref, kwin_ref, vwin_ref,
                       pe_ref, w1_ref, w2_ref, cb_ref, nearb_ref, farb_ref, newb_ref, winb_ref, gate_ref, o_ref,
                       ring, xc, kcs, vsel, s_scr, selneg_scr, sems, *, n_pages, t_new, nbp, n_ring):
    b = pl.program_id(0)
    total = pl.num_programs(0) * n_pages
    past = n_pages * PAGE
    cur = past // BLOCK
    kinds_per_row = PAGED_KINDS * N_GROUPS
    qT = qT_ref[...]

    def page_dma(t, slot):
        return pltpu.make_async_copy(cache_ref.at[pt_ref[t // n_pages, t % n_pages]], ring.at[slot], sems.at[slot])

    @pl.when(b == 0)
    def _():
        for t in range(n_ring):
            page_dma(t, t).start()

    far = farb_ref[0:1, :]

    def page_step(j, c):
        t = b * n_pages + j
        slot = t % n_ring
        page_dma(t, slot).wait()
        buf = ring.at[slot]

        def rows(kg):
            return buf[pl.ds(kg, PAGE, stride=kinds_per_row), :]

        r0 = pl.multiple_of(j * PAGE, PAGE)
        base = pl.multiple_of(j * (PAGE // BLOCK) * XC_PITCH, 8)
        for kg in range(2 * N_GROUPS):
            x = rows(kg)
            for h in range(PAGE // BLOCK):
                xc[kg, pl.ds(base + h * XC_PITCH, BLOCK), :] = x[h * BLOCK:(h + 1) * BLOCK]
        k = jnp.concatenate([rows(4), rows(5)], axis=1).astype(BF16)
        s_scr[pl.ds(r0, PAGE), :] = _qk(k, qT) + jnp.where(j == n_pages - 1, nearb_ref[...], far)
        vsel[pl.ds(r0, PAGE), :] = jnp.concatenate([rows(6), rows(7)], axis=1).astype(BF16)

        @pl.when(t + n_ring < total)
        def _():
            page_dma(t + n_ring, slot).start()

        return c

    lax.fori_loop(0, n_pages, page_step, 0)

    tail = (nbp - cur) * XC_PITCH
    for kg in range(2 * N_GROUPS):
        xc[kg, pl.ds(cur * XC_PITCH, tail), :] = jnp.zeros((tail, HEAD_DIM), F32)
        xc[kg, pl.ds(cur * XC_PITCH, t_new), :] = newc_ref[:, kg * HEAD_DIM:(kg + 1) * HEAD_DIM]
    s_scr[pl.ds(past, PAGE), :] = _qk(knew_ref[...], qT) + newb_ref[...]
    vsel[pl.ds(past, PAGE), :] = vnew_ref[...]

    for kind in range(2):
        def step(pp, acc):
            p = 2 * pp
            pe0 = pe_ref[kind, pl.ds(p, 1), :]
            pe1 = pe_ref[kind, pl.ds(p + 1, 1), :]
            parts = []
            for g in range(N_GROUPS):
                x0 = xc[2 * kind + g, pl.ds(p, nbp, stride=XC_PITCH), :] + pe0
                x1 = xc[2 * kind + g, pl.ds(p + 1, nbp, stride=XC_PITCH), :] + pe1
                parts.append(jnp.concatenate([x0, x1], axis=1))
            a = jnp.concatenate(parts, axis=0).astype(BF16)
            w = w1_ref[kind, pl.ds(pl.multiple_of(p * HEAD_DIM, 2 * HEAD_DIM), 2 * HEAD_DIM), :]
            return acc + _dot(a, w)

        acc = lax.fori_loop(0, BLOCK // 2, step, jnp.zeros((N_GROUPS * nbp, PHI_HIDDEN), F32), unroll=2)
        out = _dot(_gelu_tanh(acc).astype(BF16), w2_ref[kind])
        kcs[kind, nbp:, :] = jnp.zeros((kcs.shape[1] - nbp, N_GROUPS * HEAD_DIM), F32)
        for g in range(N_GROUPS):
            kcs[kind, 0:nbp, g * HEAD_DIM:(g + 1) * HEAD_DIM] = out[g * nbp:(g + 1) * nbp]

    def softmax_rows(s):
        mx = jnp.max(s, axis=0, keepdims=True)
        p = jnp.where(s > 0.5 * NEG, jnp.exp(s - mx), 0.0)
        l = jnp.sum(p, axis=0, keepdims=True)
        return p / jnp.where(l > 0.0, l, 1.0)

    pn = softmax_rows(_qk(kcs[0].astype(BF16), qT) + cb_ref[...])
    o_cmp = _dot(pn.T.astype(BF16), kcs[1].astype(BF16))

    tot = pn + pltpu.roll(pn, t_new, 1) + pltpu.roll(pn, 2 * t_new, 1) + pltpu.roll(pn, 3 * t_new, 1)
    lane = lax.broadcasted_iota(jnp.int32, pn.shape, 1)
    top = jnp.where((lane // t_new) % N_REP == N_REP - 1, tot, 0.0)
    imp = top + pltpu.roll(top, LANES - t_new, 1) + pltpu.roll(top, LANES - 2 * t_new, 1) \
        + pltpu.roll(top, LANES - 3 * t_new, 1)
    n_io = lax.broadcasted_iota(jnp.int32, pn.shape, 0)
    forced = (n_io == 0) | (n_io == cur) | (n_io == cur - 1)
    score = jnp.where(forced, FORCED_SCORE, imp)
    score = jnp.where(n_io <= cur, score, -jnp.inf)
    sel = _select_blocks(score, n_io)
    selneg_scr[...] = jnp.where(sel > 0.5, 0.0, NEG)

    pw = softmax_rows(_qk(kwin_ref[...], qT) + winb_ref[...])
    o_win = _dot(pw.T.astype(BF16), vwin_ref[...])

    blk = PAGE // BLOCK

    def mask_step(c, mx):
        r0 = pl.multiple_of(c * PAGE, PAGE)
        neg = jnp.concatenate(
            [jnp.broadcast_to(selneg_scr[pl.ds(c * blk + h, 1), :], (BLOCK, LANES)) for h in range(blk)], axis=0)
        s = s_scr[pl.ds(r0, PAGE), :] + neg
        s_scr[pl.ds(r0, PAGE), :] = s
        return jnp.maximum(mx, jnp.max(s, axis=0, keepdims=True))

    mx = lax.fori_loop(0, n_pages + 1, mask_step, jnp.full((1, LANES), NEG, F32))

    def pv_rows(r0, n_rows, carry):
        l_run, acc = carry
        pp = jnp.exp(s_scr[pl.ds(r0, n_rows), :] - mx)
        acc = acc + _dot(pp.T.astype(BF16), vsel[pl.ds(r0, n_rows), :])
        return l_run + jnp.sum(pp, axis=0, keepdims=True), acc

    pv_pages = 4 if n_pages % 4 == 0 else 1
    pv_keys = pv_pages * PAGE
    carry = lax.fori_loop(0, n_pages // pv_pages,
                          lambda c, carry: pv_rows(pl.multiple_of(c * pv_keys, pv_keys), pv_keys, carry),
                          (jnp.zeros((1, LANES), F32), jnp.zeros((LANES, N_GROUPS * HEAD_DIM), F32)))
    l_fin, acc = pv_rows(past, PAGE, carry)
    l_col = jnp.broadcast_to(l_fin, (LANES, LANES)).T[:, 0:1]
    o_sel = acc / l_col

    gates = _sigmoid(gate_ref[...])
    o_ref[...] = gates[:, 0:1] * o_cmp + gates[:, 1:2] * o_sel + gates[:, 2:3] * o_win


def _sample_nsa(pt, cache_pages, new_cmp, qT, knew, vnew, kwin, vwin, pe, w1b, w2b,
                cb, nearb, farb, newb, winb, gates, nbp):
    bsz, n_pages = pt.shape
    t_new = new_cmp.shape[1]
    gh = N_GROUPS * HEAD_DIM
    n_keys = n_pages * PAGE + PAGE
    n_ring = min(SAMPLE_RING_PAGES, n_pages)

    def per_b(arr):
        shp = arr.shape[1:]
        return pl.BlockSpec((None,) + shp, lambda b, pt: (b,) + (0,) * len(shp))

    def const(arr, **kw):
        return pl.BlockSpec(arr.shape, lambda b, pt: (0,) * arr.ndim, **kw)

    grid_spec = pltpu.PrefetchScalarGridSpec(
        num_scalar_prefetch=1,
        grid=(bsz,),
        in_specs=[pl.BlockSpec(memory_space=pl.ANY),
                  per_b(new_cmp), per_b(qT), per_b(knew), per_b(vnew), per_b(kwin), per_b(vwin),
                  const(pe), const(w1b, pipeline_mode=pl.Buffered(1)), const(w2b),
                  const(cb), const(nearb), const(farb), const(newb), const(winb), per_b(gates)],
        out_specs=pl.BlockSpec((None, LANES, gh), lambda b, pt: (b, 0, 0)),
        scratch_shapes=[pltpu.VMEM((n_ring,) + cache_pages.shape[1:], F32),
                        pltpu.VMEM((2 * N_GROUPS, nbp * XC_PITCH, HEAD_DIM), F32),
                        pltpu.VMEM((2, cb.shape[0], gh), F32),
                        pltpu.VMEM((n_keys, gh), BF16),
                        pltpu.VMEM((n_keys, LANES), F32),
                        pltpu.VMEM((cb.shape[0], LANES), F32),
                        pltpu.SemaphoreType.DMA((n_ring,))],
    )
    return pl.pallas_call(
        functools.partial(_sample_nsa_kernel, n_pages=n_pages, t_new=t_new, nbp=nbp, n_ring=n_ring),
        grid_spec=grid_spec,
        out_shape=jax.ShapeDtypeStruct((bsz, LANES, gh), F32),
        compiler_params=_cparams(("arbitrary",)),
        name="sample_nsa",
    )(pt, cache_pages, new_cmp, qT, knew, vnew, kwin, vwin, pe, w1b, w2b, cb, nearb, farb, newb, winb, gates)


def _gmlp_kernel(u_ref, v_ref, m_ref, bcol_ref, g_ref, b_ref, oa_ref, vn_ref):
    vn = _ln(v_ref[...]) * g_ref[...] + b_ref[...]
    vn_ref[...] = vn
    vb = vn.astype(BF16)
    u = u_ref[...]
    gd = GMLP_WIDTH // GMLP_GROUPS
    for g in range(GMLP_GROUPS):
        mixed = _dot(m_ref[g], vb[:, g * gd:(g + 1) * gd]) + bcol_ref[:, g:g + 1]
        oa_ref[:, g * gd:(g + 1) * gd] = (u[:, g * gd:(g + 1) * gd] * mixed).astype(oa_ref.dtype)


def _gmlp(z, mix_m, bcol, gn_g, gn_b, tc):
    t = z.shape[0]
    return pl.pallas_call(
        _gmlp_kernel,
        grid=(t // tc,),
        in_specs=[pl.BlockSpec((tc, GMLP_WIDTH), lambda i: (i, COL_U // GMLP_WIDTH)),
                  pl.BlockSpec((tc, GMLP_WIDTH), lambda i: (i, COL_V // GMLP_WIDTH)),
                  pl.BlockSpec((GMLP_GROUPS, tc, tc), lambda i: (0, 0, 0)),
                  pl.BlockSpec((tc, LANES), lambda i: (0, 0)),
                  pl.BlockSpec((1, GMLP_WIDTH), lambda i: (0, 0)),
                  pl.BlockSpec((1, GMLP_WIDTH), lambda i: (0, 0))],
        out_specs=[pl.BlockSpec((tc, GMLP_WIDTH), lambda i: (i, 0)),
                   pl.BlockSpec((tc, GMLP_WIDTH), lambda i: (i, 0))],
        out_shape=[jax.ShapeDtypeStruct((t, GMLP_WIDTH), BF16),
                   jax.ShapeDtypeStruct((t, GMLP_WIDTH), F32)],
        compiler_params=_cparams(("parallel",)),
        name="gmlp",
    )(z, z, mix_m, bcol, gn_g, gn_b)


def _pool_kernel(x_ref, halo_ref, pw_ref, ps_ref, o_ref, ext_scr, *, tm, zero_first, pos0, pos_step):
    i = pl.program_id(0)
    halo = halo_ref[...]
    if zero_first:
        halo = jnp.where(i == 0, 0.0, halo)
    ext_scr[0:16, :] = halo
    x = x_ref[...]
    ext_scr[16:16 + tm, :] = x
    pos = pos0 + i * pos_step + lax.broadcasted_iota(jnp.int32, (tm, POOL_GROUP_DIM), 0)
    for gi, w in enumerate(POOL_WINDOWS):
        c0 = gi * POOL_GROUP_DIM
        acc = ext_scr[16:16 + tm, c0:c0 + POOL_GROUP_DIM]
        for k in range(1, w):
            acc = acc + ext_scr[16 - k:16 - k + tm, c0:c0 + POOL_GROUP_DIM]
        count = jnp.minimum(pos + 1, w).astype(F32)
        d = acc / count - x[:, c0:c0 + POOL_GROUP_DIM]
        y = _dot(d.astype(BF16), pw_ref[gi]) * ps_ref[:, c0:c0 + POOL_GROUP_DIM]
        o_ref[:, c0:c0 + POOL_GROUP_DIM] = y.astype(o_ref.dtype)


def _pool(z, halo_src, halo_map, pwb, ps, tm, zero_first, pos0, pos_step):
    t = z.shape[0]
    return pl.pallas_call(
        functools.partial(_pool_kernel, tm=tm, zero_first=zero_first, pos0=pos0, pos_step=pos_step),
        grid=(t // tm,),
        in_specs=[pl.BlockSpec((tm, POOL_WIDTH), lambda i: (i, COL_XC // POOL_WIDTH)),
                  pl.BlockSpec((16, POOL_WIDTH), halo_map),
                  pl.BlockSpec((len(POOL_WINDOWS), POOL_GROUP_DIM, POOL_GROUP_DIM), lambda i: (0, 0, 0)),
                  pl.BlockSpec((1, POOL_WIDTH), lambda i: (0, 0))],
        out_specs=pl.BlockSpec((tm, POOL_WIDTH), lambda i: (i, 0)),
        out_shape=jax.ShapeDtypeStruct((t, POOL_WIDTH), BF16),
        scratch_shapes=[pltpu.VMEM((16 + tm, POOL_WIDTH), F32)],
        compiler_params=_cparams(("parallel",)),
        name="pool",
    )(z, halo_src, pwb, ps)


def _route(logits_t, rb_col):
    aff = _sigmoid(logits_t)
    biased = aff + rb_col
    rows = [biased[e:e + 1] for e in range(N_EXPERTS)]
    top2 = []
    gscore = []
    for g in range(N_EXPERT_GROUPS):
        grp = rows[g * EXPERTS_PER_GROUP:(g + 1) * EXPERTS_PER_GROUP]
        gs = None
        for a in range(EXPERTS_PER_GROUP):
            rank = None
            for c in range(EXPERTS_PER_GROUP):
                if c == a:
                    continue
                ahead = (grp[c] >= grp[a]) if c < a else (grp[c] > grp[a])
                rank = ahead.astype(F32) if rank is None else rank + ahead.astype(F32)
            in2 = rank < 1.5
            top2.append(in2)
            contrib = jnp.where(in2, grp[a], 0.0)
            gs = contrib if gs is None else gs + contrib
        gscore.append(gs)
    out = []
    for g in range(N_EXPERT_GROUPS):
        win = None
        for c in range(N_EXPERT_GROUPS):
            if c == g:
                continue
            ok = (gscore[g] > gscore[c]) if c < g else (gscore[g] >= gscore[c])
            win = ok if win is None else (win & ok)
        for a in range(EXPERTS_PER_GROUP):
            e = g * EXPERTS_PER_GROUP + a
            out.append(jnp.where(win & top2[e], aff[e:e + 1], 0.0))
    selw = jnp.concatenate(out, axis=0)
    return selw / jnp.sum(selw, axis=0, keepdims=True)


def _merge_kernel(x_ref, oa_ref, ob_ref, oc_ref, g0_ref, g1_ref, g2_ref, gt1_ref, sh2_ref, sc2_ref,
                  ln1g_ref, ln1b_ref, pa_ref, pb_ref, pc_ref, wo_ref, rw_ref, rb_ref,
                  x1_ref, h2_ref, comb_ref, *, alpha):
    tm = x_ref.shape[0]

    merged = _sigmoid(g0_ref[...]) * _dot(oa_ref[...], pa_ref[...])
    merged = merged + _sigmoid(g1_ref[...]) * _dot(ob_ref[...], pb_ref[...])
    merged = merged + _sigmoid(g2_ref[...]) * _dot(oc_ref[...], pc_ref[...])
    mix = _dot(merged.astype(BF16), wo_ref[...])
    x1 = _ln(alpha * x_ref[...] + (1.0 + _mod_rows(gt1_ref, tm)) * mix) * ln1g_ref[...] + ln1b_ref[...]
    x1_ref[...] = x1
    h2 = _ln(x1) * (1.0 + _mod_rows(sc2_ref, tm)) + _mod_rows(sh2_ref, tm)
    h2_ref[...] = h2.astype(BF16)
    logits = _dot(h2.astype(BF16), rw_ref[...])
    comb_t = _route(logits.T[:N_EXPERTS], rb_ref[...][:N_EXPERTS, 0:1])
    comb_full = jnp.concatenate([comb_t, jnp.zeros((LANES - N_EXPERTS, tm), F32)], axis=0)
    comb_ref[...] = comb_full.T


def _merge(x, z, oa, ob, oc, mods, ln1g, ln1b, pa, pb, pc, wo, rw, rb, tm, alpha):
    t = x.shape[0]

    def const(arr):
        return pl.BlockSpec(arr.shape, lambda i: (0,) * arr.ndim, pipeline_mode=pl.Buffered(1))

    gcol = COL_GM // D_MODEL
    return pl.pallas_call(
        functools.partial(_merge_kernel, alpha=alpha),
        grid=(t // tm,),
        in_specs=[pl.BlockSpec((tm, D_MODEL), lambda i: (i, 0)),
                  pl.BlockSpec((tm, GMLP_WIDTH), lambda i: (i, 0)),
                  pl.BlockSpec((tm, NSA_WIDTH), lambda i: (i, 0)),
                  pl.BlockSpec((tm, POOL_WIDTH), lambda i: (i, 0)),
                  pl.BlockSpec((tm, D_MODEL), lambda i: (i, gcol)),
                  pl.BlockSpec((tm, D_MODEL), lambda i: (i, gcol + 1)),
                  pl.BlockSpec((tm, D_MODEL), lambda i: (i, gcol + 2)),
                  _mod_spec(mods, tm, 2), _mod_spec(mods, tm, 3), _mod_spec(mods, tm, 4),
                  const(ln1g), const(ln1b), const(pa), const(pb), const(pc), const(wo), const(rw), const(rb)],
        out_specs=[pl.BlockSpec((tm, D_MODEL), lambda i: (i, 0)),
                   pl.BlockSpec((tm, D_MODEL), lambda i: (i, 0)),
                   pl.BlockSpec((tm, LANES), lambda i: (i, 0))],
        out_shape=[jax.ShapeDtypeStruct((t, D_MODEL), F32),
                   jax.ShapeDtypeStruct((t, D_MODEL), BF16),
                   jax.ShapeDtypeStruct((t, LANES), F32)],
        compiler_params=_cparams(("parallel",)),
        name="merge",
    )(x, oa, ob, oc, z, z, z, mods, mods, mods, ln1g, ln1b, pa, pb, pc, wo, rw, rb)


def _moe_kernel(h_ref, comb_ref, wg_ref, wu_ref, wd_ref, x1_ref, gt2_ref, g_ref, b_ref, o_ref, acc_scr, *, alpha):
    e = pl.program_id(1)
    tm = h_ref.shape[0]

    @pl.when(e == 0)
    def _():
        acc_scr[...] = jnp.zeros_like(acc_scr)

    h = h_ref[...]
    comb = comb_ref[...]
    lane = lax.broadcasted_iota(jnp.int32, comb.shape, 1)
    w_col = jnp.sum(jnp.where(lane == e, comb, 0.0), axis=1, keepdims=True)
    act = _silu(_dot(h, wg_ref[...])) * _dot(h, wu_ref[...])
    acc_scr[...] += _dot((act * w_col).astype(BF16), wd_ref[...])

    @pl.when(e == N_EXPERTS - 1)
    def _():
        o_ref[...] = _ln(alpha * x1_ref[...] + (1.0 + _mod_rows(gt2_ref, tm)) * acc_scr[...]) * g_ref[...] + b_ref[...]


def _moe(h2, comb, wg, wu, wd, x1, mods, ln2g, ln2b, tm, alpha):
    t = h2.shape[0]
    return pl.pallas_call(
        functools.partial(_moe_kernel, alpha=alpha),
        grid=(t // tm, N_EXPERTS),
        in_specs=[pl.BlockSpec((tm, D_MODEL), lambda i, e: (i, 0)),
                  pl.BlockSpec((tm, LANES), lambda i, e: (i, 0)),
                  pl.BlockSpec((None, D_MODEL, D_EXPERT), lambda i, e: (e, 0, 0)),
                  pl.BlockSpec((None, D_MODEL, D_EXPERT), lambda i, e: (e, 0, 0)),
                  pl.BlockSpec((None, D_EXPERT, D_MODEL), lambda i, e: (e, 0, 0)),
                  pl.BlockSpec((tm, D_MODEL), lambda i, e: (i, 0)),
                  _mod_spec(mods, tm, 5),
                  pl.BlockSpec((1, D_MODEL), lambda i, e: (0, 0)),
                  pl.BlockSpec((1, D_MODEL), lambda i, e: (0, 0))],
        out_specs=pl.BlockSpec((tm, D_MODEL), lambda i, e: (i, 0)),
        out_shape=jax.ShapeDtypeStruct((t, D_MODEL), F32),
        scratch_shapes=[pltpu.VMEM((tm, D_MODEL), F32)],
        compiler_params=_cparams(("parallel", "arbitrary")),
        name="moe",
    )(h2, comb, wg, wu, wd, x1, mods, ln2g, ln2b)


def _row_gather(src_ref, idx_ref, base, buf, slot, sem, n_rows, start):
    def body(r, c):
        cp = pltpu.make_async_copy(src_ref.at[pl.ds(idx_ref[base + r], 1), :],
                                   buf.at[slot, pl.ds(r, 1), :], sem.at[slot])
        if start:
            cp.start()
        else:
            cp.wait()
        return c

    lax.fori_loop(0, n_rows, body, 0)


def _moe_group_kernel(perm_ref, tgrp_ref, nlive_ref, x1_ref, comb_ref, sh2_ref, sc2_ref, wg_ref, wu_ref, wd_ref,
                      y_ref, xbuf, sems):
    k = pl.program_id(0)
    tg = comb_ref.shape[0]
    n_live = nlive_ref[0]

    @pl.when(k == 0)
    def _():
        _row_gather(x1_ref, perm_ref, 0, xbuf, 0, sems, tg, True)

    @pl.when(k + 1 < n_live)
    def _():
        _row_gather(x1_ref, perm_ref, (k + 1) * tg, xbuf, (k + 1) % 2, sems, tg, True)

    @pl.when(k < jnp.maximum(n_live, 1))
    def _():
        _row_gather(x1_ref, perm_ref, k * tg, xbuf, k % 2, sems, tg, False)

    @pl.when(k < n_live)
    def _():
        h = (_ln(xbuf[k % 2]) * (1.0 + sc2_ref[0:1, :]) + sh2_ref[0:1, :]).astype(BF16)
        comb = comb_ref[...]
        lane = lax.broadcasted_iota(jnp.int32, comb.shape, 1)
        e0 = tgrp_ref[k] * EXPERTS_PER_GROUP
        acc = jnp.zeros((tg, D_MODEL), F32)
        for e in range(EXPERTS_PER_GROUP):
            w_col = jnp.sum(jnp.where(lane == e0 + e, comb, 0.0), axis=1, keepdims=True)
            act = _silu(_dot(h, wg_ref[e])) * _dot(h, wu_ref[e])
            acc = acc + _dot((act * w_col).astype(BF16), wd_ref[e])
        y_ref[...] = acc

    @pl.when(k >= n_live)
    def _():
        y_ref[...] = jnp.zeros(y_ref.shape, F32)


def _moe_finish_kernel(pos_ref, y_ref, x1_ref, gt2_ref, g_ref, b_ref, o_ref, ybuf, sems, *, alpha):
    i = pl.program_id(0)
    tm = x1_ref.shape[0]

    @pl.when(i == 0)
    def _():
        _row_gather(y_ref, pos_ref, 0, ybuf, 0, sems, tm, True)

    @pl.when(i + 1 < pl.num_programs(0))
    def _():
        _row_gather(y_ref, pos_ref, (i + 1) * tm, ybuf, (i + 1) % 2, sems, tm, True)

    _row_gather(y_ref, pos_ref, i * tm, ybuf, i % 2, sems, tm, False)
    o_ref[...] = _ln(alpha * x1_ref[...] + (1.0 + gt2_ref[0:1, :]) * ybuf[i % 2]) * g_ref[...] + b_ref[...]


def _moe_dispatch(comb, tg):
    t = comb.shape[0]
    i32 = jnp.int32
    cg = comb[:, :N_EXPERTS].reshape(t, N_EXPERT_GROUPS, EXPERTS_PER_GROUP)
    gid = jnp.argmax(jnp.sum(cg, axis=-1) > 0.0, axis=-1).astype(i32)
    onehot = (gid[:, None] == jnp.arange(N_EXPERT_GROUPS, dtype=i32)[None, :]).astype(i32)
    csum = jnp.cumsum(onehot, axis=0)
    rank = jnp.take_along_axis(csum, gid[:, None], axis=1)[:, 0] - 1
    pcount = ((csum[-1] + tg - 1) // tg) * tg
    ends = jnp.cumsum(pcount)
    pos = ((ends - pcount)[gid] + rank).astype(i32)
    rp = t + N_EXPERT_GROUPS * tg
    perm = jnp.zeros((rp,), i32).at[pos].set(jnp.arange(t, dtype=i32))
    live = jnp.zeros((rp,), F32).at[pos].set(1.0)
    comb_sorted = comb[perm] * live[:, None]
    tile_start = jnp.arange(rp // tg, dtype=i32) * tg
    tile_group = jnp.minimum(jnp.sum(tile_start[:, None] >= ends[None, :], axis=1), N_EXPERT_GROUPS - 1).astype(i32)
    n_live = (ends[-1] // tg).astype(i32).reshape(1)
    return perm, pos, tile_group, n_live, comb_sorted


def _moe_grouped(x1, comb, wg, wu, wd, mods, ln2g, ln2b, alpha):
    t = x1.shape[0]
    tg = MOE_GROUP_TILE
    perm, pos, tile_group, n_live, comb_sorted = _moe_dispatch(comb, tg)
    rp = perm.shape[0]
    epg = EXPERTS_PER_GROUP

    def wspec(shape):
        return pl.BlockSpec((epg,) + shape, lambda k, perm, tgrp, nl: (tgrp[k], 0, 0), pipeline_mode=pl.Buffered(1))

    y = pl.pallas_call(
        _moe_group_kernel,
        grid_spec=pltpu.PrefetchScalarGridSpec(
            num_scalar_prefetch=3,
            grid=(rp // tg,),
            in_specs=[pl.BlockSpec(memory_space=pl.ANY),
                      pl.BlockSpec((tg, LANES), lambda k, *_: (k, 0)),
                      pl.BlockSpec((8, D_MODEL), lambda k, *_: (0, 3)),
                      pl.BlockSpec((8, D_MODEL), lambda k, *_: (0, 4)),
                      wspec((D_MODEL, D_EXPERT)), wspec((D_MODEL, D_EXPERT)), wspec((D_EXPERT, D_MODEL))],
            out_specs=pl.BlockSpec((tg, D_MODEL), lambda k, *_: (k, 0)),
            scratch_shapes=[pltpu.VMEM((2, tg, D_MODEL), F32), pltpu.SemaphoreType.DMA((2,))]),
        out_shape=jax.ShapeDtypeStruct((rp, D_MODEL), F32),
        compiler_params=_cparams(("arbitrary",)),
        name="moe_group",
    )(perm, tile_group, n_live, x1, comb_sorted, mods, mods, wg, wu, wd)

    tm = MOE_FINISH_TILE
    return pl.pallas_call(
        functools.partial(_moe_finish_kernel, alpha=alpha),
        grid_spec=pltpu.PrefetchScalarGridSpec(
            num_scalar_prefetch=1,
            grid=(t // tm,),
            in_specs=[pl.BlockSpec(memory_space=pl.ANY),
                      pl.BlockSpec((tm, D_MODEL), lambda i, pos: (i, 0)),
                      pl.BlockSpec((8, D_MODEL), lambda i, pos: (0, 5)),
                      pl.BlockSpec((1, D_MODEL), lambda i, pos: (0, 0)),
                      pl.BlockSpec((1, D_MODEL), lambda i, pos: (0, 0))],
            out_specs=pl.BlockSpec((tm, D_MODEL), lambda i, pos: (i, 0)),
            scratch_shapes=[pltpu.VMEM((2, tm, D_MODEL), F32), pltpu.SemaphoreType.DMA((2,))]),
        out_shape=jax.ShapeDtypeStruct((t, D_MODEL), F32),
        compiler_params=_cparams(("arbitrary",)),
        name="moe_finish",
    )(pos, y, x1, mods, ln2g, ln2b)


def _rel_bucket(dist):
    n = jnp.maximum(dist, 0)
    max_exact = N_BUCKETS // 2
    nf = jnp.maximum(n, max_exact).astype(F32)
    large = max_exact + (jnp.log(nf / max_exact) / math.log(MAX_DISTANCE / max_exact)
                         * (N_BUCKETS - max_exact)).astype(jnp.int32)
    return jnp.where(n < max_exact, n, jnp.minimum(large, N_BUCKETS - 1))


def _bias_of(rel_bias, dist):
    return rel_bias[_rel_bucket(dist)].astype(F32)


def _prompt_tiles(rel_bias):
    kk = jnp.arange(BLOCK)[:, None]
    qq = jnp.arange(BLOCK)[None, :]

    def lanes(tile8, g):
        return jnp.concatenate([tile8[:, :, g * N_REP + r] for r in range(N_REP)], axis=1)

    def tile(m, lo, hi):
        d = m * BLOCK + qq - kk
        ok = (d >= lo) & (d < hi)
        return jnp.where(ok[:, :, None], _bias_of(rel_bias, d), NEG)

    big = 1 << 20
    neg_tile = jnp.full((BLOCK, BLOCK, N_HEADS), NEG, F32)
    sel_tiles = [tile(m, 0, big) for m in range(4)] + [neg_tile]
    win_tiles = [tile(m, 0, WINDOW) for m in range(WIN_BLOCKS + 1)] + [neg_tile]
    st = jnp.stack([jnp.stack([lanes(t, g) for t in sel_tiles]) for g in range(N_GROUPS)]) / QK_SCALE
    wt = jnp.stack([jnp.stack([lanes(t, g) for t in win_tiles]) for g in range(N_GROUPS)]) / QK_SCALE
    rows = []
    for m in range(4):
        d = m * BLOCK + jnp.arange(BLOCK) - (BLOCK - 1)
        rows.append(_bias_of(rel_bias, d))
    ct = jnp.stack([jnp.stack([jnp.concatenate([rw[:, g * N_REP + r] for r in range(N_REP)]) for rw in rows]
                              + [jnp.zeros((N_REP * BLOCK,), F32)] * 4) for g in range(N_GROUPS)])
    return st, wt, ct


def _sample_lanes(tile8, t_new):
    rows = tile8.shape[0]
    x = jnp.transpose(tile8, (0, 2, 1)).reshape(rows, N_HEADS * t_new)
    return jnp.pad(x, ((0, 0), (0, LANES - N_HEADS * t_new)))


def _sample_tiles(rel_bias, past, t_new, nbp, win_rows, win_pad):
    tq = past + jnp.arange(t_new)[None, :]

    def tile(k_pos, ok_extra=None, lo=0, hi=1 << 20):
        d = tq - k_pos[:, None]
        ok = (d >= lo) & (d < hi)
        if ok_extra is not None:
            ok = ok & ok_extra[:, None]
        return _sample_lanes(jnp.where(ok[:, :, None], _bias_of(rel_bias, d), NEG), t_new)

    n = jnp.arange(nbp)
    cb = tile(n * BLOCK + BLOCK - 1, n < past // BLOCK + 1)
    nearb = tile(past - PAGE + jnp.arange(PAGE))
    farb = jnp.broadcast_to(_sample_lanes(jnp.broadcast_to(
        rel_bias[N_BUCKETS - 1].astype(F32)[None, None, :], (1, t_new, N_HEADS)), t_new), (8, LANES))
    kn = jnp.arange(PAGE)
    newb = tile(past + kn, kn < t_new)
    wb = min(WINDOW, past)
    j = jnp.arange(win_pad)
    k_pos = past - wb + j
    winb = tile(k_pos, (j < win_rows) & (k_pos >= 0), 0, WINDOW)
    return cb, nearb, farb, newb, winb


def _mix_mats(ws, bs, tc, rep):
    n = tc // rep
    causal = jnp.tril(jnp.ones((CHUNK, CHUNK), F32))
    wc = (ws * causal)[:, :n, :n]
    eye = jnp.eye(rep, dtype=F32)
    m = jnp.einsum('ab,gts->gatbs', eye, wc).reshape(GMLP_GROUPS, tc, tc)
    bcol = jnp.tile(bs[:, :n].T, (rep, 1))
    return m.astype(BF16), jnp.pad(bcol, ((0, 0), (0, LANES - GMLP_GROUPS)))


def kernel(x_prompt, x_sample, cache_nsa_kv, state_win_kv, state_pool, page_table, c_prompt, c_sample, rel_bias, router_w, router_b, w_in, nsa_phi_pe, nsa_phi_w1, nsa_phi_w2, gmlp_norm_g, gmlp_norm_b, gmlp_ws, gmlp_bs, pool_w, pool_scale, proj_a, proj_b, proj_c, w_o, ada_w, ada_b, ln1_g, ln1_b, ln2_g, ln2_b, exp_w_gate, exp_w_up, exp_w_down):
    depth = w_in.shape[0]
    alpha = (2 * depth) ** 0.25
    s = x_prompt.shape[1]
    bsz, t_new = x_sample.shape[0], x_sample.shape[1]
    n_pool = cache_nsa_kv.shape[1]
    n_pages = page_table.shape[1]
    past = n_pages * PAGE
    ts = bsz * t_new
    assert x_prompt.shape[0] == 1 and s % CHUNK_KEYS == 0 and past % PAGE == 0
    assert N_HEADS * t_new <= LANES and t_new <= BLOCK and (past // BLOCK) >= N_SEL
    nb_s = past // BLOCK + 1
    nbp = -(-nb_s // 8) * 8

    n_c = 1 + bsz
    c_all = jnp.pad(jnp.concatenate([c_prompt, c_sample], axis=0), ((0, -n_c % 8), (0, 0)))
    mods_all = _ada_mods(c_all, ada_w, ada_b)

    cache_pages = cache_nsa_kv.reshape(depth * n_pool, PAGE * PAGED_KINDS * N_GROUPS, HEAD_DIM)
    st, wt, ct = _prompt_tiles(rel_bias)
    win_rows = min(WINDOW, past) + t_new
    win_pad = -(-win_rows // LANES) * LANES
    assert nbp <= SAMPLE_BLOCKS_PAD
    cb_s, nearb_s, farb_s, newb_s, winb_s = _sample_tiles(rel_bias, past, t_new, SAMPLE_BLOCKS_PAD, win_rows, win_pad)
    rw = jnp.pad(router_w, ((0, 0), (0, LANES - N_EXPERTS))).astype(BF16)
    rb = jnp.pad(router_b.reshape(N_EXPERTS, 1), ((0, LANES - N_EXPERTS), (0, LANES - 1)))

    xp = x_prompt.reshape(s, D_MODEL)
    xs = x_sample.reshape(ts, D_MODEL)
    outs = {k: [] for k in ("nsa_p", "nsa_s", "win_p", "win_s", "pool_p", "pool_s", "v_s")}

    for l in range(depth):
        w_parts = _in_proj_weights(w_in[l])
        pe = nsa_phi_pe[l]
        w1b = nsa_phi_w1[l].astype(BF16)
        w2b = nsa_phi_w2[l].astype(BF16)
        pab, pbb, pcb, wob = (a[l].astype(BF16) for a in (proj_a, proj_b, proj_c, w_o))
        wg, wu, wd = (a[l].astype(BF16) for a in (exp_w_gate, exp_w_up, exp_w_down))
        pwb = pool_w[l].astype(BF16)
        ps = pool_scale[l].reshape(1, POOL_WIDTH)
        gng = gmlp_norm_g[l].reshape(1, GMLP_WIDTH)
        gnb = gmlp_norm_b[l].reshape(1, GMLP_WIDTH)
        ln1g, ln1b, ln2g, ln2b = (a[l].reshape(1, D_MODEL) for a in (ln1_g, ln1_b, ln2_g, ln2_b))
        mods_p = jnp.broadcast_to(mods_all[l, 0:1], (8, 6 * D_MODEL))
        mods_s = jnp.repeat(mods_all[l, 1:1 + bsz], t_new, axis=0)

        z, zb = _in_proj(xp, mods_p, w_parts, 1024)
        gh = N_GROUPS * HEAD_DIM
        outs["nsa_p"].append(z[:, COL_KV:COL_KV + PAGED_KINDS * gh].reshape(1, s, PAGED_KINDS, N_GROUPS, HEAD_DIM))
        outs["win_p"].append(z[s - min(WINDOW, s):, COL_KV + PAGED_KINDS * gh:COL_KV + KV_KINDS * gh]
                             .reshape(1, min(WINDOW, s), 2, N_GROUPS, HEAD_DIM))
        outs["pool_p"].append(z[s - POOL_BUF:, COL_XC:COL_XC + POOL_WIDTH].reshape(1, POOL_BUF, POOL_WIDTH))

        nb = s // BLOCK
        cmp = _compress_prompt(z, pe, w1b, w2b).reshape(2, N_GROUPS, nb, HEAD_DIM)
        kc = cmp[0].astype(BF16)
        vcT = jnp.transpose(cmp[1], (0, 2, 1)).astype(BF16)
        nch = s // CHUNK_KEYS

        def kv_cols(kind, g):
            c0 = COL_KV + (kind * N_GROUPS + g) * HEAD_DIM
            return zb[:, c0:c0 + HEAD_DIM].reshape(nch, CHUNK_KEYS, HEAD_DIM)

        def k_chunks(kind):
            return jnp.stack([kv_cols(kind, g) for g in range(N_GROUPS)])

        def vT_chunks(kind):
            vt = jnp.stack([jnp.transpose(kv_cols(kind, g), (0, 2, 1)) for g in range(N_GROUPS)])
            ones = jnp.zeros((N_GROUPS, nch, 8, CHUNK_KEYS), BF16).at[:, :, 0].set(1.0)
            return jnp.concatenate([vt, ones], axis=2)

        q = zb[:, COL_Q:COL_Q + NSA_WIDTH].reshape(nb, BLOCK, N_GROUPS, N_REP, HEAD_DIM)
        qT = jnp.transpose(q, (2, 4, 0, 3, 1)).reshape(N_GROUPS, HEAD_DIM, nb * N_REP * BLOCK)
        gn = z[:, COL_GN:COL_GN + 3 * N_HEADS].reshape(nb, BLOCK, 3, N_GROUPS, N_REP)
        gates = jnp.transpose(gn, (3, 0, 2, 4, 1)).reshape(N_GROUPS, nb, 3, N_REP * BLOCK)
        gates = jnp.pad(gates, ((0, 0), (0, 0), (0, 5), (0, 0)))
        ob = _nsa_prompt(qT, kc, vcT, k_chunks(2), vT_chunks(3), k_chunks(4), vT_chunks(5), st, wt, ct, gates, s)

        mm, bcol = _mix_mats(gmlp_ws[l], gmlp_bs[l], CHUNK, 1)
        oa, _ = _gmlp(z, mm, bcol, gng, gnb, CHUNK)
        tp = 512
        cxb = COL_XC // POOL_WIDTH
        oc = _pool(z, z, lambda i: (jnp.maximum(i * (tp // 16) - 1, 0), cxb), pwb, ps, tp, True, 0, tp)
        x1, h2, comb = _merge(xp, z, oa, ob, oc, mods_p, ln1g, ln1b, pab, pbb, pcb, wob, rw, rb, 256, alpha)
        xp = _moe_grouped(x1, comb, wg, wu, wd, mods_p, ln2g, ln2b, alpha)

        zs, zbs = _in_proj(xs, mods_s, w_parts, ts)
        kvs = zs[:, COL_KV:COL_KV + 1536].reshape(bsz, t_new, KV_KINDS, N_GROUPS, HEAD_DIM)
        kvbs = zbs[:, COL_KV:COL_KV + 1536].reshape(bsz, t_new, KV_KINDS, gh)
        outs["nsa_s"].append(kvs[:, :, :PAGED_KINDS])
        win_full = jnp.concatenate([state_win_kv[l], kvs[:, :, PAGED_KINDS:]], axis=1)
        outs["win_s"].append(win_full[:, win_full.shape[1] - min(WINDOW, past + t_new):])
        xcs = zs[:, COL_XC:COL_XC + POOL_WIDTH].reshape(bsz, t_new, POOL_WIDTH)
        ext = jnp.concatenate([state_pool[l], xcs], axis=1)
        outs["pool_s"].append(ext[:, ext.shape[1] - POOL_BUF:])

        pt = page_table + l * n_pool
        new_cmp = kvs[:, :, 0:2].reshape(bsz, t_new, 2 * N_GROUPS * HEAD_DIM)
        knew = jnp.pad(kvbs[:, :, 2], ((0, 0), (0, PAGE - t_new), (0, 0)))
        vnew = jnp.pad(kvbs[:, :, 3], ((0, 0), (0, PAGE - t_new), (0, 0)))
        wf = jnp.pad(win_full.reshape(bsz, win_rows, 2, gh), ((0, 0), (0, win_pad - win_rows), (0, 0), (0, 0)))
        kwin_s = wf[:, :, 0].astype(BF16)
        vwin_s = wf[:, :, 1].astype(BF16)
        qs = zbs[:, COL_Q:COL_Q + NSA_WIDTH].reshape(bsz, t_new, N_GROUPS, N_REP, HEAD_DIM)
        qs = jnp.transpose(qs, (0, 2, 4, 3, 1)).reshape(bsz, N_GROUPS, HEAD_DIM, N_REP * t_new)
        n_q = N_REP * t_new
        qT_s = jnp.concatenate(
            [jnp.pad(qs[:, g], ((0, 0), (0, 0), (g * n_q, LANES - (g + 1) * n_q))) for g in range(N_GROUPS)], axis=1)
        gns = zs[:, COL_GN:COL_GN + 3 * N_HEADS].reshape(bsz, t_new, 3, N_HEADS)
        gates_s = jnp.transpose(gns, (0, 3, 1, 2)).reshape(bsz, N_HEADS * t_new, 3)
        gates_s = jnp.pad(gates_s, ((0, 0), (0, LANES - N_HEADS * t_new), (0, LANES - 3)))
        o_s = _sample_nsa(pt, cache_pages, new_cmp, qT_s, knew, vnew, kwin_s, vwin_s, pe, w1b, w2b,
                          cb_s, nearb_s, farb_s, newb_s, winb_s, gates_s, nbp)
        o4 = o_s[:, :N_HEADS * t_new].reshape(bsz, N_GROUPS, N_REP, t_new, N_GROUPS, HEAD_DIM)
        o4 = jnp.stack([o4[:, g, :, :, g] for g in range(N_GROUPS)], axis=1)
        ob_s = jnp.transpose(o4, (0, 3, 1, 2, 4)).reshape(ts, NSA_WIDTH).astype(BF16)

        mm_s, bcol_s = _mix_mats(gmlp_ws[l], gmlp_bs[l], ts, bsz)
        oa_s, vn_s = _gmlp(zs, mm_s, bcol_s, gng, gnb, ts)
        outs["v_s"].append(vn_s.reshape(bsz, t_new, GMLP_WIDTH))
        halo = jnp.pad(state_pool[l], ((0, 0), (16 - POOL_BUF, 0), (0, 0))).reshape(bsz * 16, POOL_WIDTH)
        oc_s = _pool(zs, halo, lambda i: (i, 0), pwb, ps, t_new, False, past, 0)
        x1s, h2s, comb_s = _merge(xs, zs, oa_s, ob_s, oc_s, mods_s, ln1g, ln1b, pab, pbb, pcb, wob, rw, rb, ts, alpha)
        xs = _moe(h2s, comb_s, wg, wu, wd, x1s, mods_s, ln2g, ln2b, ts, alpha)

    return (xp.reshape(1, s, D_MODEL), xs.reshape(bsz, t_new, D_MODEL),
            jnp.stack(outs["nsa_p"]), jnp.stack(outs["nsa_s"]), jnp.stack(outs["win_p"]), jnp.stack(outs["win_s"]),
            jnp.stack(outs["pool_p"]), jnp.stack(outs["pool_s"]), jnp.stack(outs["v_s"]))
```

```python
import functools
import math

import jax
import jax.numpy as jnp
from jax import lax
from jax.experimental import pallas as pl
from jax.experimental.pallas import tpu as pltpu

F32 = jnp.float32
BF16 = jnp.bfloat16

D_MODEL = 2048
HEAD_DIM = 128
N_GROUPS = 2
N_REP = 4
N_HEADS = N_GROUPS * N_REP
NSA_WIDTH = N_HEADS * HEAD_DIM
KV_KINDS = 6
PAGED_KINDS = 4
BLOCK = 64
N_SEL = 16
WINDOW = 512
WIN_BLOCKS = WINDOW // BLOCK
PHI_HIDDEN = 2 * HEAD_DIM
PAGE = 128
GMLP_GROUPS = 4
GMLP_WIDTH = 512
CHUNK = 128
POOL_WINDOWS = (2, 4, 8, 16)
POOL_GROUP_DIM = 128
POOL_WIDTH = 512
POOL_BUF = 15
N_BUCKETS = 32
MAX_DISTANCE = 128
N_EXPERTS = 16
N_EXPERT_GROUPS = 4
EXPERTS_PER_GROUP = 4
D_EXPERT = 512
LN_EPS = 1e-5
NEG = -1e30
FORCED_SCORE = 1e9

COL_Q = 0
COL_KV = 1024
COL_U = 2560
COL_V = 3072
COL_XC = 3584
COL_GM = 4096
COL_GN = 10240
N_IN_PAD = 10752
IN_TN = 512

LANES = 128
VMEM_LIMIT = 56 * 1024 * 1024
CHUNK_BLOCKS = 8
CHUNK_KEYS = CHUNK_BLOCKS * BLOCK
XC_PITCH = BLOCK + 8
SAMPLE_RING_PAGES = 16
SAMPLE_BLOCKS_PAD = 256
MOE_GROUP_TILE = 256
MOE_FINISH_TILE = 256


def _cparams(sem):
    return pltpu.CompilerParams(dimension_semantics=sem, vmem_limit_bytes=VMEM_LIMIT)


def _ln(x):
    mu = jnp.mean(x, axis=-1, keepdims=True)
    xc = x - mu
    var = jnp.mean(xc * xc, axis=-1, keepdims=True)
    return xc * lax.rsqrt(var + LN_EPS)


def _sigmoid(x):
    return 1.0 / (1.0 + jnp.exp(-x))


def _silu(x):
    return x * _sigmoid(x)


def _gelu_tanh(x):
    return 0.5 * x * (1.0 + jnp.tanh(math.sqrt(2.0 / math.pi) * (x + 0.044715 * (x * x * x))))


def _dot(a, b):
    return jnp.dot(a, b, preferred_element_type=F32)


QK_SCALE = HEAD_DIM ** -0.5
EXP2_SCALE = QK_SCALE * math.log2(math.e)


def _qk(k, qT):
    return _dot(k, qT) * QK_SCALE


def _ada_kernel(c_ref, w_ref, b_ref, o_ref):
    c = c_ref[...]
    o_ref[...] = _dot(_silu(c).astype(BF16), w_ref[...].astype(BF16)) + b_ref[...]


def _ada_mods(c_all, ada_w, ada_b):
    depth = ada_w.shape[0]
    rows = c_all.shape[0]
    n = ada_w.shape[2]
    tn = 1024
    return pl.pallas_call(
        _ada_kernel,
        grid=(depth, n // tn),
        in_specs=[pl.BlockSpec((rows, D_MODEL), lambda l, j: (0, 0)),
                  pl.BlockSpec((None, D_MODEL, tn), lambda l, j: (l, 0, j)),
                  pl.BlockSpec((None, 1, tn), lambda l, j: (l, 0, j))],
        out_specs=pl.BlockSpec((None, rows, tn), lambda l, j: (l, 0, j)),
        out_shape=jax.ShapeDtypeStruct((depth, rows, n), F32),
        compiler_params=_cparams(("parallel", "parallel")),
        name="ada_mods",
    )(c_all, ada_w, ada_b.reshape(depth, 1, n))


def _mod_rows(ref, tm):
    return ref[...] if ref.shape[0] == tm else ref[0:1, :]


N_QKV_TILES = COL_U // IN_TN
N_MAIN_TILES = COL_GN // IN_TN


def _inproj_kernel(x_ref, sh_ref, sc_ref, wa_ref, wb_ref, wc_ref, z_ref, zb_ref, h_scr):
    tm = x_ref.shape[0]
    j = pl.program_id(1)

    @pl.when(j == 0)
    def _():
        h = _ln(x_ref[...]) * (1.0 + _mod_rows(sc_ref, tm)) + _mod_rows(sh_ref, tm)
        h_scr[...] = h.astype(BF16)

    @pl.when(j < N_QKV_TILES)
    def _():
        z = _dot(h_scr[...], wa_ref[...])
        z_ref[...] = z
        zb_ref[...] = z.astype(BF16)

    @pl.when((j >= N_QKV_TILES) & (j < N_MAIN_TILES))
    def _():
        z_ref[...] = _dot(h_scr[...], wb_ref[...])

    @pl.when(j == N_MAIN_TILES)
    def _():
        z_ref[...] = _dot(h_scr[...], wc_ref[...])


def _mod_spec(mods, tm, col):
    rows = mods.shape[0]
    if rows == 8:
        return pl.BlockSpec((8, D_MODEL), lambda i, *_: (0, col))
    return pl.BlockSpec((tm, D_MODEL), lambda i, *_: (i, col))


def _in_proj_weights(w):
    n_gate = 3 * N_HEADS
    wa = w[:, :COL_U].astype(BF16)
    wb = w[:, COL_U + n_gate:].astype(BF16)
    wc = jnp.pad(w[:, COL_U:COL_U + n_gate], ((0, 0), (0, IN_TN - n_gate))).astype(BF16)
    return wa, wb, wc


def _in_proj(x, mods, weights, tm):
    t = x.shape[0]
    wa, wb, wc = weights
    n_b = N_MAIN_TILES - N_QKV_TILES
    return pl.pallas_call(
        _inproj_kernel,
        grid=(t // tm, N_IN_PAD // IN_TN),
        in_specs=[pl.BlockSpec((tm, D_MODEL), lambda i, j: (i, 0)),
                  _mod_spec(mods, tm, 0), _mod_spec(mods, tm, 1),
                  pl.BlockSpec((D_MODEL, IN_TN), lambda i, j: (0, jnp.minimum(j, N_QKV_TILES - 1))),
                  pl.BlockSpec((D_MODEL, IN_TN), lambda i, j: (0, jnp.clip(j - N_QKV_TILES, 0, n_b - 1))),
                  pl.BlockSpec((D_MODEL, IN_TN), lambda i, j: (0, 0))],
        out_specs=[pl.BlockSpec((tm, IN_TN), lambda i, j: (i, j)),
                   pl.BlockSpec((tm, IN_TN), lambda i, j: (i, jnp.minimum(j, N_QKV_TILES - 1)))],
        out_shape=[jax.ShapeDtypeStruct((t, N_IN_PAD), F32),
                   jax.ShapeDtypeStruct((t, COL_U), BF16)],
        scratch_shapes=[pltpu.VMEM((tm, D_MODEL), BF16)],
        compiler_params=_cparams(("parallel", "arbitrary")),
        name="in_proj",
    )(x, mods, mods, wa, wb, wc)


def _compress_pair(x0_ref, x1_ref, nb, pe_ref, w1_ref, w2_ref):
    def step(p, acc):
        pe = pe_ref[pl.ds(p, 1), :]
        a0 = x0_ref[pl.ds(p, nb, stride=BLOCK), :] + pe
        a1 = x1_ref[pl.ds(p, nb, stride=BLOCK), :] + pe
        a = jnp.concatenate([a0, a1], axis=0).astype(BF16)
        w = w1_ref[pl.ds(pl.multiple_of(p * HEAD_DIM, HEAD_DIM), HEAD_DIM), :]
        return acc + _dot(a, w)

    acc = lax.fori_loop(0, BLOCK, step, jnp.zeros((2 * nb, PHI_HIDDEN), F32))
    return _dot(_gelu_tanh(acc).astype(BF16), w2_ref[...])


def _compress_prompt_kernel(x0_ref, x1_ref, pe_ref, w1_ref, w2_ref, o_ref, *, nb):
    o_ref[...] = _compress_pair(x0_ref, x1_ref, nb, pe_ref, w1_ref, w2_ref)


def _compress_prompt(z, pe, w1b, w2b):
    s = z.shape[0]
    nb = s // BLOCK
    cb = COL_KV // HEAD_DIM
    return pl.pallas_call(
        functools.partial(_compress_prompt_kernel, nb=nb),
        grid=(2,),
        in_specs=[pl.BlockSpec((s, HEAD_DIM), lambda k: (0, cb + 2 * k)),
                  pl.BlockSpec((s, HEAD_DIM), lambda k: (0, cb + 2 * k + 1)),
                  pl.BlockSpec((None, BLOCK, HEAD_DIM), lambda k: (k, 0, 0)),
                  pl.BlockSpec((None, BLOCK * HEAD_DIM, PHI_HIDDEN), lambda k: (k, 0, 0)),
                  pl.BlockSpec((None, PHI_HIDDEN, HEAD_DIM), lambda k: (k, 0, 0))],
        out_specs=pl.BlockSpec((None, 2 * nb, HEAD_DIM), lambda k: (k, 0, 0)),
        out_shape=jax.ShapeDtypeStruct((2, 2 * nb, HEAD_DIM), F32),
        compiler_params=_cparams(("parallel",)),
        name="compress_prompt",
    )(z, z, pe, w1b, w2b)


N_FORCED = 3


def _select_blocks(score, forced, n_io):
    def body(_, carry):
        work, sel = carry
        mval = jnp.max(work, axis=0, keepdims=True)
        cand = jnp.where(work == mval, n_io, jnp.int32(1 << 30))
        idx = jnp.min(cand, axis=0, keepdims=True)
        pick = n_io == idx
        return jnp.where(pick, -jnp.inf, work), jnp.where(pick, 1.0, sel)

    init = (jnp.where(forced, -jnp.inf, score), jnp.where(forced, 1.0, 0.0))
    _, sel = lax.fori_loop(0, N_SEL - N_FORCED, body, init)
    return sel


def _nsa_prompt_kernel(qT_ref, kc_ref, vcT_ref, ksel_ref, vselT_ref, kwin_ref, vwinT_ref,
                       st_ref, wt_ref, ct_ref, gate_ref, o_ref, selneg_scr, selfar_scr, s_a, s_b, p_a, p_b, *, nb):
    i = pl.program_id(1)
    qT = qT_ref[...]
    nl = N_REP * BLOCK

    n_io = lax.broadcasted_iota(jnp.int32, (nb, nl), 0)
    qq = lax.broadcasted_iota(jnp.int32, (nb, nl), 1) % BLOCK
    m = i - n_io
    ct = ct_ref[...]
    bias = jnp.where(m == 0, ct[0:1], jnp.where(m == 1, ct[1:2], jnp.where(m == 2, ct[2:3], ct[3:4])))
    valid = (m >= 1) | ((m == 0) & (qq == BLOCK - 1))
    s = jnp.where(valid, _qk(kc_ref[...], qT) + bias, NEG)
    mx = jnp.max(s, axis=0, keepdims=True)
    p = jnp.where(valid, jnp.exp(s - mx), 0.0)
    l = jnp.sum(p, axis=0, keepdims=True)
    pn = p / jnp.where(l > 0.0, l, 1.0)
    o_cmp = _dot(vcT_ref[...], pn.astype(BF16))

    half = pn[:, :LANES] + pn[:, LANES:]
    imp = half + pltpu.roll(half, BLOCK, 1)
    n_io1 = n_io[:, :LANES]
    forced = (n_io1 == 0) | (n_io1 == i) | (n_io1 == i - 1)
    score = jnp.where(forced, FORCED_SCORE, imp)
    score = jnp.where(n_io1 <= i, score, -1.0)
    sel = _select_blocks(score, forced, n_io1)
    neg1 = jnp.where((sel > 0.5) & (score >= 0.0), 0.0, NEG)
    neg2 = jnp.concatenate([neg1, neg1], axis=1)
    selneg_scr[...] = neg2
    far_row = st_ref[st_ref.shape[0] - 2][0:1, :]
    selfar_scr[...] = jnp.where(neg2 == 0.0, far_row, NEG)

    c_i = i // CHUNK_BLOCKS

    def branch(k_ref, vT_ref, tile_ref, n_tiles, use_sel, n_chunks):
        def logits(c, dst):
            dst[...] = _dot(k_ref[jnp.maximum(c, 0)], qT)

        def pv(c, p_ref):
            return _dot(vT_ref[jnp.clip(c, 0, vT_ref.shape[0] - 1)], p_ref[...])

        def consume(c, src, m_run, near):
            live = c >= 0
            cz = jnp.maximum(c, 0)
            parts = []
            for b in range(CHUNK_BLOCKS):
                n = cz * CHUNK_BLOCKS + b
                sb = src[b * BLOCK:(b + 1) * BLOCK, :]
                if near:
                    mm = i - n
                    if use_sel:
                        tidx = jnp.where(mm < 0, n_tiles - 1, jnp.minimum(mm, n_tiles - 2))
                    else:
                        tidx = jnp.where((mm < 0) | (mm > n_tiles - 2), n_tiles - 1, mm)
                    sb = sb + tile_ref[jnp.where(live, tidx, n_tiles - 1)]
                    if use_sel:
                        sb = sb + selneg_scr[pl.ds(n, 1), :]
                else:
                    sb = sb + jnp.where(live, selfar_scr[pl.ds(n, 1), :], NEG)
                parts.append(sb)
            s2 = jnp.concatenate(parts, axis=0)
            m_new = jnp.maximum(m_run, jnp.max(s2, axis=0, keepdims=True))
            alpha = jnp.exp2((m_run - m_new) * EXP2_SCALE)
            pp = jnp.exp2((s2 - m_new) * EXP2_SCALE)
            return m_new, alpha, pp.astype(BF16)

        def pair(t, carry, near):
            m_run, acc = carry
            c0 = c_i - 2 * t
            logits(c0 - 1, s_b)
            r_prev = pv(c0 + 1, p_b)
            m_run, alpha, pp = consume(c0, s_a, m_run, near)
            p_a[...] = pp
            acc = alpha * (acc + r_prev)
            logits(c0 - 2, s_a)
            r_cur = pv(c0, p_a)
            m_run, alpha, pp = consume(c0 - 1, s_b, m_run, near)
            p_b[...] = pp
            return m_run, alpha * (acc + r_cur)

        logits(c_i, s_a)
        p_b[...] = jnp.zeros(p_b.shape, BF16)
        carry = (jnp.full((1, nl), NEG, F32), jnp.zeros((vT_ref.shape[1], nl), F32))
        carry = pair(0, carry, True)
        n_pairs = (n_chunks + 1) // 2 if use_sel else 1
        if use_sel:
            carry = lax.fori_loop(1, n_pairs, lambda t, cr: pair(t, cr, False), carry)
        acc = carry[1] + pv(c_i - 2 * n_pairs + 1, p_b)
        return acc[:HEAD_DIM] / acc[HEAD_DIM:HEAD_DIM + 1]

    o_sel = branch(ksel_ref, vselT_ref, st_ref, st_ref.shape[0], True, c_i + 1)
    o_win = branch(kwin_ref, vwinT_ref, wt_ref, wt_ref.shape[0], False, jnp.minimum(c_i + 1, 2))

    gates = _sigmoid(gate_ref[...])
    oT = gates[0:1] * o_cmp + gates[1:2] * o_sel + gates[2:3] * o_win
    o = oT.T
    for r in range(N_REP):
        o_ref[:, r * HEAD_DIM:(r + 1) * HEAD_DIM] = o[r * BLOCK:(r + 1) * BLOCK].astype(o_ref.dtype)


def _nsa_prompt(qT, kc, vcT, ksel, vselT, kwin, vwinT, st, wt, ct, gates, s):
    nb = s // BLOCK
    nq = s // BLOCK
    nl = N_REP * BLOCK
    nch = s // CHUNK_KEYS

    def whole(arr):
        shp = arr.shape[1:]
        return pl.BlockSpec((None,) + shp, lambda g, i: (g,) + (0,) * len(shp))

    return pl.pallas_call(
        functools.partial(_nsa_prompt_kernel, nb=nb),
        grid=(N_GROUPS, nq),
        in_specs=[pl.BlockSpec((None, HEAD_DIM, nl), lambda g, i: (g, 0, i)),
                  whole(kc), whole(vcT), whole(ksel), whole(vselT), whole(kwin), whole(vwinT),
                  whole(st), whole(wt), whole(ct),
                  pl.BlockSpec((None, None, 8, nl), lambda g, i: (g, i, 0, 0))],
        out_specs=pl.BlockSpec((BLOCK, N_REP * HEAD_DIM), lambda g, i: (i, g)),
        out_shape=jax.ShapeDtypeStruct((s, NSA_WIDTH), BF16),
        scratch_shapes=[pltpu.VMEM((nb, nl), F32),
                        pltpu.VMEM((nb, nl), F32),
                        pltpu.VMEM((CHUNK_KEYS, nl), F32),
                        pltpu.VMEM((CHUNK_KEYS, nl), F32),
                        pltpu.VMEM((CHUNK_KEYS, nl), BF16),
                        pltpu.VMEM((CHUNK_KEYS, nl), BF16)],
        compiler_params=_cparams(("parallel", "arbitrary")),
        name="nsa_prompt",
    )(qT, kc, vcT, ksel, vselT, kwin, vwinT, st, wt, ct, gates)


def _sample_nsa_kernel(pt_ref, cache_ref, newc_ref, qT_ref, knew_ref, vnew_ref, wst_ref, kwn_ref, vwn_ref,
                       pe_ref, w1_ref, w2_ref, cb_ref, nearb_ref, farb_ref, newb_ref, winb_ref, gate_ref, o_ref,
                       ring, xc, kcs, vsel, s_scr, selneg_scr, sems, *, n_pages, t_new, nbp, n_ring):
    b = pl.program_id(0)
    total = pl.num_programs(0) * n_pages
    past = n_pages * PAGE
    cur = past // BLOCK
    kinds_per_row = PAGED_KINDS * N_GROUPS
    qT = qT_ref[...]

    def page_dma(t, slot):
        return pltpu.make_async_copy(cache_ref.at[pt_ref[t // n_pages, t % n_pages]], ring.at[slot], sems.at[slot])

    @pl.when(b == 0)
    def _():
        for t in range(n_ring):
            page_dma(t, t).start()

    far = farb_ref[0:1, :]

    def page_step(j, c):
        t = b * n_pages + j
        slot = t % n_ring
        page_dma(t, slot).wait()
        buf = ring.at[slot]

        def rows(kg):
            return buf[pl.ds(kg, PAGE, stride=kinds_per_row), :]

        r0 = pl.multiple_of(j * PAGE, PAGE)
        base = pl.multiple_of(j * (PAGE // BLOCK) * XC_PITCH, 8)
        for kg in range(2 * N_GROUPS):
            x = rows(kg)
            for h in range(PAGE // BLOCK):
                xc[kg, pl.ds(base + h * XC_PITCH, BLOCK), :] = x[h * BLOCK:(h + 1) * BLOCK]
        k = jnp.concatenate([rows(4), rows(5)], axis=1).astype(BF16)
        s_scr[pl.ds(r0, PAGE), :] = _qk(k, qT) + jnp.where(j == n_pages - 1, nearb_ref[...], far)
        vsel[pl.ds(r0, PAGE), :] = jnp.concatenate([rows(6), rows(7)], axis=1).astype(BF16)

        @pl.when(t + n_ring < total)
        def _():
            page_dma(t + n_ring, slot).start()

        return c

    lax.fori_loop(0, n_pages, page_step, 0)

    tail = (nbp - cur) * XC_PITCH
    for kg in range(2 * N_GROUPS):
        xc[kg, pl.ds(cur * XC_PITCH, tail), :] = jnp.zeros((tail, HEAD_DIM), F32)
        xc[kg, pl.ds(cur * XC_PITCH, t_new), :] = newc_ref[:, kg * HEAD_DIM:(kg + 1) * HEAD_DIM]
    s_scr[pl.ds(past, PAGE), :] = _qk(knew_ref[...], qT) + newb_ref[...]
    vsel[pl.ds(past, PAGE), :] = vnew_ref[...]

    for kind in range(2):
        def step(pp, acc):
            p = 2 * pp
            pe0 = pe_ref[kind, pl.ds(p, 1), :]
            pe1 = pe_ref[kind, pl.ds(p + 1, 1), :]
            parts = []
            for g in range(N_GROUPS):
                x0 = xc[2 * kind + g, pl.ds(p, nbp, stride=XC_PITCH), :] + pe0
                x1 = xc[2 * kind + g, pl.ds(p + 1, nbp, stride=XC_PITCH), :] + pe1
                parts.append(jnp.concatenate([x0, x1], axis=1))
            a = jnp.concatenate(parts, axis=0).astype(BF16)
            w = w1_ref[kind, pl.ds(pl.multiple_of(p * HEAD_DIM, 2 * HEAD_DIM), 2 * HEAD_DIM), :]
            return acc + _dot(a, w)

        acc = lax.fori_loop(0, BLOCK // 2, step, jnp.zeros((N_GROUPS * nbp, PHI_HIDDEN), F32), unroll=2)
        out = _dot(_gelu_tanh(acc).astype(BF16), w2_ref[kind])
        kcs[kind, nbp:, :] = jnp.zeros((kcs.shape[1] - nbp, N_GROUPS * HEAD_DIM), F32)
        for g in range(N_GROUPS):
            kcs[kind, 0:nbp, g * HEAD_DIM:(g + 1) * HEAD_DIM] = out[g * nbp:(g + 1) * nbp]

    def softmax_rows(s):
        mx = jnp.max(s, axis=0, keepdims=True)
        p = jnp.where(s > 0.5 * NEG, jnp.exp(s - mx), 0.0)
        l = jnp.sum(p, axis=0, keepdims=True)
        return p / jnp.where(l > 0.0, l, 1.0)

    pn = softmax_rows(_qk(kcs[0].astype(BF16), qT) + cb_ref[...])
    o_cmp = _dot(pn.T.astype(BF16), kcs[1].astype(BF16))

    tot = pn + pltpu.roll(pn, t_new, 1) + pltpu.roll(pn, 2 * t_new, 1) + pltpu.roll(pn, 3 * t_new, 1)
    lane = lax.broadcasted_iota(jnp.int32, pn.shape, 1)
    top = jnp.where((lane // t_new) % N_REP == N_REP - 1, tot, 0.0)
    imp = top + pltpu.roll(top, LANES - t_new, 1) + pltpu.roll(top, LANES - 2 * t_new, 1) \
        + pltpu.roll(top, LANES - 3 * t_new, 1)
    n_io = lax.broadcasted_iota(jnp.int32, pn.shape, 0)
    forced = (n_io == 0) | (n_io == cur) | (n_io == cur - 1)
    score = jnp.where(forced, FORCED_SCORE, imp)
    score = jnp.where(n_io <= cur, score, -jnp.inf)
    sel = _select_blocks(score, forced, n_io)
    selneg_scr[...] = jnp.where(sel > 0.5, 0.0, NEG)

    n_st = wst_ref.shape[0] // (2 * N_GROUPS)

    def st_rows(j):
        return wst_ref[pl.ds(j, n_st, stride=2 * N_GROUPS), :]

    kw = jnp.concatenate([st_rows(0), st_rows(1)], axis=1).astype(BF16)
    vw = jnp.concatenate([st_rows(2), st_rows(3)], axis=1).astype(BF16)
    s_w = jnp.concatenate([_qk(kw, qT) + winb_ref[0:n_st, :], _qk(kwn_ref[...], qT) + winb_ref[n_st:, :]], axis=0)
    pwt = softmax_rows(s_w).T.astype(BF16)
    o_win = _dot(pwt[:, :n_st], vw) + _dot(pwt[:, n_st:], vwn_ref[...])

    blk = PAGE // BLOCK

    def mask_step(c, mx):
        r0 = pl.multiple_of(c * PAGE, PAGE)
        neg = jnp.concatenate(
            [jnp.broadcast_to(selneg_scr[pl.ds(c * blk + h, 1), :], (BLOCK, LANES)) for h in range(blk)], axis=0)
        s = s_scr[pl.ds(r0, PAGE), :] + neg
        s_scr[pl.ds(r0, PAGE), :] = s
        return jnp.maximum(mx, jnp.max(s, axis=0, keepdims=True))

    mx = lax.fori_loop(0, n_pages + 1, mask_step, jnp.full((1, LANES), NEG, F32))

    def pv_rows(r0, n_rows, carry):
        l_run, acc = carry
        pp = jnp.exp(s_scr[pl.ds(r0, n_rows), :] - mx)
        acc = acc + _dot(pp.T.astype(BF16), vsel[pl.ds(r0, n_rows), :])
        return l_run + jnp.sum(pp, axis=0, keepdims=True), acc

    pv_pages = 4 if n_pages % 4 == 0 else 1
    pv_keys = pv_pages * PAGE
    carry = lax.fori_loop(0, n_pages // pv_pages,
                          lambda c, carry: pv_rows(pl.multiple_of(c * pv_keys, pv_keys), pv_keys, carry),
                          (jnp.zeros((1, LANES), F32), jnp.zeros((LANES, N_GROUPS * HEAD_DIM), F32)))
    l_fin, acc = pv_rows(past, PAGE, carry)
    l_col = jnp.broadcast_to(l_fin, (LANES, LANES)).T[:, 0:1]
    o_sel = acc / l_col

    gates = _sigmoid(gate_ref[...])
    o_ref[...] = gates[:, 0:1] * o_cmp + gates[:, 1:2] * o_sel + gates[:, 2:3] * o_win


def _sample_nsa(pt, cache_pages, new_cmp, qT, knew, vnew, win_state, layer, kwn, vwn, pe, w1b, w2b,
                cb, nearb, farb, newb, winb, gates, nbp):
    bsz, n_pages = pt.shape
    t_new = new_cmp.shape[1]
    gh = N_GROUPS * HEAD_DIM
    n_keys = n_pages * PAGE + PAGE
    n_ring = min(SAMPLE_RING_PAGES, n_pages)

    def per_b(arr):
        shp = arr.shape[1:]
        return pl.BlockSpec((None,) + shp, lambda b, pt: (b,) + (0,) * len(shp))

    def const(arr, **kw):
        return pl.BlockSpec(arr.shape, lambda b, pt: (0,) * arr.ndim, **kw)

    grid_spec = pltpu.PrefetchScalarGridSpec(
        num_scalar_prefetch=1,
        grid=(bsz,),
        in_specs=[pl.BlockSpec(memory_space=pl.ANY),
                  per_b(new_cmp), per_b(qT), per_b(knew), per_b(vnew),
                  pl.BlockSpec((None,) + win_state.shape[1:], lambda b, pt: (layer * bsz + b, 0, 0)),
                  per_b(kwn), per_b(vwn),
                  const(pe), const(w1b, pipeline_mode=pl.Buffered(1)), const(w2b),
                  const(cb), const(nearb), const(farb), const(newb), const(winb), per_b(gates)],
        out_specs=pl.BlockSpec((None, LANES, gh), lambda b, pt: (b, 0, 0)),
        scratch_shapes=[pltpu.VMEM((n_ring,) + cache_pages.shape[1:], F32),
                        pltpu.VMEM((2 * N_GROUPS, nbp * XC_PITCH, HEAD_DIM), F32),
                        pltpu.VMEM((2, cb.shape[0], gh), F32),
                        pltpu.VMEM((n_keys, gh), BF16),
                        pltpu.VMEM((n_keys, LANES), F32),
                        pltpu.VMEM((cb.shape[0], LANES), F32),
                        pltpu.SemaphoreType.DMA((n_ring,))],
    )
    return pl.pallas_call(
        functools.partial(_sample_nsa_kernel, n_pages=n_pages, t_new=t_new, nbp=nbp, n_ring=n_ring),
        grid_spec=grid_spec,
        out_shape=jax.ShapeDtypeStruct((bsz, LANES, gh), F32),
        compiler_params=_cparams(("arbitrary",)),
        name="sample_nsa",
    )(pt, cache_pages, new_cmp, qT, knew, vnew, win_state, kwn, vwn, pe, w1b, w2b,
      cb, nearb, farb, newb, winb, gates)


def _gmlp_kernel(u_ref, v_ref, m_ref, bcol_ref, g_ref, b_ref, oa_ref, vn_ref):
    vn = _ln(v_ref[...]) * g_ref[...] + b_ref[...]
    vn_ref[...] = vn
    vb = vn.astype(BF16)
    u = u_ref[...]
    gd = GMLP_WIDTH // GMLP_GROUPS
    for g in range(GMLP_GROUPS):
        mixed = _dot(m_ref[g], vb[:, g * gd:(g + 1) * gd]) + bcol_ref[:, g:g + 1]
        oa_ref[:, g * gd:(g + 1) * gd] = (u[:, g * gd:(g + 1) * gd] * mixed).astype(oa_ref.dtype)


def _gmlp(z, mix_m, bcol, gn_g, gn_b, tc):
    t = z.shape[0]
    return pl.pallas_call(
        _gmlp_kernel,
        grid=(t // tc,),
        in_specs=[pl.BlockSpec((tc, GMLP_WIDTH), lambda i: (i, COL_U // GMLP_WIDTH)),
                  pl.BlockSpec((tc, GMLP_WIDTH), lambda i: (i, COL_V // GMLP_WIDTH)),
                  pl.BlockSpec((GMLP_GROUPS, tc, tc), lambda i: (0, 0, 0)),
                  pl.BlockSpec((tc, LANES), lambda i: (0, 0)),
                  pl.BlockSpec((1, GMLP_WIDTH), lambda i: (0, 0)),
                  pl.BlockSpec((1, GMLP_WIDTH), lambda i: (0, 0))],
        out_specs=[pl.BlockSpec((tc, GMLP_WIDTH), lambda i: (i, 0)),
                   pl.BlockSpec((tc, GMLP_WIDTH), lambda i: (i, 0))],
        out_shape=[jax.ShapeDtypeStruct((t, GMLP_WIDTH), BF16),
                   jax.ShapeDtypeStruct((t, GMLP_WIDTH), F32)],
        compiler_params=_cparams(("parallel",)),
        name="gmlp",
    )(z, z, mix_m, bcol, gn_g, gn_b)


def _pool_kernel(x_ref, halo_ref, pw_ref, ps_ref, o_ref, ext_scr, *, tm, zero_first, pos0, pos_step):
    i = pl.program_id(0)
    halo = halo_ref[...]
    if zero_first:
        halo = jnp.where(i == 0, 0.0, halo)
    ext_scr[0:16, :] = halo
    x = x_ref[...]
    ext_scr[16:16 + tm, :] = x
    pos = pos0 + i * pos_step + lax.broadcasted_iota(jnp.int32, (tm, POOL_GROUP_DIM), 0)
    for gi, w in enumerate(POOL_WINDOWS):
        c0 = gi * POOL_GROUP_DIM
        acc = ext_scr[16:16 + tm, c0:c0 + POOL_GROUP_DIM]
        for k in range(1, w):
            acc = acc + ext_scr[16 - k:16 - k + tm, c0:c0 + POOL_GROUP_DIM]
        count = jnp.minimum(pos + 1, w).astype(F32)
        d = acc / count - x[:, c0:c0 + POOL_GROUP_DIM]
        y = _dot(d.astype(BF16), pw_ref[gi]) * ps_ref[:, c0:c0 + POOL_GROUP_DIM]
        o_ref[:, c0:c0 + POOL_GROUP_DIM] = y.astype(o_ref.dtype)


def _pool(z, halo_src, halo_map, pwb, ps, tm, zero_first, pos0, pos_step):
    t = z.shape[0]
    return pl.pallas_call(
        functools.partial(_pool_kernel, tm=tm, zero_first=zero_first, pos0=pos0, pos_step=pos_step),
        grid=(t // tm,),
        in_specs=[pl.BlockSpec((tm, POOL_WIDTH), lambda i: (i, COL_XC // POOL_WIDTH)),
                  pl.BlockSpec((16, POOL_WIDTH), halo_map),
                  pl.BlockSpec((len(POOL_WINDOWS), POOL_GROUP_DIM, POOL_GROUP_DIM), lambda i: (0, 0, 0)),
                  pl.BlockSpec((1, POOL_WIDTH), lambda i: (0, 0))],
        out_specs=pl.BlockSpec((tm, POOL_WIDTH), lambda i: (i, 0)),
        out_shape=jax.ShapeDtypeStruct((t, POOL_WIDTH), BF16),
        scratch_shapes=[pltpu.VMEM((16 + tm, POOL_WIDTH), F32)],
        compiler_params=_cparams(("parallel",)),
        name="pool",
    )(z, halo_src, pwb, ps)


def _route(logits_t, rb_col):
    aff = _sigmoid(logits_t)
    biased = aff + rb_col
    rows = [biased[e:e + 1] for e in range(N_EXPERTS)]
    top2 = []
    gscore = []
    for g in range(N_EXPERT_GROUPS):
        grp = rows[g * EXPERTS_PER_GROUP:(g + 1) * EXPERTS_PER_GROUP]
        gs = None
        for a in range(EXPERTS_PER_GROUP):
            rank = None
            for c in range(EXPERTS_PER_GROUP):
                if c == a:
                    continue
                ahead = (grp[c] >= grp[a]) if c < a else (grp[c] > grp[a])
                rank = ahead.astype(F32) if rank is None else rank + ahead.astype(F32)
            in2 = rank < 1.5
            top2.append(in2)
            contrib = jnp.where(in2, grp[a], 0.0)
            gs = contrib if gs is None else gs + contrib
        gscore.append(gs)
    out = []
    for g in range(N_EXPERT_GROUPS):
        win = None
        for c in range(N_EXPERT_GROUPS):
            if c == g:
                continue
            ok = (gscore[g] > gscore[c]) if c < g else (gscore[g] >= gscore[c])
            win = ok if win is None else (win & ok)
        for a in range(EXPERTS_PER_GROUP):
            e = g * EXPERTS_PER_GROUP + a
            out.append(jnp.where(win & top2[e], aff[e:e + 1], 0.0))
    selw = jnp.concatenate(out, axis=0)
    return selw / jnp.sum(selw, axis=0, keepdims=True)


def _merge_kernel(x_ref, oa_ref, ob_ref, oc_ref, g0_ref, g1_ref, g2_ref, gt1_ref, sh2_ref, sc2_ref,
                  ln1g_ref, ln1b_ref, pa_ref, pb_ref, pc_ref, wo_ref, rw_ref, rb_ref,
                  x1_ref, h2_ref, comb_ref, *, alpha):
    tm = x_ref.shape[0]

    merged = _sigmoid(g0_ref[...]) * _dot(oa_ref[...], pa_ref[...])
    merged = merged + _sigmoid(g1_ref[...]) * _dot(ob_ref[...], pb_ref[...])
    merged = merged + _sigmoid(g2_ref[...]) * _dot(oc_ref[...], pc_ref[...])
    mix = _dot(merged.astype(BF16), wo_ref[...])
    x1 = _ln(alpha * x_ref[...] + (1.0 + _mod_rows(gt1_ref, tm)) * mix) * ln1g_ref[...] + ln1b_ref[...]
    x1_ref[...] = x1
    h2 = _ln(x1) * (1.0 + _mod_rows(sc2_ref, tm)) + _mod_rows(sh2_ref, tm)
    h2_ref[...] = h2.astype(BF16)
    logits = _dot(h2.astype(BF16), rw_ref[...])
    comb_t = _route(logits.T[:N_EXPERTS], rb_ref[...][:N_EXPERTS, 0:1])
    comb_full = jnp.concatenate([comb_t, jnp.zeros((LANES - N_EXPERTS, tm), F32)], axis=0)
    comb_ref[...] = comb_full.T


def _merge(x, z, oa, ob, oc, mods, ln1g, ln1b, pa, pb, pc, wo, rw, rb, tm, alpha):
    t = x.shape[0]

    def const(arr):
        return pl.BlockSpec(arr.shape, lambda i: (0,) * arr.ndim, pipeline_mode=pl.Buffered(1))

    gcol = COL_GM // D_MODEL
    return pl.pallas_call(
        functools.partial(_merge_kernel, alpha=alpha),
        grid=(t // tm,),
        in_specs=[pl.BlockSpec((tm, D_MODEL), lambda i: (i, 0)),
                  pl.BlockSpec((tm, GMLP_WIDTH), lambda i: (i, 0)),
                  pl.BlockSpec((tm, NSA_WIDTH), lambda i: (i, 0)),
                  pl.BlockSpec((tm, POOL_WIDTH), lambda i: (i, 0)),
                  pl.BlockSpec((tm, D_MODEL), lambda i: (i, gcol)),
                  pl.BlockSpec((tm, D_MODEL), lambda i: (i, gcol + 1)),
                  pl.BlockSpec((tm, D_MODEL), lambda i: (i, gcol + 2)),
                  _mod_spec(mods, tm, 2), _mod_spec(mods, tm, 3), _mod_spec(mods, tm, 4),
                  const(ln1g), const(ln1b), const(pa), const(pb), const(pc), const(wo), const(rw), const(rb)],
        out_specs=[pl.BlockSpec((tm, D_MODEL), lambda i: (i, 0)),
                   pl.BlockSpec((tm, D_MODEL), lambda i: (i, 0)),
                   pl.BlockSpec((tm, LANES), lambda i: (i, 0))],
        out_shape=[jax.ShapeDtypeStruct((t, D_MODEL), F32),
                   jax.ShapeDtypeStruct((t, D_MODEL), BF16),
                   jax.ShapeDtypeStruct((t, LANES), F32)],
        compiler_params=_cparams(("parallel",)),
        name="merge",
    )(x, oa, ob, oc, z, z, z, mods, mods, mods, ln1g, ln1b, pa, pb, pc, wo, rw, rb)


def _moe_kernel(h_ref, comb_ref, wg_ref, wu_ref, wd_ref, x1_ref, gt2_ref, g_ref, b_ref, o_ref, acc_scr, *, alpha):
    e = pl.program_id(1)
    tm = h_ref.shape[0]

    @pl.when(e == 0)
    def _():
        acc_scr[...] = jnp.zeros_like(acc_scr)

    h = h_ref[...]
    comb = comb_ref[...]
    lane = lax.broadcasted_iota(jnp.int32, comb.shape, 1)
    w_col = jnp.sum(jnp.where(lane == e, comb, 0.0), axis=1, keepdims=True)
    act = _silu(_dot(h, wg_ref[...])) * _dot(h, wu_ref[...])
    acc_scr[...] += _dot((act * w_col).astype(BF16), wd_ref[...])

    @pl.when(e == N_EXPERTS - 1)
    def _():
        o_ref[...] = _ln(alpha * x1_ref[...] + (1.0 + _mod_rows(gt2_ref, tm)) * acc_scr[...]) * g_ref[...] + b_ref[...]


def _moe(h2, comb, wg, wu, wd, x1, mods, ln2g, ln2b, tm, alpha):
    t = h2.shape[0]
    return pl.pallas_call(
        functools.partial(_moe_kernel, alpha=alpha),
        grid=(t // tm, N_EXPERTS),
        in_specs=[pl.BlockSpec((tm, D_MODEL), lambda i, e: (i, 0)),
                  pl.BlockSpec((tm, LANES), lambda i, e: (i, 0)),
                  pl.BlockSpec((None, D_MODEL, D_EXPERT), lambda i, e: (e, 0, 0)),
                  pl.BlockSpec((None, D_MODEL, D_EXPERT), lambda i, e: (e, 0, 0)),
                  pl.BlockSpec((None, D_EXPERT, D_MODEL), lambda i, e: (e, 0, 0)),
                  pl.BlockSpec((tm, D_MODEL), lambda i, e: (i, 0)),
                  _mod_spec(mods, tm, 5),
                  pl.BlockSpec((1, D_MODEL), lambda i, e: (0, 0)),
                  pl.BlockSpec((1, D_MODEL), lambda i, e: (0, 0))],
        out_specs=pl.BlockSpec((tm, D_MODEL), lambda i, e: (i, 0)),
        out_shape=jax.ShapeDtypeStruct((t, D_MODEL), F32),
        scratch_shapes=[pltpu.VMEM((tm, D_MODEL), F32)],
        compiler_params=_cparams(("parallel", "arbitrary")),
        name="moe",
    )(h2, comb, wg, wu, wd, x1, mods, ln2g, ln2b)


def _row_gather(src_ref, idx_ref, base, buf, slot, sem, n_rows, start):
    def body(r, c):
        cp = pltpu.make_async_copy(src_ref.at[pl.ds(idx_ref[base + r], 1), :],
                                   buf.at[slot, pl.ds(r, 1), :], sem.at[slot])
        if start:
            cp.start()
        else:
            cp.wait()
        return c

    lax.fori_loop(0, n_rows, body, 0)


def _moe_group_kernel(perm_ref, tgrp_ref, nlive_ref, x1_ref, comb_ref, sh2_ref, sc2_ref, wg_ref, wu_ref, wd_ref,
                      y_ref, xbuf, sems):
    k = pl.program_id(0)
    tg = comb_ref.shape[0]
    n_live = nlive_ref[0]

    @pl.when(k == 0)
    def _():
        _row_gather(x1_ref, perm_ref, 0, xbuf, 0, sems, tg, True)

    @pl.when(k + 1 < n_live)
    def _():
        _row_gather(x1_ref, perm_ref, (k + 1) * tg, xbuf, (k + 1) % 2, sems, tg, True)

    @pl.when(k < jnp.maximum(n_live, 1))
    def _():
        _row_gather(x1_ref, perm_ref, k * tg, xbuf, k % 2, sems, tg, False)

    @pl.when(k < n_live)
    def _():
        h = (_ln(xbuf[k % 2]) * (1.0 + sc2_ref[0:1, :]) + sh2_ref[0:1, :]).astype(BF16)
        comb = comb_ref[...]
        lane = lax.broadcasted_iota(jnp.int32, comb.shape, 1)
        e0 = tgrp_ref[k] * EXPERTS_PER_GROUP
        acc = jnp.zeros((tg, D_MODEL), F32)
        for e in range(EXPERTS_PER_GROUP):
            w_col = jnp.sum(jnp.where(lane == e0 + e, comb, 0.0), axis=1, keepdims=True)
            act = _silu(_dot(h, wg_ref[e])) * _dot(h, wu_ref[e])
            acc = acc + _dot((act * w_col).astype(BF16), wd_ref[e])
        y_ref[...] = acc

    @pl.when(k >= n_live)
    def _():
        y_ref[...] = jnp.zeros(y_ref.shape, F32)


def _moe_finish_kernel(pos_ref, y_ref, x1_ref, gt2_ref, g_ref, b_ref, o_ref, ybuf, sems, *, alpha):
    i = pl.program_id(0)
    tm = x1_ref.shape[0]

    @pl.when(i == 0)
    def _():
        _row_gather(y_ref, pos_ref, 0, ybuf, 0, sems, tm, True)

    @pl.when(i + 1 < pl.num_programs(0))
    def _():
        _row_gather(y_ref, pos_ref, (i + 1) * tm, ybuf, (i + 1) % 2, sems, tm, True)

    _row_gather(y_ref, pos_ref, i * tm, ybuf, i % 2, sems, tm, False)
    o_ref[...] = _ln(alpha * x1_ref[...] + (1.0 + gt2_ref[0:1, :]) * ybuf[i % 2]) * g_ref[...] + b_ref[...]


def _moe_dispatch(comb, tg):
    t = comb.shape[0]
    i32 = jnp.int32
    cg = comb[:, :N_EXPERTS].reshape(t, N_EXPERT_GROUPS, EXPERTS_PER_GROUP)
    gid = jnp.argmax(jnp.sum(cg, axis=-1) > 0.0, axis=-1).astype(i32)
    onehot = (gid[:, None] == jnp.arange(N_EXPERT_GROUPS, dtype=i32)[None, :]).astype(i32)
    csum = jnp.cumsum(onehot, axis=0)
    rank = jnp.take_along_axis(csum, gid[:, None], axis=1)[:, 0] - 1
    pcount = ((csum[-1] + tg - 1) // tg) * tg
    ends = jnp.cumsum(pcount)
    pos = ((ends - pcount)[gid] + rank).astype(i32)
    rp = t + N_EXPERT_GROUPS * tg
    perm = jnp.full((rp,), -1, i32).at[pos].set(jnp.arange(t, dtype=i32))
    live = (perm >= 0).astype(F32)
    perm = jnp.maximum(perm, 0)
    comb_sorted = comb[perm] * live[:, None]
    tile_start = jnp.arange(rp // tg, dtype=i32) * tg
    tile_group = jnp.minimum(jnp.sum(tile_start[:, None] >= ends[None, :], axis=1), N_EXPERT_GROUPS - 1).astype(i32)
    n_live = (ends[-1] // tg).astype(i32).reshape(1)
    return perm, pos, tile_group, n_live, comb_sorted


def _moe_grouped(x1, comb, wg, wu, wd, mods, ln2g, ln2b, alpha):
    t = x1.shape[0]
    tg = MOE_GROUP_TILE
    perm, pos, tile_group, n_live, comb_sorted = _moe_dispatch(comb, tg)
    rp = perm.shape[0]
    epg = EXPERTS_PER_GROUP

    def wspec(shape):
        return pl.BlockSpec((epg,) + shape, lambda k, perm, tgrp, nl: (tgrp[k], 0, 0), pipeline_mode=pl.Buffered(1))

    y = pl.pallas_call(
        _moe_group_kernel,
        grid_spec=pltpu.PrefetchScalarGridSpec(
            num_scalar_prefetch=3,
            grid=(rp // tg,),
            in_specs=[pl.BlockSpec(memory_space=pl.ANY),
                      pl.BlockSpec((tg, LANES), lambda k, *_: (k, 0)),
                      pl.BlockSpec((8, D_MODEL), lambda k, *_: (0, 3)),
                      pl.BlockSpec((8, D_MODEL), lambda k, *_: (0, 4)),
                      wspec((D_MODEL, D_EXPERT)), wspec((D_MODEL, D_EXPERT)), wspec((D_EXPERT, D_MODEL))],
            out_specs=pl.BlockSpec((tg, D_MODEL), lambda k, *_: (k, 0)),
            scratch_shapes=[pltpu.VMEM((2, tg, D_MODEL), F32), pltpu.SemaphoreType.DMA((2,))]),
        out_shape=jax.ShapeDtypeStruct((rp, D_MODEL), F32),
        compiler_params=_cparams(("arbitrary",)),
        name="moe_group",
    )(perm, tile_group, n_live, x1, comb_sorted, mods, mods, wg, wu, wd)

    tm = MOE_FINISH_TILE
    return pl.pallas_call(
        functools.partial(_moe_finish_kernel, alpha=alpha),
        grid_spec=pltpu.PrefetchScalarGridSpec(
            num_scalar_prefetch=1,
            grid=(t // tm,),
            in_specs=[pl.BlockSpec(memory_space=pl.ANY),
                      pl.BlockSpec((tm, D_MODEL), lambda i, pos: (i, 0)),
                      pl.BlockSpec((8, D_MODEL), lambda i, pos: (0, 5)),
                      pl.BlockSpec((1, D_MODEL), lambda i, pos: (0, 0)),
                      pl.BlockSpec((1, D_MODEL), lambda i, pos: (0, 0))],
            out_specs=pl.BlockSpec((tm, D_MODEL), lambda i, pos: (i, 0)),
            scratch_shapes=[pltpu.VMEM((2, tm, D_MODEL), F32), pltpu.SemaphoreType.DMA((2,))]),
        out_shape=jax.ShapeDtypeStruct((t, D_MODEL), F32),
        compiler_params=_cparams(("arbitrary",)),
        name="moe_finish",
    )(pos, y, x1, mods, ln2g, ln2b)


def _rel_bucket(dist):
    n = jnp.maximum(dist, 0)
    max_exact = N_BUCKETS // 2
    nf = jnp.maximum(n, max_exact).astype(F32)
    large = max_exact + (jnp.log(nf / max_exact) / math.log(MAX_DISTANCE / max_exact)
                         * (N_BUCKETS - max_exact)).astype(jnp.int32)
    return jnp.where(n < max_exact, n, jnp.minimum(large, N_BUCKETS - 1))


def _bias_of(rel_bias, dist):
    return rel_bias[_rel_bucket(dist)].astype(F32)


def _prompt_tiles(rel_bias):
    kk = jnp.arange(BLOCK)[:, None]
    qq = jnp.arange(BLOCK)[None, :]

    def lanes(tile8, g):
        return jnp.concatenate([tile8[:, :, g * N_REP + r] for r in range(N_REP)], axis=1)

    def tile(m, lo, hi):
        d = m * BLOCK + qq - kk
        ok = (d >= lo) & (d < hi)
        return jnp.where(ok[:, :, None], _bias_of(rel_bias, d), NEG)

    big = 1 << 20
    neg_tile = jnp.full((BLOCK, BLOCK, N_HEADS), NEG, F32)
    sel_tiles = [tile(m, 0, big) for m in range(4)] + [neg_tile]
    win_tiles = [tile(m, 0, WINDOW) for m in range(WIN_BLOCKS + 1)] + [neg_tile]
    st = jnp.stack([jnp.stack([lanes(t, g) for t in sel_tiles]) for g in range(N_GROUPS)]) / QK_SCALE
    wt = jnp.stack([jnp.stack([lanes(t, g) for t in win_tiles]) for g in range(N_GROUPS)]) / QK_SCALE
    rows = []
    for m in range(4):
        d = m * BLOCK + jnp.arange(BLOCK) - (BLOCK - 1)
        rows.append(_bias_of(rel_bias, d))
    ct = jnp.stack([jnp.stack([jnp.concatenate([rw[:, g * N_REP + r] for r in range(N_REP)]) for rw in rows]
                              + [jnp.zeros((N_REP * BLOCK,), F32)] * 4) for g in range(N_GROUPS)])
    return st, wt, ct


def _sample_lanes(tile8, t_new):
    rows = tile8.shape[0]
    x = jnp.transpose(tile8, (0, 2, 1)).reshape(rows, N_HEADS * t_new)
    return jnp.pad(x, ((0, 0), (0, LANES - N_HEADS * t_new)))


def _sample_tiles(rel_bias, past, t_new, nbp, win_rows, win_pad):
    tq = past + jnp.arange(t_new)[None, :]

    def tile(k_pos, ok_extra=None, lo=0, hi=1 << 20):
        d = tq - k_pos[:, None]
        ok = (d >= lo) & (d < hi)
        if ok_extra is not None:
            ok = ok & ok_extra[:, None]
        return _sample_lanes(jnp.where(ok[:, :, None], _bias_of(rel_bias, d), NEG), t_new)

    n = jnp.arange(nbp)
    cb = tile(n * BLOCK + BLOCK - 1, n < past // BLOCK + 1)
    nearb = tile(past - PAGE + jnp.arange(PAGE))
    farb = jnp.broadcast_to(_sample_lanes(jnp.broadcast_to(
        rel_bias[N_BUCKETS - 1].astype(F32)[None, None, :], (1, t_new, N_HEADS)), t_new), (8, LANES))
    kn = jnp.arange(PAGE)
    newb = tile(past + kn, kn < t_new)
    wb = min(WINDOW, past)
    j = jnp.arange(win_pad)
    k_pos = past - wb + j
    winb = tile(k_pos, (j < win_rows) & (k_pos >= 0), 0, WINDOW)
    return cb, nearb, farb, newb, winb


def _mix_mats(ws, bs, tc, rep):
    n = tc // rep
    causal = jnp.tril(jnp.ones((CHUNK, CHUNK), F32))
    wc = (ws * causal)[:, :n, :n]
    eye = jnp.eye(rep, dtype=F32)
    m = jnp.einsum('ab,gts->gatbs', eye, wc).reshape(GMLP_GROUPS, tc, tc)
    bcol = jnp.tile(bs[:, :n].T, (rep, 1))
    return m.astype(BF16), jnp.pad(bcol, ((0, 0), (0, LANES - GMLP_GROUPS)))


def kernel(x_prompt, x_sample, cache_nsa_kv, state_win_kv, state_pool, page_table, c_prompt, c_sample, rel_bias, router_w, router_b, w_in, nsa_phi_pe, nsa_phi_w1, nsa_phi_w2, gmlp_norm_g, gmlp_norm_b, gmlp_ws, gmlp_bs, pool_w, pool_scale, proj_a, proj_b, proj_c, w_o, ada_w, ada_b, ln1_g, ln1_b, ln2_g, ln2_b, exp_w_gate, exp_w_up, exp_w_down):
    depth = w_in.shape[0]
    alpha = (2 * depth) ** 0.25
    s = x_prompt.shape[1]
    bsz, t_new = x_sample.shape[0], x_sample.shape[1]
    n_pool = cache_nsa_kv.shape[1]
    n_pages = page_table.shape[1]
    past = n_pages * PAGE
    ts = bsz * t_new
    assert x_prompt.shape[0] == 1 and s % CHUNK_KEYS == 0 and past % PAGE == 0
    assert N_HEADS * t_new <= LANES and t_new <= BLOCK and (past // BLOCK) >= N_SEL
    nb_s = past // BLOCK + 1
    nbp = -(-nb_s // 8) * 8

    n_c = 1 + bsz
    c_all = jnp.pad(jnp.concatenate([c_prompt, c_sample], axis=0), ((0, -n_c % 8), (0, 0)))
    mods_all = _ada_mods(c_all, ada_w, ada_b)

    cache_pages = cache_nsa_kv.reshape(depth * n_pool, PAGE * PAGED_KINDS * N_GROUPS, HEAD_DIM)
    st, wt, ct = _prompt_tiles(rel_bias)
    n_win = state_win_kv.shape[2]
    win_rows = n_win + t_new
    win_pad = n_win + PAGE
    win_state = state_win_kv.reshape(depth * bsz, n_win * 2 * N_GROUPS, HEAD_DIM)
    assert nbp <= SAMPLE_BLOCKS_PAD and n_win == min(WINDOW, past) and n_win % 8 == 0
    cb_s, nearb_s, farb_s, newb_s, winb_s = _sample_tiles(rel_bias, past, t_new, SAMPLE_BLOCKS_PAD, win_rows, win_pad)
    rw = jnp.pad(router_w, ((0, 0), (0, LANES - N_EXPERTS))).astype(BF16)
    rb = jnp.pad(router_b.reshape(N_EXPERTS, 1), ((0, LANES - N_EXPERTS), (0, LANES - 1)))

    xp = x_prompt.reshape(s, D_MODEL)
    xs = x_sample.reshape(ts, D_MODEL)
    outs = {k: [] for k in ("nsa_p", "nsa_s", "win_p", "win_s", "pool_p", "pool_s", "v_s")}

    for l in range(depth):
        w_parts = _in_proj_weights(w_in[l])
        pe = nsa_phi_pe[l]
        w1b = nsa_phi_w1[l].astype(BF16)
        w2b = nsa_phi_w2[l].astype(BF16)
        pab, pbb, pcb, wob = (a[l].astype(BF16) for a in (proj_a, proj_b, proj_c, w_o))
        wg, wu, wd = (a[l].astype(BF16) for a in (exp_w_gate, exp_w_up, exp_w_down))
        pwb = pool_w[l].astype(BF16)
        ps = pool_scale[l].reshape(1, POOL_WIDTH)
        gng = gmlp_norm_g[l].reshape(1, GMLP_WIDTH)
        gnb = gmlp_norm_b[l].reshape(1, GMLP_WIDTH)
        ln1g, ln1b, ln2g, ln2b = (a[l].reshape(1, D_MODEL) for a in (ln1_g, ln1_b, ln2_g, ln2_b))
        mods_p = jnp.broadcast_to(mods_all[l, 0:1], (8, 6 * D_MODEL))
        mods_s = jnp.repeat(mods_all[l, 1:1 + bsz], t_new, axis=0)

        z, zb = _in_proj(xp, mods_p, w_parts, 1024)
        gh = N_GROUPS * HEAD_DIM
        outs["nsa_p"].append(z[:, COL_KV:COL_KV + PAGED_KINDS * gh].reshape(1, s, PAGED_KINDS, N_GROUPS, HEAD_DIM))
        outs["win_p"].append(z[s - min(WINDOW, s):, COL_KV + PAGED_KINDS * gh:COL_KV + KV_KINDS * gh]
                             .reshape(1, min(WINDOW, s), 2, N_GROUPS, HEAD_DIM))
        outs["pool_p"].append(z[s - POOL_BUF:, COL_XC:COL_XC + POOL_WIDTH].reshape(1, POOL_BUF, POOL_WIDTH))

        nb = s // BLOCK
        cmp = _compress_prompt(z, pe, w1b, w2b).reshape(2, N_GROUPS, nb, HEAD_DIM)
        kc = cmp[0].astype(BF16)
        vcT = jnp.transpose(cmp[1], (0, 2, 1)).astype(BF16)
        nch = s // CHUNK_KEYS

        def kv_cols(kind, g):
            c0 = COL_KV + (kind * N_GROUPS + g) * HEAD_DIM
            return zb[:, c0:c0 + HEAD_DIM].reshape(nch, CHUNK_KEYS, HEAD_DIM)

        def k_chunks(kind):
            return jnp.stack([kv_cols(kind, g) for g in range(N_GROUPS)])

        def vT_chunks(kind):
            vt = jnp.stack([jnp.transpose(kv_cols(kind, g), (0, 2, 1)) for g in range(N_GROUPS)])
            ones = jnp.zeros((N_GROUPS, nch, 8, CHUNK_KEYS), BF16).at[:, :, 0].set(1.0)
            return jnp.concatenate([vt, ones], axis=2)

        q = zb[:, COL_Q:COL_Q + NSA_WIDTH].reshape(nb, BLOCK, N_GROUPS, N_REP, HEAD_DIM)
        qT = jnp.transpose(q, (2, 4, 0, 3, 1)).reshape(N_GROUPS, HEAD_DIM, nb * N_REP * BLOCK)
        gn = z[:, COL_GN:COL_GN + 3 * N_HEADS].reshape(nb, BLOCK, 3, N_GROUPS, N_REP)
        gates = jnp.transpose(gn, (3, 0, 2, 4, 1)).reshape(N_GROUPS, nb, 3, N_REP * BLOCK)
        gates = jnp.pad(gates, ((0, 0), (0, 0), (0, 5), (0, 0)))
        ob = _nsa_prompt(qT, kc, vcT, k_chunks(2), vT_chunks(3), k_chunks(4), vT_chunks(5), st, wt, ct, gates, s)

        mm, bcol = _mix_mats(gmlp_ws[l], gmlp_bs[l], CHUNK, 1)
        oa, _ = _gmlp(z, mm, bcol, gng, gnb, CHUNK)
        tp = 512
        cxb = COL_XC // POOL_WIDTH
        oc = _pool(z, z, lambda i: (jnp.maximum(i * (tp // 16) - 1, 0), cxb), pwb, ps, tp, True, 0, tp)
        x1, h2, comb = _merge(xp, z, oa, ob, oc, mods_p, ln1g, ln1b, pab, pbb, pcb, wob, rw, rb, 256, alpha)
        xp = _moe_grouped(x1, comb, wg, wu, wd, mods_p, ln2g, ln2b, alpha)

        zs, zbs = _in_proj(xs, mods_s, w_parts, ts)
        kvs = zs[:, COL_KV:COL_KV + 1536].reshape(bsz, t_new, KV_KINDS, N_GROUPS, HEAD_DIM)
        kvbs = zbs[:, COL_KV:COL_KV + 1536].reshape(bsz, t_new, KV_KINDS, gh)
        outs["nsa_s"].append(kvs[:, :, :PAGED_KINDS])
        win_full = jnp.concatenate([state_win_kv[l], kvs[:, :, PAGED_KINDS:]], axis=1)
        outs["win_s"].append(win_full[:, win_full.shape[1] - min(WINDOW, past + t_new):])
        xcs = zs[:, COL_XC:COL_XC + POOL_WIDTH].reshape(bsz, t_new, POOL_WIDTH)
        ext = jnp.concatenate([state_pool[l], xcs], axis=1)
        outs["pool_s"].append(ext[:, ext.shape[1] - POOL_BUF:])

        pt = page_table + l * n_pool
        new_cmp = kvs[:, :, 0:2].reshape(bsz, t_new, 2 * N_GROUPS * HEAD_DIM)
        knew = jnp.pad(kvbs[:, :, 2], ((0, 0), (0, PAGE - t_new), (0, 0)))
        vnew = jnp.pad(kvbs[:, :, 3], ((0, 0), (0, PAGE - t_new), (0, 0)))
        kwn = jnp.pad(kvbs[:, :, 4], ((0, 0), (0, PAGE - t_new), (0, 0)))
        vwn = jnp.pad(kvbs[:, :, 5], ((0, 0), (0, PAGE - t_new), (0, 0)))
        qs = zbs[:, COL_Q:COL_Q + NSA_WIDTH].reshape(bsz, t_new, N_GROUPS, N_REP, HEAD_DIM)
        qs = jnp.transpose(qs, (0, 2, 4, 3, 1)).reshape(bsz, N_GROUPS, HEAD_DIM, N_REP * t_new)
        n_q = N_REP * t_new
        qT_s = jnp.concatenate(
            [jnp.pad(qs[:, g], ((0, 0), (0, 0), (g * n_q, LANES - (g + 1) * n_q))) for g in range(N_GROUPS)], axis=1)
        gns = zs[:, COL_GN:COL_GN + 3 * N_HEADS].reshape(bsz, t_new, 3, N_HEADS)
        gates_s = jnp.transpose(gns, (0, 3, 1, 2)).reshape(bsz, N_HEADS * t_new, 3)
        gates_s = jnp.pad(gates_s, ((0, 0), (0, LANES - N_HEADS * t_new), (0, LANES - 3)))
        o_s = _sample_nsa(pt, cache_pages, new_cmp, qT_s, knew, vnew, win_state, l, kwn, vwn, pe, w1b, w2b,
                          cb_s, nearb_s, farb_s, newb_s, winb_s, gates_s, nbp)
        o4 = o_s[:, :N_HEADS * t_new].reshape(bsz, N_GROUPS, N_REP, t_new, N_GROUPS, HEAD_DIM)
        o4 = jnp.stack([o4[:, g, :, :, g] for g in range(N_GROUPS)], axis=1)
        ob_s = jnp.transpose(o4, (0, 3, 1, 2, 4)).reshape(ts, NSA_WIDTH).astype(BF16)

        mm_s, bcol_s = _mix_mats(gmlp_ws[l], gmlp_bs[l], ts, bsz)
        oa_s, vn_s = _gmlp(zs, mm_s, bcol_s, gng, gnb, ts)
        outs["v_s"].append(vn_s.reshape(bsz, t_new, GMLP_WIDTH))
        halo = jnp.pad(state_pool[l], ((0, 0), (16 - POOL_BUF, 0), (0, 0))).reshape(bsz * 16, POOL_WIDTH)
        oc_s = _pool(zs, halo, lambda i: (i, 0), pwb, ps, t_new, False, past, 0)
        x1s, h2s, comb_s = _merge(xs, zs, oa_s, ob_s, oc_s, mods_s, ln1g, ln1b, pab, pbb, pcb, wob, rw, rb, ts, alpha)
        xs = _moe(h2s, comb_s, wg, wu, wd, x1s, mods_s, ln2g, ln2b, ts, alpha)

    return (xp.reshape(1, s, D_MODEL), xs.reshape(bsz, t_new, D_MODEL),
            jnp.stack(outs["nsa_p"]), jnp.stack(outs["nsa_s"]), jnp.stack(outs["win_p"]), jnp.stack(outs["win_s"]),
            jnp.stack(outs["pool_p"]), jnp.stack(outs["pool_s"]), jnp.stack(outs["v_s"]))
```

```python
import functools
import math

import jax
import jax.numpy as jnp
import numpy as np
from jax import lax
from jax.experimental import pallas as pl
from jax.experimental.pallas import tpu as pltpu

F32 = jnp.float32
BF16 = jnp.bfloat16

D_MODEL = 2048
HEAD_DIM = 128
N_GROUPS = 2
N_REP = 4
N_HEADS = N_GROUPS * N_REP
NSA_WIDTH = N_HEADS * HEAD_DIM
KV_KINDS = 6
PAGED_KINDS = 4
BLOCK = 64
N_SEL = 16
WINDOW = 512
WIN_BLOCKS = WINDOW // BLOCK
PHI_HIDDEN = 2 * HEAD_DIM
PAGE = 128
GMLP_GROUPS = 4
GMLP_WIDTH = 512
CHUNK = 128
POOL_WINDOWS = (2, 4, 8, 16)
POOL_GROUP_DIM = 128
POOL_WIDTH = 512
POOL_BUF = 15
N_BUCKETS = 32
MAX_DISTANCE = 128
N_EXPERTS = 16
N_EXPERT_GROUPS = 4
EXPERTS_PER_GROUP = 4
D_EXPERT = 512
LN_EPS = 1e-5
NEG = -1e30
FORCED_SCORE = 1e9

COL_Q = 0
COL_KV = 1024
COL_U = 2560
COL_V = 3072
COL_XC = 3584
COL_GM = 4096
COL_GN = 10240
N_IN_PAD = 10752
IN_TN = 512

LANES = 128
VMEM_LIMIT = 56 * 1024 * 1024
CHUNK_BLOCKS = 8
CHUNK_KEYS = CHUNK_BLOCKS * BLOCK
XC_PITCH = BLOCK + 8
SAMPLE_RING_PAGES = 16
SAMPLE_BLOCKS_PAD = 256
MOE_GROUP_TILE = 256
MOE_FINISH_TILE = 256


def _cparams(sem):
    return pltpu.CompilerParams(dimension_semantics=sem, vmem_limit_bytes=VMEM_LIMIT)


def _ln(x):
    mu = jnp.mean(x, axis=-1, keepdims=True)
    xc = x - mu
    var = jnp.mean(xc * xc, axis=-1, keepdims=True)
    return xc * lax.rsqrt(var + LN_EPS)


def _sigmoid(x):
    return 1.0 / (1.0 + jnp.exp(-x))


def _silu(x):
    return x * _sigmoid(x)


def _gelu_tanh(x):
    return 0.5 * x * (1.0 + jnp.tanh(math.sqrt(2.0 / math.pi) * (x + 0.044715 * (x * x * x))))


def _dot(a, b):
    return jnp.dot(a, b, preferred_element_type=F32)


QK_SCALE = HEAD_DIM ** -0.5
EXP2_SCALE = QK_SCALE * math.log2(math.e)


def _qk(k, qT):
    return _dot(k, qT) * QK_SCALE


def _ada_kernel(c_ref, w_ref, b_ref, o_ref):
    c = c_ref[...]
    o_ref[...] = _dot(_silu(c).astype(BF16), w_ref[...].astype(BF16)) + b_ref[...]


def _ada_mods(c_all, ada_w, ada_b):
    depth = ada_w.shape[0]
    rows = c_all.shape[0]
    n = ada_w.shape[2]
    tn = 1024
    return pl.pallas_call(
        _ada_kernel,
        grid=(depth, n // tn),
        in_specs=[pl.BlockSpec((rows, D_MODEL), lambda l, j: (0, 0)),
                  pl.BlockSpec((None, D_MODEL, tn), lambda l, j: (l, 0, j)),
                  pl.BlockSpec((None, 1, tn), lambda l, j: (l, 0, j))],
        out_specs=pl.BlockSpec((None, rows, tn), lambda l, j: (l, 0, j)),
        out_shape=jax.ShapeDtypeStruct((depth, rows, n), F32),
        compiler_params=_cparams(("parallel", "parallel")),
        name="ada_mods",
    )(c_all, ada_w, ada_b.reshape(depth, 1, n))


def _mod_rows(ref, tm):
    return ref[...] if ref.shape[0] == tm else ref[0:1, :]


N_QKV_TILES = COL_U // IN_TN
N_MAIN_TILES = COL_GN // IN_TN


N_Q_TILES = COL_KV // IN_TN
KV_ROWS = PAGED_KINDS * N_GROUPS
COLS_PER_TILE = IN_TN // HEAD_DIM


def _inproj_kernel(x_ref, sh_ref, sc_ref, wa_ref, wb_ref, wc_ref, z_ref, zb_ref, kv_ref, h_scr):
    tm = x_ref.shape[0]
    j = pl.program_id(1)

    @pl.when(j == 0)
    def _():
        h = _ln(x_ref[...]) * (1.0 + _mod_rows(sc_ref, tm)) + _mod_rows(sh_ref, tm)
        h_scr[...] = h.astype(BF16)

    @pl.when(j < N_QKV_TILES)
    def _():
        z = _dot(h_scr[...], wa_ref[...])
        z_ref[...] = z
        zb_ref[...] = z.astype(BF16)
        for jt in range(KV_ROWS // COLS_PER_TILE):
            @pl.when(j == N_Q_TILES + jt)
            def _():
                for c in range(COLS_PER_TILE):
                    kv_ref[pl.ds(jt * COLS_PER_TILE + c, tm, stride=KV_ROWS), :] = \
                        z_ref[:, c * HEAD_DIM:(c + 1) * HEAD_DIM]

    @pl.when((j >= N_QKV_TILES) & (j < N_MAIN_TILES))
    def _():
        z_ref[...] = _dot(h_scr[...], wb_ref[...])

    @pl.when(j == N_MAIN_TILES)
    def _():
        z_ref[...] = _dot(h_scr[...], wc_ref[...])


def _mod_spec(mods, tm, col):
    rows = mods.shape[0]
    if rows == 8:
        return pl.BlockSpec((8, D_MODEL), lambda i, *_: (0, col))
    return pl.BlockSpec((tm, D_MODEL), lambda i, *_: (i, col))


def _in_proj_weights(w):
    n_gate = 3 * N_HEADS
    wa = w[:, :COL_U].astype(BF16)
    wb = w[:, COL_U + n_gate:].astype(BF16)
    wc = jnp.pad(w[:, COL_U:COL_U + n_gate], ((0, 0), (0, IN_TN - n_gate))).astype(BF16)
    return wa, wb, wc


def _in_proj(x, mods, weights, tm):
    t = x.shape[0]
    wa, wb, wc = weights
    n_b = N_MAIN_TILES - N_QKV_TILES
    return pl.pallas_call(
        _inproj_kernel,
        grid=(t // tm, N_IN_PAD // IN_TN),
        in_specs=[pl.BlockSpec((tm, D_MODEL), lambda i, j: (i, 0)),
                  _mod_spec(mods, tm, 0), _mod_spec(mods, tm, 1),
                  pl.BlockSpec((D_MODEL, IN_TN), lambda i, j: (0, jnp.minimum(j, N_QKV_TILES - 1))),
                  pl.BlockSpec((D_MODEL, IN_TN), lambda i, j: (0, jnp.clip(j - N_QKV_TILES, 0, n_b - 1))),
                  pl.BlockSpec((D_MODEL, IN_TN), lambda i, j: (0, 0))],
        out_specs=[pl.BlockSpec((tm, IN_TN), lambda i, j: (i, j)),
                   pl.BlockSpec((tm, IN_TN), lambda i, j: (i, jnp.minimum(j, N_QKV_TILES - 1))),
                   pl.BlockSpec((tm * KV_ROWS, HEAD_DIM), lambda i, j: (i, 0))],
        out_shape=[jax.ShapeDtypeStruct((t, N_IN_PAD), F32),
                   jax.ShapeDtypeStruct((t, COL_U), BF16),
                   jax.ShapeDtypeStruct((t * KV_ROWS, HEAD_DIM), F32)],
        scratch_shapes=[pltpu.VMEM((tm, D_MODEL), BF16)],
        compiler_params=_cparams(("parallel", "arbitrary")),
        name="in_proj",
    )(x, mods, mods, wa, wb, wc)


def _compress_pair(x0_ref, x1_ref, nb, pe_ref, w1_ref, w2_ref):
    def step(p, acc):
        pe = pe_ref[pl.ds(p, 1), :]
        a0 = x0_ref[pl.ds(p, nb, stride=BLOCK), :] + pe
        a1 = x1_ref[pl.ds(p, nb, stride=BLOCK), :] + pe
        a = jnp.concatenate([a0, a1], axis=0).astype(BF16)
        w = w1_ref[pl.ds(pl.multiple_of(p * HEAD_DIM, HEAD_DIM), HEAD_DIM), :]
        return acc + _dot(a, w)

    acc = lax.fori_loop(0, BLOCK, step, jnp.zeros((2 * nb, PHI_HIDDEN), F32))
    return _dot(_gelu_tanh(acc).astype(BF16), w2_ref[...])


def _compress_prompt_kernel(x0_ref, x1_ref, pe_ref, w1_ref, w2_ref, o_ref, *, nb):
    o_ref[...] = _compress_pair(x0_ref, x1_ref, nb, pe_ref, w1_ref, w2_ref)


def _compress_prompt(z, pe, w1b, w2b):
    s = z.shape[0]
    nb = s // BLOCK
    cb = COL_KV // HEAD_DIM
    return pl.pallas_call(
        functools.partial(_compress_prompt_kernel, nb=nb),
        grid=(2,),
        in_specs=[pl.BlockSpec((s, HEAD_DIM), lambda k: (0, cb + 2 * k)),
                  pl.BlockSpec((s, HEAD_DIM), lambda k: (0, cb + 2 * k + 1)),
                  pl.BlockSpec((None, BLOCK, HEAD_DIM), lambda k: (k, 0, 0)),
                  pl.BlockSpec((None, BLOCK * HEAD_DIM, PHI_HIDDEN), lambda k: (k, 0, 0)),
                  pl.BlockSpec((None, PHI_HIDDEN, HEAD_DIM), lambda k: (k, 0, 0))],
        out_specs=pl.BlockSpec((None, 2 * nb, HEAD_DIM), lambda k: (k, 0, 0)),
        out_shape=jax.ShapeDtypeStruct((2, 2 * nb, HEAD_DIM), F32),
        compiler_params=_cparams(("parallel",)),
        name="compress_prompt",
    )(z, z, pe, w1b, w2b)


N_FORCED = 3


def _select_blocks(score, forced, n_io):
    def body(_, carry):
        work, sel = carry
        mval = jnp.max(work, axis=0, keepdims=True)
        cand = jnp.where(work == mval, n_io, jnp.int32(1 << 30))
        idx = jnp.min(cand, axis=0, keepdims=True)
        pick = n_io == idx
        return jnp.where(pick, -jnp.inf, work), jnp.where(pick, 1.0, sel)

    init = (jnp.where(forced, -jnp.inf, score), jnp.where(forced, 1.0, 0.0))
    _, sel = lax.fori_loop(0, N_SEL - N_FORCED, body, init)
    return sel


def _nsa_prompt_kernel(qT_ref, kc_ref, vcT_ref, ksel_ref, vselT_ref, kwin_ref, vwinT_ref,
                       st_ref, wt_ref, ct_ref, gate_ref, o_ref, selneg_scr, selfar_scr, s_a, s_b, p_a, p_b, *, nb):
    i = pl.program_id(1)
    qT = qT_ref[...]
    nl = N_REP * BLOCK

    n_io = lax.broadcasted_iota(jnp.int32, (nb, nl), 0)
    qq = lax.broadcasted_iota(jnp.int32, (nb, nl), 1) % BLOCK
    m = i - n_io
    ct = ct_ref[...]
    bias = jnp.where(m == 0, ct[0:1], jnp.where(m == 1, ct[1:2], jnp.where(m == 2, ct[2:3], ct[3:4])))
    valid = (m >= 1) | ((m == 0) & (qq == BLOCK - 1))
    s = jnp.where(valid, _qk(kc_ref[...], qT) + bias, NEG)
    mx = jnp.max(s, axis=0, keepdims=True)
    p = jnp.where(valid, jnp.exp(s - mx), 0.0)
    l = jnp.sum(p, axis=0, keepdims=True)
    pn = p / jnp.where(l > 0.0, l, 1.0)
    o_cmp = _dot(vcT_ref[...], pn.astype(BF16))

    half = pn[:, :LANES] + pn[:, LANES:]
    imp = half + pltpu.roll(half, BLOCK, 1)
    n_io1 = n_io[:, :LANES]
    forced = (n_io1 == 0) | (n_io1 == i) | (n_io1 == i - 1)
    score = jnp.where(forced, FORCED_SCORE, imp)
    score = jnp.where(n_io1 <= i, score, -1.0)
    sel = _select_blocks(score, forced, n_io1)
    neg1 = jnp.where((sel > 0.5) & (score >= 0.0), 0.0, NEG)
    neg2 = jnp.concatenate([neg1, neg1], axis=1)
    selneg_scr[...] = neg2
    far_row = st_ref[st_ref.shape[0] - 2][0:1, :]
    selfar_scr[...] = jnp.where(neg2 == 0.0, far_row, NEG)

    c_i = i // CHUNK_BLOCKS

    def branch(k_ref, vT_ref, tile_ref, n_tiles, use_sel, n_chunks):
        def logits(c, dst):
            dst[...] = _dot(k_ref[jnp.maximum(c, 0)], qT)

        def pv(c, p_ref):
            return _dot(vT_ref[jnp.clip(c, 0, vT_ref.shape[0] - 1)], p_ref[...])

        def consume(c, src, m_run, near):
            live = c >= 0
            cz = jnp.maximum(c, 0)
            parts = []
            for b in range(CHUNK_BLOCKS):
                n = cz * CHUNK_BLOCKS + b
                sb = src[b * BLOCK:(b + 1) * BLOCK, :]
                if near:
                    mm = i - n
                    if use_sel:
                        tidx = jnp.where(mm < 0, n_tiles - 1, jnp.minimum(mm, n_tiles - 2))
                    else:
                        tidx = jnp.where((mm < 0) | (mm > n_tiles - 2), n_tiles - 1, mm)
                    sb = sb + tile_ref[jnp.where(live, tidx, n_tiles - 1)]
                    if use_sel:
                        sb = sb + selneg_scr[pl.ds(n, 1), :]
                else:
                    sb = sb + jnp.where(live, selfar_scr[pl.ds(n, 1), :], NEG)
                parts.append(sb)
            s2 = jnp.concatenate(parts, axis=0)
            m_new = jnp.maximum(m_run, jnp.max(s2, axis=0, keepdims=True))
            alpha = jnp.exp2((m_run - m_new) * EXP2_SCALE)
            pp = jnp.exp2((s2 - m_new) * EXP2_SCALE)
            return m_new, alpha, pp.astype(BF16)

        def pair(t, carry, near):
            m_run, acc = carry
            c0 = c_i - 2 * t
            logits(c0 - 1, s_b)
            r_prev = pv(c0 + 1, p_b)
            m_run, alpha, pp = consume(c0, s_a, m_run, near)
            p_a[...] = pp
            acc = alpha * (acc + r_prev)
            logits(c0 - 2, s_a)
            r_cur = pv(c0, p_a)
            m_run, alpha, pp = consume(c0 - 1, s_b, m_run, near)
            p_b[...] = pp
            return m_run, alpha * (acc + r_cur)

        logits(c_i, s_a)
        p_b[...] = jnp.zeros(p_b.shape, BF16)
        carry = (jnp.full((1, nl), NEG, F32), jnp.zeros((vT_ref.shape[1], nl), F32))
        carry = pair(0, carry, True)
        n_pairs = (n_chunks + 1) // 2 if use_sel else 1
        if use_sel:
            carry = lax.fori_loop(1, n_pairs, lambda t, cr: pair(t, cr, False), carry)
        acc = carry[1] + pv(c_i - 2 * n_pairs + 1, p_b)
        return acc[:HEAD_DIM] / acc[HEAD_DIM:HEAD_DIM + 1]

    o_sel = branch(ksel_ref, vselT_ref, st_ref, st_ref.shape[0], True, c_i + 1)
    o_win = branch(kwin_ref, vwinT_ref, wt_ref, wt_ref.shape[0], False, jnp.minimum(c_i + 1, 2))

    gates = _sigmoid(gate_ref[...])
    oT = gates[0:1] * o_cmp + gates[1:2] * o_sel + gates[2:3] * o_win
    o = oT.T
    for r in range(N_REP):
        o_ref[:, r * HEAD_DIM:(r + 1) * HEAD_DIM] = o[r * BLOCK:(r + 1) * BLOCK].astype(o_ref.dtype)


def _nsa_prompt(qT, kc, vcT, ksel, vselT, kwin, vwinT, st, wt, ct, gates, s):
    nb = s // BLOCK
    nq = s // BLOCK
    nl = N_REP * BLOCK
    nch = s // CHUNK_KEYS

    def whole(arr):
        shp = arr.shape[1:]
        return pl.BlockSpec((None,) + shp, lambda g, i: (g,) + (0,) * len(shp))

    return pl.pallas_call(
        functools.partial(_nsa_prompt_kernel, nb=nb),
        grid=(N_GROUPS, nq),
        in_specs=[pl.BlockSpec((None, HEAD_DIM, nl), lambda g, i: (g, 0, i)),
                  whole(kc), whole(vcT), whole(ksel), whole(vselT), whole(kwin), whole(vwinT),
                  whole(st), whole(wt), whole(ct),
                  pl.BlockSpec((None, None, 8, nl), lambda g, i: (g, i, 0, 0))],
        out_specs=pl.BlockSpec((BLOCK, N_REP * HEAD_DIM), lambda g, i: (i, g)),
        out_shape=jax.ShapeDtypeStruct((s, NSA_WIDTH), BF16),
        scratch_shapes=[pltpu.VMEM((nb, nl), F32),
                        pltpu.VMEM((nb, nl), F32),
                        pltpu.VMEM((CHUNK_KEYS, nl), F32),
                        pltpu.VMEM((CHUNK_KEYS, nl), F32),
                        pltpu.VMEM((CHUNK_KEYS, nl), BF16),
                        pltpu.VMEM((CHUNK_KEYS, nl), BF16)],
        compiler_params=_cparams(("parallel", "arbitrary")),
        name="nsa_prompt",
    )(qT, kc, vcT, ksel, vselT, kwin, vwinT, st, wt, ct, gates)


def _sample_nsa_kernel(pt_ref, cache_ref, newc_ref, qT_ref, knew_ref, vnew_ref, wst_ref, kwn_ref, vwn_ref,
                       pe_ref, w1_ref, w2_ref, cb_ref, nearb_ref, farb_ref, newb_ref, winb_ref, gate_ref, o_ref,
                       ring, xc, kcs, vsel, s_scr, selneg_scr, sems, *, n_pages, t_new, nbp, n_ring):
    b = pl.program_id(0)
    total = pl.num_programs(0) * n_pages
    past = n_pages * PAGE
    cur = past // BLOCK
    kinds_per_row = PAGED_KINDS * N_GROUPS
    qT = qT_ref[...]

    def page_dma(t, slot):
        return pltpu.make_async_copy(cache_ref.at[pt_ref[t // n_pages, t % n_pages]], ring.at[slot], sems.at[slot])

    @pl.when(b == 0)
    def _():
        for t in range(n_ring):
            page_dma(t, t).start()

    far = farb_ref[0:1, :]

    def page_step(j, c):
        t = b * n_pages + j
        slot = t % n_ring
        page_dma(t, slot).wait()
        buf = ring.at[slot]

        def rows(kg):
            return buf[pl.ds(kg, PAGE, stride=kinds_per_row), :]

        r0 = pl.multiple_of(j * PAGE, PAGE)
        base = pl.multiple_of(j * (PAGE // BLOCK) * XC_PITCH, 8)
        for kg in range(2 * N_GROUPS):
            x = rows(kg)
            for h in range(PAGE // BLOCK):
                xc[kg, pl.ds(base + h * XC_PITCH, BLOCK), :] = x[h * BLOCK:(h + 1) * BLOCK]
        k = jnp.concatenate([rows(4), rows(5)], axis=1).astype(BF16)
        s_scr[pl.ds(r0, PAGE), :] = _qk(k, qT) + jnp.where(j == n_pages - 1, nearb_ref[...], far)
        vsel[pl.ds(r0, PAGE), :] = jnp.concatenate([rows(6), rows(7)], axis=1).astype(BF16)

        @pl.when(t + n_ring < total)
        def _():
            page_dma(t + n_ring, slot).start()

        return c

    lax.fori_loop(0, n_pages, page_step, 0)

    tail = (nbp - cur) * XC_PITCH
    for kg in range(2 * N_GROUPS):
        xc[kg, pl.ds(cur * XC_PITCH, tail), :] = jnp.zeros((tail, HEAD_DIM), F32)
        xc[kg, pl.ds(cur * XC_PITCH, t_new), :] = newc_ref[:, kg * HEAD_DIM:(kg + 1) * HEAD_DIM]
    s_scr[pl.ds(past, PAGE), :] = _qk(knew_ref[...], qT) + newb_ref[...]
    vsel[pl.ds(past, PAGE), :] = vnew_ref[...]

    for kind in range(2):
        def step(pp, acc):
            p = 2 * pp
            pe0 = pe_ref[kind, pl.ds(p, 1), :]
            pe1 = pe_ref[kind, pl.ds(p + 1, 1), :]
            parts = []
            for g in range(N_GROUPS):
                x0 = xc[2 * kind + g, pl.ds(p, nbp, stride=XC_PITCH), :] + pe0
                x1 = xc[2 * kind + g, pl.ds(p + 1, nbp, stride=XC_PITCH), :] + pe1
                parts.append(jnp.concatenate([x0, x1], axis=1))
            a = jnp.concatenate(parts, axis=0).astype(BF16)
            w = w1_ref[kind, pl.ds(pl.multiple_of(p * HEAD_DIM, 2 * HEAD_DIM), 2 * HEAD_DIM), :]
            return acc + _dot(a, w)

        acc = lax.fori_loop(0, BLOCK // 2, step, jnp.zeros((N_GROUPS * nbp, PHI_HIDDEN), F32), unroll=2)
        out = _dot(_gelu_tanh(acc).astype(BF16), w2_ref[kind])
        kcs[kind, nbp:, :] = jnp.zeros((kcs.shape[1] - nbp, N_GROUPS * HEAD_DIM), F32)
        for g in range(N_GROUPS):
            kcs[kind, 0:nbp, g * HEAD_DIM:(g + 1) * HEAD_DIM] = out[g * nbp:(g + 1) * nbp]

    def softmax_rows(s):
        mx = jnp.max(s, axis=0, keepdims=True)
        p = jnp.where(s > 0.5 * NEG, jnp.exp(s - mx), 0.0)
        l = jnp.sum(p, axis=0, keepdims=True)
        return p / jnp.where(l > 0.0, l, 1.0)

    pn = softmax_rows(_qk(kcs[0].astype(BF16), qT) + cb_ref[...])
    o_cmp = _dot(pn.T.astype(BF16), kcs[1].astype(BF16))

    tot = pn + pltpu.roll(pn, t_new, 1) + pltpu.roll(pn, 2 * t_new, 1) + pltpu.roll(pn, 3 * t_new, 1)
    lane = lax.broadcasted_iota(jnp.int32, pn.shape, 1)
    top = jnp.where((lane // t_new) % N_REP == N_REP - 1, tot, 0.0)
    imp = top + pltpu.roll(top, LANES - t_new, 1) + pltpu.roll(top, LANES - 2 * t_new, 1) \
        + pltpu.roll(top, LANES - 3 * t_new, 1)
    n_io = lax.broadcasted_iota(jnp.int32, pn.shape, 0)
    forced = (n_io == 0) | (n_io == cur) | (n_io == cur - 1)
    score = jnp.where(forced, FORCED_SCORE, imp)
    score = jnp.where(n_io <= cur, score, -jnp.inf)
    sel = _select_blocks(score, forced, n_io)
    selneg_scr[...] = jnp.where(sel > 0.5, 0.0, NEG)

    n_st = wst_ref.shape[0] // (2 * N_GROUPS)

    def st_rows(j):
        return wst_ref[pl.ds(j, n_st, stride=2 * N_GROUPS), :]

    kw = jnp.concatenate([st_rows(0), st_rows(1)], axis=1).astype(BF16)
    vw = jnp.concatenate([st_rows(2), st_rows(3)], axis=1).astype(BF16)
    s_w = jnp.concatenate([_qk(kw, qT) + winb_ref[0:n_st, :], _qk(kwn_ref[...], qT) + winb_ref[n_st:, :]], axis=0)
    pwt = softmax_rows(s_w).T.astype(BF16)
    o_win = _dot(pwt[:, :n_st], vw) + _dot(pwt[:, n_st:], vwn_ref[...])

    blk = PAGE // BLOCK

    def mask_step(c, mx):
        r0 = pl.multiple_of(c * PAGE, PAGE)
        neg = jnp.concatenate(
            [jnp.broadcast_to(selneg_scr[pl.ds(c * blk + h, 1), :], (BLOCK, LANES)) for h in range(blk)], axis=0)
        s = s_scr[pl.ds(r0, PAGE), :] + neg
        s_scr[pl.ds(r0, PAGE), :] = s
        return jnp.maximum(mx, jnp.max(s, axis=0, keepdims=True))

    mx = lax.fori_loop(0, n_pages + 1, mask_step, jnp.full((1, LANES), NEG, F32))

    def pv_rows(r0, n_rows, carry):
        l_run, acc = carry
        pp = jnp.exp(s_scr[pl.ds(r0, n_rows), :] - mx)
        acc = acc + _dot(pp.T.astype(BF16), vsel[pl.ds(r0, n_rows), :])
        return l_run + jnp.sum(pp, axis=0, keepdims=True), acc

    pv_pages = 4 if n_pages % 4 == 0 else 1
    pv_keys = pv_pages * PAGE
    carry = lax.fori_loop(0, n_pages // pv_pages,
                          lambda c, carry: pv_rows(pl.multiple_of(c * pv_keys, pv_keys), pv_keys, carry),
                          (jnp.zeros((1, LANES), F32), jnp.zeros((LANES, N_GROUPS * HEAD_DIM), F32)))
    l_fin, acc = pv_rows(past, PAGE, carry)
    l_col = jnp.broadcast_to(l_fin, (LANES, LANES)).T[:, 0:1]
    o_sel = acc / l_col

    gates = _sigmoid(gate_ref[...])
    o_ref[...] = gates[:, 0:1] * o_cmp + gates[:, 1:2] * o_sel + gates[:, 2:3] * o_win


def _sample_nsa(pt, cache_pages, new_cmp, qT, knew, vnew, win_state, layer, kwn, vwn, pe, w1b, w2b,
                cb, nearb, farb, newb, winb, gates, nbp):
    bsz, n_pages = pt.shape
    t_new = new_cmp.shape[1]
    gh = N_GROUPS * HEAD_DIM
    n_keys = n_pages * PAGE + PAGE
    n_ring = min(SAMPLE_RING_PAGES, n_pages)

    def per_b(arr):
        shp = arr.shape[1:]
        return pl.BlockSpec((None,) + shp, lambda b, pt: (b,) + (0,) * len(shp))

    def const(arr, **kw):
        return pl.BlockSpec(arr.shape, lambda b, pt: (0,) * arr.ndim, **kw)

    grid_spec = pltpu.PrefetchScalarGridSpec(
        num_scalar_prefetch=1,
        grid=(bsz,),
        in_specs=[pl.BlockSpec(memory_space=pl.ANY),
                  per_b(new_cmp), per_b(qT), per_b(knew), per_b(vnew),
                  pl.BlockSpec((None,) + win_state.shape[1:], lambda b, pt: (layer * bsz + b, 0, 0)),
                  per_b(kwn), per_b(vwn),
                  const(pe), const(w1b, pipeline_mode=pl.Buffered(1)), const(w2b),
                  const(cb), const(nearb), const(farb), const(newb), const(winb), per_b(gates)],
        out_specs=pl.BlockSpec((None, LANES, gh), lambda b, pt: (b, 0, 0)),
        scratch_shapes=[pltpu.VMEM((n_ring,) + cache_pages.shape[1:], F32),
                        pltpu.VMEM((2 * N_GROUPS, nbp * XC_PITCH, HEAD_DIM), F32),
                        pltpu.VMEM((2, cb.shape[0], gh), F32),
                        pltpu.VMEM((n_keys, gh), BF16),
                        pltpu.VMEM((n_keys, LANES), F32),
                        pltpu.VMEM((cb.shape[0], LANES), F32),
                        pltpu.SemaphoreType.DMA((n_ring,))],
    )
    return pl.pallas_call(
        functools.partial(_sample_nsa_kernel, n_pages=n_pages, t_new=t_new, nbp=nbp, n_ring=n_ring),
        grid_spec=grid_spec,
        out_shape=jax.ShapeDtypeStruct((bsz, LANES, gh), F32),
        compiler_params=_cparams(("arbitrary",)),
        name="sample_nsa",
    )(pt, cache_pages, new_cmp, qT, knew, vnew, win_state, kwn, vwn, pe, w1b, w2b,
      cb, nearb, farb, newb, winb, gates)


def _gmlp_kernel(u_ref, v_ref, m_ref, bcol_ref, g_ref, b_ref, oa_ref, vn_ref):
    vn = _ln(v_ref[...]) * g_ref[...] + b_ref[...]
    vn_ref[...] = vn
    vb = vn.astype(BF16)
    u = u_ref[...]
    gd = GMLP_WIDTH // GMLP_GROUPS
    for g in range(GMLP_GROUPS):
        mixed = _dot(m_ref[g], vb[:, g * gd:(g + 1) * gd]) + bcol_ref[:, g:g + 1]
        oa_ref[:, g * gd:(g + 1) * gd] = (u[:, g * gd:(g + 1) * gd] * mixed).astype(oa_ref.dtype)


def _gmlp(z, mix_m, bcol, gn_g, gn_b, tc):
    t = z.shape[0]
    return pl.pallas_call(
        _gmlp_kernel,
        grid=(t // tc,),
        in_specs=[pl.BlockSpec((tc, GMLP_WIDTH), lambda i: (i, COL_U // GMLP_WIDTH)),
                  pl.BlockSpec((tc, GMLP_WIDTH), lambda i: (i, COL_V // GMLP_WIDTH)),
                  pl.BlockSpec((GMLP_GROUPS, tc, tc), lambda i: (0, 0, 0)),
                  pl.BlockSpec((tc, LANES), lambda i: (0, 0)),
                  pl.BlockSpec((1, GMLP_WIDTH), lambda i: (0, 0)),
                  pl.BlockSpec((1, GMLP_WIDTH), lambda i: (0, 0))],
        out_specs=[pl.BlockSpec((tc, GMLP_WIDTH), lambda i: (i, 0)),
                   pl.BlockSpec((tc, GMLP_WIDTH), lambda i: (i, 0))],
        out_shape=[jax.ShapeDtypeStruct((t, GMLP_WIDTH), BF16),
                   jax.ShapeDtypeStruct((t, GMLP_WIDTH), F32)],
        compiler_params=_cparams(("parallel",)),
        name="gmlp",
    )(z, z, mix_m, bcol, gn_g, gn_b)


def _pool_kernel(x_ref, halo_ref, pw_ref, ps_ref, o_ref, ext_scr, *, tm, zero_first, pos0, pos_step):
    i = pl.program_id(0)
    halo = halo_ref[...]
    if zero_first:
        halo = jnp.where(i == 0, 0.0, halo)
    ext_scr[0:16, :] = halo
    x = x_ref[...]
    ext_scr[16:16 + tm, :] = x
    pos = pos0 + i * pos_step + lax.broadcasted_iota(jnp.int32, (tm, POOL_GROUP_DIM), 0)
    for gi, w in enumerate(POOL_WINDOWS):
        c0 = gi * POOL_GROUP_DIM
        acc = ext_scr[16:16 + tm, c0:c0 + POOL_GROUP_DIM]
        for k in range(1, w):
            acc = acc + ext_scr[16 - k:16 - k + tm, c0:c0 + POOL_GROUP_DIM]
        count = jnp.minimum(pos + 1, w).astype(F32)
        d = acc / count - x[:, c0:c0 + POOL_GROUP_DIM]
        y = _dot(d.astype(BF16), pw_ref[gi]) * ps_ref[:, c0:c0 + POOL_GROUP_DIM]
        o_ref[:, c0:c0 + POOL_GROUP_DIM] = y.astype(o_ref.dtype)


def _pool(z, halo_src, halo_map, pwb, ps, tm, zero_first, pos0, pos_step):
    t = z.shape[0]
    return pl.pallas_call(
        functools.partial(_pool_kernel, tm=tm, zero_first=zero_first, pos0=pos0, pos_step=pos_step),
        grid=(t // tm,),
        in_specs=[pl.BlockSpec((tm, POOL_WIDTH), lambda i: (i, COL_XC // POOL_WIDTH)),
                  pl.BlockSpec((16, POOL_WIDTH), halo_map),
                  pl.BlockSpec((len(POOL_WINDOWS), POOL_GROUP_DIM, POOL_GROUP_DIM), lambda i: (0, 0, 0)),
                  pl.BlockSpec((1, POOL_WIDTH), lambda i: (0, 0))],
        out_specs=pl.BlockSpec((tm, POOL_WIDTH), lambda i: (i, 0)),
        out_shape=jax.ShapeDtypeStruct((t, POOL_WIDTH), BF16),
        scratch_shapes=[pltpu.VMEM((16 + tm, POOL_WIDTH), F32)],
        compiler_params=_cparams(("parallel",)),
        name="pool",
    )(z, halo_src, pwb, ps)


def _route(logits_t, rb_col):
    aff = _sigmoid(logits_t)
    biased = aff + rb_col
    rows = [biased[e:e + 1] for e in range(N_EXPERTS)]
    top2 = []
    gscore = []
    for g in range(N_EXPERT_GROUPS):
        grp = rows[g * EXPERTS_PER_GROUP:(g + 1) * EXPERTS_PER_GROUP]
        gs = None
        for a in range(EXPERTS_PER_GROUP):
            rank = None
            for c in range(EXPERTS_PER_GROUP):
                if c == a:
                    continue
                ahead = (grp[c] >= grp[a]) if c < a else (grp[c] > grp[a])
                rank = ahead.astype(F32) if rank is None else rank + ahead.astype(F32)
            in2 = rank < 1.5
            top2.append(in2)
            contrib = jnp.where(in2, grp[a], 0.0)
            gs = contrib if gs is None else gs + contrib
        gscore.append(gs)
    out = []
    for g in range(N_EXPERT_GROUPS):
        win = None
        for c in range(N_EXPERT_GROUPS):
            if c == g:
                continue
            ok = (gscore[g] > gscore[c]) if c < g else (gscore[g] >= gscore[c])
            win = ok if win is None else (win & ok)
        for a in range(EXPERTS_PER_GROUP):
            e = g * EXPERTS_PER_GROUP + a
            out.append(jnp.where(win & top2[e], aff[e:e + 1], 0.0))
    selw = jnp.concatenate(out, axis=0)
    return selw / jnp.sum(selw, axis=0, keepdims=True)


def _merge_kernel(x_ref, oa_ref, ob_ref, oc_ref, g0_ref, g1_ref, g2_ref, gt1_ref, sh2_ref, sc2_ref,
                  ln1g_ref, ln1b_ref, pa_ref, pb_ref, pc_ref, wo_ref, rw_ref, rb_ref,
                  x1_ref, h2_ref, comb_ref, *, alpha):
    tm = x_ref.shape[0]

    merged = _sigmoid(g0_ref[...]) * _dot(oa_ref[...], pa_ref[...])
    merged = merged + _sigmoid(g1_ref[...]) * _dot(ob_ref[...], pb_ref[...])
    merged = merged + _sigmoid(g2_ref[...]) * _dot(oc_ref[...], pc_ref[...])
    mix = _dot(merged.astype(BF16), wo_ref[...])
    x1 = _ln(alpha * x_ref[...] + (1.0 + _mod_rows(gt1_ref, tm)) * mix) * ln1g_ref[...] + ln1b_ref[...]
    x1_ref[...] = x1
    h2 = _ln(x1) * (1.0 + _mod_rows(sc2_ref, tm)) + _mod_rows(sh2_ref, tm)
    h2_ref[...] = h2.astype(BF16)
    logits = _dot(h2.astype(BF16), rw_ref[...])
    comb_t = _route(logits.T[:N_EXPERTS], rb_ref[...][:N_EXPERTS, 0:1])
    comb_full = jnp.concatenate([comb_t, jnp.zeros((LANES - N_EXPERTS, tm), F32)], axis=0)
    comb_ref[...] = comb_full.T


def _merge(x, z, oa, ob, oc, mods, ln1g, ln1b, pa, pb, pc, wo, rw, rb, tm, alpha):
    t = x.shape[0]

    def const(arr):
        return pl.BlockSpec(arr.shape, lambda i: (0,) * arr.ndim, pipeline_mode=pl.Buffered(1))

    gcol = COL_GM // D_MODEL
    return pl.pallas_call(
        functools.partial(_merge_kernel, alpha=alpha),
        grid=(t // tm,),
        in_specs=[pl.BlockSpec((tm, D_MODEL), lambda i: (i, 0)),
                  pl.BlockSpec((tm, GMLP_WIDTH), lambda i: (i, 0)),
                  pl.BlockSpec((tm, NSA_WIDTH), lambda i: (i, 0)),
                  pl.BlockSpec((tm, POOL_WIDTH), lambda i: (i, 0)),
                  pl.BlockSpec((tm, D_MODEL), lambda i: (i, gcol)),
                  pl.BlockSpec((tm, D_MODEL), lambda i: (i, gcol + 1)),
                  pl.BlockSpec((tm, D_MODEL), lambda i: (i, gcol + 2)),
                  _mod_spec(mods, tm, 2), _mod_spec(mods, tm, 3), _mod_spec(mods, tm, 4),
                  const(ln1g), const(ln1b), const(pa), const(pb), const(pc), const(wo), const(rw), const(rb)],
        out_specs=[pl.BlockSpec((tm, D_MODEL), lambda i: (i, 0)),
                   pl.BlockSpec((tm, D_MODEL), lambda i: (i, 0)),
                   pl.BlockSpec((tm, LANES), lambda i: (i, 0))],
        out_shape=[jax.ShapeDtypeStruct((t, D_MODEL), F32),
                   jax.ShapeDtypeStruct((t, D_MODEL), BF16),
                   jax.ShapeDtypeStruct((t, LANES), F32)],
        compiler_params=_cparams(("parallel",)),
        name="merge",
    )(x, oa, ob, oc, z, z, z, mods, mods, mods, ln1g, ln1b, pa, pb, pc, wo, rw, rb)


def _moe_kernel(h_ref, comb_ref, wg_ref, wu_ref, wd_ref, x1_ref, gt2_ref, g_ref, b_ref, o_ref, acc_scr, *, alpha):
    e = pl.program_id(1)
    tm = h_ref.shape[0]

    @pl.when(e == 0)
    def _():
        acc_scr[...] = jnp.zeros_like(acc_scr)

    h = h_ref[...]
    comb = comb_ref[...]
    lane = lax.broadcasted_iota(jnp.int32, comb.shape, 1)
    w_col = jnp.sum(jnp.where(lane == e, comb, 0.0), axis=1, keepdims=True)
    act = _silu(_dot(h, wg_ref[...])) * _dot(h, wu_ref[...])
    acc_scr[...] += _dot((act * w_col).astype(BF16), wd_ref[...])

    @pl.when(e == N_EXPERTS - 1)
    def _():
        o_ref[...] = _ln(alpha * x1_ref[...] + (1.0 + _mod_rows(gt2_ref, tm)) * acc_scr[...]) * g_ref[...] + b_ref[...]


def _moe(h2, comb, wg, wu, wd, x1, mods, ln2g, ln2b, tm, alpha):
    t = h2.shape[0]
    return pl.pallas_call(
        functools.partial(_moe_kernel, alpha=alpha),
        grid=(t // tm, N_EXPERTS),
        in_specs=[pl.BlockSpec((tm, D_MODEL), lambda i, e: (i, 0)),
                  pl.BlockSpec((tm, LANES), lambda i, e: (i, 0)),
                  pl.BlockSpec((None, D_MODEL, D_EXPERT), lambda i, e: (e, 0, 0)),
                  pl.BlockSpec((None, D_MODEL, D_EXPERT), lambda i, e: (e, 0, 0)),
                  pl.BlockSpec((None, D_EXPERT, D_MODEL), lambda i, e: (e, 0, 0)),
                  pl.BlockSpec((tm, D_MODEL), lambda i, e: (i, 0)),
                  _mod_spec(mods, tm, 5),
                  pl.BlockSpec((1, D_MODEL), lambda i, e: (0, 0)),
                  pl.BlockSpec((1, D_MODEL), lambda i, e: (0, 0))],
        out_specs=pl.BlockSpec((tm, D_MODEL), lambda i, e: (i, 0)),
        out_shape=jax.ShapeDtypeStruct((t, D_MODEL), F32),
        scratch_shapes=[pltpu.VMEM((tm, D_MODEL), F32)],
        compiler_params=_cparams(("parallel", "arbitrary")),
        name="moe",
    )(h2, comb, wg, wu, wd, x1, mods, ln2g, ln2b)


def _row_gather(src_ref, idx_ref, base, buf, slot, sem, n_rows, start):
    def body(r, c):
        cp = pltpu.make_async_copy(src_ref.at[pl.ds(idx_ref[base + r], 1), :],
                                   buf.at[slot, pl.ds(r, 1), :], sem.at[slot])
        if start:
            cp.start()
        else:
            cp.wait()
        return c

    lax.fori_loop(0, n_rows, body, 0)


def _moe_group_kernel(perm_ref, tgrp_ref, nlive_ref, x1_ref, comb_ref, sh2_ref, sc2_ref, wg_ref, wu_ref, wd_ref,
                      y_ref, xbuf, sems):
    k = pl.program_id(0)
    tg = comb_ref.shape[0]
    n_live = nlive_ref[0]

    @pl.when(k == 0)
    def _():
        _row_gather(x1_ref, perm_ref, 0, xbuf, 0, sems, tg, True)

    @pl.when(k + 1 < n_live)
    def _():
        _row_gather(x1_ref, perm_ref, (k + 1) * tg, xbuf, (k + 1) % 2, sems, tg, True)

    @pl.when(k < jnp.maximum(n_live, 1))
    def _():
        _row_gather(x1_ref, perm_ref, k * tg, xbuf, k % 2, sems, tg, False)

    @pl.when(k < n_live)
    def _():
        h = (_ln(xbuf[k % 2]) * (1.0 + sc2_ref[0:1, :]) + sh2_ref[0:1, :]).astype(BF16)
        comb = comb_ref[...]
        lane = lax.broadcasted_iota(jnp.int32, comb.shape, 1)
        e0 = tgrp_ref[k] * EXPERTS_PER_GROUP
        acc = jnp.zeros((tg, D_MODEL), F32)
        for e in range(EXPERTS_PER_GROUP):
            w_col = jnp.sum(jnp.where(lane == e0 + e, comb, 0.0), axis=1, keepdims=True)
            act = _silu(_dot(h, wg_ref[e])) * _dot(h, wu_ref[e])
            acc = acc + _dot((act * w_col).astype(BF16), wd_ref[e])
        y_ref[...] = acc

    @pl.when(k >= n_live)
    def _():
        y_ref[...] = jnp.zeros(y_ref.shape, F32)


def _moe_finish_kernel(pos_ref, y_ref, x1_ref, gt2_ref, g_ref, b_ref, o_ref, ybuf, sems, *, alpha):
    i = pl.program_id(0)
    tm = x1_ref.shape[0]

    @pl.when(i == 0)
    def _():
        _row_gather(y_ref, pos_ref, 0, ybuf, 0, sems, tm, True)

    @pl.when(i + 1 < pl.num_programs(0))
    def _():
        _row_gather(y_ref, pos_ref, (i + 1) * tm, ybuf, (i + 1) % 2, sems, tm, True)

    _row_gather(y_ref, pos_ref, i * tm, ybuf, i % 2, sems, tm, False)
    o_ref[...] = _ln(alpha * x1_ref[...] + (1.0 + gt2_ref[0:1, :]) * ybuf[i % 2]) * g_ref[...] + b_ref[...]


def _moe_dispatch(comb, tg):
    t = comb.shape[0]
    i32 = jnp.int32
    cg = comb[:, :N_EXPERTS].reshape(t, N_EXPERT_GROUPS, EXPERTS_PER_GROUP)
    gid = jnp.argmax(jnp.sum(cg, axis=-1) > 0.0, axis=-1).astype(i32)
    onehot = (gid[:, None] == jnp.arange(N_EXPERT_GROUPS, dtype=i32)[None, :]).astype(i32)
    csum = jnp.cumsum(onehot, axis=0)
    rank = jnp.take_along_axis(csum, gid[:, None], axis=1)[:, 0] - 1
    pcount = ((csum[-1] + tg - 1) // tg) * tg
    ends = jnp.cumsum(pcount)
    pos = ((ends - pcount)[gid] + rank).astype(i32)
    rp = t + N_EXPERT_GROUPS * tg
    perm = jnp.full((rp,), -1, i32).at[pos].set(jnp.arange(t, dtype=i32))
    live = (perm >= 0).astype(F32)
    perm = jnp.maximum(perm, 0)
    comb_sorted = comb[perm] * live[:, None]
    tile_start = jnp.arange(rp // tg, dtype=i32) * tg
    tile_group = jnp.minimum(jnp.sum(tile_start[:, None] >= ends[None, :], axis=1), N_EXPERT_GROUPS - 1).astype(i32)
    n_live = (ends[-1] // tg).astype(i32).reshape(1)
    return perm, pos, tile_group, n_live, comb_sorted


def _moe_grouped(x1, comb, wg, wu, wd, mods, ln2g, ln2b, alpha):
    t = x1.shape[0]
    tg = MOE_GROUP_TILE
    perm, pos, tile_group, n_live, comb_sorted = _moe_dispatch(comb, tg)
    rp = perm.shape[0]
    epg = EXPERTS_PER_GROUP

    def wspec(shape):
        return pl.BlockSpec((epg,) + shape, lambda k, perm, tgrp, nl: (tgrp[k], 0, 0), pipeline_mode=pl.Buffered(1))

    y = pl.pallas_call(
        _moe_group_kernel,
        grid_spec=pltpu.PrefetchScalarGridSpec(
            num_scalar_prefetch=3,
            grid=(rp // tg,),
            in_specs=[pl.BlockSpec(memory_space=pl.ANY),
                      pl.BlockSpec((tg, LANES), lambda k, *_: (k, 0)),
                      pl.BlockSpec((8, D_MODEL), lambda k, *_: (0, 3)),
                      pl.BlockSpec((8, D_MODEL), lambda k, *_: (0, 4)),
                      wspec((D_MODEL, D_EXPERT)), wspec((D_MODEL, D_EXPERT)), wspec((D_EXPERT, D_MODEL))],
            out_specs=pl.BlockSpec((tg, D_MODEL), lambda k, *_: (k, 0)),
            scratch_shapes=[pltpu.VMEM((2, tg, D_MODEL), F32), pltpu.SemaphoreType.DMA((2,))]),
        out_shape=jax.ShapeDtypeStruct((rp, D_MODEL), F32),
        compiler_params=_cparams(("arbitrary",)),
        name="moe_group",
    )(perm, tile_group, n_live, x1, comb_sorted, mods, mods, wg, wu, wd)

    tm = MOE_FINISH_TILE
    return pl.pallas_call(
        functools.partial(_moe_finish_kernel, alpha=alpha),
        grid_spec=pltpu.PrefetchScalarGridSpec(
            num_scalar_prefetch=1,
            grid=(t // tm,),
            in_specs=[pl.BlockSpec(memory_space=pl.ANY),
                      pl.BlockSpec((tm, D_MODEL), lambda i, pos: (i, 0)),
                      pl.BlockSpec((8, D_MODEL), lambda i, pos: (0, 5)),
                      pl.BlockSpec((1, D_MODEL), lambda i, pos: (0, 0)),
                      pl.BlockSpec((1, D_MODEL), lambda i, pos: (0, 0))],
            out_specs=pl.BlockSpec((tm, D_MODEL), lambda i, pos: (i, 0)),
            scratch_shapes=[pltpu.VMEM((2, tm, D_MODEL), F32), pltpu.SemaphoreType.DMA((2,))]),
        out_shape=jax.ShapeDtypeStruct((t, D_MODEL), F32),
        compiler_params=_cparams(("arbitrary",)),
        name="moe_finish",
    )(pos, y, x1, mods, ln2g, ln2b)


def _rel_bucket(dist):
    n = jnp.maximum(dist, 0)
    max_exact = N_BUCKETS // 2
    nf = jnp.maximum(n, max_exact).astype(F32)
    large = max_exact + (jnp.log(nf / max_exact) / math.log(MAX_DISTANCE / max_exact)
                         * (N_BUCKETS - max_exact)).astype(jnp.int32)
    return jnp.where(n < max_exact, n, jnp.minimum(large, N_BUCKETS - 1))


def _bias_lookup(rel_bias, dists):
    dists = [np.asarray(d) for d in dists]
    flat = jnp.asarray(np.concatenate([d.reshape(-1) for d in dists]), jnp.int32)
    vals = rel_bias[_rel_bucket(flat)].astype(F32)
    out, o = [], 0
    for d in dists:
        out.append(vals[o:o + d.size].reshape(d.shape + (N_HEADS,)))
        o += d.size
    return out


def _prompt_tiles(rel_bias):
    kk = np.arange(BLOCK)[:, None]
    qq = np.arange(BLOCK)[None, :]
    dmats = [m * BLOCK + qq - kk for m in range(WIN_BLOCKS + 1)]
    drows = [m * BLOCK + np.arange(BLOCK) - (BLOCK - 1) for m in range(4)]
    looked = _bias_lookup(rel_bias, dmats + drows)
    bmats, rows = looked[:len(dmats)], looked[len(dmats):]

    def lanes(tile8, g):
        return jnp.concatenate([tile8[:, :, g * N_REP + r] for r in range(N_REP)], axis=1)

    def tile(m, lo, hi):
        ok = (dmats[m] >= lo) & (dmats[m] < hi)
        return jnp.where(ok[:, :, None], bmats[m], NEG)

    big = 1 << 20
    neg_tile = jnp.full((BLOCK, BLOCK, N_HEADS), NEG, F32)
    sel_tiles = [tile(m, 0, big) for m in range(4)] + [neg_tile]
    win_tiles = [tile(m, 0, WINDOW) for m in range(WIN_BLOCKS + 1)] + [neg_tile]
    st = jnp.stack([jnp.stack([lanes(t, g) for t in sel_tiles]) for g in range(N_GROUPS)]) / QK_SCALE
    wt = jnp.stack([jnp.stack([lanes(t, g) for t in win_tiles]) for g in range(N_GROUPS)]) / QK_SCALE
    ct = jnp.stack([jnp.stack([jnp.concatenate([rw[:, g * N_REP + r] for r in range(N_REP)]) for rw in rows]
                              + [jnp.zeros((N_REP * BLOCK,), F32)] * 4) for g in range(N_GROUPS)])
    return st, wt, ct


def _sample_lanes(tile8, t_new):
    rows = tile8.shape[0]
    x = jnp.transpose(tile8, (0, 2, 1)).reshape(rows, N_HEADS * t_new)
    return jnp.pad(x, ((0, 0), (0, LANES - N_HEADS * t_new)))


def _sample_tiles(rel_bias, past, t_new, nbp, win_rows, win_pad):
    tq = past + np.arange(t_new)[None, :]
    n = np.arange(nbp)
    kn = np.arange(PAGE)
    j = np.arange(win_pad)
    k_win = past - min(WINDOW, past) + j
    far = 1 << 20
    specs = [(n * BLOCK + BLOCK - 1, n < past // BLOCK + 1, far),
             (past - PAGE + np.arange(PAGE), None, far),
             (past + kn, kn < t_new, far),
             (k_win, (j < win_rows) & (k_win >= 0), WINDOW)]
    dists = [tq - k_pos[:, None] for k_pos, _, _ in specs]
    tiles = []
    for (_, extra, hi), d, b in zip(specs, dists, _bias_lookup(rel_bias, dists)):
        ok = (d >= 0) & (d < hi)
        if extra is not None:
            ok = ok & extra[:, None]
        tiles.append(_sample_lanes(jnp.where(ok[:, :, None], b, NEG), t_new))
    cb, nearb, newb, winb = tiles
    farb = jnp.broadcast_to(_sample_lanes(jnp.broadcast_to(
        rel_bias[N_BUCKETS - 1].astype(F32)[None, None, :], (1, t_new, N_HEADS)), t_new), (8, LANES))
    return cb, nearb, farb, newb, winb


def _mix_mats(ws, bs, tc, rep):
    n = tc // rep
    causal = jnp.tril(jnp.ones((CHUNK, CHUNK), F32))
    wc = (ws * causal)[:, :n, :n]
    eye = jnp.eye(rep, dtype=F32)
    m = jnp.einsum('ab,gts->gatbs', eye, wc).reshape(GMLP_GROUPS, tc, tc)
    bcol = jnp.tile(bs[:, :n].T, (rep, 1))
    return m.astype(BF16), jnp.pad(bcol, ((0, 0), (0, LANES - GMLP_GROUPS)))


def kernel(x_prompt, x_sample, cache_nsa_kv, state_win_kv, state_pool, page_table, c_prompt, c_sample, rel_bias, router_w, router_b, w_in, nsa_phi_pe, nsa_phi_w1, nsa_phi_w2, gmlp_norm_g, gmlp_norm_b, gmlp_ws, gmlp_bs, pool_w, pool_scale, proj_a, proj_b, proj_c, w_o, ada_w, ada_b, ln1_g, ln1_b, ln2_g, ln2_b, exp_w_gate, exp_w_up, exp_w_down):
    depth = w_in.shape[0]
    alpha = (2 * depth) ** 0.25
    s = x_prompt.shape[1]
    bsz, t_new = x_sample.shape[0], x_sample.shape[1]
    n_pool = cache_nsa_kv.shape[1]
    n_pages = page_table.shape[1]
    past = n_pages * PAGE
    ts = bsz * t_new
    assert x_prompt.shape[0] == 1 and s % CHUNK_KEYS == 0 and past % PAGE == 0
    assert N_HEADS * t_new <= LANES and t_new <= BLOCK and (past // BLOCK) >= N_SEL
    nb_s = past // BLOCK + 1
    nbp = -(-nb_s // 8) * 8

    n_c = 1 + bsz
    c_all = jnp.pad(jnp.concatenate([c_prompt, c_sample], axis=0), ((0, -n_c % 8), (0, 0)))
    mods_all = _ada_mods(c_all, ada_w, ada_b)

    cache_pages = cache_nsa_kv.reshape(depth * n_pool, PAGE * PAGED_KINDS * N_GROUPS, HEAD_DIM)
    st, wt, ct = _prompt_tiles(rel_bias)
    n_win = state_win_kv.shape[2]
    win_rows = n_win + t_new
    win_pad = n_win + PAGE
    win_state = state_win_kv.reshape(depth * bsz, n_win * 2 * N_GROUPS, HEAD_DIM)
    assert nbp <= SAMPLE_BLOCKS_PAD and n_win == min(WINDOW, past) and n_win % 8 == 0
    cb_s, nearb_s, farb_s, newb_s, winb_s = _sample_tiles(rel_bias, past, t_new, SAMPLE_BLOCKS_PAD, win_rows, win_pad)
    rw = jnp.pad(router_w, ((0, 0), (0, LANES - N_EXPERTS))).astype(BF16)
    rb = jnp.pad(router_b.reshape(N_EXPERTS, 1), ((0, LANES - N_EXPERTS), (0, LANES - 1)))

    xp = x_prompt.reshape(s, D_MODEL)
    xs = x_sample.reshape(ts, D_MODEL)
    outs = {k: [] for k in ("nsa_p", "nsa_s", "win_p", "win_s", "pool_p", "pool_s", "v_s")}

    for l in range(depth):
        w_parts = _in_proj_weights(w_in[l])
        pe = nsa_phi_pe[l]
        w1b = nsa_phi_w1[l].astype(BF16)
        w2b = nsa_phi_w2[l].astype(BF16)
        pab, pbb, pcb, wob = (a[l].astype(BF16) for a in (proj_a, proj_b, proj_c, w_o))
        wg, wu, wd = (a[l].astype(BF16) for a in (exp_w_gate, exp_w_up, exp_w_down))
        pwb = pool_w[l].astype(BF16)
        ps = pool_scale[l].reshape(1, POOL_WIDTH)
        gng = gmlp_norm_g[l].reshape(1, GMLP_WIDTH)
        gnb = gmlp_norm_b[l].reshape(1, GMLP_WIDTH)
        ln1g, ln1b, ln2g, ln2b = (a[l].reshape(1, D_MODEL) for a in (ln1_g, ln1_b, ln2_g, ln2_b))
        mods_p = jnp.broadcast_to(mods_all[l, 0:1], (8, 6 * D_MODEL))
        mods_s = jnp.repeat(mods_all[l, 1:1 + bsz], t_new, axis=0)

        z, zb, kv_rows = _in_proj(xp, mods_p, w_parts, 1024)
        gh = N_GROUPS * HEAD_DIM
        outs["nsa_p"].append(kv_rows.reshape(1, s, PAGED_KINDS, N_GROUPS, HEAD_DIM))
        outs["win_p"].append(z[s - min(WINDOW, s):, COL_KV + PAGED_KINDS * gh:COL_KV + KV_KINDS * gh]
                             .reshape(1, min(WINDOW, s), 2, N_GROUPS, HEAD_DIM))
        outs["pool_p"].append(z[s - POOL_BUF:, COL_XC:COL_XC + POOL_WIDTH].reshape(1, POOL_BUF, POOL_WIDTH))

        nb = s // BLOCK
        cmp = _compress_prompt(z, pe, w1b, w2b).reshape(2, N_GROUPS, nb, HEAD_DIM)
        kc = cmp[0].astype(BF16)
        vcT = jnp.transpose(cmp[1], (0, 2, 1)).astype(BF16)
        nch = s // CHUNK_KEYS

        def kv_cols(kind, g):
            c0 = COL_KV + (kind * N_GROUPS + g) * HEAD_DIM
            return zb[:, c0:c0 + HEAD_DIM].reshape(nch, CHUNK_KEYS, HEAD_DIM)

        def k_chunks(kind):
            return jnp.stack([kv_cols(kind, g) for g in range(N_GROUPS)])

        def vT_chunks(kind):
            vt = jnp.stack([jnp.transpose(kv_cols(kind, g), (0, 2, 1)) for g in range(N_GROUPS)])
            ones = jnp.zeros((N_GROUPS, nch, 8, CHUNK_KEYS), BF16).at[:, :, 0].set(1.0)
            return jnp.concatenate([vt, ones], axis=2)

        q = zb[:, COL_Q:COL_Q + NSA_WIDTH].reshape(nb, BLOCK, N_GROUPS, N_REP, HEAD_DIM)
        qT = jnp.transpose(q, (2, 4, 0, 3, 1)).reshape(N_GROUPS, HEAD_DIM, nb * N_REP * BLOCK)
        gn = z[:, COL_GN:COL_GN + 3 * N_HEADS].reshape(nb, BLOCK, 3, N_GROUPS, N_REP)
        gates = jnp.transpose(gn, (3, 0, 2, 4, 1)).reshape(N_GROUPS, nb, 3, N_REP * BLOCK)
        gates = jnp.pad(gates, ((0, 0), (0, 0), (0, 5), (0, 0)))
        ob = _nsa_prompt(qT, kc, vcT, k_chunks(2), vT_chunks(3), k_chunks(4), vT_chunks(5), st, wt, ct, gates, s)

        mm, bcol = _mix_mats(gmlp_ws[l], gmlp_bs[l], CHUNK, 1)
        oa, _ = _gmlp(z, mm, bcol, gng, gnb, CHUNK)
        tp = 512
        cxb = COL_XC // POOL_WIDTH
        oc = _pool(z, z, lambda i: (jnp.maximum(i * (tp // 16) - 1, 0), cxb), pwb, ps, tp, True, 0, tp)
        x1, h2, comb = _merge(xp, z, oa, ob, oc, mods_p, ln1g, ln1b, pab, pbb, pcb, wob, rw, rb, 256, alpha)
        xp = _moe_grouped(x1, comb, wg, wu, wd, mods_p, ln2g, ln2b, alpha)

        zs, zbs, kv_rows_s = _in_proj(xs, mods_s, w_parts, ts)
        kvs = zs[:, COL_KV:COL_KV + 1536].reshape(bsz, t_new, KV_KINDS, N_GROUPS, HEAD_DIM)
        kvbs = zbs[:, COL_KV:COL_KV + 1536].reshape(bsz, t_new, KV_KINDS, gh)
        outs["nsa_s"].append(kv_rows_s.reshape(bsz, t_new, PAGED_KINDS, N_GROUPS, HEAD_DIM))
        win_full = jnp.concatenate([state_win_kv[l], kvs[:, :, PAGED_KINDS:]], axis=1)
        outs["win_s"].append(win_full[:, win_full.shape[1] - min(WINDOW, past + t_new):])
        xcs = zs[:, COL_XC:COL_XC + POOL_WIDTH].reshape(bsz, t_new, POOL_WIDTH)
        ext = jnp.concatenate([state_pool[l], xcs], axis=1)
        outs["pool_s"].append(ext[:, ext.shape[1] - POOL_BUF:])

        pt = page_table + l * n_pool
        new_cmp = kvs[:, :, 0:2].reshape(bsz, t_new, 2 * N_GROUPS * HEAD_DIM)
        knew = jnp.pad(kvbs[:, :, 2], ((0, 0), (0, PAGE - t_new), (0, 0)))
        vnew = jnp.pad(kvbs[:, :, 3], ((0, 0), (0, PAGE - t_new), (0, 0)))
        kwn = jnp.pad(kvbs[:, :, 4], ((0, 0), (0, PAGE - t_new), (0, 0)))
        vwn = jnp.pad(kvbs[:, :, 5], ((0, 0), (0, PAGE - t_new), (0, 0)))
        qs = zbs[:, COL_Q:COL_Q + NSA_WIDTH].reshape(bsz, t_new, N_GROUPS, N_REP, HEAD_DIM)
        qs = jnp.transpose(qs, (0, 2, 4, 3, 1)).reshape(bsz, N_GROUPS, HEAD_DIM, N_REP * t_new)
        n_q = N_REP * t_new
        qT_s = jnp.concatenate(
            [jnp.pad(qs[:, g], ((0, 0), (0, 0), (g * n_q, LANES - (g + 1) * n_q))) for g in range(N_GROUPS)], axis=1)
        gns = zs[:, COL_GN:COL_GN + 3 * N_HEADS].reshape(bsz, t_new, 3, N_HEADS)
        gates_s = jnp.transpose(gns, (0, 3, 1, 2)).reshape(bsz, N_HEADS * t_new, 3)
        gates_s = jnp.pad(gates_s, ((0, 0), (0, LANES - N_HEADS * t_new), (0, LANES - 3)))
        o_s = _sample_nsa(pt, cache_pages, new_cmp, qT_s, knew, vnew, win_state, l, kwn, vwn, pe, w1b, w2b,
                          cb_s, nearb_s, farb_s, newb_s, winb_s, gates_s, nbp)
        o4 = o_s[:, :N_HEADS * t_new].reshape(bsz, N_GROUPS, N_REP, t_new, N_GROUPS, HEAD_DIM)
        o4 = jnp.stack([o4[:, g, :, :, g] for g in range(N_GROUPS)], axis=1)
        ob_s = jnp.transpose(o4, (0, 3, 1, 2, 4)).reshape(ts, NSA_WIDTH).astype(BF16)

        mm_s, bcol_s = _mix_mats(gmlp_ws[l], gmlp_bs[l], ts, bsz)
        oa_s, vn_s = _gmlp(zs, mm_s, bcol_s, gng, gnb, ts)
        outs["v_s"].append(vn_s.reshape(bsz, t_new, GMLP_WIDTH))
        halo = jnp.pad(state_pool[l], ((0, 0), (16 - POOL_BUF, 0), (0, 0))).reshape(bsz * 16, POOL_WIDTH)
        oc_s = _pool(zs, halo, lambda i: (i, 0), pwb, ps, t_new, False, past, 0)
        x1s, h2s, comb_s = _merge(xs, zs, oa_s, ob_s, oc_s, mods_s, ln1g, ln1b, pab, pbb, pcb, wob, rw, rb, ts, alpha)
        xs = _moe(h2s, comb_s, wg, wu, wd, x1s, mods_s, ln2g, ln2b, ts, alpha)

    return (xp.reshape(1, s, D_MODEL), xs.reshape(bsz, t_new, D_MODEL),
            jnp.stack(outs["nsa_p"]), jnp.stack(outs["nsa_s"]), jnp.stack(outs["win_p"]), jnp.stack(outs["win_s"]),
            jnp.stack(outs["pool_p"]), jnp.stack(outs["pool_s"]), jnp.stack(outs["v_s"]))
```

```python
import functools
import math

import jax
import jax.numpy as jnp
from jax import lax
from jax.experimental import pallas as pl
from jax.experimental.pallas import tpu as pltpu

F32 = jnp.float32
BF16 = jnp.bfloat16

D_MODEL = 2048
HEAD_DIM = 128
N_GROUPS = 2
N_REP = 4
N_HEADS = N_GROUPS * N_REP
NSA_WIDTH = N_HEADS * HEAD_DIM
KV_KINDS = 6
PAGED_KINDS = 4
BLOCK = 64
N_SEL = 16
WINDOW = 512
WIN_BLOCKS = WINDOW // BLOCK
PHI_HIDDEN = 2 * HEAD_DIM
PAGE = 128
GMLP_GROUPS = 4
GMLP_WIDTH = 512
CHUNK = 128
POOL_WINDOWS = (2, 4, 8, 16)
POOL_GROUP_DIM = 128
POOL_WIDTH = 512
POOL_BUF = 15
N_BUCKETS = 32
MAX_DISTANCE = 128
N_EXPERTS = 16
N_EXPERT_GROUPS = 4
EXPERTS_PER_GROUP = 4
D_EXPERT = 512
LN_EPS = 1e-5
NEG = -1e30
FORCED_SCORE = 1e9

COL_Q = 0
COL_KV = 1024
COL_U = 2560
COL_V = 3072
COL_XC = 3584
COL_GM = 4096
COL_GN = 10240
N_IN_PAD = 10752
IN_TN = 512

LANES = 128
VMEM_LIMIT = 56 * 1024 * 1024
CHUNK_BLOCKS = 8
CHUNK_KEYS = CHUNK_BLOCKS * BLOCK
XC_PITCH = BLOCK + 8
SAMPLE_RING_PAGES = 16
SAMPLE_BLOCKS_PAD = 256
MOE_GROUP_TILE = 256


def _cparams(sem):
    return pltpu.CompilerParams(dimension_semantics=sem, vmem_limit_bytes=VMEM_LIMIT)


def _ln(x):
    mu = jnp.mean(x, axis=-1, keepdims=True)
    xc = x - mu
    var = jnp.mean(xc * xc, axis=-1, keepdims=True)
    return xc * lax.rsqrt(var + LN_EPS)


def _sigmoid(x):
    return 1.0 / (1.0 + jnp.exp(-x))


def _silu(x):
    return x * _sigmoid(x)


def _gelu_tanh(x):
    return 0.5 * x * (1.0 + jnp.tanh(math.sqrt(2.0 / math.pi) * (x + 0.044715 * (x * x * x))))


def _dot(a, b):
    return jnp.dot(a, b, preferred_element_type=F32)


QK_SCALE = HEAD_DIM ** -0.5
EXP2_SCALE = QK_SCALE * math.log2(math.e)


def _qk(k, qT):
    return _dot(k, qT) * QK_SCALE


def _ada_kernel(c_ref, w_ref, b_ref, o_ref):
    c = c_ref[...]
    o_ref[...] = _dot(_silu(c).astype(BF16), w_ref[...].astype(BF16)) + b_ref[...]


def _ada_mods(c_all, ada_w, ada_b):
    depth = ada_w.shape[0]
    rows = c_all.shape[0]
    n = ada_w.shape[2]
    tn = 1024
    return pl.pallas_call(
        _ada_kernel,
        grid=(depth, n // tn),
        in_specs=[pl.BlockSpec((rows, D_MODEL), lambda l, j: (0, 0)),
                  pl.BlockSpec((None, D_MODEL, tn), lambda l, j: (l, 0, j)),
                  pl.BlockSpec((None, 1, tn), lambda l, j: (l, 0, j))],
        out_specs=pl.BlockSpec((None, rows, tn), lambda l, j: (l, 0, j)),
        out_shape=jax.ShapeDtypeStruct((depth, rows, n), F32),
        compiler_params=_cparams(("parallel", "parallel")),
        name="ada_mods",
    )(c_all, ada_w, ada_b.reshape(depth, 1, n))


def _mod_rows(ref, tm):
    return ref[...] if ref.shape[0] == tm else ref[0:1, :]


N_QKV_TILES = COL_U // IN_TN
N_MAIN_TILES = COL_GN // IN_TN


N_Q_TILES = COL_KV // IN_TN
KV_ROWS = PAGED_KINDS * N_GROUPS
COLS_PER_TILE = IN_TN // HEAD_DIM


def _inproj_kernel(x_ref, sh_ref, sc_ref, wa_ref, wb_ref, wc_ref, z_ref, zb_ref, kv_ref, h_scr):
    tm = x_ref.shape[0]
    j = pl.program_id(1)

    @pl.when(j == 0)
    def _():
        h = _ln(x_ref[...]) * (1.0 + _mod_rows(sc_ref, tm)) + _mod_rows(sh_ref, tm)
        h_scr[...] = h.astype(BF16)

    @pl.when(j < N_QKV_TILES)
    def _():
        z = _dot(h_scr[...], wa_ref[...])
        z_ref[...] = z
        zb_ref[...] = z.astype(BF16)
        for jt in range(KV_ROWS // COLS_PER_TILE):
            @pl.when(j == N_Q_TILES + jt)
            def _():
                for c in range(COLS_PER_TILE):
                    kv_ref[pl.ds(jt * COLS_PER_TILE + c, tm, stride=KV_ROWS), :] = \
                        z_ref[:, c * HEAD_DIM:(c + 1) * HEAD_DIM]

    @pl.when((j >= N_QKV_TILES) & (j < N_MAIN_TILES))
    def _():
        z_ref[...] = _dot(h_scr[...], wb_ref[...])

    @pl.when(j == N_MAIN_TILES)
    def _():
        z_ref[...] = _dot(h_scr[...], wc_ref[...])


def _mod_spec(mods, tm, col):
    rows = mods.shape[0]
    if rows == 8:
        return pl.BlockSpec((8, D_MODEL), lambda i, *_: (0, col))
    return pl.BlockSpec((tm, D_MODEL), lambda i, *_: (i, col))


def _in_proj_weights(w):
    n_gate = 3 * N_HEADS
    wa = w[:, :COL_U].astype(BF16)
    wb = w[:, COL_U + n_gate:].astype(BF16)
    wc = jnp.pad(w[:, COL_U:COL_U + n_gate], ((0, 0), (0, IN_TN - n_gate))).astype(BF16)
    return wa, wb, wc


def _in_proj(x, mods, weights, tm):
    t = x.shape[0]
    wa, wb, wc = weights
    n_b = N_MAIN_TILES - N_QKV_TILES
    return pl.pallas_call(
        _inproj_kernel,
        grid=(t // tm, N_IN_PAD // IN_TN),
        in_specs=[pl.BlockSpec((tm, D_MODEL), lambda i, j: (i, 0)),
                  _mod_spec(mods, tm, 0), _mod_spec(mods, tm, 1),
                  pl.BlockSpec((D_MODEL, IN_TN), lambda i, j: (0, jnp.minimum(j, N_QKV_TILES - 1))),
                  pl.BlockSpec((D_MODEL, IN_TN), lambda i, j: (0, jnp.clip(j - N_QKV_TILES, 0, n_b - 1))),
                  pl.BlockSpec((D_MODEL, IN_TN), lambda i, j: (0, 0))],
        out_specs=[pl.BlockSpec((tm, IN_TN), lambda i, j: (i, j)),
                   pl.BlockSpec((tm, IN_TN), lambda i, j: (i, jnp.minimum(j, N_QKV_TILES - 1))),
                   pl.BlockSpec((tm * KV_ROWS, HEAD_DIM), lambda i, j: (i, 0))],
        out_shape=[jax.ShapeDtypeStruct((t, N_IN_PAD), F32),
                   jax.ShapeDtypeStruct((t, COL_U), BF16),
                   jax.ShapeDtypeStruct((t * KV_ROWS, HEAD_DIM), F32)],
        scratch_shapes=[pltpu.VMEM((tm, D_MODEL), BF16)],
        compiler_params=_cparams(("parallel", "arbitrary")),
        name="in_proj",
    )(x, mods, mods, wa, wb, wc)


def _compress_pair(x0_ref, x1_ref, nb, pe_ref, w1_ref, w2_ref):
    def step(p, acc):
        pe = pe_ref[pl.ds(p, 1), :]
        a0 = x0_ref[pl.ds(p, nb, stride=BLOCK), :] + pe
        a1 = x1_ref[pl.ds(p, nb, stride=BLOCK), :] + pe
        a = jnp.concatenate([a0, a1], axis=0).astype(BF16)
        w = w1_ref[pl.ds(pl.multiple_of(p * HEAD_DIM, HEAD_DIM), HEAD_DIM), :]
        return acc + _dot(a, w)

    acc = lax.fori_loop(0, BLOCK, step, jnp.zeros((2 * nb, PHI_HIDDEN), F32))
    return _dot(_gelu_tanh(acc).astype(BF16), w2_ref[...])


def _compress_prompt_kernel(x0_ref, x1_ref, pe_ref, w1_ref, w2_ref, o_ref, *, nb):
    o_ref[...] = _compress_pair(x0_ref, x1_ref, nb, pe_ref, w1_ref, w2_ref)


def _compress_prompt(z, pe, w1b, w2b):
    s = z.shape[0]
    nb = s // BLOCK
    cb = COL_KV // HEAD_DIM
    return pl.pallas_call(
        functools.partial(_compress_prompt_kernel, nb=nb),
        grid=(2,),
        in_specs=[pl.BlockSpec((s, HEAD_DIM), lambda k: (0, cb + 2 * k)),
                  pl.BlockSpec((s, HEAD_DIM), lambda k: (0, cb + 2 * k + 1)),
                  pl.BlockSpec((None, BLOCK, HEAD_DIM), lambda k: (k, 0, 0)),
                  pl.BlockSpec((None, BLOCK * HEAD_DIM, PHI_HIDDEN), lambda k: (k, 0, 0)),
                  pl.BlockSpec((None, PHI_HIDDEN, HEAD_DIM), lambda k: (k, 0, 0))],
        out_specs=pl.BlockSpec((None, 2 * nb, HEAD_DIM), lambda k: (k, 0, 0)),
        out_shape=jax.ShapeDtypeStruct((2, 2 * nb, HEAD_DIM), F32),
        compiler_params=_cparams(("parallel",)),
        name="compress_prompt",
    )(z, z, pe, w1b, w2b)


N_FORCED = 3


def _select_blocks(score, forced, n_io):
    def body(_, carry):
        work, sel = carry
        mval = jnp.max(work, axis=0, keepdims=True)
        cand = jnp.where(work == mval, n_io, jnp.int32(1 << 30))
        idx = jnp.min(cand, axis=0, keepdims=True)
        pick = n_io == idx
        return jnp.where(pick, -jnp.inf, work), jnp.where(pick, 1.0, sel)

    init = (jnp.where(forced, -jnp.inf, score), jnp.where(forced, 1.0, 0.0))
    _, sel = lax.fori_loop(0, N_SEL - N_FORCED, body, init)
    return sel


def _nsa_prompt_kernel(qT_ref, kc_ref, vcT_ref, ksel_ref, vselT_ref, kwin_ref, vwinT_ref,
                       st_ref, wt_ref, ct_ref, gate_ref, o_ref, selneg_scr, selfar_scr, s_a, s_b, p_a, p_b, *, nb):
    i = pl.program_id(1)
    qT = qT_ref[...]
    nl = N_REP * BLOCK

    n_io = lax.broadcasted_iota(jnp.int32, (nb, nl), 0)
    qq = lax.broadcasted_iota(jnp.int32, (nb, nl), 1) % BLOCK
    m = i - n_io
    ct = ct_ref[...]
    bias = jnp.where(m == 0, ct[0:1], jnp.where(m == 1, ct[1:2], jnp.where(m == 2, ct[2:3], ct[3:4])))
    valid = (m >= 1) | ((m == 0) & (qq == BLOCK - 1))
    s = jnp.where(valid, _qk(kc_ref[...], qT) + bias, NEG)
    mx = jnp.max(s, axis=0, keepdims=True)
    p = jnp.where(valid, jnp.exp(s - mx), 0.0)
    l = jnp.sum(p, axis=0, keepdims=True)
    pn = p / jnp.where(l > 0.0, l, 1.0)
    o_cmp = _dot(vcT_ref[...], pn.astype(BF16))

    half = pn[:, :LANES] + pn[:, LANES:]
    imp = half + pltpu.roll(half, BLOCK, 1)
    n_io1 = n_io[:, :LANES]
    forced = (n_io1 == 0) | (n_io1 == i) | (n_io1 == i - 1)
    score = jnp.where(forced, FORCED_SCORE, imp)
    score = jnp.where(n_io1 <= i, score, -1.0)
    sel = _select_blocks(score, forced, n_io1)
    neg1 = jnp.where((sel > 0.5) & (score >= 0.0), 0.0, NEG)
    neg2 = jnp.concatenate([neg1, neg1], axis=1)
    selneg_scr[...] = neg2
    far_row = st_ref[st_ref.shape[0] - 2][0:1, :]
    selfar_scr[...] = jnp.where(neg2 == 0.0, far_row, NEG)

    c_i = i // CHUNK_BLOCKS

    def branch(k_ref, vT_ref, tile_ref, n_tiles, use_sel, n_chunks):
        def logits(c, dst):
            dst[...] = _dot(k_ref[jnp.maximum(c, 0)], qT)

        def pv(c, p_ref):
            return _dot(vT_ref[jnp.clip(c, 0, vT_ref.shape[0] - 1)], p_ref[...])

        def consume(c, src, m_run, near):
            live = c >= 0
            cz = jnp.maximum(c, 0)
            parts = []
            for b in range(CHUNK_BLOCKS):
                n = cz * CHUNK_BLOCKS + b
                sb = src[b * BLOCK:(b + 1) * BLOCK, :]
                if near:
                    mm = i - n
                    if use_sel:
                        tidx = jnp.where(mm < 0, n_tiles - 1, jnp.minimum(mm, n_tiles - 2))
                    else:
                        tidx = jnp.where((mm < 0) | (mm > n_tiles - 2), n_tiles - 1, mm)
                    sb = sb + tile_ref[jnp.where(live, tidx, n_tiles - 1)]
                    if use_sel:
                        sb = sb + selneg_scr[pl.ds(n, 1), :]
                else:
                    sb = sb + jnp.where(live, selfar_scr[pl.ds(n, 1), :], NEG)
                parts.append(sb)
            s2 = jnp.concatenate(parts, axis=0)
            m_new = jnp.maximum(m_run, jnp.max(s2, axis=0, keepdims=True))
            alpha = jnp.exp2((m_run - m_new) * EXP2_SCALE)
            pp = jnp.exp2((s2 - m_new) * EXP2_SCALE)
            return m_new, alpha, pp.astype(BF16)

        def pair(t, carry, near):
            m_run, acc = carry
            c0 = c_i - 2 * t
            logits(c0 - 1, s_b)
            r_prev = pv(c0 + 1, p_b)
            m_run, alpha, pp = consume(c0, s_a, m_run, near)
            p_a[...] = pp
            acc = alpha * (acc + r_prev)
            logits(c0 - 2, s_a)
            r_cur = pv(c0, p_a)
            m_run, alpha, pp = consume(c0 - 1, s_b, m_run, near)
            p_b[...] = pp
            return m_run, alpha * (acc + r_cur)

        logits(c_i, s_a)
        p_b[...] = jnp.zeros(p_b.shape, BF16)
        carry = (jnp.full((1, nl), NEG, F32), jnp.zeros((vT_ref.shape[1], nl), F32))
        carry = pair(0, carry, True)
        n_pairs = (n_chunks + 1) // 2 if use_sel else 1
        if use_sel:
            carry = lax.fori_loop(1, n_pairs, lambda t, cr: pair(t, cr, False), carry)
        acc = carry[1] + pv(c_i - 2 * n_pairs + 1, p_b)
        return acc[:HEAD_DIM] / acc[HEAD_DIM:HEAD_DIM + 1]

    o_sel = branch(ksel_ref, vselT_ref, st_ref, st_ref.shape[0], True, c_i + 1)
    o_win = branch(kwin_ref, vwinT_ref, wt_ref, wt_ref.shape[0], False, jnp.minimum(c_i + 1, 2))

    gates = _sigmoid(gate_ref[...])
    oT = gates[0:1] * o_cmp + gates[1:2] * o_sel + gates[2:3] * o_win
    o = oT.T
    for r in range(N_REP):
        o_ref[:, r * HEAD_DIM:(r + 1) * HEAD_DIM] = o[r * BLOCK:(r + 1) * BLOCK].astype(o_ref.dtype)


def _nsa_prompt(qT, kc, vcT, ksel, vselT, kwin, vwinT, st, wt, ct, gates, s):
    nb = s // BLOCK
    nq = s // BLOCK
    nl = N_REP * BLOCK
    nch = s // CHUNK_KEYS

    def whole(arr):
        shp = arr.shape[1:]
        return pl.BlockSpec((None,) + shp, lambda g, i: (g,) + (0,) * len(shp))

    return pl.pallas_call(
        functools.partial(_nsa_prompt_kernel, nb=nb),
        grid=(N_GROUPS, nq),
        in_specs=[pl.BlockSpec((None, HEAD_DIM, nl), lambda g, i: (g, 0, i)),
                  whole(kc), whole(vcT), whole(ksel), whole(vselT), whole(kwin), whole(vwinT),
                  whole(st), whole(wt), whole(ct),
                  pl.BlockSpec((None, None, 8, nl), lambda g, i: (g, i, 0, 0))],
        out_specs=pl.BlockSpec((BLOCK, N_REP * HEAD_DIM), lambda g, i: (i, g)),
        out_shape=jax.ShapeDtypeStruct((s, NSA_WIDTH), BF16),
        scratch_shapes=[pltpu.VMEM((nb, nl), F32),
                        pltpu.VMEM((nb, nl), F32),
                        pltpu.VMEM((CHUNK_KEYS, nl), F32),
                        pltpu.VMEM((CHUNK_KEYS, nl), F32),
                        pltpu.VMEM((CHUNK_KEYS, nl), BF16),
                        pltpu.VMEM((CHUNK_KEYS, nl), BF16)],
        compiler_params=_cparams(("parallel", "arbitrary")),
        name="nsa_prompt",
    )(qT, kc, vcT, ksel, vselT, kwin, vwinT, st, wt, ct, gates)


def _sample_nsa_kernel(pt_ref, cache_ref, newc_ref, qT_ref, knew_ref, vnew_ref, wst_ref, kwn_ref, vwn_ref,
                       pe_ref, w1_ref, w2_ref, cb_ref, nearb_ref, farb_ref, newb_ref, winb_ref, gate_ref, o_ref,
                       ring, xc, kcs, vsel, s_scr, selneg_scr, sems, *, n_pages, t_new, nbp, n_ring):
    b = pl.program_id(0)
    total = pl.num_programs(0) * n_pages
    past = n_pages * PAGE
    cur = past // BLOCK
    kinds_per_row = PAGED_KINDS * N_GROUPS
    qT = qT_ref[...]

    def page_dma(t, slot):
        return pltpu.make_async_copy(cache_ref.at[pt_ref[t // n_pages, t % n_pages]], ring.at[slot], sems.at[slot])

    @pl.when(b == 0)
    def _():
        for t in range(n_ring):
            page_dma(t, t).start()

    far = farb_ref[0:1, :]

    def page_step(j, c):
        t = b * n_pages + j
        slot = t % n_ring
        page_dma(t, slot).wait()
        buf = ring.at[slot]

        def rows(kg):
            return buf[pl.ds(kg, PAGE, stride=kinds_per_row), :]

        r0 = pl.multiple_of(j * PAGE, PAGE)
        base = pl.multiple_of(j * (PAGE // BLOCK) * XC_PITCH, 8)
        for kg in range(2 * N_GROUPS):
            x = rows(kg)
            for h in range(PAGE // BLOCK):
                xc[kg, pl.ds(base + h * XC_PITCH, BLOCK), :] = x[h * BLOCK:(h + 1) * BLOCK]
        k = jnp.concatenate([rows(4), rows(5)], axis=1).astype(BF16)
        s_scr[pl.ds(r0, PAGE), :] = _qk(k, qT) + jnp.where(j == n_pages - 1, nearb_ref[...], far)
        vsel[pl.ds(r0, PAGE), :] = jnp.concatenate([rows(6), rows(7)], axis=1).astype(BF16)

        @pl.when(t + n_ring < total)
        def _():
            page_dma(t + n_ring, slot).start()

        return c

    lax.fori_loop(0, n_pages, page_step, 0)

    tail = (nbp - cur) * XC_PITCH
    for kg in range(2 * N_GROUPS):
        xc[kg, pl.ds(cur * XC_PITCH, tail), :] = jnp.zeros((tail, HEAD_DIM), F32)
        xc[kg, pl.ds(cur * XC_PITCH, t_new), :] = newc_ref[:, kg * HEAD_DIM:(kg + 1) * HEAD_DIM]
    s_scr[pl.ds(past, PAGE), :] = _qk(knew_ref[...], qT) + newb_ref[...]
    vsel[pl.ds(past, PAGE), :] = vnew_ref[...]

    for kind in range(2):
        def step(pp, acc):
            p = 2 * pp
            pe0 = pe_ref[kind, pl.ds(p, 1), :]
            pe1 = pe_ref[kind, pl.ds(p + 1, 1), :]
            parts = []
            for g in range(N_GROUPS):
                x0 = xc[2 * kind + g, pl.ds(p, nbp, stride=XC_PITCH), :] + pe0
                x1 = xc[2 * kind + g, pl.ds(p + 1, nbp, stride=XC_PITCH), :] + pe1
                parts.append(jnp.concatenate([x0, x1], axis=1))
            a = jnp.concatenate(parts, axis=0).astype(BF16)
            w = w1_ref[kind, pl.ds(pl.multiple_of(p * HEAD_DIM, 2 * HEAD_DIM), 2 * HEAD_DIM), :]
            return acc + _dot(a, w)

        acc = lax.fori_loop(0, BLOCK // 2, step, jnp.zeros((N_GROUPS * nbp, PHI_HIDDEN), F32), unroll=2)
        out = _dot(_gelu_tanh(acc).astype(BF16), w2_ref[kind])
        kcs[kind, nbp:, :] = jnp.zeros((kcs.shape[1] - nbp, N_GROUPS * HEAD_DIM), F32)
        for g in range(N_GROUPS):
            kcs[kind, 0:nbp, g * HEAD_DIM:(g + 1) * HEAD_DIM] = out[g * nbp:(g + 1) * nbp]

    def softmax_rows(s):
        mx = jnp.max(s, axis=0, keepdims=True)
        p = jnp.where(s > 0.5 * NEG, jnp.exp(s - mx), 0.0)
        l = jnp.sum(p, axis=0, keepdims=True)
        return p / jnp.where(l > 0.0, l, 1.0)

    pn = softmax_rows(_qk(kcs[0].astype(BF16), qT) + cb_ref[...])
    o_cmp = _dot(pn.T.astype(BF16), kcs[1].astype(BF16))

    tot = pn + pltpu.roll(pn, t_new, 1) + pltpu.roll(pn, 2 * t_new, 1) + pltpu.roll(pn, 3 * t_new, 1)
    lane = lax.broadcasted_iota(jnp.int32, pn.shape, 1)
    top = jnp.where((lane // t_new) % N_REP == N_REP - 1, tot, 0.0)
    imp = top + pltpu.roll(top, LANES - t_new, 1) + pltpu.roll(top, LANES - 2 * t_new, 1) \
        + pltpu.roll(top, LANES - 3 * t_new, 1)
    n_io = lax.broadcasted_iota(jnp.int32, pn.shape, 0)
    forced = (n_io == 0) | (n_io == cur) | (n_io == cur - 1)
    score = jnp.where(forced, FORCED_SCORE, imp)
    score = jnp.where(n_io <= cur, score, -jnp.inf)
    sel = _select_blocks(score, forced, n_io)
    selneg_scr[...] = jnp.where(sel > 0.5, 0.0, NEG)

    n_st = wst_ref.shape[0] // (2 * N_GROUPS)

    def st_rows(j):
        return wst_ref[pl.ds(j, n_st, stride=2 * N_GROUPS), :]

    kw = jnp.concatenate([st_rows(0), st_rows(1)], axis=1).astype(BF16)
    vw = jnp.concatenate([st_rows(2), st_rows(3)], axis=1).astype(BF16)
    s_w = jnp.concatenate([_qk(kw, qT) + winb_ref[0:n_st, :], _qk(kwn_ref[...], qT) + winb_ref[n_st:, :]], axis=0)
    pwt = softmax_rows(s_w).T.astype(BF16)
    o_win = _dot(pwt[:, :n_st], vw) + _dot(pwt[:, n_st:], vwn_ref[...])

    blk = PAGE // BLOCK

    def mask_step(c, mx):
        r0 = pl.multiple_of(c * PAGE, PAGE)
        neg = jnp.concatenate(
            [jnp.broadcast_to(selneg_scr[pl.ds(c * blk + h, 1), :], (BLOCK, LANES)) for h in range(blk)], axis=0)
        s = s_scr[pl.ds(r0, PAGE), :] + neg
        s_scr[pl.ds(r0, PAGE), :] = s
        return jnp.maximum(mx, jnp.max(s, axis=0, keepdims=True))

    mx = lax.fori_loop(0, n_pages + 1, mask_step, jnp.full((1, LANES), NEG, F32))

    def pv_rows(r0, n_rows, carry):
        l_run, acc = carry
        pp = jnp.exp(s_scr[pl.ds(r0, n_rows), :] - mx)
        acc = acc + _dot(pp.T.astype(BF16), vsel[pl.ds(r0, n_rows), :])
        return l_run + jnp.sum(pp, axis=0, keepdims=True), acc

    pv_pages = 4 if n_pages % 4 == 0 else 1
    pv_keys = pv_pages * PAGE
    carry = lax.fori_loop(0, n_pages // pv_pages,
                          lambda c, carry: pv_rows(pl.multiple_of(c * pv_keys, pv_keys), pv_keys, carry),
                          (jnp.zeros((1, LANES), F32), jnp.zeros((LANES, N_GROUPS * HEAD_DIM), F32)))
    l_fin, acc = pv_rows(past, PAGE, carry)
    l_col = jnp.broadcast_to(l_fin, (LANES, LANES)).T[:, 0:1]
    o_sel = acc / l_col

    gates = _sigmoid(gate_ref[...])
    o_ref[...] = gates[:, 0:1] * o_cmp + gates[:, 1:2] * o_sel + gates[:, 2:3] * o_win


def _sample_nsa(pt, cache_pages, new_cmp, qT, knew, vnew, win_state, layer, kwn, vwn, pe, w1b, w2b,
                cb, nearb, farb, newb, winb, gates, nbp):
    bsz, n_pages = pt.shape
    t_new = new_cmp.shape[1]
    gh = N_GROUPS * HEAD_DIM
    n_keys = n_pages * PAGE + PAGE
    n_ring = min(SAMPLE_RING_PAGES, n_pages)

    def per_b(arr):
        shp = arr.shape[1:]
        return pl.BlockSpec((None,) + shp, lambda b, pt: (b,) + (0,) * len(shp))

    def const(arr, **kw):
        return pl.BlockSpec(arr.shape, lambda b, pt: (0,) * arr.ndim, **kw)

    grid_spec = pltpu.PrefetchScalarGridSpec(
        num_scalar_prefetch=1,
        grid=(bsz,),
        in_specs=[pl.BlockSpec(memory_space=pl.ANY),
                  per_b(new_cmp), per_b(qT), per_b(knew), per_b(vnew),
                  pl.BlockSpec((None,) + win_state.shape[1:], lambda b, pt: (layer * bsz + b, 0, 0)),
                  per_b(kwn), per_b(vwn),
                  const(pe), const(w1b, pipeline_mode=pl.Buffered(1)), const(w2b),
                  const(cb), const(nearb), const(farb), const(newb), const(winb), per_b(gates)],
        out_specs=pl.BlockSpec((None, LANES, gh), lambda b, pt: (b, 0, 0)),
        scratch_shapes=[pltpu.VMEM((n_ring,) + cache_pages.shape[1:], F32),
                        pltpu.VMEM((2 * N_GROUPS, nbp * XC_PITCH, HEAD_DIM), F32),
                        pltpu.VMEM((2, cb.shape[0], gh), F32),
                        pltpu.VMEM((n_keys, gh), BF16),
                        pltpu.VMEM((n_keys, LANES), F32),
                        pltpu.VMEM((cb.shape[0], LANES), F32),
                        pltpu.SemaphoreType.DMA((n_ring,))],
    )
    return pl.pallas_call(
        functools.partial(_sample_nsa_kernel, n_pages=n_pages, t_new=t_new, nbp=nbp, n_ring=n_ring),
        grid_spec=grid_spec,
        out_shape=jax.ShapeDtypeStruct((bsz, LANES, gh), F32),
        compiler_params=_cparams(("arbitrary",)),
        name="sample_nsa",
    )(pt, cache_pages, new_cmp, qT, knew, vnew, win_state, kwn, vwn, pe, w1b, w2b,
      cb, nearb, farb, newb, winb, gates)


def _gmlp_kernel(u_ref, v_ref, m_ref, bcol_ref, g_ref, b_ref, oa_ref, vn_ref):
    vn = _ln(v_ref[...]) * g_ref[...] + b_ref[...]
    vn_ref[...] = vn
    vb = vn.astype(BF16)
    u = u_ref[...]
    gd = GMLP_WIDTH // GMLP_GROUPS
    for g in range(GMLP_GROUPS):
        mixed = _dot(m_ref[g], vb[:, g * gd:(g + 1) * gd]) + bcol_ref[:, g:g + 1]
        oa_ref[:, g * gd:(g + 1) * gd] = (u[:, g * gd:(g + 1) * gd] * mixed).astype(oa_ref.dtype)


def _gmlp(z, mix_m, bcol, gn_g, gn_b, tc):
    t = z.shape[0]
    return pl.pallas_call(
        _gmlp_kernel,
        grid=(t // tc,),
        in_specs=[pl.BlockSpec((tc, GMLP_WIDTH), lambda i: (i, COL_U // GMLP_WIDTH)),
                  pl.BlockSpec((tc, GMLP_WIDTH), lambda i: (i, COL_V // GMLP_WIDTH)),
                  pl.BlockSpec((GMLP_GROUPS, tc, tc), lambda i: (0, 0, 0)),
                  pl.BlockSpec((tc, LANES), lambda i: (0, 0)),
                  pl.BlockSpec((1, GMLP_WIDTH), lambda i: (0, 0)),
                  pl.BlockSpec((1, GMLP_WIDTH), lambda i: (0, 0))],
        out_specs=[pl.BlockSpec((tc, GMLP_WIDTH), lambda i: (i, 0)),
                   pl.BlockSpec((tc, GMLP_WIDTH), lambda i: (i, 0))],
        out_shape=[jax.ShapeDtypeStruct((t, GMLP_WIDTH), BF16),
                   jax.ShapeDtypeStruct((t, GMLP_WIDTH), F32)],
        compiler_params=_cparams(("parallel",)),
        name="gmlp",
    )(z, z, mix_m, bcol, gn_g, gn_b)


def _pool_kernel(x_ref, halo_ref, pw_ref, ps_ref, o_ref, ext_scr, *, tm, zero_first, pos0, pos_step):
    i = pl.program_id(0)
    halo = halo_ref[...]
    if zero_first:
        halo = jnp.where(i == 0, 0.0, halo)
    ext_scr[0:16, :] = halo
    x = x_ref[...]
    ext_scr[16:16 + tm, :] = x
    pos = pos0 + i * pos_step + lax.broadcasted_iota(jnp.int32, (tm, POOL_GROUP_DIM), 0)
    for gi, w in enumerate(POOL_WINDOWS):
        c0 = gi * POOL_GROUP_DIM
        acc = ext_scr[16:16 + tm, c0:c0 + POOL_GROUP_DIM]
        for k in range(1, w):
            acc = acc + ext_scr[16 - k:16 - k + tm, c0:c0 + POOL_GROUP_DIM]
        count = jnp.minimum(pos + 1, w).astype(F32)
        d = acc / count - x[:, c0:c0 + POOL_GROUP_DIM]
        y = _dot(d.astype(BF16), pw_ref[gi]) * ps_ref[:, c0:c0 + POOL_GROUP_DIM]
        o_ref[:, c0:c0 + POOL_GROUP_DIM] = y.astype(o_ref.dtype)


def _pool(z, halo_src, halo_map, pwb, ps, tm, zero_first, pos0, pos_step):
    t = z.shape[0]
    return pl.pallas_call(
        functools.partial(_pool_kernel, tm=tm, zero_first=zero_first, pos0=pos0, pos_step=pos_step),
        grid=(t // tm,),
        in_specs=[pl.BlockSpec((tm, POOL_WIDTH), lambda i: (i, COL_XC // POOL_WIDTH)),
                  pl.BlockSpec((16, POOL_WIDTH), halo_map),
                  pl.BlockSpec((len(POOL_WINDOWS), POOL_GROUP_DIM, POOL_GROUP_DIM), lambda i: (0, 0, 0)),
                  pl.BlockSpec((1, POOL_WIDTH), lambda i: (0, 0))],
        out_specs=pl.BlockSpec((tm, POOL_WIDTH), lambda i: (i, 0)),
        out_shape=jax.ShapeDtypeStruct((t, POOL_WIDTH), BF16),
        scratch_shapes=[pltpu.VMEM((16 + tm, POOL_WIDTH), F32)],
        compiler_params=_cparams(("parallel",)),
        name="pool",
    )(z, halo_src, pwb, ps)


def _route(logits_t, rb_col):
    aff = _sigmoid(logits_t)
    biased = aff + rb_col
    rows = [biased[e:e + 1] for e in range(N_EXPERTS)]
    top2 = []
    gscore = []
    for g in range(N_EXPERT_GROUPS):
        grp = rows[g * EXPERTS_PER_GROUP:(g + 1) * EXPERTS_PER_GROUP]
        gs = None
        for a in range(EXPERTS_PER_GROUP):
            rank = None
            for c in range(EXPERTS_PER_GROUP):
                if c == a:
                    continue
                ahead = (grp[c] >= grp[a]) if c < a else (grp[c] > grp[a])
                rank = ahead.astype(F32) if rank is None else rank + ahead.astype(F32)
            in2 = rank < 1.5
            top2.append(in2)
            contrib = jnp.where(in2, grp[a], 0.0)
            gs = contrib if gs is None else gs + contrib
        gscore.append(gs)
    out = []
    for g in range(N_EXPERT_GROUPS):
        win = None
        for c in range(N_EXPERT_GROUPS):
            if c == g:
                continue
            ok = (gscore[g] > gscore[c]) if c < g else (gscore[g] >= gscore[c])
            win = ok if win is None else (win & ok)
        for a in range(EXPERTS_PER_GROUP):
            e = g * EXPERTS_PER_GROUP + a
            out.append(jnp.where(win & top2[e], aff[e:e + 1], 0.0))
    selw = jnp.concatenate(out, axis=0)
    return selw / jnp.sum(selw, axis=0, keepdims=True)


def _merge_kernel(x_ref, oa_ref, ob_ref, oc_ref, g0_ref, g1_ref, g2_ref, gt1_ref, sh2_ref, sc2_ref,
                  ln1g_ref, ln1b_ref, pa_ref, pb_ref, pc_ref, wo_ref, rw_ref, rb_ref,
                  x1_ref, h2_ref, comb_ref, *, alpha):
    tm = x_ref.shape[0]

    merged = _sigmoid(g0_ref[...]) * _dot(oa_ref[...], pa_ref[...])
    merged = merged + _sigmoid(g1_ref[...]) * _dot(ob_ref[...], pb_ref[...])
    merged = merged + _sigmoid(g2_ref[...]) * _dot(oc_ref[...], pc_ref[...])
    mix = _dot(merged.astype(BF16), wo_ref[...])
    x1 = _ln(alpha * x_ref[...] + (1.0 + _mod_rows(gt1_ref, tm)) * mix) * ln1g_ref[...] + ln1b_ref[...]
    x1_ref[...] = x1
    h2 = _ln(x1) * (1.0 + _mod_rows(sc2_ref, tm)) + _mod_rows(sh2_ref, tm)
    h2_ref[...] = h2.astype(BF16)
    logits = _dot(h2.astype(BF16), rw_ref[...])
    comb_t = _route(logits.T[:N_EXPERTS], rb_ref[...][:N_EXPERTS, 0:1])
    comb_full = jnp.concatenate([comb_t, jnp.zeros((LANES - N_EXPERTS, tm), F32)], axis=0)
    comb_ref[...] = comb_full.T


def _merge(x, z, oa, ob, oc, mods, ln1g, ln1b, pa, pb, pc, wo, rw, rb, tm, alpha):
    t = x.shape[0]

    def const(arr):
        return pl.BlockSpec(arr.shape, lambda i: (0,) * arr.ndim, pipeline_mode=pl.Buffered(1))

    gcol = COL_GM // D_MODEL
    return pl.pallas_call(
        functools.partial(_merge_kernel, alpha=alpha),
        grid=(t // tm,),
        in_specs=[pl.BlockSpec((tm, D_MODEL), lambda i: (i, 0)),
                  pl.BlockSpec((tm, GMLP_WIDTH), lambda i: (i, 0)),
                  pl.BlockSpec((tm, NSA_WIDTH), lambda i: (i, 0)),
                  pl.BlockSpec((tm, POOL_WIDTH), lambda i: (i, 0)),
                  pl.BlockSpec((tm, D_MODEL), lambda i: (i, gcol)),
                  pl.BlockSpec((tm, D_MODEL), lambda i: (i, gcol + 1)),
                  pl.BlockSpec((tm, D_MODEL), lambda i: (i, gcol + 2)),
                  _mod_spec(mods, tm, 2), _mod_spec(mods, tm, 3), _mod_spec(mods, tm, 4),
                  const(ln1g), const(ln1b), const(pa), const(pb), const(pc), const(wo), const(rw), const(rb)],
        out_specs=[pl.BlockSpec((tm, D_MODEL), lambda i: (i, 0)),
                   pl.BlockSpec((tm, D_MODEL), lambda i: (i, 0)),
                   pl.BlockSpec((tm, LANES), lambda i: (i, 0))],
        out_shape=[jax.ShapeDtypeStruct((t, D_MODEL), F32),
                   jax.ShapeDtypeStruct((t, D_MODEL), BF16),
                   jax.ShapeDtypeStruct((t, LANES), F32)],
        compiler_params=_cparams(("parallel",)),
        name="merge",
    )(x, oa, ob, oc, z, z, z, mods, mods, mods, ln1g, ln1b, pa, pb, pc, wo, rw, rb)


def _moe_kernel(h_ref, comb_ref, wg_ref, wu_ref, wd_ref, x1_ref, gt2_ref, g_ref, b_ref, o_ref, acc_scr, *, alpha):
    e = pl.program_id(1)
    tm = h_ref.shape[0]

    @pl.when(e == 0)
    def _():
        acc_scr[...] = jnp.zeros_like(acc_scr)

    h = h_ref[...]
    comb = comb_ref[...]
    lane = lax.broadcasted_iota(jnp.int32, comb.shape, 1)
    w_col = jnp.sum(jnp.where(lane == e, comb, 0.0), axis=1, keepdims=True)
    act = _silu(_dot(h, wg_ref[...])) * _dot(h, wu_ref[...])
    acc_scr[...] += _dot((act * w_col).astype(BF16), wd_ref[...])

    @pl.when(e == N_EXPERTS - 1)
    def _():
        o_ref[...] = _ln(alpha * x1_ref[...] + (1.0 + _mod_rows(gt2_ref, tm)) * acc_scr[...]) * g_ref[...] + b_ref[...]


def _moe(h2, comb, wg, wu, wd, x1, mods, ln2g, ln2b, tm, alpha):
    t = h2.shape[0]
    return pl.pallas_call(
        functools.partial(_moe_kernel, alpha=alpha),
        grid=(t // tm, N_EXPERTS),
        in_specs=[pl.BlockSpec((tm, D_MODEL), lambda i, e: (i, 0)),
                  pl.BlockSpec((tm, LANES), lambda i, e: (i, 0)),
                  pl.BlockSpec((None, D_MODEL, D_EXPERT), lambda i, e: (e, 0, 0)),
                  pl.BlockSpec((None, D_MODEL, D_EXPERT), lambda i, e: (e, 0, 0)),
                  pl.BlockSpec((None, D_EXPERT, D_MODEL), lambda i, e: (e, 0, 0)),
                  pl.BlockSpec((tm, D_MODEL), lambda i, e: (i, 0)),
                  _mod_spec(mods, tm, 5),
                  pl.BlockSpec((1, D_MODEL), lambda i, e: (0, 0)),
                  pl.BlockSpec((1, D_MODEL), lambda i, e: (0, 0))],
        out_specs=pl.BlockSpec((tm, D_MODEL), lambda i, e: (i, 0)),
        out_shape=jax.ShapeDtypeStruct((t, D_MODEL), F32),
        scratch_shapes=[pltpu.VMEM((tm, D_MODEL), F32)],
        compiler_params=_cparams(("parallel", "arbitrary")),
        name="moe",
    )(h2, comb, wg, wu, wd, x1, mods, ln2g, ln2b)


def _row_gather(src_ref, idx_ref, base, buf, slot, sem, n_rows, start):
    def body(r, c):
        cp = pltpu.make_async_copy(src_ref.at[pl.ds(idx_ref[base + r], 1), :],
                                   buf.at[slot, pl.ds(r, 1), :], sem.at[slot])
        if start:
            cp.start()
        else:
            cp.wait()
        return c

    lax.fori_loop(0, n_rows, body, 0)


def _row_scatter(buf, slot, dst_ref, idx_ref, live_ref, base, sem, n_rows, start):
    def body(r, c):
        @pl.when(live_ref[base + r] > 0)
        def _():
            cp = pltpu.make_async_copy(buf.at[slot, pl.ds(r, 1), :],
                                       dst_ref.at[pl.ds(idx_ref[base + r], 1), :], sem.at[slot])
            if start:
                cp.start()
            else:
                cp.wait()

        return c

    lax.fori_loop(0, n_rows, body, 0)


def _moe_group_kernel(perm_ref, live_ref, tgrp_ref, nlive_ref, x1_ref, comb_ref, sh2_ref, sc2_ref, gt2_ref,
                      g_ref, b_ref, wg_ref, wu_ref, wd_ref, o_ref, xbuf, obuf, sems, osems, *, alpha):
    k = pl.program_id(0)
    tg = comb_ref.shape[0]
    n_live = nlive_ref[0]

    @pl.when(k == 0)
    def _():
        _row_gather(x1_ref, perm_ref, 0, xbuf, 0, sems, tg, True)

    @pl.when(k + 1 < n_live)
    def _():
        _row_gather(x1_ref, perm_ref, (k + 1) * tg, xbuf, (k + 1) % 2, sems, tg, True)

    @pl.when(k < jnp.maximum(n_live, 1))
    def _():
        _row_gather(x1_ref, perm_ref, k * tg, xbuf, k % 2, sems, tg, False)

    @pl.when(k < n_live)
    def _():
        slot = k % 2

        @pl.when(k >= 2)
        def _():
            _row_scatter(obuf, slot, o_ref, perm_ref, live_ref, (k - 2) * tg, osems, tg, False)

        x1 = xbuf[slot]
        h = (_ln(x1) * (1.0 + sc2_ref[0:1, :]) + sh2_ref[0:1, :]).astype(BF16)
        comb = comb_ref[...]
        lane = lax.broadcasted_iota(jnp.int32, comb.shape, 1)
        e0 = tgrp_ref[k] * EXPERTS_PER_GROUP
        acc = jnp.zeros((tg, D_MODEL), F32)
        for e in range(EXPERTS_PER_GROUP):
            w_col = jnp.sum(jnp.where(lane == e0 + e, comb, 0.0), axis=1, keepdims=True)
            act = _silu(_dot(h, wg_ref[e])) * _dot(h, wu_ref[e])
            acc = acc + _dot((act * w_col).astype(BF16), wd_ref[e])
        obuf[slot] = _ln(alpha * x1 + (1.0 + gt2_ref[0:1, :]) * acc) * g_ref[...] + b_ref[...]
        _row_scatter(obuf, slot, o_ref, perm_ref, live_ref, k * tg, osems, tg, True)

        @pl.when(k == n_live - 1)
        def _():
            @pl.when(k >= 1)
            def _():
                _row_scatter(obuf, 1 - slot, o_ref, perm_ref, live_ref, (k - 1) * tg, osems, tg, False)

            _row_scatter(obuf, slot, o_ref, perm_ref, live_ref, k * tg, osems, tg, False)


def _moe_dispatch(comb, tg):
    t = comb.shape[0]
    i32 = jnp.int32
    cg = comb[:, :N_EXPERTS].reshape(t, N_EXPERT_GROUPS, EXPERTS_PER_GROUP)
    gid = jnp.argmax(jnp.sum(cg, axis=-1) > 0.0, axis=-1).astype(i32)
    onehot = (gid[:, None] == jnp.arange(N_EXPERT_GROUPS, dtype=i32)[None, :]).astype(i32)
    csum = jnp.cumsum(onehot, axis=0)
    rank = jnp.take_along_axis(csum, gid[:, None], axis=1)[:, 0] - 1
    pcount = ((csum[-1] + tg - 1) // tg) * tg
    ends = jnp.cumsum(pcount)
    pos = ((ends - pcount)[gid] + rank).astype(i32)
    rp = t + N_EXPERT_GROUPS * tg
    perm = jnp.full((rp,), -1, i32).at[pos].set(jnp.arange(t, dtype=i32))
    live = (perm >= 0).astype(i32)
    perm = jnp.maximum(perm, 0)
    comb_sorted = comb[perm] * live.astype(F32)[:, None]
    tile_start = jnp.arange(rp // tg, dtype=i32) * tg
    tile_group = jnp.minimum(jnp.sum(tile_start[:, None] >= ends[None, :], axis=1), N_EXPERT_GROUPS - 1).astype(i32)
    n_live = (ends[-1] // tg).astype(i32).reshape(1)
    return perm, live, tile_group, n_live, comb_sorted


def _moe_grouped(x1, comb, wg, wu, wd, mods, ln2g, ln2b, alpha):
    t = x1.shape[0]
    tg = MOE_GROUP_TILE
    perm, live, tile_group, n_live, comb_sorted = _moe_dispatch(comb, tg)
    rp = perm.shape[0]
    epg = EXPERTS_PER_GROUP

    def wspec(shape):
        return pl.BlockSpec((epg,) + shape, lambda k, perm, live, tgrp, nl: (tgrp[k], 0, 0),
                            pipeline_mode=pl.Buffered(1))

    def mod(col):
        return pl.BlockSpec((8, D_MODEL), lambda k, *_: (0, col))

    def row():
        return pl.BlockSpec((1, D_MODEL), lambda k, *_: (0, 0))

    return pl.pallas_call(
        functools.partial(_moe_group_kernel, alpha=alpha),
        grid_spec=pltpu.PrefetchScalarGridSpec(
            num_scalar_prefetch=4,
            grid=(rp // tg,),
            in_specs=[pl.BlockSpec(memory_space=pl.ANY),
                      pl.BlockSpec((tg, LANES), lambda k, *_: (k, 0)),
                      mod(3), mod(4), mod(5), row(), row(),
                      wspec((D_MODEL, D_EXPERT)), wspec((D_MODEL, D_EXPERT)), wspec((D_EXPERT, D_MODEL))],
            out_specs=pl.BlockSpec(memory_space=pl.ANY),
            scratch_shapes=[pltpu.VMEM((2, tg, D_MODEL), F32), pltpu.VMEM((2, tg, D_MODEL), F32),
                            pltpu.SemaphoreType.DMA((2,)), pltpu.SemaphoreType.DMA((2,))]),
        out_shape=jax.ShapeDtypeStruct((t, D_MODEL), F32),
        compiler_params=_cparams(("arbitrary",)),
        name="moe_group",
    )(perm, live, tile_group, n_live, x1, comb_sorted, mods, mods, mods, ln2g, ln2b, wg, wu, wd)


def _rel_bucket(dist):
    n = jnp.maximum(dist, 0)
    max_exact = N_BUCKETS // 2
    nf = jnp.maximum(n, max_exact).astype(F32)
    large = max_exact + (jnp.log(nf / max_exact) / math.log(MAX_DISTANCE / max_exact)
                         * (N_BUCKETS - max_exact)).astype(jnp.int32)
    return jnp.where(n < max_exact, n, jnp.minimum(large, N_BUCKETS - 1))


def _bias_of(rel_bias, dist):
    return rel_bias[_rel_bucket(dist)].astype(F32)


def _prompt_tiles(rel_bias):
    kk = jnp.arange(BLOCK)[:, None]
    qq = jnp.arange(BLOCK)[None, :]

    def lanes(tile8, g):
        return jnp.concatenate([tile8[:, :, g * N_REP + r] for r in range(N_REP)], axis=1)

    def tile(m, lo, hi):
        d = m * BLOCK + qq - kk
        ok = (d >= lo) & (d < hi)
        return jnp.where(ok[:, :, None], _bias_of(rel_bias, d), NEG)

    big = 1 << 20
    neg_tile = jnp.full((BLOCK, BLOCK, N_HEADS), NEG, F32)
    sel_tiles = [tile(m, 0, big) for m in range(4)] + [neg_tile]
    win_tiles = [tile(m, 0, WINDOW) for m in range(WIN_BLOCKS + 1)] + [neg_tile]
    st = jnp.stack([jnp.stack([lanes(t, g) for t in sel_tiles]) for g in range(N_GROUPS)]) / QK_SCALE
    wt = jnp.stack([jnp.stack([lanes(t, g) for t in win_tiles]) for g in range(N_GROUPS)]) / QK_SCALE
    rows = []
    for m in range(4):
        d = m * BLOCK + jnp.arange(BLOCK) - (BLOCK - 1)
        rows.append(_bias_of(rel_bias, d))
    ct = jnp.stack([jnp.stack([jnp.concatenate([rw[:, g * N_REP + r] for r in range(N_REP)]) for rw in rows]
                              + [jnp.zeros((N_REP * BLOCK,), F32)] * 4) for g in range(N_GROUPS)])
    return st, wt, ct


def _sample_lanes(tile8, t_new):
    rows = tile8.shape[0]
    x = jnp.transpose(tile8, (0, 2, 1)).reshape(rows, N_HEADS * t_new)
    return jnp.pad(x, ((0, 0), (0, LANES - N_HEADS * t_new)))


def _sample_tiles(rel_bias, past, t_new, nbp, win_rows, win_pad):
    tq = past + jnp.arange(t_new)[None, :]

    def tile(k_pos, ok_extra=None, lo=0, hi=1 << 20):
        d = tq - k_pos[:, None]
        ok = (d >= lo) & (d < hi)
        if ok_extra is not None:
            ok = ok & ok_extra[:, None]
        return _sample_lanes(jnp.where(ok[:, :, None], _bias_of(rel_bias, d), NEG), t_new)

    n = jnp.arange(nbp)
    cb = tile(n * BLOCK + BLOCK - 1, n < past // BLOCK + 1)
    nearb = tile(past - PAGE + jnp.arange(PAGE))
    farb = jnp.broadcast_to(_sample_lanes(jnp.broadcast_to(
        rel_bias[N_BUCKETS - 1].astype(F32)[None, None, :], (1, t_new, N_HEADS)), t_new), (8, LANES))
    kn = jnp.arange(PAGE)
    newb = tile(past + kn, kn < t_new)
    wb = min(WINDOW, past)
    j = jnp.arange(win_pad)
    k_pos = past - wb + j
    winb = tile(k_pos, (j < win_rows) & (k_pos >= 0), 0, WINDOW)
    return cb, nearb, farb, newb, winb


def _mix_mats(ws, bs, tc, rep):
    n = tc // rep
    causal = jnp.tril(jnp.ones((CHUNK, CHUNK), F32))
    wc = (ws * causal)[:, :n, :n]
    eye = jnp.eye(rep, dtype=F32)
    m = jnp.einsum('ab,gts->gatbs', eye, wc).reshape(GMLP_GROUPS, tc, tc)
    bcol = jnp.tile(bs[:, :n].T, (rep, 1))
    return m.astype(BF16), jnp.pad(bcol, ((0, 0), (0, LANES - GMLP_GROUPS)))


def kernel(x_prompt, x_sample, cache_nsa_kv, state_win_kv, state_pool, page_table, c_prompt, c_sample, rel_bias, router_w, router_b, w_in, nsa_phi_pe, nsa_phi_w1, nsa_phi_w2, gmlp_norm_g, gmlp_norm_b, gmlp_ws, gmlp_bs, pool_w, pool_scale, proj_a, proj_b, proj_c, w_o, ada_w, ada_b, ln1_g, ln1_b, ln2_g, ln2_b, exp_w_gate, exp_w_up, exp_w_down):
    depth = w_in.shape[0]
    alpha = (2 * depth) ** 0.25
    s = x_prompt.shape[1]
    bsz, t_new = x_sample.shape[0], x_sample.shape[1]
    n_pool = cache_nsa_kv.shape[1]
    n_pages = page_table.shape[1]
    past = n_pages * PAGE
    ts = bsz * t_new
    assert x_prompt.shape[0] == 1 and s % CHUNK_KEYS == 0 and past % PAGE == 0
    assert N_HEADS * t_new <= LANES and t_new <= BLOCK and (past // BLOCK) >= N_SEL
    nb_s = past // BLOCK + 1
    nbp = -(-nb_s // 8) * 8

    n_c = 1 + bsz
    c_all = jnp.pad(jnp.concatenate([c_prompt, c_sample], axis=0), ((0, -n_c % 8), (0, 0)))
    mods_all = _ada_mods(c_all, ada_w, ada_b)

    cache_pages = cache_nsa_kv.reshape(depth * n_pool, PAGE * PAGED_KINDS * N_GROUPS, HEAD_DIM)
    st, wt, ct = _prompt_tiles(rel_bias)
    n_win = state_win_kv.shape[2]
    win_rows = n_win + t_new
    win_pad = n_win + PAGE
    win_state = state_win_kv.reshape(depth * bsz, n_win * 2 * N_GROUPS, HEAD_DIM)
    assert nbp <= SAMPLE_BLOCKS_PAD and n_win == min(WINDOW, past) and n_win % 8 == 0
    cb_s, nearb_s, farb_s, newb_s, winb_s = _sample_tiles(rel_bias, past, t_new, SAMPLE_BLOCKS_PAD, win_rows, win_pad)
    rw = jnp.pad(router_w, ((0, 0), (0, LANES - N_EXPERTS))).astype(BF16)
    rb = jnp.pad(router_b.reshape(N_EXPERTS, 1), ((0, LANES - N_EXPERTS), (0, LANES - 1)))

    xp = x_prompt.reshape(s, D_MODEL)
    xs = x_sample.reshape(ts, D_MODEL)
    outs = {k: [] for k in ("nsa_p", "nsa_s", "win_p", "win_s", "pool_p", "pool_s", "v_s")}

    for l in range(depth):
        w_parts = _in_proj_weights(w_in[l])
        pe = nsa_phi_pe[l]
        w1b = nsa_phi_w1[l].astype(BF16)
        w2b = nsa_phi_w2[l].astype(BF16)
        pab, pbb, pcb, wob = (a[l].astype(BF16) for a in (proj_a, proj_b, proj_c, w_o))
        wg, wu, wd = (a[l].astype(BF16) for a in (exp_w_gate, exp_w_up, exp_w_down))
        pwb = pool_w[l].astype(BF16)
        ps = pool_scale[l].reshape(1, POOL_WIDTH)
        gng = gmlp_norm_g[l].reshape(1, GMLP_WIDTH)
        gnb = gmlp_norm_b[l].reshape(1, GMLP_WIDTH)
        ln1g, ln1b, ln2g, ln2b = (a[l].reshape(1, D_MODEL) for a in (ln1_g, ln1_b, ln2_g, ln2_b))
        mods_p = jnp.broadcast_to(mods_all[l, 0:1], (8, 6 * D_MODEL))
        mods_s = jnp.repeat(mods_all[l, 1:1 + bsz], t_new, axis=0)

        z, zb, kv_rows = _in_proj(xp, mods_p, w_parts, 1024)
        gh = N_GROUPS * HEAD_DIM
        outs["nsa_p"].append(kv_rows.reshape(1, s, PAGED_KINDS, N_GROUPS, HEAD_DIM))
        outs["win_p"].append(z[s - min(WINDOW, s):, COL_KV + PAGED_KINDS * gh:COL_KV + KV_KINDS * gh]
                             .reshape(1, min(WINDOW, s), 2, N_GROUPS, HEAD_DIM))
        outs["pool_p"].append(z[s - POOL_BUF:, COL_XC:COL_XC + POOL_WIDTH].reshape(1, POOL_BUF, POOL_WIDTH))

        nb = s // BLOCK
        cmp = _compress_prompt(z, pe, w1b, w2b).reshape(2, N_GROUPS, nb, HEAD_DIM)
        kc = cmp[0].astype(BF16)
        vcT = jnp.transpose(cmp[1], (0, 2, 1)).astype(BF16)
        nch = s // CHUNK_KEYS

        def kv_cols(kind, g):
            c0 = COL_KV + (kind * N_GROUPS + g) * HEAD_DIM
            return zb[:, c0:c0 + HEAD_DIM].reshape(nch, CHUNK_KEYS, HEAD_DIM)

        def k_chunks(kind):
            return jnp.stack([kv_cols(kind, g) for g in range(N_GROUPS)])

        def vT_chunks(kind):
            vt = jnp.stack([jnp.transpose(kv_cols(kind, g), (0, 2, 1)) for g in range(N_GROUPS)])
            ones = jnp.zeros((N_GROUPS, nch, 8, CHUNK_KEYS), BF16).at[:, :, 0].set(1.0)
            return jnp.concatenate([vt, ones], axis=2)

        q = zb[:, COL_Q:COL_Q + NSA_WIDTH].reshape(nb, BLOCK, N_GROUPS, N_REP, HEAD_DIM)
        qT = jnp.transpose(q, (2, 4, 0, 3, 1)).reshape(N_GROUPS, HEAD_DIM, nb * N_REP * BLOCK)
        gn = z[:, COL_GN:COL_GN + 3 * N_HEADS].reshape(nb, BLOCK, 3, N_GROUPS, N_REP)
        gates = jnp.transpose(gn, (3, 0, 2, 4, 1)).reshape(N_GROUPS, nb, 3, N_REP * BLOCK)
        gates = jnp.pad(gates, ((0, 0), (0, 0), (0, 5), (0, 0)))
        ob = _nsa_prompt(qT, kc, vcT, k_chunks(2), vT_chunks(3), k_chunks(4), vT_chunks(5), st, wt, ct, gates, s)

        mm, bcol = _mix_mats(gmlp_ws[l], gmlp_bs[l], CHUNK, 1)
        oa, _ = _gmlp(z, mm, bcol, gng, gnb, CHUNK)
        tp = 512
        cxb = COL_XC // POOL_WIDTH
        oc = _pool(z, z, lambda i: (jnp.maximum(i * (tp // 16) - 1, 0), cxb), pwb, ps, tp, True, 0, tp)
        x1, h2, comb = _merge(xp, z, oa, ob, oc, mods_p, ln1g, ln1b, pab, pbb, pcb, wob, rw, rb, 256, alpha)
        xp = _moe_grouped(x1, comb, wg, wu, wd, mods_p, ln2g, ln2b, alpha)

        zs, zbs, kv_rows_s = _in_proj(xs, mods_s, w_parts, ts)
        kvs = zs[:, COL_KV:COL_KV + 1536].reshape(bsz, t_new, KV_KINDS, N_GROUPS, HEAD_DIM)
        kvbs = zbs[:, COL_KV:COL_KV + 1536].reshape(bsz, t_new, KV_KINDS, gh)
        outs["nsa_s"].append(kv_rows_s.reshape(bsz, t_new, PAGED_KINDS, N_GROUPS, HEAD_DIM))
        win_full = jnp.concatenate([state_win_kv[l], kvs[:, :, PAGED_KINDS:]], axis=1)
        outs["win_s"].append(win_full[:, win_full.shape[1] - min(WINDOW, past + t_new):])
        xcs = zs[:, COL_XC:COL_XC + POOL_WIDTH].reshape(bsz, t_new, POOL_WIDTH)
        ext = jnp.concatenate([state_pool[l], xcs], axis=1)
        outs["pool_s"].append(ext[:, ext.shape[1] - POOL_BUF:])

        pt = page_table + l * n_pool
        new_cmp = kvs[:, :, 0:2].reshape(bsz, t_new, 2 * N_GROUPS * HEAD_DIM)
        knew = jnp.pad(kvbs[:, :, 2], ((0, 0), (0, PAGE - t_new), (0, 0)))
        vnew = jnp.pad(kvbs[:, :, 3], ((0, 0), (0, PAGE - t_new), (0, 0)))
        kwn = jnp.pad(kvbs[:, :, 4], ((0, 0), (0, PAGE - t_new), (0, 0)))
        vwn = jnp.pad(kvbs[:, :, 5], ((0, 0), (0, PAGE - t_new), (0, 0)))
        qs = zbs[:, COL_Q:COL_Q + NSA_WIDTH].reshape(bsz, t_new, N_GROUPS, N_REP, HEAD_DIM)
        qs = jnp.transpose(qs, (0, 2, 4, 3, 1)).reshape(bsz, N_GROUPS, HEAD_DIM, N_REP * t_new)
        n_q = N_REP * t_new
        qT_s = jnp.concatenate(
            [jnp.pad(qs[:, g], ((0, 0), (0, 0), (g * n_q, LANES - (g + 1) * n_q))) for g in range(N_GROUPS)], axis=1)
        gns = zs[:, COL_GN:COL_GN + 3 * N_HEADS].reshape(bsz, t_new, 3, N_HEADS)
        gates_s = jnp.transpose(gns, (0, 3, 1, 2)).reshape(bsz, N_HEADS * t_new, 3)
        gates_s = jnp.pad(gates_s, ((0, 0), (0, LANES - N_HEADS * t_new), (0, LANES - 3)))
        o_s = _sample_nsa(pt, cache_pages, new_cmp, qT_s, knew, vnew, win_state, l, kwn, vwn, pe, w1b, w2b,
                          cb_s, nearb_s, farb_s, newb_s, winb_s, gates_s, nbp)
        o4 = o_s[:, :N_HEADS * t_new].reshape(bsz, N_GROUPS, N_REP, t_new, N_GROUPS, HEAD_DIM)
        o4 = jnp.stack([o4[:, g, :, :, g] for g in range(N_GROUPS)], axis=1)
        ob_s = jnp.transpose(o4, (0, 3, 1, 2, 4)).reshape(ts, NSA_WIDTH).astype(BF16)

        mm_s, bcol_s = _mix_mats(gmlp_ws[l], gmlp_bs[l], ts, bsz)
        oa_s, vn_s = _gmlp(zs, mm_s, bcol_s, gng, gnb, ts)
        outs["v_s"].append(vn_s.reshape(bsz, t_new, GMLP_WIDTH))
        halo = jnp.pad(state_pool[l], ((0, 0), (16 - POOL_BUF, 0), (0, 0))).reshape(bsz * 16, POOL_WIDTH)
        oc_s = _pool(zs, halo, lambda i: (i, 0), pwb, ps, t_new, False, past, 0)
        x1s, h2s, comb_s = _merge(xs, zs, oa_s, ob_s, oc_s, mods_s, ln1g, ln1b, pab, pbb, pcb, wob, rw, rb, ts, alpha)
        xs = _moe(h2s, comb_s, wg, wu, wd, x1s, mods_s, ln2g, ln2b, ts, alpha)

    return (xp.reshape(1, s, D_MODEL), xs.reshape(bsz, t_new, D_MODEL),
            jnp.stack(outs["nsa_p"]), jnp.stack(outs["nsa_s"]), jnp.stack(outs["win_p"]), jnp.stack(outs["win_s"]),
            jnp.stack(outs["pool_p"]), jnp.stack(outs["pool_s"]), jnp.stack(outs["v_s"]))
```

```python
import functools
import math

import jax
import jax.numpy as jnp
from jax import lax
from jax.experimental import pallas as pl
from jax.experimental.pallas import tpu as pltpu

F32 = jnp.float32
BF16 = jnp.bfloat16

D_MODEL = 2048
HEAD_DIM = 128
N_GROUPS = 2
N_REP = 4
N_HEADS = N_GROUPS * N_REP
NSA_WIDTH = N_HEADS * HEAD_DIM
KV_KINDS = 6
PAGED_KINDS = 4
BLOCK = 64
N_SEL = 16
WINDOW = 512
WIN_BLOCKS = WINDOW // BLOCK
PHI_HIDDEN = 2 * HEAD_DIM
PAGE = 128
GMLP_GROUPS = 4
GMLP_WIDTH = 512
CHUNK = 128
POOL_WINDOWS = (2, 4, 8, 16)
POOL_GROUP_DIM = 128
POOL_WIDTH = 512
POOL_BUF = 15
N_BUCKETS = 32
MAX_DISTANCE = 128
N_EXPERTS = 16
N_EXPERT_GROUPS = 4
EXPERTS_PER_GROUP = 4
D_EXPERT = 512
LN_EPS = 1e-5
NEG = -1e30
FORCED_SCORE = 1e9

COL_Q = 0
COL_KV = 1024
COL_U = 2560
COL_V = 3072
COL_XC = 3584
COL_GM = 4096
COL_GN = 10240
N_IN_PAD = 10752
IN_TN = 512

LANES = 128
VMEM_LIMIT = 56 * 1024 * 1024
CHUNK_BLOCKS = 8
CHUNK_KEYS = CHUNK_BLOCKS * BLOCK
XC_PITCH = BLOCK + 8
SAMPLE_RING_PAGES = 16
SAMPLE_BLOCKS_PAD = 256
MOE_GROUP_TILE = 256
MOE_FINISH_TILE = 256


def _cparams(sem):
    return pltpu.CompilerParams(dimension_semantics=sem, vmem_limit_bytes=VMEM_LIMIT)


def _ln(x):
    mu = jnp.mean(x, axis=-1, keepdims=True)
    xc = x - mu
    var = jnp.mean(xc * xc, axis=-1, keepdims=True)
    return xc * lax.rsqrt(var + LN_EPS)


def _sigmoid(x):
    return 1.0 / (1.0 + jnp.exp(-x))


def _silu(x):
    return x * _sigmoid(x)


def _gelu_tanh(x):
    return 0.5 * x * (1.0 + jnp.tanh(math.sqrt(2.0 / math.pi) * (x + 0.044715 * (x * x * x))))


def _dot(a, b):
    return jnp.dot(a, b, preferred_element_type=F32)


QK_SCALE = HEAD_DIM ** -0.5
EXP2_SCALE = QK_SCALE * math.log2(math.e)


def _qk(k, qT):
    return _dot(k, qT) * QK_SCALE


def _ada_kernel(c_ref, w_ref, b_ref, o_ref):
    c = c_ref[...]
    o_ref[...] = _dot(_silu(c).astype(BF16), w_ref[...].astype(BF16)) + b_ref[...]


def _ada_mods(c_all, ada_w, ada_b):
    depth = ada_w.shape[0]
    rows = c_all.shape[0]
    n = ada_w.shape[2]
    tn = 1024
    return pl.pallas_call(
        _ada_kernel,
        grid=(depth, n // tn),
        in_specs=[pl.BlockSpec((rows, D_MODEL), lambda l, j: (0, 0)),
                  pl.BlockSpec((None, D_MODEL, tn), lambda l, j: (l, 0, j)),
                  pl.BlockSpec((None, 1, tn), lambda l, j: (l, 0, j))],
        out_specs=pl.BlockSpec((None, rows, tn), lambda l, j: (l, 0, j)),
        out_shape=jax.ShapeDtypeStruct((depth, rows, n), F32),
        compiler_params=_cparams(("parallel", "parallel")),
        name="ada_mods",
    )(c_all, ada_w, ada_b.reshape(depth, 1, n))


def _mod_rows(ref, tm):
    return ref[...] if ref.shape[0] == tm else ref[0:1, :]


N_QKV_TILES = COL_U // IN_TN
N_MAIN_TILES = COL_GN // IN_TN


def _inproj_kernel(x_ref, sh_ref, sc_ref, wa_ref, wb_ref, wc_ref, z_ref, zb_ref, h_scr):
    tm = x_ref.shape[0]
    j = pl.program_id(1)

    @pl.when(j == 0)
    def _():
        h = _ln(x_ref[...]) * (1.0 + _mod_rows(sc_ref, tm)) + _mod_rows(sh_ref, tm)
        h_scr[...] = h.astype(BF16)

    @pl.when(j < N_QKV_TILES)
    def _():
        z = _dot(h_scr[...], wa_ref[...])
        z_ref[...] = z
        zb_ref[...] = z.astype(BF16)

    @pl.when((j >= N_QKV_TILES) & (j < N_MAIN_TILES))
    def _():
        z_ref[...] = _dot(h_scr[...], wb_ref[...])

    @pl.when(j == N_MAIN_TILES)
    def _():
        z_ref[...] = _dot(h_scr[...], wc_ref[...])


def _mod_spec(mods, tm, col):
    rows = mods.shape[0]
    if rows == 8:
        return pl.BlockSpec((8, D_MODEL), lambda i, *_: (0, col))
    return pl.BlockSpec((tm, D_MODEL), lambda i, *_: (i, col))


def _in_proj_weights(w):
    n_gate = 3 * N_HEADS
    wa = w[:, :COL_U].astype(BF16)
    wb = w[:, COL_U + n_gate:].astype(BF16)
    wc = jnp.pad(w[:, COL_U:COL_U + n_gate], ((0, 0), (0, IN_TN - n_gate))).astype(BF16)
    return wa, wb, wc


def _in_proj(x, mods, weights, tm):
    t = x.shape[0]
    wa, wb, wc = weights
    n_b = N_MAIN_TILES - N_QKV_TILES
    return pl.pallas_call(
        _inproj_kernel,
        grid=(t // tm, N_IN_PAD // IN_TN),
        in_specs=[pl.BlockSpec((tm, D_MODEL), lambda i, j: (i, 0)),
                  _mod_spec(mods, tm, 0), _mod_spec(mods, tm, 1),
                  pl.BlockSpec((D_MODEL, IN_TN), lambda i, j: (0, jnp.minimum(j, N_QKV_TILES - 1))),
                  pl.BlockSpec((D_MODEL, IN_TN), lambda i, j: (0, jnp.clip(j - N_QKV_TILES, 0, n_b - 1))),
                  pl.BlockSpec((D_MODEL, IN_TN), lambda i, j: (0, 0))],
        out_specs=[pl.BlockSpec((tm, IN_TN), lambda i, j: (i, j)),
                   pl.BlockSpec((tm, IN_TN), lambda i, j: (i, jnp.minimum(j, N_QKV_TILES - 1)))],
        out_shape=[jax.ShapeDtypeStruct((t, N_IN_PAD), F32),
                   jax.ShapeDtypeStruct((t, COL_U), BF16)],
        scratch_shapes=[pltpu.VMEM((tm, D_MODEL), BF16)],
        compiler_params=_cparams(("parallel", "arbitrary")),
        name="in_proj",
    )(x, mods, mods, wa, wb, wc)


def _compress_pair(x0_ref, x1_ref, nb, pe_ref, w1_ref, w2_ref):
    def step(p, acc):
        pe = pe_ref[pl.ds(p, 1), :]
        a0 = x0_ref[pl.ds(p, nb, stride=BLOCK), :] + pe
        a1 = x1_ref[pl.ds(p, nb, stride=BLOCK), :] + pe
        a = jnp.concatenate([a0, a1], axis=0).astype(BF16)
        w = w1_ref[pl.ds(pl.multiple_of(p * HEAD_DIM, HEAD_DIM), HEAD_DIM), :]
        return acc + _dot(a, w)

    acc = lax.fori_loop(0, BLOCK, step, jnp.zeros((2 * nb, PHI_HIDDEN), F32))
    return _dot(_gelu_tanh(acc).astype(BF16), w2_ref[...])


def _compress_prompt_kernel(x0_ref, x1_ref, pe_ref, w1_ref, w2_ref, o_ref, *, nb):
    o_ref[...] = _compress_pair(x0_ref, x1_ref, nb, pe_ref, w1_ref, w2_ref)


def _compress_prompt(z, pe, w1b, w2b):
    s = z.shape[0]
    nb = s // BLOCK
    cb = COL_KV // HEAD_DIM
    return pl.pallas_call(
        functools.partial(_compress_prompt_kernel, nb=nb),
        grid=(2,),
        in_specs=[pl.BlockSpec((s, HEAD_DIM), lambda k: (0, cb + 2 * k)),
                  pl.BlockSpec((s, HEAD_DIM), lambda k: (0, cb + 2 * k + 1)),
                  pl.BlockSpec((None, BLOCK, HEAD_DIM), lambda k: (k, 0, 0)),
                  pl.BlockSpec((None, BLOCK * HEAD_DIM, PHI_HIDDEN), lambda k: (k, 0, 0)),
                  pl.BlockSpec((None, PHI_HIDDEN, HEAD_DIM), lambda k: (k, 0, 0))],
        out_specs=pl.BlockSpec((None, 2 * nb, HEAD_DIM), lambda k: (k, 0, 0)),
        out_shape=jax.ShapeDtypeStruct((2, 2 * nb, HEAD_DIM), F32),
        compiler_params=_cparams(("parallel",)),
        name="compress_prompt",
    )(z, z, pe, w1b, w2b)


N_FORCED = 3


def _select_blocks(score, forced, n_io):
    def body(_, carry):
        work, sel = carry
        mval = jnp.max(work, axis=0, keepdims=True)
        cand = jnp.where(work == mval, n_io, jnp.int32(1 << 30))
        idx = jnp.min(cand, axis=0, keepdims=True)
        pick = n_io == idx
        return jnp.where(pick, -jnp.inf, work), jnp.where(pick, 1.0, sel)

    init = (jnp.where(forced, -jnp.inf, score), jnp.where(forced, 1.0, 0.0))
    _, sel = lax.fori_loop(0, N_SEL - N_FORCED, body, init)
    return sel


def _nsa_prompt_kernel(qT_ref, kc_ref, vcT_ref, ksel_ref, vselT_ref, kwin_ref, vwinT_ref,
                       st_ref, wt_ref, ct_ref, gate_ref, o_ref, selneg_scr, selfar_scr, s_a, s_b, p_a, p_b, *, nb):
    i = pl.program_id(1)
    qT = qT_ref[...]
    nl = N_REP * BLOCK

    n_io = lax.broadcasted_iota(jnp.int32, (nb, nl), 0)
    qq = lax.broadcasted_iota(jnp.int32, (nb, nl), 1) % BLOCK
    m = i - n_io
    ct = ct_ref[...]
    bias = jnp.where(m == 0, ct[0:1], jnp.where(m == 1, ct[1:2], jnp.where(m == 2, ct[2:3], ct[3:4])))
    valid = (m >= 1) | ((m == 0) & (qq == BLOCK - 1))
    s = jnp.where(valid, _qk(kc_ref[...], qT) + bias, NEG)
    mx = jnp.max(s, axis=0, keepdims=True)
    p = jnp.where(valid, jnp.exp(s - mx), 0.0)
    l = jnp.sum(p, axis=0, keepdims=True)
    pn = p / jnp.where(l > 0.0, l, 1.0)
    o_cmp = _dot(vcT_ref[...], pn.astype(BF16))

    half = pn[:, :LANES] + pn[:, LANES:]
    imp = half + pltpu.roll(half, BLOCK, 1)
    n_io1 = n_io[:, :LANES]
    forced = (n_io1 == 0) | (n_io1 == i) | (n_io1 == i - 1)
    score = jnp.where(forced, FORCED_SCORE, imp)
    score = jnp.where(n_io1 <= i, score, -1.0)
    sel = _select_blocks(score, forced, n_io1)
    neg1 = jnp.where((sel > 0.5) & (score >= 0.0), 0.0, NEG)
    neg2 = jnp.concatenate([neg1, neg1], axis=1)
    selneg_scr[...] = neg2
    far_row = st_ref[st_ref.shape[0] - 2][0:1, :]
    selfar_scr[...] = jnp.where(neg2 == 0.0, far_row, NEG)

    c_i = i // CHUNK_BLOCKS

    def branch(k_ref, vT_ref, tile_ref, n_tiles, use_sel, n_chunks):
        def logits(c, dst):
            dst[...] = _dot(k_ref[jnp.maximum(c, 0)], qT)

        def pv(c, p_ref):
            return _dot(vT_ref[jnp.clip(c, 0, vT_ref.shape[0] - 1)], p_ref[...])

        def consume(c, src, m_run, near):
            live = c >= 0
            cz = jnp.maximum(c, 0)
            parts = []
            for b in range(CHUNK_BLOCKS):
                n = cz * CHUNK_BLOCKS + b
                sb = src[b * BLOCK:(b + 1) * BLOCK, :]
                if near:
                    mm = i - n
                    if use_sel:
                        tidx = jnp.where(mm < 0, n_tiles - 1, jnp.minimum(mm, n_tiles - 2))
                    else:
                        tidx = jnp.where((mm < 0) | (mm > n_tiles - 2), n_tiles - 1, mm)
                    sb = sb + tile_ref[jnp.where(live, tidx, n_tiles - 1)]
                    if use_sel:
                        sb = sb + selneg_scr[pl.ds(n, 1), :]
                else:
                    sb = sb + jnp.where(live, selfar_scr[pl.ds(n, 1), :], NEG)
                parts.append(sb)
            s2 = jnp.concatenate(parts, axis=0)
            m_new = jnp.maximum(m_run, jnp.max(s2, axis=0, keepdims=True))
            alpha = jnp.exp2((m_run - m_new) * EXP2_SCALE)
            pp = jnp.exp2((s2 - m_new) * EXP2_SCALE)
            return m_new, alpha, pp.astype(BF16)

        def pair(t, carry, near):
            m_run, acc = carry
            c0 = c_i - 2 * t
            logits(c0 - 1, s_b)
            r_prev = pv(c0 + 1, p_b)
            m_run, alpha, pp = consume(c0, s_a, m_run, near)
            p_a[...] = pp
            acc = alpha * (acc + r_prev)
            logits(c0 - 2, s_a)
            r_cur = pv(c0, p_a)
            m_run, alpha, pp = consume(c0 - 1, s_b, m_run, near)
            p_b[...] = pp
            return m_run, alpha * (acc + r_cur)

        logits(c_i, s_a)
        p_b[...] = jnp.zeros(p_b.shape, BF16)
        carry = (jnp.full((1, nl), NEG, F32), jnp.zeros((vT_ref.shape[1], nl), F32))
        carry = pair(0, carry, True)
        n_pairs = (n_chunks + 1) // 2 if use_sel else 1
        if use_sel:
            carry = lax.fori_loop(1, n_pairs, lambda t, cr: pair(t, cr, False), carry)
        acc = carry[1] + pv(c_i - 2 * n_pairs + 1, p_b)
        return acc[:HEAD_DIM] / acc[HEAD_DIM:HEAD_DIM + 1]

    o_sel = branch(ksel_ref, vselT_ref, st_ref, st_ref.shape[0], True, c_i + 1)
    o_win = branch(kwin_ref, vwinT_ref, wt_ref, wt_ref.shape[0], False, jnp.minimum(c_i + 1, 2))

    gates = _sigmoid(gate_ref[...])
    oT = gates[0:1] * o_cmp + gates[1:2] * o_sel + gates[2:3] * o_win
    o = oT.T
    for r in range(N_REP):
        o_ref[:, r * HEAD_DIM:(r + 1) * HEAD_DIM] = o[r * BLOCK:(r + 1) * BLOCK].astype(o_ref.dtype)


def _nsa_prompt(qT, kc, vcT, ksel, vselT, kwin, vwinT, st, wt, ct, gates, s):
    nb = s // BLOCK
    nq = s // BLOCK
    nl = N_REP * BLOCK
    nch = s // CHUNK_KEYS

    def whole(arr):
        shp = arr.shape[1:]
        return pl.BlockSpec((None,) + shp, lambda g, i: (g,) + (0,) * len(shp))

    return pl.pallas_call(
        functools.partial(_nsa_prompt_kernel, nb=nb),
        grid=(N_GROUPS, nq),
        in_specs=[pl.BlockSpec((None, HEAD_DIM, nl), lambda g, i: (g, 0, i)),
                  whole(kc), whole(vcT), whole(ksel), whole(vselT), whole(kwin), whole(vwinT),
                  whole(st), whole(wt), whole(ct),
                  pl.BlockSpec((None, None, 8, nl), lambda g, i: (g, i, 0, 0))],
        out_specs=pl.BlockSpec((BLOCK, N_REP * HEAD_DIM), lambda g, i: (i, g)),
        out_shape=jax.ShapeDtypeStruct((s, NSA_WIDTH), BF16),
        scratch_shapes=[pltpu.VMEM((nb, nl), F32),
                        pltpu.VMEM((nb, nl), F32),
                        pltpu.VMEM((CHUNK_KEYS, nl), F32),
                        pltpu.VMEM((CHUNK_KEYS, nl), F32),
                        pltpu.VMEM((CHUNK_KEYS, nl), BF16),
                        pltpu.VMEM((CHUNK_KEYS, nl), BF16)],
        compiler_params=_cparams(("parallel", "arbitrary")),
        name="nsa_prompt",
    )(qT, kc, vcT, ksel, vselT, kwin, vwinT, st, wt, ct, gates)


def _sample_nsa_kernel(pt_ref, cache_ref, newc_ref, qT_ref, knew_ref, vnew_ref, wst_ref, kwn_ref, vwn_ref,
                       pe_ref, w1_ref, w2_ref, cb_ref, nearb_ref, farb_ref, newb_ref, winb_ref, gate_ref, o_ref,
                       ring, xc, kcs, vsel, s_scr, selneg_scr, sems, *, n_pages, t_new, nbp, n_ring):
    b = pl.program_id(0)
    total = pl.num_programs(0) * n_pages
    past = n_pages * PAGE
    cur = past // BLOCK
    kinds_per_row = PAGED_KINDS * N_GROUPS
    qT = qT_ref[...]

    def page_dma(t, slot):
        return pltpu.make_async_copy(cache_ref.at[pt_ref[t // n_pages, t % n_pages]], ring.at[slot], sems.at[slot])

    @pl.when(b == 0)
    def _():
        for t in range(n_ring):
            page_dma(t, t).start()

    far = farb_ref[0:1, :]

    def page_step(j, c):
        t = b * n_pages + j
        slot = t % n_ring
        page_dma(t, slot).wait()
        buf = ring.at[slot]

        def rows(kg):
            return buf[pl.ds(kg, PAGE, stride=kinds_per_row), :]

        r0 = pl.multiple_of(j * PAGE, PAGE)
        base = pl.multiple_of(j * (PAGE // BLOCK) * XC_PITCH, 8)
        for kg in range(2 * N_GROUPS):
            x = rows(kg)
            for h in range(PAGE // BLOCK):
                xc[kg, pl.ds(base + h * XC_PITCH, BLOCK), :] = x[h * BLOCK:(h + 1) * BLOCK]
        k = jnp.concatenate([rows(4), rows(5)], axis=1).astype(BF16)
        s_scr[pl.ds(r0, PAGE), :] = _qk(k, qT) + jnp.where(j == n_pages - 1, nearb_ref[...], far)
        vsel[pl.ds(r0, PAGE), :] = jnp.concatenate([rows(6), rows(7)], axis=1).astype(BF16)

        @pl.when(t + n_ring < total)
        def _():
            page_dma(t + n_ring, slot).start()

        return c

    lax.fori_loop(0, n_pages, page_step, 0)

    tail = (nbp - cur) * XC_PITCH
    for kg in range(2 * N_GROUPS):
        xc[kg, pl.ds(cur * XC_PITCH, tail), :] = jnp.zeros((tail, HEAD_DIM), F32)
        xc[kg, pl.ds(cur * XC_PITCH, t_new), :] = newc_ref[:, kg * HEAD_DIM:(kg + 1) * HEAD_DIM]
    s_scr[pl.ds(past, PAGE), :] = _qk(knew_ref[...], qT) + newb_ref[...]
    vsel[pl.ds(past, PAGE), :] = vnew_ref[...]

    for kind in range(2):
        def step(pp, acc):
            p = 2 * pp
            pe0 = pe_ref[kind, pl.ds(p, 1), :]
            pe1 = pe_ref[kind, pl.ds(p + 1, 1), :]
            parts = []
            for g in range(N_GROUPS):
                x0 = xc[2 * kind + g, pl.ds(p, nbp, stride=XC_PITCH), :] + pe0
                x1 = xc[2 * kind + g, pl.ds(p + 1, nbp, stride=XC_PITCH), :] + pe1
                parts.append(jnp.concatenate([x0, x1], axis=1))
            a = jnp.concatenate(parts, axis=0).astype(BF16)
            w = w1_ref[kind, pl.ds(pl.multiple_of(p * HEAD_DIM, 2 * HEAD_DIM), 2 * HEAD_DIM), :]
            return acc + _dot(a, w)

        acc = lax.fori_loop(0, BLOCK // 2, step, jnp.zeros((N_GROUPS * nbp, PHI_HIDDEN), F32), unroll=2)
        out = _dot(_gelu_tanh(acc).astype(BF16), w2_ref[kind])
        kcs[kind, nbp:, :] = jnp.zeros((kcs.shape[1] - nbp, N_GROUPS * HEAD_DIM), F32)
        for g in range(N_GROUPS):
            kcs[kind, 0:nbp, g * HEAD_DIM:(g + 1) * HEAD_DIM] = out[g * nbp:(g + 1) * nbp]

    def softmax_rows(s):
        mx = jnp.max(s, axis=0, keepdims=True)
        p = jnp.where(s > 0.5 * NEG, jnp.exp(s - mx), 0.0)
        l = jnp.sum(p, axis=0, keepdims=True)
        return p / jnp.where(l > 0.0, l, 1.0)

    pn = softmax_rows(_qk(kcs[0].astype(BF16), qT) + cb_ref[...])
    o_cmp = _dot(pn.T.astype(BF16), kcs[1].astype(BF16))

    tot = pn + pltpu.roll(pn, t_new, 1) + pltpu.roll(pn, 2 * t_new, 1) + pltpu.roll(pn, 3 * t_new, 1)
    lane = lax.broadcasted_iota(jnp.int32, pn.shape, 1)
    top = jnp.where((lane // t_new) % N_REP == N_REP - 1, tot, 0.0)
    imp = top + pltpu.roll(top, LANES - t_new, 1) + pltpu.roll(top, LANES - 2 * t_new, 1) \
        + pltpu.roll(top, LANES - 3 * t_new, 1)
    n_io = lax.broadcasted_iota(jnp.int32, pn.shape, 0)
    forced = (n_io == 0) | (n_io == cur) | (n_io == cur - 1)
    score = jnp.where(forced, FORCED_SCORE, imp)
    score = jnp.where(n_io <= cur, score, -jnp.inf)
    sel = _select_blocks(score, forced, n_io)
    selneg_scr[...] = jnp.where(sel > 0.5, 0.0, NEG)

    n_st = wst_ref.shape[0] // (2 * N_GROUPS)

    def st_rows(j):
        return wst_ref[pl.ds(j, n_st, stride=2 * N_GROUPS), :]

    kw = jnp.concatenate([st_rows(0), st_rows(1)], axis=1).astype(BF16)
    vw = jnp.concatenate([st_rows(2), st_rows(3)], axis=1).astype(BF16)
    s_w = jnp.concatenate([_qk(kw, qT) + winb_ref[0:n_st, :], _qk(kwn_ref[...], qT) + winb_ref[n_st:, :]], axis=0)
    pwt = softmax_rows(s_w).T.astype(BF16)
    o_win = _dot(pwt[:, :n_st], vw) + _dot(pwt[:, n_st:], vwn_ref[...])

    blk = PAGE // BLOCK

    def mask_step(c, mx):
        r0 = pl.multiple_of(c * PAGE, PAGE)
        neg = jnp.concatenate(
            [jnp.broadcast_to(selneg_scr[pl.ds(c * blk + h, 1), :], (BLOCK, LANES)) for h in range(blk)], axis=0)
        s = s_scr[pl.ds(r0, PAGE), :] + neg
        s_scr[pl.ds(r0, PAGE), :] = s
        return jnp.maximum(mx, jnp.max(s, axis=0, keepdims=True))

    mx = lax.fori_loop(0, n_pages + 1, mask_step, jnp.full((1, LANES), NEG, F32))

    def pv_rows(r0, n_rows, carry):
        l_run, acc = carry
        pp = jnp.exp(s_scr[pl.ds(r0, n_rows), :] - mx)
        acc = acc + _dot(pp.T.astype(BF16), vsel[pl.ds(r0, n_rows), :])
        return l_run + jnp.sum(pp, axis=0, keepdims=True), acc

    pv_pages = 8 if n_pages % 8 == 0 else 1
    pv_keys = pv_pages * PAGE
    carry = lax.fori_loop(0, n_pages // pv_pages,
                          lambda c, carry: pv_rows(pl.multiple_of(c * pv_keys, pv_keys), pv_keys, carry),
                          (jnp.zeros((1, LANES), F32), jnp.zeros((LANES, N_GROUPS * HEAD_DIM), F32)))
    l_fin, acc = pv_rows(past, PAGE, carry)
    l_col = jnp.broadcast_to(l_fin, (LANES, LANES)).T[:, 0:1]
    o_sel = acc / l_col

    gates = _sigmoid(gate_ref[...])
    o_ref[...] = gates[:, 0:1] * o_cmp + gates[:, 1:2] * o_sel + gates[:, 2:3] * o_win


def _sample_nsa(pt, cache_pages, new_cmp, qT, knew, vnew, win_state, layer, kwn, vwn, pe, w1b, w2b,
                cb, nearb, farb, newb, winb, gates, nbp):
    bsz, n_pages = pt.shape
    t_new = new_cmp.shape[1]
    gh = N_GROUPS * HEAD_DIM
    n_keys = n_pages * PAGE + PAGE
    n_ring = min(SAMPLE_RING_PAGES, n_pages)

    def per_b(arr):
        shp = arr.shape[1:]
        return pl.BlockSpec((None,) + shp, lambda b, pt: (b,) + (0,) * len(shp))

    def const(arr, **kw):
        return pl.BlockSpec(arr.shape, lambda b, pt: (0,) * arr.ndim, **kw)

    grid_spec = pltpu.PrefetchScalarGridSpec(
        num_scalar_prefetch=1,
        grid=(bsz,),
        in_specs=[pl.BlockSpec(memory_space=pl.ANY),
                  per_b(new_cmp), per_b(qT), per_b(knew), per_b(vnew),
                  pl.BlockSpec((None,) + win_state.shape[1:], lambda b, pt: (layer * bsz + b, 0, 0)),
                  per_b(kwn), per_b(vwn),
                  const(pe), const(w1b, pipeline_mode=pl.Buffered(1)), const(w2b),
                  const(cb), const(nearb), const(farb), const(newb), const(winb), per_b(gates)],
        out_specs=pl.BlockSpec((None, LANES, gh), lambda b, pt: (b, 0, 0)),
        scratch_shapes=[pltpu.VMEM((n_ring,) + cache_pages.shape[1:], F32),
                        pltpu.VMEM((2 * N_GROUPS, nbp * XC_PITCH, HEAD_DIM), F32),
                        pltpu.VMEM((2, cb.shape[0], gh), F32),
                        pltpu.VMEM((n_keys, gh), BF16),
                        pltpu.VMEM((n_keys, LANES), F32),
                        pltpu.VMEM((cb.shape[0], LANES), F32),
                        pltpu.SemaphoreType.DMA((n_ring,))],
    )
    return pl.pallas_call(
        functools.partial(_sample_nsa_kernel, n_pages=n_pages, t_new=t_new, nbp=nbp, n_ring=n_ring),
        grid_spec=grid_spec,
        out_shape=jax.ShapeDtypeStruct((bsz, LANES, gh), F32),
        compiler_params=_cparams(("arbitrary",)),
        name="sample_nsa",
    )(pt, cache_pages, new_cmp, qT, knew, vnew, win_state, kwn, vwn, pe, w1b, w2b,
      cb, nearb, farb, newb, winb, gates)


def _gmlp_kernel(u_ref, v_ref, m_ref, bcol_ref, g_ref, b_ref, oa_ref, vn_ref):
    vn = _ln(v_ref[...]) * g_ref[...] + b_ref[...]
    vn_ref[...] = vn
    vb = vn.astype(BF16)
    u = u_ref[...]
    gd = GMLP_WIDTH // GMLP_GROUPS
    for g in range(GMLP_GROUPS):
        mixed = _dot(m_ref[g], vb[:, g * gd:(g + 1) * gd]) + bcol_ref[:, g:g + 1]
        oa_ref[:, g * gd:(g + 1) * gd] = (u[:, g * gd:(g + 1) * gd] * mixed).astype(oa_ref.dtype)


def _gmlp(z, mix_m, bcol, gn_g, gn_b, tc):
    t = z.shape[0]
    return pl.pallas_call(
        _gmlp_kernel,
        grid=(t // tc,),
        in_specs=[pl.BlockSpec((tc, GMLP_WIDTH), lambda i: (i, COL_U // GMLP_WIDTH)),
                  pl.BlockSpec((tc, GMLP_WIDTH), lambda i: (i, COL_V // GMLP_WIDTH)),
                  pl.BlockSpec((GMLP_GROUPS, tc, tc), lambda i: (0, 0, 0)),
                  pl.BlockSpec((tc, LANES), lambda i: (0, 0)),
                  pl.BlockSpec((1, GMLP_WIDTH), lambda i: (0, 0)),
                  pl.BlockSpec((1, GMLP_WIDTH), lambda i: (0, 0))],
        out_specs=[pl.BlockSpec((tc, GMLP_WIDTH), lambda i: (i, 0)),
                   pl.BlockSpec((tc, GMLP_WIDTH), lambda i: (i, 0))],
        out_shape=[jax.ShapeDtypeStruct((t, GMLP_WIDTH), BF16),
                   jax.ShapeDtypeStruct((t, GMLP_WIDTH), F32)],
        compiler_params=_cparams(("parallel",)),
        name="gmlp",
    )(z, z, mix_m, bcol, gn_g, gn_b)


def _pool_kernel(x_ref, halo_ref, pw_ref, ps_ref, o_ref, ext_scr, *, tm, zero_first, pos0, pos_step):
    i = pl.program_id(0)
    halo = halo_ref[...]
    if zero_first:
        halo = jnp.where(i == 0, 0.0, halo)
    ext_scr[0:16, :] = halo
    x = x_ref[...]
    ext_scr[16:16 + tm, :] = x
    pos = pos0 + i * pos_step + lax.broadcasted_iota(jnp.int32, (tm, POOL_GROUP_DIM), 0)
    for gi, w in enumerate(POOL_WINDOWS):
        c0 = gi * POOL_GROUP_DIM
        acc = ext_scr[16:16 + tm, c0:c0 + POOL_GROUP_DIM]
        for k in range(1, w):
            acc = acc + ext_scr[16 - k:16 - k + tm, c0:c0 + POOL_GROUP_DIM]
        count = jnp.minimum(pos + 1, w).astype(F32)
        d = acc / count - x[:, c0:c0 + POOL_GROUP_DIM]
        y = _dot(d.astype(BF16), pw_ref[gi]) * ps_ref[:, c0:c0 + POOL_GROUP_DIM]
        o_ref[:, c0:c0 + POOL_GROUP_DIM] = y.astype(o_ref.dtype)


def _pool(z, halo_src, halo_map, pwb, ps, tm, zero_first, pos0, pos_step):
    t = z.shape[0]
    return pl.pallas_call(
        functools.partial(_pool_kernel, tm=tm, zero_first=zero_first, pos0=pos0, pos_step=pos_step),
        grid=(t // tm,),
        in_specs=[pl.BlockSpec((tm, POOL_WIDTH), lambda i: (i, COL_XC // POOL_WIDTH)),
                  pl.BlockSpec((16, POOL_WIDTH), halo_map),
                  pl.BlockSpec((len(POOL_WINDOWS), POOL_GROUP_DIM, POOL_GROUP_DIM), lambda i: (0, 0, 0)),
                  pl.BlockSpec((1, POOL_WIDTH), lambda i: (0, 0))],
        out_specs=pl.BlockSpec((tm, POOL_WIDTH), lambda i: (i, 0)),
        out_shape=jax.ShapeDtypeStruct((t, POOL_WIDTH), BF16),
        scratch_shapes=[pltpu.VMEM((16 + tm, POOL_WIDTH), F32)],
        compiler_params=_cparams(("parallel",)),
        name="pool",
    )(z, halo_src, pwb, ps)


def _route(logits_t, rb_col):
    aff = _sigmoid(logits_t)
    biased = aff + rb_col
    rows = [biased[e:e + 1] for e in range(N_EXPERTS)]
    top2 = []
    gscore = []
    for g in range(N_EXPERT_GROUPS):
        grp = rows[g * EXPERTS_PER_GROUP:(g + 1) * EXPERTS_PER_GROUP]
        gs = None
        for a in range(EXPERTS_PER_GROUP):
            rank = None
            for c in range(EXPERTS_PER_GROUP):
                if c == a:
                    continue
                ahead = (grp[c] >= grp[a]) if c < a else (grp[c] > grp[a])
                rank = ahead.astype(F32) if rank is None else rank + ahead.astype(F32)
            in2 = rank < 1.5
            top2.append(in2)
            contrib = jnp.where(in2, grp[a], 0.0)
            gs = contrib if gs is None else gs + contrib
        gscore.append(gs)
    out = []
    for g in range(N_EXPERT_GROUPS):
        win = None
        for c in range(N_EXPERT_GROUPS):
            if c == g:
                continue
            ok = (gscore[g] > gscore[c]) if c < g else (gscore[g] >= gscore[c])
            win = ok if win is None else (win & ok)
        for a in range(EXPERTS_PER_GROUP):
            e = g * EXPERTS_PER_GROUP + a
            out.append(jnp.where(win & top2[e], aff[e:e + 1], 0.0))
    selw = jnp.concatenate(out, axis=0)
    return selw / jnp.sum(selw, axis=0, keepdims=True)


def _merge_kernel(x_ref, oa_ref, ob_ref, oc_ref, g0_ref, g1_ref, g2_ref, gt1_ref, sh2_ref, sc2_ref,
                  ln1g_ref, ln1b_ref, pa_ref, pb_ref, pc_ref, wo_ref, rw_ref, rb_ref,
                  x1_ref, h2_ref, comb_ref, *, alpha):
    tm = x_ref.shape[0]

    merged = _sigmoid(g0_ref[...]) * _dot(oa_ref[...], pa_ref[...])
    merged = merged + _sigmoid(g1_ref[...]) * _dot(ob_ref[...], pb_ref[...])
    merged = merged + _sigmoid(g2_ref[...]) * _dot(oc_ref[...], pc_ref[...])
    mix = _dot(merged.astype(BF16), wo_ref[...])
    x1 = _ln(alpha * x_ref[...] + (1.0 + _mod_rows(gt1_ref, tm)) * mix) * ln1g_ref[...] + ln1b_ref[...]
    x1_ref[...] = x1
    h2 = _ln(x1) * (1.0 + _mod_rows(sc2_ref, tm)) + _mod_rows(sh2_ref, tm)
    h2_ref[...] = h2.astype(BF16)
    logits = _dot(h2.astype(BF16), rw_ref[...])
    comb_t = _route(logits.T[:N_EXPERTS], rb_ref[...][:N_EXPERTS, 0:1])
    comb_full = jnp.concatenate([comb_t, jnp.zeros((LANES - N_EXPERTS, tm), F32)], axis=0)
    comb_ref[...] = comb_full.T


def _merge(x, z, oa, ob, oc, mods, ln1g, ln1b, pa, pb, pc, wo, rw, rb, tm, alpha):
    t = x.shape[0]

    def const(arr):
        return pl.BlockSpec(arr.shape, lambda i: (0,) * arr.ndim, pipeline_mode=pl.Buffered(1))

    gcol = COL_GM // D_MODEL
    return pl.pallas_call(
        functools.partial(_merge_kernel, alpha=alpha),
        grid=(t // tm,),
        in_specs=[pl.BlockSpec((tm, D_MODEL), lambda i: (i, 0)),
                  pl.BlockSpec((tm, GMLP_WIDTH), lambda i: (i, 0)),
                  pl.BlockSpec((tm, NSA_WIDTH), lambda i: (i, 0)),
                  pl.BlockSpec((tm, POOL_WIDTH), lambda i: (i, 0)),
                  pl.BlockSpec((tm, D_MODEL), lambda i: (i, gcol)),
                  pl.BlockSpec((tm, D_MODEL), lambda i: (i, gcol + 1)),
                  pl.BlockSpec((tm, D_MODEL), lambda i: (i, gcol + 2)),
                  _mod_spec(mods, tm, 2), _mod_spec(mods, tm, 3), _mod_spec(mods, tm, 4),
                  const(ln1g), const(ln1b), const(pa), const(pb), const(pc), const(wo), const(rw), const(rb)],
        out_specs=[pl.BlockSpec((tm, D_MODEL), lambda i: (i, 0)),
                   pl.BlockSpec((tm, D_MODEL), lambda i: (i, 0)),
                   pl.BlockSpec((tm, LANES), lambda i: (i, 0))],
        out_shape=[jax.ShapeDtypeStruct((t, D_MODEL), F32),
                   jax.ShapeDtypeStruct((t, D_MODEL), BF16),
                   jax.ShapeDtypeStruct((t, LANES), F32)],
        compiler_params=_cparams(("parallel",)),
        name="merge",
    )(x, oa, ob, oc, z, z, z, mods, mods, mods, ln1g, ln1b, pa, pb, pc, wo, rw, rb)


def _moe_kernel(h_ref, comb_ref, wg_ref, wu_ref, wd_ref, x1_ref, gt2_ref, g_ref, b_ref, o_ref, acc_scr, *, alpha):
    e = pl.program_id(1)
    tm = h_ref.shape[0]

    @pl.when(e == 0)
    def _():
        acc_scr[...] = jnp.zeros_like(acc_scr)

    h = h_ref[...]
    comb = comb_ref[...]
    lane = lax.broadcasted_iota(jnp.int32, comb.shape, 1)
    w_col = jnp.sum(jnp.where(lane == e, comb, 0.0), axis=1, keepdims=True)
    act = _silu(_dot(h, wg_ref[...])) * _dot(h, wu_ref[...])
    acc_scr[...] += _dot((act * w_col).astype(BF16), wd_ref[...])

    @pl.when(e == N_EXPERTS - 1)
    def _():
        o_ref[...] = _ln(alpha * x1_ref[...] + (1.0 + _mod_rows(gt2_ref, tm)) * acc_scr[...]) * g_ref[...] + b_ref[...]


def _moe(h2, comb, wg, wu, wd, x1, mods, ln2g, ln2b, tm, alpha):
    t = h2.shape[0]
    return pl.pallas_call(
        functools.partial(_moe_kernel, alpha=alpha),
        grid=(t // tm, N_EXPERTS),
        in_specs=[pl.BlockSpec((tm, D_MODEL), lambda i, e: (i, 0)),
                  pl.BlockSpec((tm, LANES), lambda i, e: (i, 0)),
                  pl.BlockSpec((None, D_MODEL, D_EXPERT), lambda i, e: (e, 0, 0)),
                  pl.BlockSpec((None, D_MODEL, D_EXPERT), lambda i, e: (e, 0, 0)),
                  pl.BlockSpec((None, D_EXPERT, D_MODEL), lambda i, e: (e, 0, 0)),
                  pl.BlockSpec((tm, D_MODEL), lambda i, e: (i, 0)),
                  _mod_spec(mods, tm, 5),
                  pl.BlockSpec((1, D_MODEL), lambda i, e: (0, 0)),
                  pl.BlockSpec((1, D_MODEL), lambda i, e: (0, 0))],
        out_specs=pl.BlockSpec((tm, D_MODEL), lambda i, e: (i, 0)),
        out_shape=jax.ShapeDtypeStruct((t, D_MODEL), F32),
        scratch_shapes=[pltpu.VMEM((tm, D_MODEL), F32)],
        compiler_params=_cparams(("parallel", "arbitrary")),
        name="moe",
    )(h2, comb, wg, wu, wd, x1, mods, ln2g, ln2b)


def _row_gather(src_ref, idx_ref, base, buf, slot, sem, n_rows, start):
    def body(r, c):
        cp = pltpu.make_async_copy(src_ref.at[pl.ds(idx_ref[base + r], 1), :],
                                   buf.at[slot, pl.ds(r, 1), :], sem.at[slot])
        if start:
            cp.start()
        else:
            cp.wait()
        return c

    lax.fori_loop(0, n_rows, body, 0, unroll=8)


def _moe_group_kernel(perm_ref, tgrp_ref, nlive_ref, x1_ref, comb_ref, sh2_ref, sc2_ref, wg_ref, wu_ref, wd_ref,
                      y_ref, xbuf, sems):
    k = pl.program_id(0)
    tg = comb_ref.shape[0]
    n_live = nlive_ref[0]

    @pl.when(k == 0)
    def _():
        _row_gather(x1_ref, perm_ref, 0, xbuf, 0, sems, tg, True)

    @pl.when(k + 1 < n_live)
    def _():
        _row_gather(x1_ref, perm_ref, (k + 1) * tg, xbuf, (k + 1) % 2, sems, tg, True)

    @pl.when(k < jnp.maximum(n_live, 1))
    def _():
        _row_gather(x1_ref, perm_ref, k * tg, xbuf, k % 2, sems, tg, False)

    @pl.when(k < n_live)
    def _():
        h = (_ln(xbuf[k % 2]) * (1.0 + sc2_ref[0:1, :]) + sh2_ref[0:1, :]).astype(BF16)
        comb = comb_ref[...]
        lane = lax.broadcasted_iota(jnp.int32, comb.shape, 1)
        e0 = tgrp_ref[k] * EXPERTS_PER_GROUP
        acc = jnp.zeros((tg, D_MODEL), F32)
        for e in range(EXPERTS_PER_GROUP):
            w_col = jnp.sum(jnp.where(lane == e0 + e, comb, 0.0), axis=1, keepdims=True)
            act = _silu(_dot(h, wg_ref[e])) * _dot(h, wu_ref[e])
            acc = acc + _dot((act * w_col).astype(BF16), wd_ref[e])
        y_ref[...] = acc

    @pl.when(k >= n_live)
    def _():
        y_ref[...] = jnp.zeros(y_ref.shape, F32)


def _moe_finish_kernel(pos_ref, y_ref, x1_ref, gt2_ref, g_ref, b_ref, o_ref, ybuf, sems, *, alpha):
    i = pl.program_id(0)
    tm = x1_ref.shape[0]

    @pl.when(i == 0)
    def _():
        _row_gather(y_ref, pos_ref, 0, ybuf, 0, sems, tm, True)

    @pl.when(i + 1 < pl.num_programs(0))
    def _():
        _row_gather(y_ref, pos_ref, (i + 1) * tm, ybuf, (i + 1) % 2, sems, tm, True)

    _row_gather(y_ref, pos_ref, i * tm, ybuf, i % 2, sems, tm, False)
    o_ref[...] = _ln(alpha * x1_ref[...] + (1.0 + gt2_ref[0:1, :]) * ybuf[i % 2]) * g_ref[...] + b_ref[...]


def _moe_dispatch(comb, tg):
    t = comb.shape[0]
    i32 = jnp.int32
    cg = comb[:, :N_EXPERTS].reshape(t, N_EXPERT_GROUPS, EXPERTS_PER_GROUP)
    gid = jnp.argmax(jnp.sum(cg, axis=-1) > 0.0, axis=-1).astype(i32)
    onehot = (gid[:, None] == jnp.arange(N_EXPERT_GROUPS, dtype=i32)[None, :]).astype(i32)
    csum = jnp.cumsum(onehot, axis=0)
    rank = jnp.take_along_axis(csum, gid[:, None], axis=1)[:, 0] - 1
    pcount = ((csum[-1] + tg - 1) // tg) * tg
    ends = jnp.cumsum(pcount)
    pos = ((ends - pcount)[gid] + rank).astype(i32)
    rp = t + N_EXPERT_GROUPS * tg
    perm = jnp.full((rp,), -1, i32).at[pos].set(jnp.arange(t, dtype=i32))
    live = (perm >= 0).astype(F32)
    perm = jnp.maximum(perm, 0)
    comb_sorted = comb[perm] * live[:, None]
    tile_start = jnp.arange(rp // tg, dtype=i32) * tg
    tile_group = jnp.minimum(jnp.sum(tile_start[:, None] >= ends[None, :], axis=1), N_EXPERT_GROUPS - 1).astype(i32)
    n_live = (ends[-1] // tg).astype(i32).reshape(1)
    return perm, pos, tile_group, n_live, comb_sorted


def _moe_grouped(x1, comb, wg, wu, wd, mods, ln2g, ln2b, alpha):
    t = x1.shape[0]
    tg = MOE_GROUP_TILE
    perm, pos, tile_group, n_live, comb_sorted = _moe_dispatch(comb, tg)
    rp = perm.shape[0]
    epg = EXPERTS_PER_GROUP

    def wspec(shape):
        return pl.BlockSpec((epg,) + shape, lambda k, perm, tgrp, nl: (tgrp[k], 0, 0), pipeline_mode=pl.Buffered(1))

    y = pl.pallas_call(
        _moe_group_kernel,
        grid_spec=pltpu.PrefetchScalarGridSpec(
            num_scalar_prefetch=3,
            grid=(rp // tg,),
            in_specs=[pl.BlockSpec(memory_space=pl.ANY),
                      pl.BlockSpec((tg, LANES), lambda k, *_: (k, 0)),
                      pl.BlockSpec((8, D_MODEL), lambda k, *_: (0, 3)),
                      pl.BlockSpec((8, D_MODEL), lambda k, *_: (0, 4)),
                      wspec((D_MODEL, D_EXPERT)), wspec((D_MODEL, D_EXPERT)), wspec((D_EXPERT, D_MODEL))],
            out_specs=pl.BlockSpec((tg, D_MODEL), lambda k, *_: (k, 0)),
            scratch_shapes=[pltpu.VMEM((2, tg, D_MODEL), F32), pltpu.SemaphoreType.DMA((2,))]),
        out_shape=jax.ShapeDtypeStruct((rp, D_MODEL), F32),
        compiler_params=_cparams(("arbitrary",)),
        name="moe_group",
    )(perm, tile_group, n_live, x1, comb_sorted, mods, mods, wg, wu, wd)

    tm = MOE_FINISH_TILE
    return pl.pallas_call(
        functools.partial(_moe_finish_kernel, alpha=alpha),
        grid_spec=pltpu.PrefetchScalarGridSpec(
            num_scalar_prefetch=1,
            grid=(t // tm,),
            in_specs=[pl.BlockSpec(memory_space=pl.ANY),
                      pl.BlockSpec((tm, D_MODEL), lambda i, pos: (i, 0)),
                      pl.BlockSpec((8, D_MODEL), lambda i, pos: (0, 5)),
                      pl.BlockSpec((1, D_MODEL), lambda i, pos: (0, 0)),
                      pl.BlockSpec((1, D_MODEL), lambda i, pos: (0, 0))],
            out_specs=pl.BlockSpec((tm, D_MODEL), lambda i, pos: (i, 0)),
            scratch_shapes=[pltpu.VMEM((2, tm, D_MODEL), F32), pltpu.SemaphoreType.DMA((2,))]),
        out_shape=jax.ShapeDtypeStruct((t, D_MODEL), F32),
        compiler_params=_cparams(("arbitrary",)),
        name="moe_finish",
    )(pos, y, x1, mods, ln2g, ln2b)


def _rel_bucket(dist):
    n = jnp.maximum(dist, 0)
    max_exact = N_BUCKETS // 2
    nf = jnp.maximum(n, max_exact).astype(F32)
    large = max_exact + (jnp.log(nf / max_exact) / math.log(MAX_DISTANCE / max_exact)
                         * (N_BUCKETS - max_exact)).astype(jnp.int32)
    return jnp.where(n < max_exact, n, jnp.minimum(large, N_BUCKETS - 1))


def _bias_of(rel_bias, dist):
    return rel_bias[_rel_bucket(dist)].astype(F32)


def _prompt_tiles(rel_bias):
    kk = jnp.arange(BLOCK)[:, None]
    qq = jnp.arange(BLOCK)[None, :]

    def lanes(tile8, g):
        return jnp.concatenate([tile8[:, :, g * N_REP + r] for r in range(N_REP)], axis=1)

    def tile(m, lo, hi):
        d = m * BLOCK + qq - kk
        ok = (d >= lo) & (d < hi)
        return jnp.where(ok[:, :, None], _bias_of(rel_bias, d), NEG)

    big = 1 << 20
    neg_tile = jnp.full((BLOCK, BLOCK, N_HEADS), NEG, F32)
    sel_tiles = [tile(m, 0, big) for m in range(4)] + [neg_tile]
    win_tiles = [tile(m, 0, WINDOW) for m in range(WIN_BLOCKS + 1)] + [neg_tile]
    st = jnp.stack([jnp.stack([lanes(t, g) for t in sel_tiles]) for g in range(N_GROUPS)]) / QK_SCALE
    wt = jnp.stack([jnp.stack([lanes(t, g) for t in win_tiles]) for g in range(N_GROUPS)]) / QK_SCALE
    rows = []
    for m in range(4):
        d = m * BLOCK + jnp.arange(BLOCK) - (BLOCK - 1)
        rows.append(_bias_of(rel_bias, d))
    ct = jnp.stack([jnp.stack([jnp.concatenate([rw[:, g * N_REP + r] for r in range(N_REP)]) for rw in rows]
                              + [jnp.zeros((N_REP * BLOCK,), F32)] * 4) for g in range(N_GROUPS)])
    return st, wt, ct


def _sample_lanes(tile8, t_new):
    rows = tile8.shape[0]
    x = jnp.transpose(tile8, (0, 2, 1)).reshape(rows, N_HEADS * t_new)
    return jnp.pad(x, ((0, 0), (0, LANES - N_HEADS * t_new)))


def _sample_tiles(rel_bias, past, t_new, nbp, win_rows, win_pad):
    tq = past + jnp.arange(t_new)[None, :]

    def tile(k_pos, ok_extra=None, lo=0, hi=1 << 20):
        d = tq - k_pos[:, None]
        ok = (d >= lo) & (d < hi)
        if ok_extra is not None:
            ok = ok & ok_extra[:, None]
        return _sample_lanes(jnp.where(ok[:, :, None], _bias_of(rel_bias, d), NEG), t_new)

    n = jnp.arange(nbp)
    cb = tile(n * BLOCK + BLOCK - 1, n < past // BLOCK + 1)
    nearb = tile(past - PAGE + jnp.arange(PAGE))
    farb = jnp.broadcast_to(_sample_lanes(jnp.broadcast_to(
        rel_bias[N_BUCKETS - 1].astype(F32)[None, None, :], (1, t_new, N_HEADS)), t_new), (8, LANES))
    kn = jnp.arange(PAGE)
    newb = tile(past + kn, kn < t_new)
    wb = min(WINDOW, past)
    j = jnp.arange(win_pad)
    k_pos = past - wb + j
    winb = tile(k_pos, (j < win_rows) & (k_pos >= 0), 0, WINDOW)
    return cb, nearb, farb, newb, winb


def _mix_mats(ws, bs, tc, rep):
    n = tc // rep
    causal = jnp.tril(jnp.ones((CHUNK, CHUNK), F32))
    wc = (ws * causal)[:, :n, :n]
    eye = jnp.eye(rep, dtype=F32)
    m = jnp.einsum('ab,gts->gatbs', eye, wc).reshape(GMLP_GROUPS, tc, tc)
    bcol = jnp.tile(bs[:, :n].T, (rep, 1))
    return m.astype(BF16), jnp.pad(bcol, ((0, 0), (0, LANES - GMLP_GROUPS)))


def kernel(x_prompt, x_sample, cache_nsa_kv, state_win_kv, state_pool, page_table, c_prompt, c_sample, rel_bias, router_w, router_b, w_in, nsa_phi_pe, nsa_phi_w1, nsa_phi_w2, gmlp_norm_g, gmlp_norm_b, gmlp_ws, gmlp_bs, pool_w, pool_scale, proj_a, proj_b, proj_c, w_o, ada_w, ada_b, ln1_g, ln1_b, ln2_g, ln2_b, exp_w_gate, exp_w_up, exp_w_down):
    depth = w_in.shape[0]
    alpha = (2 * depth) ** 0.25
    s = x_prompt.shape[1]
    bsz, t_new = x_sample.shape[0], x_sample.shape[1]
    n_pool = cache_nsa_kv.shape[1]
    n_pages = page_table.shape[1]
    past = n_pages * PAGE
    ts = bsz * t_new
    assert x_prompt.shape[0] == 1 and s % CHUNK_KEYS == 0 and past % PAGE == 0
    assert N_HEADS * t_new <= LANES and t_new <= BLOCK and (past // BLOCK) >= N_SEL
    nb_s = past // BLOCK + 1
    nbp = -(-nb_s // 8) * 8

    n_c = 1 + bsz
    c_all = jnp.pad(jnp.concatenate([c_prompt, c_sample], axis=0), ((0, -n_c % 8), (0, 0)))
    mods_all = _ada_mods(c_all, ada_w, ada_b)

    cache_pages = cache_nsa_kv.reshape(depth * n_pool, PAGE * PAGED_KINDS * N_GROUPS, HEAD_DIM)
    st, wt, ct = _prompt_tiles(rel_bias)
    n_win = state_win_kv.shape[2]
    win_rows = n_win + t_new
    win_pad = n_win + PAGE
    win_state = state_win_kv.reshape(depth * bsz, n_win * 2 * N_GROUPS, HEAD_DIM)
    assert nbp <= SAMPLE_BLOCKS_PAD and n_win == min(WINDOW, past) and n_win % 8 == 0
    cb_s, nearb_s, farb_s, newb_s, winb_s = _sample_tiles(rel_bias, past, t_new, SAMPLE_BLOCKS_PAD, win_rows, win_pad)
    rw = jnp.pad(router_w, ((0, 0), (0, LANES - N_EXPERTS))).astype(BF16)
    rb = jnp.pad(router_b.reshape(N_EXPERTS, 1), ((0, LANES - N_EXPERTS), (0, LANES - 1)))

    xp = x_prompt.reshape(s, D_MODEL)
    xs = x_sample.reshape(ts, D_MODEL)
    outs = {k: [] for k in ("nsa_p", "nsa_s", "win_p", "win_s", "pool_p", "pool_s", "v_s")}

    for l in range(depth):
        w_parts = _in_proj_weights(w_in[l])
        pe = nsa_phi_pe[l]
        w1b = nsa_phi_w1[l].astype(BF16)
        w2b = nsa_phi_w2[l].astype(BF16)
        pab, pbb, pcb, wob = (a[l].astype(BF16) for a in (proj_a, proj_b, proj_c, w_o))
        wg, wu, wd = (a[l].astype(BF16) for a in (exp_w_gate, exp_w_up, exp_w_down))
        pwb = pool_w[l].astype(BF16)
        ps = pool_scale[l].reshape(1, POOL_WIDTH)
        gng = gmlp_norm_g[l].reshape(1, GMLP_WIDTH)
        gnb = gmlp_norm_b[l].reshape(1, GMLP_WIDTH)
        ln1g, ln1b, ln2g, ln2b = (a[l].reshape(1, D_MODEL) for a in (ln1_g, ln1_b, ln2_g, ln2_b))
        mods_p = jnp.broadcast_to(mods_all[l, 0:1], (8, 6 * D_MODEL))
        mods_s = jnp.repeat(mods_all[l, 1:1 + bsz], t_new, axis=0)

        z, zb = _in_proj(xp, mods_p, w_parts, 1024)
        gh = N_GROUPS * HEAD_DIM
        outs["nsa_p"].append(z[:, COL_KV:COL_KV + PAGED_KINDS * gh].reshape(1, s, PAGED_KINDS, N_GROUPS, HEAD_DIM))
        outs["win_p"].append(z[s - min(WINDOW, s):, COL_KV + PAGED_KINDS * gh:COL_KV + KV_KINDS * gh]
                             .reshape(1, min(WINDOW, s), 2, N_GROUPS, HEAD_DIM))
        outs["pool_p"].append(z[s - POOL_BUF:, COL_XC:COL_XC + POOL_WIDTH].reshape(1, POOL_BUF, POOL_WIDTH))

        nb = s // BLOCK
        cmp = _compress_prompt(z, pe, w1b, w2b).reshape(2, N_GROUPS, nb, HEAD_DIM)
        kc = cmp[0].astype(BF16)
        vcT = jnp.transpose(cmp[1], (0, 2, 1)).astype(BF16)
        nch = s // CHUNK_KEYS

        def kv_cols(kind, g):
            c0 = COL_KV + (kind * N_GROUPS + g) * HEAD_DIM
            return zb[:, c0:c0 + HEAD_DIM].reshape(nch, CHUNK_KEYS, HEAD_DIM)

        def k_chunks(kind):
            return jnp.stack([kv_cols(kind, g) for g in range(N_GROUPS)])

        def vT_chunks(kind):
            vt = jnp.stack([jnp.transpose(kv_cols(kind, g), (0, 2, 1)) for g in range(N_GROUPS)])
            ones = jnp.zeros((N_GROUPS, nch, 8, CHUNK_KEYS), BF16).at[:, :, 0].set(1.0)
            return jnp.concatenate([vt, ones], axis=2)

        q = zb[:, COL_Q:COL_Q + NSA_WIDTH].reshape(nb, BLOCK, N_GROUPS, N_REP, HEAD_DIM)
        qT = jnp.transpose(q, (2, 4, 0, 3, 1)).reshape(N_GROUPS, HEAD_DIM, nb * N_REP * BLOCK)
        gn = z[:, COL_GN:COL_GN + 3 * N_HEADS].reshape(nb, BLOCK, 3, N_GROUPS, N_REP)
        gates = jnp.transpose(gn, (3, 0, 2, 4, 1)).reshape(N_GROUPS, nb, 3, N_REP * BLOCK)
        gates = jnp.pad(gates, ((0, 0), (0, 0), (0, 5), (0, 0)))
        ob = _nsa_prompt(qT, kc, vcT, k_chunks(2), vT_chunks(3), k_chunks(4), vT_chunks(5), st, wt, ct, gates, s)

        mm, bcol = _mix_mats(gmlp_ws[l], gmlp_bs[l], CHUNK, 1)
        oa, _ = _gmlp(z, mm, bcol, gng, gnb, CHUNK)
        tp = 512
        cxb = COL_XC // POOL_WIDTH
        oc = _pool(z, z, lambda i: (jnp.maximum(i * (tp // 16) - 1, 0), cxb), pwb, ps, tp, True, 0, tp)
        x1, h2, comb = _merge(xp, z, oa, ob, oc, mods_p, ln1g, ln1b, pab, pbb, pcb, wob, rw, rb, 256, alpha)
        xp = _moe_grouped(x1, comb, wg, wu, wd, mods_p, ln2g, ln2b, alpha)

        zs, zbs = _in_proj(xs, mods_s, w_parts, ts)
        kvs = zs[:, COL_KV:COL_KV + 1536].reshape(bsz, t_new, KV_KINDS, N_GROUPS, HEAD_DIM)
        kvbs = zbs[:, COL_KV:COL_KV + 1536].reshape(bsz, t_new, KV_KINDS, gh)
        outs["nsa_s"].append(kvs[:, :, :PAGED_KINDS])
        win_full = jnp.concatenate([state_win_kv[l], kvs[:, :, PAGED_KINDS:]], axis=1)
        outs["win_s"].append(win_full[:, win_full.shape[1] - min(WINDOW, past + t_new):])
        xcs = zs[:, COL_XC:COL_XC + POOL_WIDTH].reshape(bsz, t_new, POOL_WIDTH)
        ext = jnp.concatenate([state_pool[l], xcs], axis=1)
        outs["pool_s"].append(ext[:, ext.shape[1] - POOL_BUF:])

        pt = page_table + l * n_pool
        new_cmp = kvs[:, :, 0:2].reshape(bsz, t_new, 2 * N_GROUPS * HEAD_DIM)
        knew = jnp.pad(kvbs[:, :, 2], ((0, 0), (0, PAGE - t_new), (0, 0)))
        vnew = jnp.pad(kvbs[:, :, 3], ((0, 0), (0, PAGE - t_new), (0, 0)))
        kwn = jnp.pad(kvbs[:, :, 4], ((0, 0), (0, PAGE - t_new), (0, 0)))
        vwn = jnp.pad(kvbs[:, :, 5], ((0, 0), (0, PAGE - t_new), (0, 0)))
        qs = zbs[:, COL_Q:COL_Q + NSA_WIDTH].reshape(bsz, t_new, N_GROUPS, N_REP, HEAD_DIM)
        qs = jnp.transpose(qs, (0, 2, 4, 3, 1)).reshape(bsz, N_GROUPS, HEAD_DIM, N_REP * t_new)
        n_q = N_REP * t_new
        qT_s = jnp.concatenate(
            [jnp.pad(qs[:, g], ((0, 0), (0, 0), (g * n_q, LANES - (g + 1) * n_q))) for g in range(N_GROUPS)], axis=1)
        gns = zs[:, COL_GN:COL_GN + 3 * N_HEADS].reshape(bsz, t_new, 3, N_HEADS)
        gates_s = jnp.transpose(gns, (0, 3, 1, 2)).reshape(bsz, N_HEADS * t_new, 3)
        gates_s = jnp.pad(gates_s, ((0, 0), (0, LANES - N_HEADS * t_new), (0, LANES - 3)))
        o_s = _sample_nsa(pt, cache_pages, new_cmp, qT_s, knew, vnew, win_state, l, kwn, vwn, pe, w1b, w2b,
                          cb_s, nearb_s, farb_s, newb_s, winb_s, gates_s, nbp)
        o4 = o_s[:, :N_HEADS * t_new].reshape(bsz, N_GROUPS, N_REP, t_new, N_GROUPS, HEAD_DIM)
        o4 = jnp.stack([o4[:, g, :, :, g] for g in range(N_GROUPS)], axis=1)
        ob_s = jnp.transpose(o4, (0, 3, 1, 2, 4)).reshape(ts, NSA_WIDTH).astype(BF16)

        mm_s, bcol_s = _mix_mats(gmlp_ws[l], gmlp_bs[l], ts, bsz)
        oa_s, vn_s = _gmlp(zs, mm_s, bcol_s, gng, gnb, ts)
        outs["v_s"].append(vn_s.reshape(bsz, t_new, GMLP_WIDTH))
        halo = jnp.pad(state_pool[l], ((0, 0), (16 - POOL_BUF, 0), (0, 0))).reshape(bsz * 16, POOL_WIDTH)
        oc_s = _pool(zs, halo, lambda i: (i, 0), pwb, ps, t_new, False, past, 0)
        x1s, h2s, comb_s = _merge(xs, zs, oa_s, ob_s, oc_s, mods_s, ln1g, ln1b, pab, pbb, pcb, wob, rw, rb, ts, alpha)
        xs = _moe(h2s, comb_s, wg, wu, wd, x1s, mods_s, ln2g, ln2b, ts, alpha)

    return (xp.reshape(1, s, D_MODEL), xs.reshape(bsz, t_new, D_MODEL),
            jnp.stack(outs["nsa_p"]), jnp.stack(outs["nsa_s"]), jnp.stack(outs["win_p"]), jnp.stack(outs["win_s"]),
            jnp.stack(outs["pool_p"]), jnp.stack(outs["pool_s"]), jnp.stack(outs["v_s"]))
```

```python
import functools
import math

import jax
import jax.numpy as jnp
from jax import lax
from jax.experimental import pallas as pl
from jax.experimental.pallas import tpu as pltpu

F32 = jnp.float32
BF16 = jnp.bfloat16

D_MODEL = 2048
HEAD_DIM = 128
N_GROUPS = 2
N_REP = 4
N_HEADS = N_GROUPS * N_REP
NSA_WIDTH = N_HEADS * HEAD_DIM
KV_KINDS = 6
PAGED_KINDS = 4
BLOCK = 64
N_SEL = 16
WINDOW = 512
WIN_BLOCKS = WINDOW // BLOCK
PHI_HIDDEN = 2 * HEAD_DIM
PAGE = 128
GMLP_GROUPS = 4
GMLP_WIDTH = 512
CHUNK = 128
POOL_WINDOWS = (2, 4, 8, 16)
POOL_GROUP_DIM = 128
POOL_WIDTH = 512
POOL_BUF = 15
N_BUCKETS = 32
MAX_DISTANCE = 128
N_EXPERTS = 16
N_EXPERT_GROUPS = 4
EXPERTS_PER_GROUP = 4
D_EXPERT = 512
LN_EPS = 1e-5
NEG = -1e30
FORCED_SCORE = 1e9

COL_Q = 0
COL_KV = 1024
COL_U = 2560
COL_V = 3072
COL_XC = 3584
COL_GM = 4096
COL_GN = 10240
N_IN_PAD = 10752
IN_TN = 512

LANES = 128
VMEM_LIMIT = 56 * 1024 * 1024
CHUNK_BLOCKS = 8
CHUNK_KEYS = CHUNK_BLOCKS * BLOCK
XC_PITCH = BLOCK + 8
SAMPLE_RING_PAGES = 16
SAMPLE_BLOCKS_PAD = 256
MOE_GROUP_TILE = 256
MOE_FINISH_TILE = 256
ROW_DMA_GROUP = 8


def _cparams(sem):
    return pltpu.CompilerParams(dimension_semantics=sem, vmem_limit_bytes=VMEM_LIMIT)


def _ln(x):
    mu = jnp.mean(x, axis=-1, keepdims=True)
    xc = x - mu
    var = jnp.mean(xc * xc, axis=-1, keepdims=True)
    return xc * lax.rsqrt(var + LN_EPS)


def _sigmoid(x):
    return 1.0 / (1.0 + jnp.exp(-x))


def _silu(x):
    return x * _sigmoid(x)


def _gelu_tanh(x):
    return 0.5 * x * (1.0 + jnp.tanh(math.sqrt(2.0 / math.pi) * (x + 0.044715 * (x * x * x))))


def _dot(a, b):
    return jnp.dot(a, b, preferred_element_type=F32)


QK_SCALE = HEAD_DIM ** -0.5
EXP2_SCALE = QK_SCALE * math.log2(math.e)


def _qk(k, qT):
    return _dot(k, qT) * QK_SCALE


def _ada_kernel(c_ref, w_ref, b_ref, o_ref):
    c = c_ref[...]
    o_ref[...] = _dot(_silu(c).astype(BF16), w_ref[...].astype(BF16)) + b_ref[...]


def _ada_mods(c_all, ada_w, ada_b):
    depth = ada_w.shape[0]
    rows = c_all.shape[0]
    n = ada_w.shape[2]
    tn = 1024
    return pl.pallas_call(
        _ada_kernel,
        grid=(depth, n // tn),
        in_specs=[pl.BlockSpec((rows, D_MODEL), lambda l, j: (0, 0)),
                  pl.BlockSpec((None, D_MODEL, tn), lambda l, j: (l, 0, j)),
                  pl.BlockSpec((None, 1, tn), lambda l, j: (l, 0, j))],
        out_specs=pl.BlockSpec((None, rows, tn), lambda l, j: (l, 0, j)),
        out_shape=jax.ShapeDtypeStruct((depth, rows, n), F32),
        compiler_params=_cparams(("parallel", "parallel")),
        name="ada_mods",
    )(c_all, ada_w, ada_b.reshape(depth, 1, n))


def _mod_rows(ref, tm):
    return ref[...] if ref.shape[0] == tm else ref[0:1, :]


N_QKV_TILES = COL_U // IN_TN
N_MAIN_TILES = COL_GN // IN_TN


def _inproj_kernel(x_ref, sh_ref, sc_ref, wa_ref, wb_ref, wc_ref, z_ref, zb_ref, h_scr):
    tm = x_ref.shape[0]
    j = pl.program_id(1)

    @pl.when(j == 0)
    def _():
        h = _ln(x_ref[...]) * (1.0 + _mod_rows(sc_ref, tm)) + _mod_rows(sh_ref, tm)
        h_scr[...] = h.astype(BF16)

    @pl.when(j < N_QKV_TILES)
    def _():
        z = _dot(h_scr[...], wa_ref[...])
        z_ref[...] = z
        zb_ref[...] = z.astype(BF16)

    @pl.when((j >= N_QKV_TILES) & (j < N_MAIN_TILES))
    def _():
        z_ref[...] = _dot(h_scr[...], wb_ref[...])

    @pl.when(j == N_MAIN_TILES)
    def _():
        z_ref[...] = _dot(h_scr[...], wc_ref[...])


def _mod_spec(mods, tm, col):
    rows = mods.shape[0]
    if rows == 8:
        return pl.BlockSpec((8, D_MODEL), lambda i, *_: (0, col))
    return pl.BlockSpec((tm, D_MODEL), lambda i, *_: (i, col))


def _in_proj_weights(w):
    n_gate = 3 * N_HEADS
    wa = w[:, :COL_U].astype(BF16)
    wb = w[:, COL_U + n_gate:].astype(BF16)
    wc = jnp.pad(w[:, COL_U:COL_U + n_gate], ((0, 0), (0, IN_TN - n_gate))).astype(BF16)
    return wa, wb, wc


def _in_proj(x, mods, weights, tm):
    t = x.shape[0]
    wa, wb, wc = weights
    n_b = N_MAIN_TILES - N_QKV_TILES
    return pl.pallas_call(
        _inproj_kernel,
        grid=(t // tm, N_IN_PAD // IN_TN),
        in_specs=[pl.BlockSpec((tm, D_MODEL), lambda i, j: (i, 0)),
                  _mod_spec(mods, tm, 0), _mod_spec(mods, tm, 1),
                  pl.BlockSpec((D_MODEL, IN_TN), lambda i, j: (0, jnp.minimum(j, N_QKV_TILES - 1))),
                  pl.BlockSpec((D_MODEL, IN_TN), lambda i, j: (0, jnp.clip(j - N_QKV_TILES, 0, n_b - 1))),
                  pl.BlockSpec((D_MODEL, IN_TN), lambda i, j: (0, 0))],
        out_specs=[pl.BlockSpec((tm, IN_TN), lambda i, j: (i, j)),
                   pl.BlockSpec((tm, IN_TN), lambda i, j: (i, jnp.minimum(j, N_QKV_TILES - 1)))],
        out_shape=[jax.ShapeDtypeStruct((t, N_IN_PAD), F32),
                   jax.ShapeDtypeStruct((t, COL_U), BF16)],
        scratch_shapes=[pltpu.VMEM((tm, D_MODEL), BF16)],
        compiler_params=_cparams(("parallel", "arbitrary")),
        name="in_proj",
    )(x, mods, mods, wa, wb, wc)


def _compress_pair(x0_ref, x1_ref, nb, pe_ref, w1_ref, w2_ref):
    def step(p, acc):
        pe = pe_ref[pl.ds(p, 1), :]
        a0 = x0_ref[pl.ds(p, nb, stride=BLOCK), :] + pe
        a1 = x1_ref[pl.ds(p, nb, stride=BLOCK), :] + pe
        a = jnp.concatenate([a0, a1], axis=0).astype(BF16)
        w = w1_ref[pl.ds(pl.multiple_of(p * HEAD_DIM, HEAD_DIM), HEAD_DIM), :]
        return acc + _dot(a, w)

    acc = lax.fori_loop(0, BLOCK, step, jnp.zeros((2 * nb, PHI_HIDDEN), F32))
    return _dot(_gelu_tanh(acc).astype(BF16), w2_ref[...])


def _compress_prompt_kernel(x0_ref, x1_ref, pe_ref, w1_ref, w2_ref, o_ref, *, nb):
    o_ref[...] = _compress_pair(x0_ref, x1_ref, nb, pe_ref, w1_ref, w2_ref)


def _compress_prompt(z, pe, w1b, w2b):
    s = z.shape[0]
    nb = s // BLOCK
    cb = COL_KV // HEAD_DIM
    return pl.pallas_call(
        functools.partial(_compress_prompt_kernel, nb=nb),
        grid=(2,),
        in_specs=[pl.BlockSpec((s, HEAD_DIM), lambda k: (0, cb + 2 * k)),
                  pl.BlockSpec((s, HEAD_DIM), lambda k: (0, cb + 2 * k + 1)),
                  pl.BlockSpec((None, BLOCK, HEAD_DIM), lambda k: (k, 0, 0)),
                  pl.BlockSpec((None, BLOCK * HEAD_DIM, PHI_HIDDEN), lambda k: (k, 0, 0)),
                  pl.BlockSpec((None, PHI_HIDDEN, HEAD_DIM), lambda k: (k, 0, 0))],
        out_specs=pl.BlockSpec((None, 2 * nb, HEAD_DIM), lambda k: (k, 0, 0)),
        out_shape=jax.ShapeDtypeStruct((2, 2 * nb, HEAD_DIM), F32),
        compiler_params=_cparams(("parallel",)),
        name="compress_prompt",
    )(z, z, pe, w1b, w2b)


N_FORCED = 3


def _select_blocks(score, forced, n_io):
    def body(_, carry):
        work, sel = carry
        mval = jnp.max(work, axis=0, keepdims=True)
        cand = jnp.where(work == mval, n_io, jnp.int32(1 << 30))
        idx = jnp.min(cand, axis=0, keepdims=True)
        pick = n_io == idx
        return jnp.where(pick, -jnp.inf, work), jnp.where(pick, 1.0, sel)

    init = (jnp.where(forced, -jnp.inf, score), jnp.where(forced, 1.0, 0.0))
    _, sel = lax.fori_loop(0, N_SEL - N_FORCED, body, init)
    return sel


def _nsa_prompt_kernel(qT_ref, kc_ref, vcT_ref, ksel_ref, vselT_ref, kwin_ref, vwinT_ref,
                       st_ref, wt_ref, ct_ref, gate_ref, o_ref, selneg_scr, selfar_scr, s_a, s_b, p_a, p_b, *, nb):
    i = pl.program_id(1)
    qT = qT_ref[...]
    nl = N_REP * BLOCK

    n_io = lax.broadcasted_iota(jnp.int32, (nb, nl), 0)
    qq = lax.broadcasted_iota(jnp.int32, (nb, nl), 1) % BLOCK
    m = i - n_io
    ct = ct_ref[...]
    bias = jnp.where(m == 0, ct[0:1], jnp.where(m == 1, ct[1:2], jnp.where(m == 2, ct[2:3], ct[3:4])))
    valid = (m >= 1) | ((m == 0) & (qq == BLOCK - 1))
    s = jnp.where(valid, _qk(kc_ref[...], qT) + bias, NEG)
    mx = jnp.max(s, axis=0, keepdims=True)
    p = jnp.where(valid, jnp.exp(s - mx), 0.0)
    l = jnp.sum(p, axis=0, keepdims=True)
    pn = p / jnp.where(l > 0.0, l, 1.0)
    o_cmp = _dot(vcT_ref[...], pn.astype(BF16))

    half = pn[:, :LANES] + pn[:, LANES:]
    imp = half + pltpu.roll(half, BLOCK, 1)
    n_io1 = n_io[:, :LANES]
    forced = (n_io1 == 0) | (n_io1 == i) | (n_io1 == i - 1)
    score = jnp.where(forced, FORCED_SCORE, imp)
    score = jnp.where(n_io1 <= i, score, -1.0)
    sel = _select_blocks(score, forced, n_io1)
    neg1 = jnp.where((sel > 0.5) & (score >= 0.0), 0.0, NEG)
    neg2 = jnp.concatenate([neg1, neg1], axis=1)
    selneg_scr[...] = neg2
    far_row = st_ref[st_ref.shape[0] - 2][0:1, :]
    selfar_scr[...] = jnp.where(neg2 == 0.0, far_row, NEG)

    c_i = i // CHUNK_BLOCKS

    def branch(k_ref, vT_ref, tile_ref, n_tiles, use_sel, n_chunks):
        def logits(c, dst):
            dst[...] = _dot(k_ref[jnp.maximum(c, 0)], qT)

        def pv(c, p_ref):
            return _dot(vT_ref[jnp.clip(c, 0, vT_ref.shape[0] - 1)], p_ref[...])

        def consume(c, src, m_run, near):
            live = c >= 0
            cz = jnp.maximum(c, 0)
            parts = []
            for b in range(CHUNK_BLOCKS):
                n = cz * CHUNK_BLOCKS + b
                sb = src[b * BLOCK:(b + 1) * BLOCK, :]
                if near:
                    mm = i - n
                    if use_sel:
                        tidx = jnp.where(mm < 0, n_tiles - 1, jnp.minimum(mm, n_tiles - 2))
                    else:
                        tidx = jnp.where((mm < 0) | (mm > n_tiles - 2), n_tiles - 1, mm)
                    sb = sb + tile_ref[jnp.where(live, tidx, n_tiles - 1)]
                    if use_sel:
                        sb = sb + selneg_scr[pl.ds(n, 1), :]
                else:
                    sb = sb + jnp.where(live, selfar_scr[pl.ds(n, 1), :], NEG)
                parts.append(sb)
            s2 = jnp.concatenate(parts, axis=0)
            m_new = jnp.maximum(m_run, jnp.max(s2, axis=0, keepdims=True))
            alpha = jnp.exp2((m_run - m_new) * EXP2_SCALE)
            pp = jnp.exp2((s2 - m_new) * EXP2_SCALE)
            return m_new, alpha, pp.astype(BF16)

        def pair(t, carry, near):
            m_run, acc = carry
            c0 = c_i - 2 * t
            logits(c0 - 1, s_b)
            r_prev = pv(c0 + 1, p_b)
            m_run, alpha, pp = consume(c0, s_a, m_run, near)
            p_a[...] = pp
            acc = alpha * (acc + r_prev)
            logits(c0 - 2, s_a)
            r_cur = pv(c0, p_a)
            m_run, alpha, pp = consume(c0 - 1, s_b, m_run, near)
            p_b[...] = pp
            return m_run, alpha * (acc + r_cur)

        logits(c_i, s_a)
        p_b[...] = jnp.zeros(p_b.shape, BF16)
        carry = (jnp.full((1, nl), NEG, F32), jnp.zeros((vT_ref.shape[1], nl), F32))
        carry = pair(0, carry, True)
        n_pairs = (n_chunks + 1) // 2 if use_sel else 1
        if use_sel:
            carry = lax.fori_loop(1, n_pairs, lambda t, cr: pair(t, cr, False), carry)
        acc = carry[1] + pv(c_i - 2 * n_pairs + 1, p_b)
        return acc[:HEAD_DIM] / acc[HEAD_DIM:HEAD_DIM + 1]

    o_sel = branch(ksel_ref, vselT_ref, st_ref, st_ref.shape[0], True, c_i + 1)
    o_win = branch(kwin_ref, vwinT_ref, wt_ref, wt_ref.shape[0], False, jnp.minimum(c_i + 1, 2))

    gates = _sigmoid(gate_ref[...])
    oT = gates[0:1] * o_cmp + gates[1:2] * o_sel + gates[2:3] * o_win
    o = oT.T
    for r in range(N_REP):
        o_ref[:, r * HEAD_DIM:(r + 1) * HEAD_DIM] = o[r * BLOCK:(r + 1) * BLOCK].astype(o_ref.dtype)


def _nsa_prompt(qT, kc, vcT, ksel, vselT, kwin, vwinT, st, wt, ct, gates, s):
    nb = s // BLOCK
    nq = s // BLOCK
    nl = N_REP * BLOCK
    nch = s // CHUNK_KEYS

    def whole(arr):
        shp = arr.shape[1:]
        return pl.BlockSpec((None,) + shp, lambda g, i: (g,) + (0,) * len(shp))

    return pl.pallas_call(
        functools.partial(_nsa_prompt_kernel, nb=nb),
        grid=(N_GROUPS, nq),
        in_specs=[pl.BlockSpec((None, HEAD_DIM, nl), lambda g, i: (g, 0, i)),
                  whole(kc), whole(vcT), whole(ksel), whole(vselT), whole(kwin), whole(vwinT),
                  whole(st), whole(wt), whole(ct),
                  pl.BlockSpec((None, None, 8, nl), lambda g, i: (g, i, 0, 0))],
        out_specs=pl.BlockSpec((BLOCK, N_REP * HEAD_DIM), lambda g, i: (i, g)),
        out_shape=jax.ShapeDtypeStruct((s, NSA_WIDTH), BF16),
        scratch_shapes=[pltpu.VMEM((nb, nl), F32),
                        pltpu.VMEM((nb, nl), F32),
                        pltpu.VMEM((CHUNK_KEYS, nl), F32),
                        pltpu.VMEM((CHUNK_KEYS, nl), F32),
                        pltpu.VMEM((CHUNK_KEYS, nl), BF16),
                        pltpu.VMEM((CHUNK_KEYS, nl), BF16)],
        compiler_params=_cparams(("parallel", "arbitrary")),
        name="nsa_prompt",
    )(qT, kc, vcT, ksel, vselT, kwin, vwinT, st, wt, ct, gates)


def _sample_nsa_kernel(pt_ref, cache_ref, newc_ref, qT_ref, knew_ref, vnew_ref, wst_ref, kwn_ref, vwn_ref,
                       pe_ref, w1_ref, w2_ref, cb_ref, nearb_ref, farb_ref, newb_ref, winb_ref, gate_ref, o_ref,
                       ring, xc, kcs, vsel, s_scr, selneg_scr, sems, *, n_pages, t_new, nbp, n_ring):
    b = pl.program_id(0)
    total = pl.num_programs(0) * n_pages
    past = n_pages * PAGE
    cur = past // BLOCK
    kinds_per_row = PAGED_KINDS * N_GROUPS
    qT = qT_ref[...]

    def page_dma(t, slot):
        return pltpu.make_async_copy(cache_ref.at[pt_ref[t // n_pages, t % n_pages]], ring.at[slot], sems.at[slot])

    @pl.when(b == 0)
    def _():
        for t in range(n_ring):
            page_dma(t, t).start()

    far = farb_ref[0:1, :]

    def page_step(j, c):
        t = b * n_pages + j
        slot = t % n_ring
        page_dma(t, slot).wait()
        buf = ring.at[slot]

        def rows(kg):
            return buf[pl.ds(kg, PAGE, stride=kinds_per_row), :]

        r0 = pl.multiple_of(j * PAGE, PAGE)
        base = pl.multiple_of(j * (PAGE // BLOCK) * XC_PITCH, 8)
        for kg in range(2 * N_GROUPS):
            x = rows(kg)
            for h in range(PAGE // BLOCK):
                xc[kg, pl.ds(base + h * XC_PITCH, BLOCK), :] = x[h * BLOCK:(h + 1) * BLOCK]
        k = jnp.concatenate([rows(4), rows(5)], axis=1).astype(BF16)
        s_scr[pl.ds(r0, PAGE), :] = _qk(k, qT) + jnp.where(j == n_pages - 1, nearb_ref[...], far)
        vsel[pl.ds(r0, PAGE), :] = jnp.concatenate([rows(6), rows(7)], axis=1).astype(BF16)

        @pl.when(t + n_ring < total)
        def _():
            page_dma(t + n_ring, slot).start()

        return c

    lax.fori_loop(0, n_pages, page_step, 0)

    tail = (nbp - cur) * XC_PITCH
    for kg in range(2 * N_GROUPS):
        xc[kg, pl.ds(cur * XC_PITCH, tail), :] = jnp.zeros((tail, HEAD_DIM), F32)
        xc[kg, pl.ds(cur * XC_PITCH, t_new), :] = newc_ref[:, kg * HEAD_DIM:(kg + 1) * HEAD_DIM]
    s_scr[pl.ds(past, PAGE), :] = _qk(knew_ref[...], qT) + newb_ref[...]
    vsel[pl.ds(past, PAGE), :] = vnew_ref[...]

    for kind in range(2):
        def step(pp, acc):
            p = 2 * pp
            pe0 = pe_ref[kind, pl.ds(p, 1), :]
            pe1 = pe_ref[kind, pl.ds(p + 1, 1), :]
            parts = []
            for g in range(N_GROUPS):
                x0 = xc[2 * kind + g, pl.ds(p, nbp, stride=XC_PITCH), :] + pe0
                x1 = xc[2 * kind + g, pl.ds(p + 1, nbp, stride=XC_PITCH), :] + pe1
                parts.append(jnp.concatenate([x0, x1], axis=1))
            a = jnp.concatenate(parts, axis=0).astype(BF16)
            w = w1_ref[kind, pl.ds(pl.multiple_of(p * HEAD_DIM, 2 * HEAD_DIM), 2 * HEAD_DIM), :]
            return acc + _dot(a, w)

        acc = lax.fori_loop(0, BLOCK // 2, step, jnp.zeros((N_GROUPS * nbp, PHI_HIDDEN), F32), unroll=2)
        out = _dot(_gelu_tanh(acc).astype(BF16), w2_ref[kind])
        kcs[kind, nbp:, :] = jnp.zeros((kcs.shape[1] - nbp, N_GROUPS * HEAD_DIM), F32)
        for g in range(N_GROUPS):
            kcs[kind, 0:nbp, g * HEAD_DIM:(g + 1) * HEAD_DIM] = out[g * nbp:(g + 1) * nbp]

    def softmax_rows(s):
        mx = jnp.max(s, axis=0, keepdims=True)
        p = jnp.where(s > 0.5 * NEG, jnp.exp(s - mx), 0.0)
        l = jnp.sum(p, axis=0, keepdims=True)
        return p / jnp.where(l > 0.0, l, 1.0)

    pn = softmax_rows(_qk(kcs[0].astype(BF16), qT) + cb_ref[...])
    o_cmp = _dot(pn.T.astype(BF16), kcs[1].astype(BF16))

    tot = pn + pltpu.roll(pn, t_new, 1) + pltpu.roll(pn, 2 * t_new, 1) + pltpu.roll(pn, 3 * t_new, 1)
    lane = lax.broadcasted_iota(jnp.int32, pn.shape, 1)
    top = jnp.where((lane // t_new) % N_REP == N_REP - 1, tot, 0.0)
    imp = top + pltpu.roll(top, LANES - t_new, 1) + pltpu.roll(top, LANES - 2 * t_new, 1) \
        + pltpu.roll(top, LANES - 3 * t_new, 1)
    n_io = lax.broadcasted_iota(jnp.int32, pn.shape, 0)
    forced = (n_io == 0) | (n_io == cur) | (n_io == cur - 1)
    score = jnp.where(forced, FORCED_SCORE, imp)
    score = jnp.where(n_io <= cur, score, -jnp.inf)
    sel = _select_blocks(score, forced, n_io)
    selneg_scr[...] = jnp.where(sel > 0.5, 0.0, NEG)

    n_st = wst_ref.shape[0] // (2 * N_GROUPS)

    def st_rows(j):
        return wst_ref[pl.ds(j, n_st, stride=2 * N_GROUPS), :]

    kw = jnp.concatenate([st_rows(0), st_rows(1)], axis=1).astype(BF16)
    vw = jnp.concatenate([st_rows(2), st_rows(3)], axis=1).astype(BF16)
    s_w = jnp.concatenate([_qk(kw, qT) + winb_ref[0:n_st, :], _qk(kwn_ref[...], qT) + winb_ref[n_st:, :]], axis=0)
    pwt = softmax_rows(s_w).T.astype(BF16)
    o_win = _dot(pwt[:, :n_st], vw) + _dot(pwt[:, n_st:], vwn_ref[...])

    blk = PAGE // BLOCK

    def mask_step(c, mx):
        r0 = pl.multiple_of(c * PAGE, PAGE)
        neg = jnp.concatenate(
            [jnp.broadcast_to(selneg_scr[pl.ds(c * blk + h, 1), :], (BLOCK, LANES)) for h in range(blk)], axis=0)
        s = s_scr[pl.ds(r0, PAGE), :] + neg
        s_scr[pl.ds(r0, PAGE), :] = s
        return jnp.maximum(mx, jnp.max(s, axis=0, keepdims=True))

    mx = lax.fori_loop(0, n_pages + 1, mask_step, jnp.full((1, LANES), NEG, F32))

    def pv_rows(r0, n_rows, carry):
        l_run, acc = carry
        pp = jnp.exp(s_scr[pl.ds(r0, n_rows), :] - mx)
        acc = acc + _dot(pp.T.astype(BF16), vsel[pl.ds(r0, n_rows), :])
        return l_run + jnp.sum(pp, axis=0, keepdims=True), acc

    pv_pages = 8 if n_pages % 8 == 0 else 1
    pv_keys = pv_pages * PAGE
    carry = lax.fori_loop(0, n_pages // pv_pages,
                          lambda c, carry: pv_rows(pl.multiple_of(c * pv_keys, pv_keys), pv_keys, carry),
                          (jnp.zeros((1, LANES), F32), jnp.zeros((LANES, N_GROUPS * HEAD_DIM), F32)))
    l_fin, acc = pv_rows(past, PAGE, carry)
    l_col = jnp.broadcast_to(l_fin, (LANES, LANES)).T[:, 0:1]
    o_sel = acc / l_col

    gates = _sigmoid(gate_ref[...])
    o_ref[...] = gates[:, 0:1] * o_cmp + gates[:, 1:2] * o_sel + gates[:, 2:3] * o_win


def _sample_nsa(pt, cache_pages, new_cmp, qT, knew, vnew, win_state, layer, kwn, vwn, pe, w1b, w2b,
                cb, nearb, farb, newb, winb, gates, nbp):
    bsz, n_pages = pt.shape
    t_new = new_cmp.shape[1]
    gh = N_GROUPS * HEAD_DIM
    n_keys = n_pages * PAGE + PAGE
    n_ring = min(SAMPLE_RING_PAGES, n_pages)

    def per_b(arr):
        shp = arr.shape[1:]
        return pl.BlockSpec((None,) + shp, lambda b, pt: (b,) + (0,) * len(shp))

    def const(arr, **kw):
        return pl.BlockSpec(arr.shape, lambda b, pt: (0,) * arr.ndim, **kw)

    grid_spec = pltpu.PrefetchScalarGridSpec(
        num_scalar_prefetch=1,
        grid=(bsz,),
        in_specs=[pl.BlockSpec(memory_space=pl.ANY),
                  per_b(new_cmp), per_b(qT), per_b(knew), per_b(vnew),
                  pl.BlockSpec((None,) + win_state.shape[1:], lambda b, pt: (layer * bsz + b, 0, 0)),
                  per_b(kwn), per_b(vwn),
                  const(pe), const(w1b, pipeline_mode=pl.Buffered(1)), const(w2b),
                  const(cb), const(nearb), const(farb), const(newb), const(winb), per_b(gates)],
        out_specs=pl.BlockSpec((None, LANES, gh), lambda b, pt: (b, 0, 0)),
        scratch_shapes=[pltpu.VMEM((n_ring,) + cache_pages.shape[1:], F32),
                        pltpu.VMEM((2 * N_GROUPS, nbp * XC_PITCH, HEAD_DIM), F32),
                        pltpu.VMEM((2, cb.shape[0], gh), F32),
                        pltpu.VMEM((n_keys, gh), BF16),
                        pltpu.VMEM((n_keys, LANES), F32),
                        pltpu.VMEM((cb.shape[0], LANES), F32),
                        pltpu.SemaphoreType.DMA((n_ring,))],
    )
    return pl.pallas_call(
        functools.partial(_sample_nsa_kernel, n_pages=n_pages, t_new=t_new, nbp=nbp, n_ring=n_ring),
        grid_spec=grid_spec,
        out_shape=jax.ShapeDtypeStruct((bsz, LANES, gh), F32),
        compiler_params=_cparams(("arbitrary",)),
        name="sample_nsa",
    )(pt, cache_pages, new_cmp, qT, knew, vnew, win_state, kwn, vwn, pe, w1b, w2b,
      cb, nearb, farb, newb, winb, gates)


def _gmlp_kernel(u_ref, v_ref, m_ref, bcol_ref, g_ref, b_ref, oa_ref, vn_ref):
    vn = _ln(v_ref[...]) * g_ref[...] + b_ref[...]
    vn_ref[...] = vn
    vb = vn.astype(BF16)
    u = u_ref[...]
    gd = GMLP_WIDTH // GMLP_GROUPS
    for g in range(GMLP_GROUPS):
        mixed = _dot(m_ref[g], vb[:, g * gd:(g + 1) * gd]) + bcol_ref[:, g:g + 1]
        oa_ref[:, g * gd:(g + 1) * gd] = (u[:, g * gd:(g + 1) * gd] * mixed).astype(oa_ref.dtype)


def _gmlp(z, mix_m, bcol, gn_g, gn_b, tc):
    t = z.shape[0]
    return pl.pallas_call(
        _gmlp_kernel,
        grid=(t // tc,),
        in_specs=[pl.BlockSpec((tc, GMLP_WIDTH), lambda i: (i, COL_U // GMLP_WIDTH)),
                  pl.BlockSpec((tc, GMLP_WIDTH), lambda i: (i, COL_V // GMLP_WIDTH)),
                  pl.BlockSpec((GMLP_GROUPS, tc, tc), lambda i: (0, 0, 0)),
                  pl.BlockSpec((tc, LANES), lambda i: (0, 0)),
                  pl.BlockSpec((1, GMLP_WIDTH), lambda i: (0, 0)),
                  pl.BlockSpec((1, GMLP_WIDTH), lambda i: (0, 0))],
        out_specs=[pl.BlockSpec((tc, GMLP_WIDTH), lambda i: (i, 0)),
                   pl.BlockSpec((tc, GMLP_WIDTH), lambda i: (i, 0))],
        out_shape=[jax.ShapeDtypeStruct((t, GMLP_WIDTH), BF16),
                   jax.ShapeDtypeStruct((t, GMLP_WIDTH), F32)],
        compiler_params=_cparams(("parallel",)),
        name="gmlp",
    )(z, z, mix_m, bcol, gn_g, gn_b)


def _pool_kernel(x_ref, halo_ref, pw_ref, ps_ref, o_ref, ext_scr, *, tm, zero_first, pos0, pos_step):
    i = pl.program_id(0)
    halo = halo_ref[...]
    if zero_first:
        halo = jnp.where(i == 0, 0.0, halo)
    ext_scr[0:16, :] = halo
    x = x_ref[...]
    ext_scr[16:16 + tm, :] = x
    pos = pos0 + i * pos_step + lax.broadcasted_iota(jnp.int32, (tm, POOL_GROUP_DIM), 0)
    for gi, w in enumerate(POOL_WINDOWS):
        c0 = gi * POOL_GROUP_DIM
        acc = ext_scr[16:16 + tm, c0:c0 + POOL_GROUP_DIM]
        for k in range(1, w):
            acc = acc + ext_scr[16 - k:16 - k + tm, c0:c0 + POOL_GROUP_DIM]
        count = jnp.minimum(pos + 1, w).astype(F32)
        d = acc / count - x[:, c0:c0 + POOL_GROUP_DIM]
        y = _dot(d.astype(BF16), pw_ref[gi]) * ps_ref[:, c0:c0 + POOL_GROUP_DIM]
        o_ref[:, c0:c0 + POOL_GROUP_DIM] = y.astype(o_ref.dtype)


def _pool(z, halo_src, halo_map, pwb, ps, tm, zero_first, pos0, pos_step):
    t = z.shape[0]
    return pl.pallas_call(
        functools.partial(_pool_kernel, tm=tm, zero_first=zero_first, pos0=pos0, pos_step=pos_step),
        grid=(t // tm,),
        in_specs=[pl.BlockSpec((tm, POOL_WIDTH), lambda i: (i, COL_XC // POOL_WIDTH)),
                  pl.BlockSpec((16, POOL_WIDTH), halo_map),
                  pl.BlockSpec((len(POOL_WINDOWS), POOL_GROUP_DIM, POOL_GROUP_DIM), lambda i: (0, 0, 0)),
                  pl.BlockSpec((1, POOL_WIDTH), lambda i: (0, 0))],
        out_specs=pl.BlockSpec((tm, POOL_WIDTH), lambda i: (i, 0)),
        out_shape=jax.ShapeDtypeStruct((t, POOL_WIDTH), BF16),
        scratch_shapes=[pltpu.VMEM((16 + tm, POOL_WIDTH), F32)],
        compiler_params=_cparams(("parallel",)),
        name="pool",
    )(z, halo_src, pwb, ps)


def _route(logits_t, rb_col):
    aff = _sigmoid(logits_t)
    biased = aff + rb_col
    rows = [biased[e:e + 1] for e in range(N_EXPERTS)]
    top2 = []
    gscore = []
    for g in range(N_EXPERT_GROUPS):
        grp = rows[g * EXPERTS_PER_GROUP:(g + 1) * EXPERTS_PER_GROUP]
        gs = None
        for a in range(EXPERTS_PER_GROUP):
            rank = None
            for c in range(EXPERTS_PER_GROUP):
                if c == a:
                    continue
                ahead = (grp[c] >= grp[a]) if c < a else (grp[c] > grp[a])
                rank = ahead.astype(F32) if rank is None else rank + ahead.astype(F32)
            in2 = rank < 1.5
            top2.append(in2)
            contrib = jnp.where(in2, grp[a], 0.0)
            gs = contrib if gs is None else gs + contrib
        gscore.append(gs)
    out = []
    for g in range(N_EXPERT_GROUPS):
        win = None
        for c in range(N_EXPERT_GROUPS):
            if c == g:
                continue
            ok = (gscore[g] > gscore[c]) if c < g else (gscore[g] >= gscore[c])
            win = ok if win is None else (win & ok)
        for a in range(EXPERTS_PER_GROUP):
            e = g * EXPERTS_PER_GROUP + a
            out.append(jnp.where(win & top2[e], aff[e:e + 1], 0.0))
    selw = jnp.concatenate(out, axis=0)
    return selw / jnp.sum(selw, axis=0, keepdims=True)


def _merge_kernel(x_ref, oa_ref, ob_ref, oc_ref, g0_ref, g1_ref, g2_ref, gt1_ref, sh2_ref, sc2_ref,
                  ln1g_ref, ln1b_ref, pa_ref, pb_ref, pc_ref, wo_ref, rw_ref, rb_ref,
                  x1_ref, h2_ref, comb_ref, *, alpha):
    tm = x_ref.shape[0]

    merged = _sigmoid(g0_ref[...]) * _dot(oa_ref[...], pa_ref[...])
    merged = merged + _sigmoid(g1_ref[...]) * _dot(ob_ref[...], pb_ref[...])
    merged = merged + _sigmoid(g2_ref[...]) * _dot(oc_ref[...], pc_ref[...])
    mix = _dot(merged.astype(BF16), wo_ref[...])
    x1 = _ln(alpha * x_ref[...] + (1.0 + _mod_rows(gt1_ref, tm)) * mix) * ln1g_ref[...] + ln1b_ref[...]
    x1_ref[...] = x1
    h2 = _ln(x1) * (1.0 + _mod_rows(sc2_ref, tm)) + _mod_rows(sh2_ref, tm)
    h2_ref[...] = h2.astype(BF16)
    logits = _dot(h2.astype(BF16), rw_ref[...])
    comb_t = _route(logits.T[:N_EXPERTS], rb_ref[...][:N_EXPERTS, 0:1])
    comb_full = jnp.concatenate([comb_t, jnp.zeros((LANES - N_EXPERTS, tm), F32)], axis=0)
    comb_ref[...] = comb_full.T


def _merge(x, z, oa, ob, oc, mods, ln1g, ln1b, pa, pb, pc, wo, rw, rb, tm, alpha):
    t = x.shape[0]

    def const(arr):
        return pl.BlockSpec(arr.shape, lambda i: (0,) * arr.ndim, pipeline_mode=pl.Buffered(1))

    gcol = COL_GM // D_MODEL
    return pl.pallas_call(
        functools.partial(_merge_kernel, alpha=alpha),
        grid=(t // tm,),
        in_specs=[pl.BlockSpec((tm, D_MODEL), lambda i: (i, 0)),
                  pl.BlockSpec((tm, GMLP_WIDTH), lambda i: (i, 0)),
                  pl.BlockSpec((tm, NSA_WIDTH), lambda i: (i, 0)),
                  pl.BlockSpec((tm, POOL_WIDTH), lambda i: (i, 0)),
                  pl.BlockSpec((tm, D_MODEL), lambda i: (i, gcol)),
                  pl.BlockSpec((tm, D_MODEL), lambda i: (i, gcol + 1)),
                  pl.BlockSpec((tm, D_MODEL), lambda i: (i, gcol + 2)),
                  _mod_spec(mods, tm, 2), _mod_spec(mods, tm, 3), _mod_spec(mods, tm, 4),
                  const(ln1g), const(ln1b), const(pa), const(pb), const(pc), const(wo), const(rw), const(rb)],
        out_specs=[pl.BlockSpec((tm, D_MODEL), lambda i: (i, 0)),
                   pl.BlockSpec((tm, D_MODEL), lambda i: (i, 0)),
                   pl.BlockSpec((tm, LANES), lambda i: (i, 0))],
        out_shape=[jax.ShapeDtypeStruct((t, D_MODEL), F32),
                   jax.ShapeDtypeStruct((t, D_MODEL), BF16),
                   jax.ShapeDtypeStruct((t, LANES), F32)],
        compiler_params=_cparams(("parallel",)),
        name="merge",
    )(x, oa, ob, oc, z, z, z, mods, mods, mods, ln1g, ln1b, pa, pb, pc, wo, rw, rb)


def _moe_kernel(h_ref, comb_ref, wg_ref, wu_ref, wd_ref, x1_ref, gt2_ref, g_ref, b_ref, o_ref, acc_scr, *, alpha):
    e = pl.program_id(1)
    tm = h_ref.shape[0]

    @pl.when(e == 0)
    def _():
        acc_scr[...] = jnp.zeros_like(acc_scr)

    h = h_ref[...]
    comb = comb_ref[...]
    lane = lax.broadcasted_iota(jnp.int32, comb.shape, 1)
    w_col = jnp.sum(jnp.where(lane == e, comb, 0.0), axis=1, keepdims=True)
    act = _silu(_dot(h, wg_ref[...])) * _dot(h, wu_ref[...])
    acc_scr[...] += _dot((act * w_col).astype(BF16), wd_ref[...])

    @pl.when(e == N_EXPERTS - 1)
    def _():
        o_ref[...] = _ln(alpha * x1_ref[...] + (1.0 + _mod_rows(gt2_ref, tm)) * acc_scr[...]) * g_ref[...] + b_ref[...]


def _moe(h2, comb, wg, wu, wd, x1, mods, ln2g, ln2b, tm, alpha):
    t = h2.shape[0]
    return pl.pallas_call(
        functools.partial(_moe_kernel, alpha=alpha),
        grid=(t // tm, N_EXPERTS),
        in_specs=[pl.BlockSpec((tm, D_MODEL), lambda i, e: (i, 0)),
                  pl.BlockSpec((tm, LANES), lambda i, e: (i, 0)),
                  pl.BlockSpec((None, D_MODEL, D_EXPERT), lambda i, e: (e, 0, 0)),
                  pl.BlockSpec((None, D_MODEL, D_EXPERT), lambda i, e: (e, 0, 0)),
                  pl.BlockSpec((None, D_EXPERT, D_MODEL), lambda i, e: (e, 0, 0)),
                  pl.BlockSpec((tm, D_MODEL), lambda i, e: (i, 0)),
                  _mod_spec(mods, tm, 5),
                  pl.BlockSpec((1, D_MODEL), lambda i, e: (0, 0)),
                  pl.BlockSpec((1, D_MODEL), lambda i, e: (0, 0))],
        out_specs=pl.BlockSpec((tm, D_MODEL), lambda i, e: (i, 0)),
        out_shape=jax.ShapeDtypeStruct((t, D_MODEL), F32),
        scratch_shapes=[pltpu.VMEM((tm, D_MODEL), F32)],
        compiler_params=_cparams(("parallel", "arbitrary")),
        name="moe",
    )(h2, comb, wg, wu, wd, x1, mods, ln2g, ln2b)


def _row_gather(src_ref, idx_ref, base, buf, slot, sem, n_rows, start):
    def body(blk, c):
        for j in range(ROW_DMA_GROUP):
            r = blk * ROW_DMA_GROUP + j
            cp = pltpu.make_async_copy(src_ref.at[pl.ds(idx_ref[base + r], 1), :],
                                       buf.at[slot, pl.ds(r, 1), :], sem.at[slot])
            if start:
                cp.start(priority=j % 2)
            else:
                cp.wait()
        return c

    lax.fori_loop(0, n_rows // ROW_DMA_GROUP, body, 0)


def _moe_group_kernel(perm_ref, tgrp_ref, nlive_ref, x1_ref, comb_ref, sh2_ref, sc2_ref, wg_ref, wu_ref, wd_ref,
                      y_ref, xbuf, sems):
    k = pl.program_id(0)
    tg = comb_ref.shape[0]
    n_live = nlive_ref[0]

    @pl.when(k == 0)
    def _():
        _row_gather(x1_ref, perm_ref, 0, xbuf, 0, sems, tg, True)

    @pl.when(k + 1 < n_live)
    def _():
        _row_gather(x1_ref, perm_ref, (k + 1) * tg, xbuf, (k + 1) % 2, sems, tg, True)

    @pl.when(k < jnp.maximum(n_live, 1))
    def _():
        _row_gather(x1_ref, perm_ref, k * tg, xbuf, k % 2, sems, tg, False)

    @pl.when(k < n_live)
    def _():
        h = (_ln(xbuf[k % 2]) * (1.0 + sc2_ref[0:1, :]) + sh2_ref[0:1, :]).astype(BF16)
        comb = comb_ref[...]
        lane = lax.broadcasted_iota(jnp.int32, comb.shape, 1)
        e0 = tgrp_ref[k] * EXPERTS_PER_GROUP
        acc = jnp.zeros((tg, D_MODEL), F32)
        for e in range(EXPERTS_PER_GROUP):
            w_col = jnp.sum(jnp.where(lane == e0 + e, comb, 0.0), axis=1, keepdims=True)
            act = _silu(_dot(h, wg_ref[e])) * _dot(h, wu_ref[e])
            acc = acc + _dot((act * w_col).astype(BF16), wd_ref[e])
        y_ref[...] = acc

    @pl.when(k >= n_live)
    def _():
        y_ref[...] = jnp.zeros(y_ref.shape, F32)


def _moe_finish_kernel(pos_ref, y_ref, x1_ref, gt2_ref, g_ref, b_ref, o_ref, ybuf, sems, *, alpha):
    i = pl.program_id(0)
    tm = x1_ref.shape[0]

    @pl.when(i == 0)
    def _():
        _row_gather(y_ref, pos_ref, 0, ybuf, 0, sems, tm, True)

    @pl.when(i + 1 < pl.num_programs(0))
    def _():
        _row_gather(y_ref, pos_ref, (i + 1) * tm, ybuf, (i + 1) % 2, sems, tm, True)

    _row_gather(y_ref, pos_ref, i * tm, ybuf, i % 2, sems, tm, False)
    o_ref[...] = _ln(alpha * x1_ref[...] + (1.0 + gt2_ref[0:1, :]) * ybuf[i % 2]) * g_ref[...] + b_ref[...]


def _moe_dispatch(comb, tg):
    t = comb.shape[0]
    i32 = jnp.int32
    cg = comb[:, :N_EXPERTS].reshape(t, N_EXPERT_GROUPS, EXPERTS_PER_GROUP)
    gid = jnp.argmax(jnp.sum(cg, axis=-1) > 0.0, axis=-1).astype(i32)
    onehot = (gid[:, None] == jnp.arange(N_EXPERT_GROUPS, dtype=i32)[None, :]).astype(i32)
    csum = jnp.cumsum(onehot, axis=0)
    rank = jnp.take_along_axis(csum, gid[:, None], axis=1)[:, 0] - 1
    pcount = ((csum[-1] + tg - 1) // tg) * tg
    ends = jnp.cumsum(pcount)
    pos = ((ends - pcount)[gid] + rank).astype(i32)
    rp = t + N_EXPERT_GROUPS * tg
    perm = jnp.full((rp,), -1, i32).at[pos].set(jnp.arange(t, dtype=i32))
    live = (perm >= 0).astype(F32)
    perm = jnp.maximum(perm, 0)
    comb_sorted = comb[perm] * live[:, None]
    tile_start = jnp.arange(rp // tg, dtype=i32) * tg
    tile_group = jnp.minimum(jnp.sum(tile_start[:, None] >= ends[None, :], axis=1), N_EXPERT_GROUPS - 1).astype(i32)
    n_live = (ends[-1] // tg).astype(i32).reshape(1)
    return perm, pos, tile_group, n_live, comb_sorted


def _moe_grouped(x1, comb, wg, wu, wd, mods, ln2g, ln2b, alpha):
    t = x1.shape[0]
    tg = MOE_GROUP_TILE
    perm, pos, tile_group, n_live, comb_sorted = _moe_dispatch(comb, tg)
    rp = perm.shape[0]
    epg = EXPERTS_PER_GROUP

    def wspec(shape):
        return pl.BlockSpec((epg,) + shape, lambda k, perm, tgrp, nl: (tgrp[k], 0, 0), pipeline_mode=pl.Buffered(1))

    y = pl.pallas_call(
        _moe_group_kernel,
        grid_spec=pltpu.PrefetchScalarGridSpec(
            num_scalar_prefetch=3,
            grid=(rp // tg,),
            in_specs=[pl.BlockSpec(memory_space=pl.ANY),
                      pl.BlockSpec((tg, LANES), lambda k, *_: (k, 0)),
                      pl.BlockSpec((8, D_MODEL), lambda k, *_: (0, 3)),
                      pl.BlockSpec((8, D_MODEL), lambda k, *_: (0, 4)),
                      wspec((D_MODEL, D_EXPERT)), wspec((D_MODEL, D_EXPERT)), wspec((D_EXPERT, D_MODEL))],
            out_specs=pl.BlockSpec((tg, D_MODEL), lambda k, *_: (k, 0)),
            scratch_shapes=[pltpu.VMEM((2, tg, D_MODEL), F32), pltpu.SemaphoreType.DMA((2,))]),
        out_shape=jax.ShapeDtypeStruct((rp, D_MODEL), F32),
        compiler_params=_cparams(("arbitrary",)),
        name="moe_group",
    )(perm, tile_group, n_live, x1, comb_sorted, mods, mods, wg, wu, wd)

    tm = MOE_FINISH_TILE
    return pl.pallas_call(
        functools.partial(_moe_finish_kernel, alpha=alpha),
        grid_spec=pltpu.PrefetchScalarGridSpec(
            num_scalar_prefetch=1,
            grid=(t // tm,),
            in_specs=[pl.BlockSpec(memory_space=pl.ANY),
                      pl.BlockSpec((tm, D_MODEL), lambda i, pos: (i, 0)),
                      pl.BlockSpec((8, D_MODEL), lambda i, pos: (0, 5)),
                      pl.BlockSpec((1, D_MODEL), lambda i, pos: (0, 0)),
                      pl.BlockSpec((1, D_MODEL), lambda i, pos: (0, 0))],
            out_specs=pl.BlockSpec((tm, D_MODEL), lambda i, pos: (i, 0)),
            scratch_shapes=[pltpu.VMEM((2, tm, D_MODEL), F32), pltpu.SemaphoreType.DMA((2,))]),
        out_shape=jax.ShapeDtypeStruct((t, D_MODEL), F32),
        compiler_params=_cparams(("arbitrary",)),
        name="moe_finish",
    )(pos, y, x1, mods, ln2g, ln2b)


def _rel_bucket(dist):
    n = jnp.maximum(dist, 0)
    max_exact = N_BUCKETS // 2
    nf = jnp.maximum(n, max_exact).astype(F32)
    large = max_exact + (jnp.log(nf / max_exact) / math.log(MAX_DISTANCE / max_exact)
                         * (N_BUCKETS - max_exact)).astype(jnp.int32)
    return jnp.where(n < max_exact, n, jnp.minimum(large, N_BUCKETS - 1))


def _bias_of(rel_bias, dist):
    return rel_bias[_rel_bucket(dist)].astype(F32)


def _prompt_tiles(rel_bias):
    kk = jnp.arange(BLOCK)[:, None]
    qq = jnp.arange(BLOCK)[None, :]

    def lanes(tile8, g):
        return jnp.concatenate([tile8[:, :, g * N_REP + r] for r in range(N_REP)], axis=1)

    def tile(m, lo, hi):
        d = m * BLOCK + qq - kk
        ok = (d >= lo) & (d < hi)
        return jnp.where(ok[:, :, None], _bias_of(rel_bias, d), NEG)

    big = 1 << 20
    neg_tile = jnp.full((BLOCK, BLOCK, N_HEADS), NEG, F32)
    sel_tiles = [tile(m, 0, big) for m in range(4)] + [neg_tile]
    win_tiles = [tile(m, 0, WINDOW) for m in range(WIN_BLOCKS + 1)] + [neg_tile]
    st = jnp.stack([jnp.stack([lanes(t, g) for t in sel_tiles]) for g in range(N_GROUPS)]) / QK_SCALE
    wt = jnp.stack([jnp.stack([lanes(t, g) for t in win_tiles]) for g in range(N_GROUPS)]) / QK_SCALE
    rows = []
    for m in range(4):
        d = m * BLOCK + jnp.arange(BLOCK) - (BLOCK - 1)
        rows.append(_bias_of(rel_bias, d))
    ct = jnp.stack([jnp.stack([jnp.concatenate([rw[:, g * N_REP + r] for r in range(N_REP)]) for rw in rows]
                              + [jnp.zeros((N_REP * BLOCK,), F32)] * 4) for g in range(N_GROUPS)])
    return st, wt, ct


def _sample_lanes(tile8, t_new):
    rows = tile8.shape[0]
    x = jnp.transpose(tile8, (0, 2, 1)).reshape(rows, N_HEADS * t_new)
    return jnp.pad(x, ((0, 0), (0, LANES - N_HEADS * t_new)))


def _sample_tiles(rel_bias, past, t_new, nbp, win_rows, win_pad):
    tq = past + jnp.arange(t_new)[None, :]

    def tile(k_pos, ok_extra=None, lo=0, hi=1 << 20):
        d = tq - k_pos[:, None]
        ok = (d >= lo) & (d < hi)
        if ok_extra is not None:
            ok = ok & ok_extra[:, None]
        return _sample_lanes(jnp.where(ok[:, :, None], _bias_of(rel_bias, d), NEG), t_new)

    n = jnp.arange(nbp)
    cb = tile(n * BLOCK + BLOCK - 1, n < past // BLOCK + 1)
    nearb = tile(past - PAGE + jnp.arange(PAGE))
    farb = jnp.broadcast_to(_sample_lanes(jnp.broadcast_to(
        rel_bias[N_BUCKETS - 1].astype(F32)[None, None, :], (1, t_new, N_HEADS)), t_new), (8, LANES))
    kn = jnp.arange(PAGE)
    newb = tile(past + kn, kn < t_new)
    wb = min(WINDOW, past)
    j = jnp.arange(win_pad)
    k_pos = past - wb + j
    winb = tile(k_pos, (j < win_rows) & (k_pos >= 0), 0, WINDOW)
    return cb, nearb, farb, newb, winb


def _mix_mats(ws, bs, tc, rep):
    n = tc // rep
    causal = jnp.tril(jnp.ones((CHUNK, CHUNK), F32))
    wc = (ws * causal)[:, :n, :n]
    eye = jnp.eye(rep, dtype=F32)
    m = jnp.einsum('ab,gts->gatbs', eye, wc).reshape(GMLP_GROUPS, tc, tc)
    bcol = jnp.tile(bs[:, :n].T, (rep, 1))
    return m.astype(BF16), jnp.pad(bcol, ((0, 0), (0, LANES - GMLP_GROUPS)))


def kernel(x_prompt, x_sample, cache_nsa_kv, state_win_kv, state_pool, page_table, c_prompt, c_sample, rel_bias, router_w, router_b, w_in, nsa_phi_pe, nsa_phi_w1, nsa_phi_w2, gmlp_norm_g, gmlp_norm_b, gmlp_ws, gmlp_bs, pool_w, pool_scale, proj_a, proj_b, proj_c, w_o, ada_w, ada_b, ln1_g, ln1_b, ln2_g, ln2_b, exp_w_gate, exp_w_up, exp_w_down):
    depth = w_in.shape[0]
    alpha = (2 * depth) ** 0.25
    s = x_prompt.shape[1]
    bsz, t_new = x_sample.shape[0], x_sample.shape[1]
    n_pool = cache_nsa_kv.shape[1]
    n_pages = page_table.shape[1]
    past = n_pages * PAGE
    ts = bsz * t_new
    assert x_prompt.shape[0] == 1 and s % CHUNK_KEYS == 0 and past % PAGE == 0
    assert N_HEADS * t_new <= LANES and t_new <= BLOCK and (past // BLOCK) >= N_SEL
    nb_s = past // BLOCK + 1
    nbp = -(-nb_s // 8) * 8

    n_c = 1 + bsz
    c_all = jnp.pad(jnp.concatenate([c_prompt, c_sample], axis=0), ((0, -n_c % 8), (0, 0)))
    mods_all = _ada_mods(c_all, ada_w, ada_b)

    cache_pages = cache_nsa_kv.reshape(depth * n_pool, PAGE * PAGED_KINDS * N_GROUPS, HEAD_DIM)
    st, wt, ct = _prompt_tiles(rel_bias)
    n_win = state_win_kv.shape[2]
    win_rows = n_win + t_new
    win_pad = n_win + PAGE
    win_state = state_win_kv.reshape(depth * bsz, n_win * 2 * N_GROUPS, HEAD_DIM)
    assert nbp <= SAMPLE_BLOCKS_PAD and n_win == min(WINDOW, past) and n_win % 8 == 0
    cb_s, nearb_s, farb_s, newb_s, winb_s = _sample_tiles(rel_bias, past, t_new, SAMPLE_BLOCKS_PAD, win_rows, win_pad)
    rw = jnp.pad(router_w, ((0, 0), (0, LANES - N_EXPERTS))).astype(BF16)
    rb = jnp.pad(router_b.reshape(N_EXPERTS, 1), ((0, LANES - N_EXPERTS), (0, LANES - 1)))

    xp = x_prompt.reshape(s, D_MODEL)
    xs = x_sample.reshape(ts, D_MODEL)
    outs = {k: [] for k in ("nsa_p", "nsa_s", "win_p", "win_s", "pool_p", "pool_s", "v_s")}

    for l in range(depth):
        w_parts = _in_proj_weights(w_in[l])
        pe = nsa_phi_pe[l]
        w1b = nsa_phi_w1[l].astype(BF16)
        w2b = nsa_phi_w2[l].astype(BF16)
        pab, pbb, pcb, wob = (a[l].astype(BF16) for a in (proj_a, proj_b, proj_c, w_o))
        wg, wu, wd = (a[l].astype(BF16) for a in (exp_w_gate, exp_w_up, exp_w_down))
        pwb = pool_w[l].astype(BF16)
        ps = pool_scale[l].reshape(1, POOL_WIDTH)
        gng = gmlp_norm_g[l].reshape(1, GMLP_WIDTH)
        gnb = gmlp_norm_b[l].reshape(1, GMLP_WIDTH)
        ln1g, ln1b, ln2g, ln2b = (a[l].reshape(1, D_MODEL) for a in (ln1_g, ln1_b, ln2_g, ln2_b))
        mods_p = jnp.broadcast_to(mods_all[l, 0:1], (8, 6 * D_MODEL))
        mods_s = jnp.repeat(mods_all[l, 1:1 + bsz], t_new, axis=0)

        z, zb = _in_proj(xp, mods_p, w_parts, 1024)
        gh = N_GROUPS * HEAD_DIM
        outs["nsa_p"].append(z[:, COL_KV:COL_KV + PAGED_KINDS * gh].reshape(1, s, PAGED_KINDS, N_GROUPS, HEAD_DIM))
        outs["win_p"].append(z[s - min(WINDOW, s):, COL_KV + PAGED_KINDS * gh:COL_KV + KV_KINDS * gh]
                             .reshape(1, min(WINDOW, s), 2, N_GROUPS, HEAD_DIM))
        outs["pool_p"].append(z[s - POOL_BUF:, COL_XC:COL_XC + POOL_WIDTH].reshape(1, POOL_BUF, POOL_WIDTH))

        nb = s // BLOCK
        cmp = _compress_prompt(z, pe, w1b, w2b).reshape(2, N_GROUPS, nb, HEAD_DIM)
        kc = cmp[0].astype(BF16)
        vcT = jnp.transpose(cmp[1], (0, 2, 1)).astype(BF16)
        nch = s // CHUNK_KEYS

        def kv_cols(kind, g):
            c0 = COL_KV + (kind * N_GROUPS + g) * HEAD_DIM
            return zb[:, c0:c0 + HEAD_DIM].reshape(nch, CHUNK_KEYS, HEAD_DIM)

        def k_chunks(kind):
            return jnp.stack([kv_cols(kind, g) for g in range(N_GROUPS)])

        def vT_chunks(kind):
            vt = jnp.stack([jnp.transpose(kv_cols(kind, g), (0, 2, 1)) for g in range(N_GROUPS)])
            ones = jnp.zeros((N_GROUPS, nch, 8, CHUNK_KEYS), BF16).at[:, :, 0].set(1.0)
            return jnp.concatenate([vt, ones], axis=2)

        q = zb[:, COL_Q:COL_Q + NSA_WIDTH].reshape(nb, BLOCK, N_GROUPS, N_REP, HEAD_DIM)
        qT = jnp.transpose(q, (2, 4, 0, 3, 1)).reshape(N_GROUPS, HEAD_DIM, nb * N_REP * BLOCK)
        gn = z[:, COL_GN:COL_GN + 3 * N_HEADS].reshape(nb, BLOCK, 3, N_GROUPS, N_REP)
        gates = jnp.transpose(gn, (3, 0, 2, 4, 1)).reshape(N_GROUPS, nb, 3, N_REP * BLOCK)
        gates = jnp.pad(gates, ((0, 0), (0, 0), (0, 5), (0, 0)))
        ob = _nsa_prompt(qT, kc, vcT, k_chunks(2), vT_chunks(3), k_chunks(4), vT_chunks(5), st, wt, ct, gates, s)

        mm, bcol = _mix_mats(gmlp_ws[l], gmlp_bs[l], CHUNK, 1)
        oa, _ = _gmlp(z, mm, bcol, gng, gnb, CHUNK)
        tp = 512
        cxb = COL_XC // POOL_WIDTH
        oc = _pool(z, z, lambda i: (jnp.maximum(i * (tp // 16) - 1, 0), cxb), pwb, ps, tp, True, 0, tp)
        x1, h2, comb = _merge(xp, z, oa, ob, oc, mods_p, ln1g, ln1b, pab, pbb, pcb, wob, rw, rb, 256, alpha)
        xp = _moe_grouped(x1, comb, wg, wu, wd, mods_p, ln2g, ln2b, alpha)

        zs, zbs = _in_proj(xs, mods_s, w_parts, ts)
        kvs = zs[:, COL_KV:COL_KV + 1536].reshape(bsz, t_new, KV_KINDS, N_GROUPS, HEAD_DIM)
        kvbs = zbs[:, COL_KV:COL_KV + 1536].reshape(bsz, t_new, KV_KINDS, gh)
        outs["nsa_s"].append(kvs[:, :, :PAGED_KINDS])
        win_full = jnp.concatenate([state_win_kv[l], kvs[:, :, PAGED_KINDS:]], axis=1)
        outs["win_s"].append(win_full[:, win_full.shape[1] - min(WINDOW, past + t_new):])
        xcs = zs[:, COL_XC:COL_XC + POOL_WIDTH].reshape(bsz, t_new, POOL_WIDTH)
        ext = jnp.concatenate([state_pool[l], xcs], axis=1)
        outs["pool_s"].append(ext[:, ext.shape[1] - POOL_BUF:])

        pt = page_table + l * n_pool
        new_cmp = kvs[:, :, 0:2].reshape(bsz, t_new, 2 * N_GROUPS * HEAD_DIM)
        knew = jnp.pad(kvbs[:, :, 2], ((0, 0), (0, PAGE - t_new), (0, 0)))
        vnew = jnp.pad(kvbs[:, :, 3], ((0, 0), (0, PAGE - t_new), (0, 0)))
        kwn = jnp.pad(kvbs[:, :, 4], ((0, 0), (0, PAGE - t_new), (0, 0)))
        vwn = jnp.pad(kvbs[:, :, 5], ((0, 0), (0, PAGE - t_new), (0, 0)))
        qs = zbs[:, COL_Q:COL_Q + NSA_WIDTH].reshape(bsz, t_new, N_GROUPS, N_REP, HEAD_DIM)
        qs = jnp.transpose(qs, (0, 2, 4, 3, 1)).reshape(bsz, N_GROUPS, HEAD_DIM, N_REP * t_new)
        n_q = N_REP * t_new
        qT_s = jnp.concatenate(
            [jnp.pad(qs[:, g], ((0, 0), (0, 0), (g * n_q, LANES - (g + 1) * n_q))) for g in range(N_GROUPS)], axis=1)
        gns = zs[:, COL_GN:COL_GN + 3 * N_HEADS].reshape(bsz, t_new, 3, N_HEADS)
        gates_s = jnp.transpose(gns, (0, 3, 1, 2)).reshape(bsz, N_HEADS * t_new, 3)
        gates_s = jnp.pad(gates_s, ((0, 0), (0, LANES - N_HEADS * t_new), (0, LANES - 3)))
        o_s = _sample_nsa(pt, cache_pages, new_cmp, qT_s, knew, vnew, win_state, l, kwn, vwn, pe, w1b, w2b,
                          cb_s, nearb_s, farb_s, newb_s, winb_s, gates_s, nbp)
        o4 = o_s[:, :N_HEADS * t_new].reshape(bsz, N_GROUPS, N_REP, t_new, N_GROUPS, HEAD_DIM)
        o4 = jnp.stack([o4[:, g, :, :, g] for g in range(N_GROUPS)], axis=1)
        ob_s = jnp.transpose(o4, (0, 3, 1, 2, 4)).reshape(ts, NSA_WIDTH).astype(BF16)

        mm_s, bcol_s = _mix_mats(gmlp_ws[l], gmlp_bs[l], ts, bsz)
        oa_s, vn_s = _gmlp(zs, mm_s, bcol_s, gng, gnb, ts)
        outs["v_s"].append(vn_s.reshape(bsz, t_new, GMLP_WIDTH))
        halo = jnp.pad(state_pool[l], ((0, 0), (16 - POOL_BUF, 0), (0, 0))).reshape(bsz * 16, POOL_WIDTH)
        oc_s = _pool(zs, halo, lambda i: (i, 0), pwb, ps, t_new, False, past, 0)
        x1s, h2s, comb_s = _merge(xs, zs, oa_s, ob_s, oc_s, mods_s, ln1g, ln1b, pab, pbb, pcb, wob, rw, rb, ts, alpha)
        xs = _moe(h2s, comb_s, wg, wu, wd, x1s, mods_s, ln2g, ln2b, ts, alpha)

    return (xp.reshape(1, s, D_MODEL), xs.reshape(bsz, t_new, D_MODEL),
            jnp.stack(outs["nsa_p"]), jnp.stack(outs["nsa_s"]), jnp.stack(outs["win_p"]), jnp.stack(outs["win_s"]),
            jnp.stack(outs["pool_p"]), jnp.stack(outs["pool_s"]), jnp.stack(outs["v_s"]))
```
